```python
import jax, jax.numpy as jnp
from jax import lax
import numpy as np

D_MODEL = 1024
BATCH = 4
SEQ = 8192
DEPTH = 2

GRID_W = 64
CTX_LEN = 256
ROPE_BASE = 10000.0
NORM_EPS = 1e-6

RET_HEADS = 4
RET_DK = 64
RET_DV = 64
RET_CHUNK = 128
FOURIER_GROUPS = 4
FOURIER_GROUP_DIM = 64
NA_HEADS = 4
NA_HEAD_DIM = 64
NA_WIN_R = 8
NA_WIN_C = 16
MLA_HEADS = 4
MLA_NOPE = 64
MLA_ROPE = 32
MLA_V = 64
MLA_Q_RANK = 192
MLA_KV_RANK = 128
N_BRANCHES = 4
BRANCH_WIDTH = 256
Q_BLOCK = 128
FFN_DIM = 2816
MOE_EXPERTS = 8
MOE_TOP_K = 2
MOE_FFN_DIM = 3584
MOE_BLOCK = 256

KV_SIZES = (RET_HEADS * RET_DK, RET_HEADS * RET_DV, NA_HEADS * NA_HEAD_DIM, NA_HEADS * NA_HEAD_DIM,
            MLA_KV_RANK, MLA_ROPE)
Q_SIZES = (RET_HEADS * RET_DK, RET_HEADS * RET_DV, RET_HEADS * RET_DV,
           FOURIER_GROUPS * FOURIER_GROUP_DIM, NA_HEADS * NA_HEAD_DIM, MLA_Q_RANK,
           N_BRANCHES * D_MODEL)
KV_COLS = sum(KV_SIZES)
IN_COLS = KV_COLS + sum(Q_SIZES)

kernel_name = "hybrid_diffusion_parallel_mixer_trunk"


def rmsnorm(x, g):
    xf = x.astype(jnp.float32)
    y = xf * lax.rsqrt(jnp.mean(xf * xf, axis=-1, keepdims=True) + NORM_EPS)
    return (y * g.astype(jnp.float32)).astype(x.dtype)


def split_sizes(t, sizes):
    return jnp.split(t, [int(s) for s in np.cumsum(sizes)[:-1]], axis=-1)


def split_heads(t, n_heads):
    return t.reshape(*t.shape[:-1], n_heads, -1)


def axial_rope_tables(n, rot_dim):
    t = jnp.arange(n)
    row = (t // GRID_W).astype(jnp.float32)
    col = (t % GRID_W).astype(jnp.float32)
    nf = rot_dim // 4
    inv = ROPE_BASE ** (-jnp.arange(nf, dtype=jnp.float32) / nf)
    ang = jnp.concatenate([row[:, None] * inv, col[:, None] * inv], axis=-1)
    return jnp.cos(ang), jnp.sin(ang)


def apply_rope(x, cos, sin):
    half = x.shape[-1] // 2
    x1, x2 = x[..., :half], x[..., half:]
    c, s = cos[:, None, :], sin[:, None, :]
    return jnp.concatenate([x1 * c - x2 * s, x2 * c + x1 * s], axis=-1).astype(x.dtype)


def block_attention(q, k, v, scale):
    B, T, H, dq = q.shape
    nb = T // Q_BLOCK
    qb = jnp.moveaxis(q.reshape(B, nb, Q_BLOCK, H, dq), 1, 0)

    def attend(q_blk):
        s = jnp.einsum('bqhd,bkhd->bhqk', q_blk, k).astype(jnp.float32) * scale
        p = jax.nn.softmax(s, axis=-1).astype(v.dtype)
        return jnp.einsum('bhqk,bkhe->bqhe', p, v)

    o = lax.map(attend, qb)
    return jnp.moveaxis(o, 0, 1).reshape(B, T, H * v.shape[-1])


def retention_scan(q, k, v, log_gamma, s0):
    B, T, H, dk = q.shape
    dv = v.shape[-1]
    C = RET_CHUNK
    n = T // C
    qc = q.reshape(B, n, C, H, dk)
    kc = k.reshape(B, n, C, H, dk)
    vc = v.reshape(B, n, C, H, dv)
    pos = jnp.arange(C, dtype=jnp.float32)
    diff = pos[:, None] - pos[None, :]
    mask = jnp.where(diff >= 0, jnp.exp(log_gamma[:, None, None] * jnp.maximum(diff, 0.0)), 0.0)
    inner = jnp.einsum('bnhij,bnjhe->bnihe',
                       jnp.einsum('bnihd,bnjhd->bnhij', qc, kc) * mask, vc)
    k_w = jnp.exp(log_gamma[None, :] * (C - 1.0 - pos)[:, None])
    d_state = jnp.einsum('bnjhd,bnjhe->nbhde', kc * k_w[:, :, None], vc)
    chunk_decay = jnp.exp(log_gamma * C)[None, :, None, None]

    def step(s_prev, ds):
        return chunk_decay * s_prev + ds, s_prev

    s_final, s_before = lax.scan(step, s0, d_state)
    q_w = jnp.exp(log_gamma[None, :] * (pos + 1.0)[:, None])
    cross = jnp.einsum('bnihd,nbhde->bnihe', qc * q_w[:, :, None], s_before)
    return (inner + cross).reshape(B, T, H, dv), s_final


def retention_state(k, v, log_gamma):
    T = k.shape[1]
    w = jnp.exp(log_gamma[None, :] * (T - 1.0 - jnp.arange(T, dtype=jnp.float32))[:, None])
    return jnp.einsum('bthd,bthe->bhde', k * w[:, :, None], v)


def head_rms(o):
    o = o * lax.rsqrt(jnp.mean(o * o, axis=-1, keepdims=True) + NORM_EPS)
    return o.reshape(*o.shape[:2], -1)


def gate_directions(o_f, o_b, g_f, g_b):
    dt = g_f.dtype
    return jax.nn.silu(g_f) * head_rms(o_f).astype(dt) + jax.nn.silu(g_b) * head_rms(o_b).astype(dt)


def retention_branch(q, k, v, g_f, g_b, qc, kc, vc, gc_f, gc_b, decay_f, decay_b, rope):
    f32 = jnp.float32
    cos, sin = rope
    lg_f = jax.nn.log_sigmoid(decay_f.astype(f32))
    lg_b = jax.nn.log_sigmoid(decay_b.astype(f32))
    scale = RET_DK ** -0.5
    q = apply_rope(split_heads(q, RET_HEADS).astype(f32), cos, sin)
    k = apply_rope(split_heads(k, RET_HEADS).astype(f32), cos, sin) * scale
    v = split_heads(v, RET_HEADS).astype(f32)
    kc = split_heads(kc, RET_HEADS).astype(f32) * scale
    vc = split_heads(vc, RET_HEADS).astype(f32)
    flip = lambda t: jnp.flip(t, axis=1)
    if qc is None:
        s_f = retention_state(kc, vc, lg_f)
        s_b = retention_state(flip(kc), flip(vc), lg_b)
        yc = None
    else:
        zero = jnp.zeros((q.shape[0], RET_HEADS, RET_DK, RET_DV), f32)
        qc = split_heads(qc, RET_HEADS).astype(f32)
        oc_f, s_f = retention_scan(qc, kc, vc, lg_f, zero)
        oc_b, s_b = retention_scan(flip(qc), flip(kc), flip(vc), lg_b, zero)
        yc = gate_directions(oc_f, flip(oc_b), gc_f, gc_b)
    o_f, _ = retention_scan(q, k, v, lg_f, s_f)
    o_b, _ = retention_scan(flip(q), flip(k), flip(v), lg_b, s_b)
    y = gate_directions(o_f, flip(o_b), g_f, g_b)
    return y, yc


def fourier_mix(f):
    B, T, _ = f.shape
    fg = f.astype(jnp.float32).reshape(B, T, FOURIER_GROUPS, FOURIER_GROUP_DIM)
    spec = jnp.fft.fft2(fg, axes=(1, 3), norm='ortho')
    return jnp.real(spec).reshape(B, T, -1).astype(f.dtype)


def na_branch(q, k, v, qc, kc, vc, rpb):
    B, N, _ = q.shape
    rows = N // GRID_W
    wr = min(NA_WIN_R, rows)
    wc = NA_WIN_C
    n_loc = wr * wc
    scale = NA_HEAD_DIM ** -0.5
    qg = q.reshape(B, rows, GRID_W, NA_HEADS, NA_HEAD_DIM)
    kg = k.reshape(B, rows, GRID_W, NA_HEADS, NA_HEAD_DIM)
    vg = v.reshape(B, rows, GRID_W, NA_HEADS, NA_HEAD_DIM)
    kc = split_heads(kc, NA_HEADS)
    vc = split_heads(vc, NA_HEADS)
    cols = jnp.arange(GRID_W)
    col_idx = jnp.clip(cols - wc // 2, 0, GRID_W - wc)[:, None] + jnp.arange(wc)[None, :]
    col_bias_idx = col_idx - cols[:, None] + (NA_WIN_C - 1)
    rpb_f = rpb.astype(jnp.float32)

    def row_block(r):
        rs = jnp.clip(r - wr // 2, 0, rows - wr)
        q_r = lax.dynamic_index_in_dim(qg, r, axis=1, keepdims=False)
        k_r = lax.dynamic_slice_in_dim(kg, rs, wr, axis=1)[:, :, col_idx]
        v_r = lax.dynamic_slice_in_dim(vg, rs, wr, axis=1)[:, :, col_idx]
        k_r = jnp.moveaxis(k_r, 2, 1).reshape(B, GRID_W, n_loc, NA_HEADS, NA_HEAD_DIM)
        v_r = jnp.moveaxis(v_r, 2, 1).reshape(B, GRID_W, n_loc, NA_HEADS, NA_HEAD_DIM)
        row_bias_idx = rs + jnp.arange(wr) - r + (NA_WIN_R - 1)
        bias = rpb_f[:, row_bias_idx][:, :, col_bias_idx]
        bias = bias.transpose(0, 2, 1, 3).reshape(NA_HEADS, GRID_W, n_loc)
        s_loc = jnp.einsum('bqhd,bqkhd->bhqk', q_r, k_r).astype(jnp.float32) * scale + bias
        s_ctx = jnp.einsum('bqhd,bkhd->bhqk', q_r, kc).astype(jnp.float32) * scale
        p = jax.nn.softmax(jnp.concatenate([s_loc, s_ctx], axis=-1), axis=-1).astype(v.dtype)
        return (jnp.einsum('bhqk,bqkhd->bqhd', p[..., :n_loc], v_r)
                + jnp.einsum('bhqk,bkhd->bqhd', p[..., n_loc:], vc))

    o = lax.map(row_block, jnp.arange(rows))
    y = jnp.moveaxis(o, 0, 1).reshape(B, N, NA_HEADS * NA_HEAD_DIM)
    yc = None if qc is None else block_attention(split_heads(qc, NA_HEADS), kc, vc, scale)
    return y, yc


def mla_keys(c_kv, k_rope, kv_norm, w_ukv, rope):
    B, T, _ = c_kv.shape
    kv = split_heads(rmsnorm(c_kv, kv_norm) @ w_ukv, MLA_HEADS)
    k_nope, v = kv[..., :MLA_NOPE], kv[..., MLA_NOPE:]
    k_rope = k_rope[:, :, None, :]
    if rope is not None:
        k_rope = apply_rope(k_rope, *rope)
    k = jnp.concatenate([k_nope, jnp.broadcast_to(k_rope, (B, T, MLA_HEADS, MLA_ROPE))], axis=-1)
    return k, v


def mla_queries(c_q, q_norm, w_uq, rope):
    q = split_heads(rmsnorm(c_q, q_norm) @ w_uq, MLA_HEADS)
    q_nope, q_rope = q[..., :MLA_NOPE], q[..., MLA_NOPE:]
    if rope is not None:
        q_rope = apply_rope(q_rope, *rope)
    return jnp.concatenate([q_nope, q_rope], axis=-1)


def mla_branch(c_q, c_kv, k_r, cc_q, cc_kv, ck_r, q_norm, kv_norm, w_uq, w_ukv, rope):
    scale = (MLA_NOPE + MLA_ROPE) ** -0.5
    k, v = mla_keys(c_kv, k_r, kv_norm, w_ukv, rope)
    kc, vc = mla_keys(cc_kv, ck_r, kv_norm, w_ukv, None)
    q = mla_queries(c_q, q_norm, w_uq, rope)
    y = block_attention(q, jnp.concatenate([k, kc], axis=1), jnp.concatenate([v, vc], axis=1), scale)
    yc = None if cc_q is None else block_attention(mla_queries(cc_q, q_norm, w_uq, None), kc, vc, scale)
    return y, yc


def merge_branches(branches, gate_cols, w_branch, w_out):
    B, T, _ = gate_cols.shape
    g = jax.nn.sigmoid(gate_cols.reshape(B, T, N_BRANCHES, D_MODEL))
    m = g[:, :, 0] * (branches[0] @ w_branch[0])
    for i in range(1, N_BRANCHES):
        m = m + g[:, :, i] * (branches[i] @ w_branch[i])
    return m @ w_out


def mixer_sublayer(h, hc, w_in, decay_f, decay_b, q_norm, kv_norm, w_uq, w_ukv, rpb, w_branch, w_out,
                   ret_rope, mla_rope, ctx_out):
    u = h @ w_in
    r_k, r_v, n_k, n_v, m_ckv, m_kr = split_sizes(u[..., :KV_COLS], KV_SIZES)
    r_q, r_gf, r_gb, f_in, n_q, m_cq, gate = split_sizes(u[..., KV_COLS:], Q_SIZES)
    if ctx_out:
        uc = hc @ w_in
        c_qside = split_sizes(uc[..., KV_COLS:], Q_SIZES)
    else:
        uc = hc @ w_in[:, :KV_COLS]
        c_qside = (None,) * len(Q_SIZES)
    cr_k, cr_v, cn_k, cn_v, cm_ckv, cm_kr = split_sizes(uc[..., :KV_COLS], KV_SIZES)
    cr_q, cr_gf, cr_gb, cf_in, cn_q, cm_cq, cgate = c_qside

    y_ret, yc_ret = retention_branch(r_q, r_k, r_v, r_gf, r_gb, cr_q, cr_k, cr_v, cr_gf, cr_gb,
                                     decay_f, decay_b, ret_rope)
    y_four = fourier_mix(f_in)
    y_na, yc_na = na_branch(n_q, n_k, n_v, cn_q, cn_k, cn_v, rpb)
    y_mla, yc_mla = mla_branch(m_cq, m_ckv, m_kr, cm_cq, cm_ckv, cm_kr, q_norm, kv_norm, w_uq, w_ukv, mla_rope)
    y = merge_branches((y_ret, y_four, y_na, y_mla), gate, w_branch, w_out)
    if not ctx_out:
        return y, None
    yc = merge_branches((yc_ret, fourier_mix(cf_in), yc_na, yc_mla), cgate, w_branch, w_out)
    return y, yc


def swiglu(h, w_gate, w_up, w_down):
    return (jax.nn.silu(h @ w_gate) * (h @ w_up)) @ w_down


def moe_swiglu(h, router, w_gate, w_up, w_down):
    B, T, D = h.shape
    hf = h.reshape(-1, D)
    n_tok = hf.shape[0]
    logits = (hf @ router).astype(jnp.float32)
    top_logit, top_idx = lax.top_k(logits, MOE_TOP_K)
    top_w = jax.nn.softmax(top_logit, axis=-1)
    n_assign = n_tok * MOE_TOP_K
    exp_flat = top_idx.reshape(-1)
    tok_flat = jnp.repeat(jnp.arange(n_tok, dtype=jnp.int32), MOE_TOP_K)
    order = jnp.argsort(exp_flat)
    exp_sorted = exp_flat[order]
    tok_sorted = tok_flat[order]
    w_sorted = top_w.reshape(-1)[order]
    counts = jnp.bincount(exp_flat, length=MOE_EXPERTS)
    padded = (counts + MOE_BLOCK - 1) // MOE_BLOCK * MOE_BLOCK
    pad_end = jnp.cumsum(padded)
    pad_start = pad_end - padded
    grp_start = jnp.cumsum(counts) - counts
    dest = pad_start[exp_sorted] + jnp.arange(n_assign) - grp_start[exp_sorted]
    n_rows = -(-n_assign // MOE_BLOCK) * MOE_BLOCK + MOE_EXPERTS * MOE_BLOCK
    n_blocks = n_rows // MOE_BLOCK
    row_tok = jnp.full((n_rows,), n_tok, jnp.int32).at[dest].set(tok_sorted)
    h_pad = jnp.concatenate([hf, jnp.zeros((1, D), hf.dtype)], axis=0)
    xb = h_pad[row_tok].reshape(n_blocks, MOE_BLOCK, D)
    blk_start = jnp.arange(n_blocks) * MOE_BLOCK
    blk_exp = jnp.minimum(jnp.sum(pad_end[None, :] <= blk_start[:, None], axis=1), MOE_EXPERTS - 1)

    def expert_block(args):
        x_blk, e = args
        return (jax.nn.silu(x_blk @ w_gate[e]) * (x_blk @ w_up[e])) @ w_down[e]

    yb = lax.map(expert_block, (xb, blk_exp)).reshape(n_rows, D)
    y_assign = yb[dest] * w_sorted[:, None].astype(yb.dtype)
    out = jnp.zeros_like(hf).at[tok_sorted].add(y_assign)
    return out.reshape(B, T, D)


def setup_inputs(seed: int = 0) -> dict:
    key = jax.random.key(seed)
    ks = iter(jax.random.split(key, 32))

    def nrm(shape, std):
        return jax.random.normal(next(ks), shape, jnp.float32) * std

    n_dense = (DEPTH + 1) // 2
    n_moe = DEPTH // 2
    d_in = D_MODEL ** -0.5
    decay0 = jnp.asarray(np.log(2.0 ** (5.0 + np.arange(RET_HEADS)) - 1.0), jnp.float32)
    return {
        'x': nrm((BATCH, SEQ, D_MODEL), 1.0),
        'c': nrm((BATCH, D_MODEL), 1.0),
        'ctx': nrm((BATCH, CTX_LEN, D_MODEL), 1.0),
        'c_ctx': nrm((D_MODEL,), 1.0),
        'ada_w': nrm((DEPTH, D_MODEL, 6 * D_MODEL), 0.5 * d_in),
        'ada_b': nrm((DEPTH, 6 * D_MODEL), 0.01),
        'norm_mix': 1.0 + nrm((DEPTH, D_MODEL), 0.01),
        'norm_ffn': 1.0 + nrm((DEPTH, D_MODEL), 0.01),
        'w_in': nrm((DEPTH, D_MODEL, IN_COLS), d_in),
        'ret_decay_fwd': decay0[None, :] + nrm((DEPTH, RET_HEADS), 0.01),
        'ret_decay_bwd': decay0[None, :] + nrm((DEPTH, RET_HEADS), 0.01),
        'mla_q_norm': 1.0 + nrm((DEPTH, MLA_Q_RANK), 0.01),
        'mla_kv_norm': 1.0 + nrm((DEPTH, MLA_KV_RANK), 0.01),
        'mla_w_uq': nrm((DEPTH, MLA_Q_RANK, MLA_HEADS * (MLA_NOPE + MLA_ROPE)), MLA_Q_RANK ** -0.5),
        'mla_w_ukv': nrm((DEPTH, MLA_KV_RANK, MLA_HEADS * (MLA_NOPE + MLA_V)), MLA_KV_RANK ** -0.5),
        'na_rpb': nrm((DEPTH, NA_HEADS, 2 * NA_WIN_R - 1, 2 * NA_WIN_C - 1), 0.02),
        'w_branch': nrm((DEPTH, N_BRANCHES, BRANCH_WIDTH, D_MODEL), BRANCH_WIDTH ** -0.5),
        'w_out': nrm((DEPTH, D_MODEL, D_MODEL), d_in),
        'ffn_w_gate': nrm((n_dense, D_MODEL, FFN_DIM), d_in),
        'ffn_w_up': nrm((n_dense, D_MODEL, FFN_DIM), d_in),
        'ffn_w_down': nrm((n_dense, FFN_DIM, D_MODEL), FFN_DIM ** -0.5),
        'moe_router': nrm((n_moe, D_MODEL, MOE_EXPERTS), d_in),
        'moe_w_gate': nrm((n_moe, MOE_EXPERTS, D_MODEL, MOE_FFN_DIM), d_in),
        'moe_w_up': nrm((n_moe, MOE_EXPERTS, D_MODEL, MOE_FFN_DIM), d_in),
        'moe_w_down': nrm((n_moe, MOE_EXPERTS, MOE_FFN_DIM, D_MODEL), MOE_FFN_DIM ** -0.5),
        'norm_final': 1.0 + nrm((D_MODEL,), 0.01),
    }


def reference(x, c, ctx, c_ctx, ada_w, ada_b, norm_mix, norm_ffn, w_in, ret_decay_fwd, ret_decay_bwd,
              mla_q_norm, mla_kv_norm, mla_w_uq, mla_w_ukv, na_rpb, w_branch, w_out,
              ffn_w_gate, ffn_w_up, ffn_w_down, moe_router, moe_w_gate, moe_w_up, moe_w_down, norm_final):
    n_lat = x.shape[1]
    ret_rope = axial_rope_tables(n_lat, RET_DK)
    mla_rope = axial_rope_tables(n_lat, MLA_ROPE)
    xc = ctx
    for i in range(DEPTH):
        ctx_out = i < DEPTH - 1
        mod = (jax.nn.silu(c) @ ada_w[i] + ada_b[i])[:, None, :]
        mod_c = (jax.nn.silu(c_ctx) @ ada_w[i] + ada_b[i])[None, None, :]
        sh1, sc1, g1, sh2, sc2, g2 = jnp.split(mod, 6, axis=-1)
        csh1, csc1, cg1, csh2, csc2, cg2 = jnp.split(mod_c, 6, axis=-1)

        h = rmsnorm(x, norm_mix[i]) * (1 + sc1) + sh1
        hc = rmsnorm(xc, norm_mix[i]) * (1 + csc1) + csh1
        y, yc = mixer_sublayer(h, hc, w_in[i], ret_decay_fwd[i], ret_decay_bwd[i], mla_q_norm[i],
                               mla_kv_norm[i], mla_w_uq[i], mla_w_ukv[i], na_rpb[i], w_branch[i], w_out[i],
                               ret_rope, mla_rope, ctx_out)
        x = x + g1 * y

        j = i // 2
        if i % 2 == 0:
            ffn = lambda t: swiglu(t, ffn_w_gate[j], ffn_w_up[j], ffn_w_down[j])
        else:
            ffn = lambda t: moe_swiglu(t, moe_router[j], moe_w_gate[j], moe_w_up[j], moe_w_down[j])
        x = x + g2 * ffn(rmsnorm(x, norm_ffn[i]) * (1 + sc2) + sh2)
        if ctx_out:
            xc = xc + cg1 * yc
            xc = xc + cg2 * ffn(rmsnorm(xc, norm_ffn[i]) * (1 + csc2) + csh2)
    return rmsnorm(x, norm_final)
```

```python
import functools

import numpy as np
import jax
import jax.numpy as jnp
from jax import lax
from jax.experimental import pallas as pl
from jax.experimental.pallas import tpu as pltpu

F32 = jnp.float32
BF16 = jnp.bfloat16

GRID_W = 64
ROPE_BASE = 10000.0
NORM_EPS = 1e-6
HEADS = 4
RET_DK = 64
FOURIER_GROUP_DIM = 64
NA_HEAD_DIM = 64
NA_WIN_R = 8
NA_WIN_C = 16
MLA_NOPE = 64
MLA_ROPE = 32
MLA_V = 64
MLA_Q_RANK = 192
MLA_KV_RANK = 128
MOE_TOP_K = 2
BW = 256

COL_GATE = 0
CB_RQ, CB_RK, CB_RV, CB_RGF, CB_RGB, CB_F, CB_NQ, CB_NK, CB_NV, CB_MCQ, CB_MKV = range(11)
LANES = 128
NEG = -1e30

VMEM_LIMIT = 48 * 1024 * 1024


def _cp(sem, vmem=VMEM_LIMIT):
    return pltpu.CompilerParams(dimension_semantics=sem, vmem_limit_bytes=vmem)


def _dot(a, b):
    return jnp.dot(a, b, preferred_element_type=F32)


def _dot_nt(a, b):
    return lax.dot_general(a, b, (((1,), (1,)), ((), ())), preferred_element_type=F32)


def _dot_tn(a, b):
    return lax.dot_general(a, b, (((0,), (0,)), ((), ())), preferred_element_type=F32)


def _silu(x):
    return x * jax.nn.sigmoid(x)


def _adaln_kernel(c_ref, w_ref, b_ref, o_ref):
    s = _silu(c_ref[...])
    o_ref[0] = _dot(s.astype(BF16), w_ref[0].astype(BF16)) + b_ref[0]


def adaln(cc, ada_w, ada_b):
    depth, d, n6 = ada_w.shape
    tn = n6 // 4
    return pl.pallas_call(
        _adaln_kernel,
        grid=(depth, n6 // tn),
        in_specs=[
            pl.BlockSpec((8, d), lambda l, j: (0, 0)),
            pl.BlockSpec((1, d, tn), lambda l, j: (l, 0, j)),
            pl.BlockSpec((1, 1, tn), lambda l, j: (l, 0, j)),
        ],
        out_specs=pl.BlockSpec((1, 8, tn), lambda l, j: (l, 0, j)),
        out_shape=jax.ShapeDtypeStruct((depth, 8, n6), F32),
        compiler_params=_cp(("arbitrary", "arbitrary")),
        name="adaln",
    )(cc, ada_w, ada_b.reshape(depth, 1, n6))


def _inproj_kernel(x_ref, g_ref, sc_ref, sh_ref, w_ref, o_ref, h_ref):
    @pl.when(pl.program_id(1) == 0)
    def _():
        x = x_ref[...]
        y = x * lax.rsqrt(jnp.mean(x * x, axis=-1, keepdims=True) + NORM_EPS)
        h = (y * g_ref[...]) * (1.0 + sc_ref[0]) + sh_ref[0]
        h_ref[...] = h.astype(BF16)

    o_ref[...] = _dot(h_ref[...], w_ref[...]).astype(o_ref.dtype)


def _mod_spec(d, mod_row, tm, k):
    return pl.BlockSpec((1, 1, d), lambda i, *_: (mod_row(i * tm) * 6 + k, 0, 0))


def norm_inproj(x, gain, mods, mod_row, w, tm, tn):
    n, d = x.shape
    nc = w.shape[1]
    return pl.pallas_call(
        _inproj_kernel,
        grid=(n // tm, nc // tn),
        in_specs=[
            pl.BlockSpec((tm, d), lambda i, j: (i, 0)),
            pl.BlockSpec((1, d), lambda i, j: (0, 0)),
            _mod_spec(d, mod_row, tm, 1),
            _mod_spec(d, mod_row, tm, 0),
            pl.BlockSpec((d, tn), lambda i, j: (0, j)),
        ],
        out_specs=pl.BlockSpec((tm, tn), lambda i, j: (i, j)),
        out_shape=jax.ShapeDtypeStruct((n, nc), BF16),
        scratch_shapes=[pltpu.VMEM((tm, d), BF16)],
        compiler_params=_cp(("arbitrary", "arbitrary")),
        name="norm_inproj",
    )(x, gain.reshape(1, d), mods, mods, w)


def _ret_kernel(lg_ref, q_ref, k_ref, v_ref, g_ref, cos_ref, sin_ref, s0_ref, lgq_ref, lgv_ref,
                y_ref, sout_ref, s_ref, *, n_chunks):
    d = pl.program_id(1)
    c = pl.program_id(2)
    rev = d == 1
    cs = q_ref.shape[0]
    w = q_ref.shape[1]
    half = w // 2

    @pl.when(c == 0)
    def _():
        s_ref[...] = s0_ref[0, 0]

    cos = cos_ref[...]
    sin = sin_ref[...]

    def rope(t):
        t1, t2 = t[:, :half], t[:, half:]
        return jnp.concatenate([t1 * cos - t2 * sin, t2 * cos + t1 * sin], axis=-1)

    q = rope(q_ref[...].astype(F32))
    k = rope(k_ref[...].astype(F32)) * (RET_DK ** -0.5)
    vb = v_ref[...]

    pos_i = lax.broadcasted_iota(jnp.int32, (cs, 1), 0).astype(F32)
    pos_j = lax.broadcasted_iota(jnp.int32, (1, cs), 1).astype(F32)
    p_i = jnp.where(rev, cs - 1.0 - pos_i, pos_i)
    p_j = jnp.where(rev, cs - 1.0 - pos_j, pos_j)
    diff = p_i - p_j
    lgq = lgq_ref[0]
    lgv = lgv_ref[0]
    q_w = jnp.exp(lgq * (p_i + 1.0))
    k_w = jnp.exp(lgq * (cs - 1.0 - p_i))

    lane = lax.broadcasted_iota(jnp.int32, (1, w), 1)
    head_q = (lane % half) // (half // HEADS)
    head_v = lane // (w // HEADS)

    s_prev = s_ref[...]
    o = _dot((q * q_w).astype(BF16), s_prev.astype(BF16))
    qb = q.astype(BF16)
    kb = k.astype(BF16)
    zero_b = jnp.zeros_like(qb)
    for h in range(HEADS):
        a = _dot_nt(jnp.where(head_q == h, qb, zero_b), kb)
        decay = jnp.where(diff >= 0, jnp.exp(lg_ref[d, h] * jnp.maximum(diff, 0.0)), 0.0)
        oh = _dot((a * decay).astype(BF16), vb)
        o = o + jnp.where(head_v == h, oh, 0.0)

    ds = _dot_tn((k * k_w).astype(BF16), vb)
    row_head = (lax.broadcasted_iota(jnp.int32, (w, 1), 0) % half) // (half // HEADS)
    s_new = s_prev * jnp.exp(lgv * float(cs)) + jnp.where(row_head == head_v, ds, 0.0)
    s_ref[...] = s_new

    @pl.when(c == n_chunks - 1)
    def _():
        sout_ref[0, 0] = s_new

    hd = w // HEADS
    gi = lax.broadcasted_iota(jnp.int32, (w, w), 0) // hd
    gj = lax.broadcasted_iota(jnp.int32, (w, w), 1) // hd
    avg = jnp.where(gi == gj, 1.0 / hd, 0.0).astype(BF16)
    ms = _dot((o * o).astype(BF16), avg)
    on = o * lax.rsqrt(ms + NORM_EPS)
    y_ref[0] = (_silu(g_ref[...].astype(F32)) * on).astype(y_ref.dtype)


def retention(u, cb0, lg, cos, sin, s0, batch, cs):
    n = u.shape[0]
    t = n // batch
    nch = t // cs
    w = BW
    half = w // 2
    lgq = jnp.tile(jnp.repeat(lg, half // HEADS, axis=1), (1, 2)).reshape(2, 1, w)
    lgv = jnp.repeat(lg, w // HEADS, axis=1).reshape(2, 1, w)

    def rows(b, d, c):
        return b * nch + jnp.where(d == 1, nch - 1 - c, c)

    def col(cb):
        return pl.BlockSpec((cs, w), lambda b, d, c: (rows(b, d, c), cb0 + cb))

    def tab():
        return pl.BlockSpec((cs, half), lambda b, d, c: (jnp.where(d == 1, nch - 1 - c, c), 0))

    y, s_out = pl.pallas_call(
        functools.partial(_ret_kernel, n_chunks=nch),
        grid=(batch, 2, nch),
        in_specs=[
            pl.BlockSpec(memory_space=pltpu.SMEM),
            col(CB_RQ), col(CB_RK), col(CB_RV),
            pl.BlockSpec((cs, w), lambda b, d, c: (rows(b, d, c), cb0 + CB_RGF + d)),
            tab(), tab(),
            pl.BlockSpec((1, 1, w, w), lambda b, d, c: (b, d, 0, 0)),
            pl.BlockSpec((1, 1, w), lambda b, d, c: (d, 0, 0)),
            pl.BlockSpec((1, 1, w), lambda b, d, c: (d, 0, 0)),
        ],
        out_specs=[
            pl.BlockSpec((1, cs, w), lambda b, d, c: (d, rows(b, d, c), 0)),
            pl.BlockSpec((1, 1, w, w), lambda b, d, c: (b, d, 0, 0)),
        ],
        out_shape=[
            jax.ShapeDtypeStruct((2, n, w), BF16),
            jax.ShapeDtypeStruct((batch, 2, w, w), F32),
        ],
        scratch_shapes=[pltpu.VMEM((w, w), F32)],
        compiler_params=_cp(("arbitrary", "arbitrary", "arbitrary")),
        name="retention",
    )(lg, u, u, u, u, cos, sin, s0, lgq, lgv)
    return y, s_out


def _dft_tables(t, t1, t2):
    k1 = jnp.arange(t1, dtype=jnp.int32)
    a = jnp.arange(t1, dtype=jnp.int32)
    m = jnp.arange(t2, dtype=jnp.int32)
    ph1 = (k1[None, :, None] * (a[None, None, :] * t2 + m[:, None, None])) % t
    ang1 = ph1.astype(F32) * (2.0 * np.pi / t)
    ph2 = (m[:, None] * m[None, :]) % t2
    ang2 = ph2.astype(F32) * (2.0 * np.pi / t2)
    return (jnp.cos(ang1).astype(BF16), jnp.sin(ang1).astype(BF16),
            jnp.cos(ang2).astype(BF16), jnp.sin(ang2).astype(BF16))


def _channel_tables(width):
    ch = jnp.arange(width, dtype=jnp.int32)
    same = (ch[:, None] // FOURIER_GROUP_DIM) == (ch[None, :] // FOURIER_GROUP_DIM)
    ph = ((ch[:, None] % FOURIER_GROUP_DIM) * (ch[None, :] % FOURIER_GROUP_DIM)) % FOURIER_GROUP_DIM
    ang = ph.astype(F32) * (2.0 * np.pi / FOURIER_GROUP_DIM)
    return (jnp.where(same, jnp.cos(ang), 0.0).astype(BF16),
            jnp.where(same, jnp.sin(ang), 0.0).astype(BF16))


def _fourier_kernel(x_ref, c1_ref, s1_ref, c2_ref, s2_ref, cc_ref, sc_ref, o_ref, xa, yre, yim,
                    *, t1, t2, norm):
    xa[...] = x_ref[...].astype(F32)

    def stage1(m, carry):
        xs = xa[pl.ds(m, t1, stride=t2), :].astype(BF16)
        r0 = pl.multiple_of(m * t1, t1)
        yre[pl.ds(r0, t1), :] = _dot(c1_ref[m], xs)
        yim[pl.ds(r0, t1), :] = -_dot(s1_ref[m], xs)
        return carry

    lax.fori_loop(0, t2, stage1, 0)

    c2 = c2_ref[...]
    s2 = s2_ref[...]
    cc = cc_ref[...]
    sc = sc_ref[...]

    def stage2(k1, carry):
        yr = yre[pl.ds(k1, t2, stride=t1), :].astype(BF16)
        yi = yim[pl.ds(k1, t2, stride=t1), :].astype(BF16)
        zr = _dot(c2, yr) + _dot(s2, yi)
        zi = _dot(c2, yi) - _dot(s2, yr)
        out = (_dot(zr.astype(BF16), cc) + _dot(zi.astype(BF16), sc)) * norm
        xa[pl.ds(k1, t2, stride=t1), :] = out
        return carry

    lax.fori_loop(0, t1, stage2, 0)
    o_ref[...] = xa[...].astype(o_ref.dtype)


def fourier_long(u, cb, batch, t2=LANES):
    n = u.shape[0]
    t = n // batch
    t1 = t // t2
    c1, s1, c2, s2 = _dft_tables(t, t1, t2)
    cc, sc = _channel_tables(LANES)
    norm = float(1.0 / np.sqrt(t * FOURIER_GROUP_DIM))
    full = lambda shape: pl.BlockSpec(shape, lambda b, hh: (0,) * len(shape))
    return pl.pallas_call(
        functools.partial(_fourier_kernel, t1=t1, t2=t2, norm=norm),
        grid=(batch, BW // LANES),
        in_specs=[
            pl.BlockSpec((t, LANES), lambda b, hh: (b, cb * (BW // LANES) + hh)),
            full((t2, t1, t1)), full((t2, t1, t1)), full((t2, t2)), full((t2, t2)),
            full((LANES, LANES)), full((LANES, LANES)),
        ],
        out_specs=pl.BlockSpec((t, LANES), lambda b, hh: (b, hh)),
        out_shape=jax.ShapeDtypeStruct((n, BW), BF16),
        scratch_shapes=[pltpu.VMEM((t, LANES), F32)] * 3,
        compiler_params=_cp(("arbitrary", "arbitrary")),
        name="fourier",
    )(u, c1, s1, c2, s2, cc, sc)


def _fourier_small_kernel(x_ref, ct_ref, st_ref, cc_ref, sc_ref, o_ref, *, norm):
    x = x_ref[...]
    zr = _dot(ct_ref[...], x)
    zi = -_dot(st_ref[...], x)
    out = (_dot(zr.astype(BF16), cc_ref[...]) + _dot(zi.astype(BF16), sc_ref[...])) * norm
    o_ref[...] = out.astype(o_ref.dtype)


def fourier_short(u, cb, batch):
    n = u.shape[0]
    t = n // batch
    pos = jnp.arange(t, dtype=jnp.int32)
    ang = ((pos[:, None] * pos[None, :]) % t).astype(F32) * (2.0 * np.pi / t)
    ct, st = jnp.cos(ang).astype(BF16), jnp.sin(ang).astype(BF16)
    cc, sc = _channel_tables(BW)
    norm = float(1.0 / np.sqrt(t * FOURIER_GROUP_DIM))
    full = lambda shape: pl.BlockSpec(shape, lambda b: (0,) * len(shape))
    return pl.pallas_call(
        functools.partial(_fourier_small_kernel, norm=norm),
        grid=(batch,),
        in_specs=[pl.BlockSpec((t, BW), lambda b: (b, cb)), full((t, t)), full((t, t)),
                  full((BW, BW)), full((BW, BW))],
        out_specs=pl.BlockSpec((t, BW), lambda b: (b, 0)),
        out_shape=jax.ShapeDtypeStruct((n, BW), BF16),
        compiler_params=_cp(("arbitrary",)),
        name="fourier_ctx",
    )(u, ct, st, cc, sc)


def _na_bias_table(rpb):
    var = jnp.arange(NA_WIN_R)[:, None, None, None]
    kr = jnp.arange(NA_WIN_R)[None, :, None, None]
    qc = jnp.arange(GRID_W)[None, None, :, None]
    kc = jnp.arange(GRID_W)[None, None, None, :]
    ri = jnp.broadcast_to(kr - var + (NA_WIN_R - 1), (NA_WIN_R, NA_WIN_R, GRID_W, GRID_W))
    ci = jnp.broadcast_to(kc - qc + (NA_WIN_C - 1), (NA_WIN_R, NA_WIN_R, GRID_W, GRID_W))
    start = jnp.clip(qc - NA_WIN_C // 2, 0, GRID_W - NA_WIN_C)
    valid = jnp.broadcast_to((kc >= start) & (kc < start + NA_WIN_C), ci.shape)
    tab = rpb.astype(F32)[:, ri, jnp.clip(ci, 0, 2 * NA_WIN_C - 2)]
    tab = jnp.where(valid[None], tab, NEG)
    tab = tab.transpose(1, 0, 3, 2, 4)
    return tab.reshape(NA_WIN_R, HEADS, GRID_W, NA_WIN_R * GRID_W)


def _na_kernel(q_ref, k_ref, v_ref, kc_ref, vc_ref, bias_ref, o_ref, *, rows_per_step, n_rows):
    i = pl.program_id(1)
    w = q_ref.shape[1]
    lane = lax.broadcasted_iota(jnp.int32, (1, w), 1)
    head = lane // (w // HEADS)
    scale = NA_HEAD_DIM ** -0.5
    kc = kc_ref[...]
    vc = vc_ref[...]
    win = NA_WIN_R * GRID_W

    def row(rl, carry):
        r = i * rows_per_step + rl
        rs = jnp.clip(r - NA_WIN_R // 2, 0, n_rows - NA_WIN_R)
        var = r - rs
        q0 = pl.multiple_of(rl * GRID_W, GRID_W)
        k0 = pl.multiple_of(rs * GRID_W, GRID_W)
        q = q_ref[pl.ds(q0, GRID_W), :]
        kw = k_ref[pl.ds(k0, win), :]
        vw = v_ref[pl.ds(k0, win), :]
        zero_b = jnp.zeros_like(q)
        acc = jnp.zeros((GRID_W, w), F32)
        for h in range(HEADS):
            qh = jnp.where(head == h, q, zero_b)
            s_loc = _dot_nt(qh, kw) * scale + bias_ref[var, h]
            s_ctx = _dot_nt(qh, kc) * scale
            m = jnp.maximum(jnp.max(s_loc, axis=-1, keepdims=True), jnp.max(s_ctx, axis=-1, keepdims=True))
            p_loc = jnp.exp(s_loc - m)
            p_ctx = jnp.exp(s_ctx - m)
            l = jnp.sum(p_loc, axis=-1, keepdims=True) + jnp.sum(p_ctx, axis=-1, keepdims=True)
            pv = _dot(p_loc.astype(BF16), vw) + _dot(p_ctx.astype(BF16), vc)
            acc = acc + jnp.where(head == h, pv / l, 0.0)
        o_ref[pl.ds(q0, GRID_W), :] = acc.astype(o_ref.dtype)
        return carry

    lax.fori_loop(0, rows_per_step, row, 0)


def na_attention(u, uc, cb0, bias_tab, batch, rows_per_step):
    n = u.shape[0]
    t = n // batch
    tc = uc.shape[0] // batch
    n_rows = t // GRID_W
    steps = n_rows // rows_per_step
    tq = rows_per_step * GRID_W
    return pl.pallas_call(
        functools.partial(_na_kernel, rows_per_step=rows_per_step, n_rows=n_rows),
        grid=(batch, steps),
        in_specs=[
            pl.BlockSpec((tq, BW), lambda b, i: (b * steps + i, cb0 + CB_NQ)),
            pl.BlockSpec((t, BW), lambda b, i: (b, cb0 + CB_NK)),
            pl.BlockSpec((t, BW), lambda b, i: (b, cb0 + CB_NV)),
            pl.BlockSpec((tc, BW), lambda b, i: (b, cb0 + CB_NK)),
            pl.BlockSpec((tc, BW), lambda b, i: (b, cb0 + CB_NV)),
            pl.BlockSpec(bias_tab.shape, lambda b, i: (0, 0, 0, 0)),
        ],
        out_specs=pl.BlockSpec((tq, BW), lambda b, i: (b * steps + i, 0)),
        out_shape=jax.ShapeDtypeStruct((n, BW), BF16),
        compiler_params=_cp(("arbitrary", "arbitrary")),
        name="na_attention",
    )(u, u, u, uc, uc, bias_tab)


def _na_ctx_kernel(q_ref, k_ref, v_ref, o_ref):
    w = q_ref.shape[1]
    lane = lax.broadcasted_iota(jnp.int32, (1, w), 1)
    head = lane // (w // HEADS)
    scale = NA_HEAD_DIM ** -0.5
    q = q_ref[...]
    k = k_ref[...]
    v = v_ref[...]
    zero_b = jnp.zeros_like(q)
    acc = jnp.zeros(q.shape, F32)
    for h in range(HEADS):
        s = _dot_nt(jnp.where(head == h, q, zero_b), k) * scale
        p = jnp.exp(s - jnp.max(s, axis=-1, keepdims=True))
        l = jnp.sum(p, axis=-1, keepdims=True)
        acc = acc + jnp.where(head == h, _dot(p.astype(BF16), v) / l, 0.0)
    o_ref[...] = acc.astype(o_ref.dtype)


def na_ctx_attention(uc, cb0, batch):
    tc = uc.shape[0] // batch
    spec = lambda cb: pl.BlockSpec((tc, BW), lambda b: (b, cb0 + cb))
    return pl.pallas_call(
        _na_ctx_kernel,
        grid=(batch,),
        in_specs=[spec(CB_NQ), spec(CB_NK), spec(CB_NV)],
        out_specs=pl.BlockSpec((tc, BW), lambda b: (b, 0)),
        out_shape=jax.ShapeDtypeStruct((uc.shape[0], BW), BF16),
        compiler_params=_cp(("arbitrary",)),
        name="na_ctx_attention",
    )(uc, uc, uc)


def _mla_prep_kernel(cq_ref, ckv_ref, kr_ref, cos_ref, sin_ref, qn_ref, kvn_ref, wq_ref, wqr_ref,
                     wk_ref, wv_ref, p1_ref, p2_ref, one_ref, q_ref, k_ref, v_ref):
    cos = cos_ref[...]
    sin = sin_ref[...]
    cos4 = jnp.concatenate([cos] * HEADS, axis=-1)
    sin4 = jnp.concatenate([sin] * HEADS, axis=-1)

    cq = cq_ref[...].astype(F32)
    ms = jnp.sum(cq * cq, axis=-1, keepdims=True) * (1.0 / MLA_Q_RANK)
    cqn = ((cq * lax.rsqrt(ms + NORM_EPS)) * qn_ref[...]).astype(BF16)
    q = _dot(cqn, wq_ref[...]) * cos4 + _dot(cqn, wqr_ref[...]) * sin4
    q_ref[...] = (q * ((MLA_NOPE + MLA_ROPE) ** -0.5)).astype(q_ref.dtype)

    ckv = ckv_ref[...].astype(F32)
    ms = jnp.mean(ckv * ckv, axis=-1, keepdims=True)
    ckvn = ((ckv * lax.rsqrt(ms + NORM_EPS)) * kvn_ref[...]).astype(BF16)
    kr = kr_ref[...]
    k_rot = _dot(kr, p1_ref[...]) * cos + _dot(kr, p2_ref[...]) * sin
    k = _dot(ckvn, wk_ref[...]) + jnp.concatenate([k_rot] * HEADS, axis=-1)
    k_ref[...] = k.astype(k_ref.dtype)
    v_ref[...] = (_dot(ckvn, wv_ref[...]) + one_ref[...]).astype(v_ref.dtype)


def _mla_weights(w_uq, w_ukv):
    qr = w_uq.shape[0]
    dq = MLA_NOPE + MLA_ROPE
    hr = MLA_ROPE // 2
    wq3 = w_uq.reshape(qr, HEADS, dq)
    zq = jnp.zeros((qr, HEADS, LANES - dq), F32)
    wq = jnp.concatenate([wq3, zq], axis=-1)
    x1 = wq3[..., MLA_NOPE:MLA_NOPE + hr]
    x2 = wq3[..., MLA_NOPE + hr:]
    wqr = jnp.concatenate([jnp.zeros((qr, HEADS, MLA_NOPE), F32), -x2, x1, zq], axis=-1)
    pad_rows = lambda m: jnp.pad(m.reshape(qr, HEADS * LANES), ((0, BW - qr), (0, 0)))
    kvr = w_ukv.shape[0]
    wkv3 = w_ukv.reshape(kvr, HEADS, MLA_NOPE + MLA_V)
    zk = jnp.zeros((kvr, HEADS, LANES - MLA_NOPE), F32)
    wk = jnp.concatenate([wkv3[..., :MLA_NOPE], zk], axis=-1).reshape(kvr, HEADS * LANES)
    wv = jnp.concatenate([wkv3[..., MLA_NOPE:], zk], axis=-1).reshape(kvr, HEADS * LANES)
    j = np.arange(hr)
    p1 = np.zeros((LANES, LANES), np.float32)
    p1[np.arange(MLA_ROPE), MLA_NOPE + np.arange(MLA_ROPE)] = 1.0
    p2 = np.zeros((LANES, LANES), np.float32)
    p2[hr + j, MLA_NOPE + j] = -1.0
    p2[j, MLA_NOPE + hr + j] = 1.0
    one = np.zeros((1, HEADS * LANES), np.float32)
    one[0, MLA_V + LANES * np.arange(HEADS)] = 1.0
    return (pad_rows(wq).astype(BF16), pad_rows(wqr).astype(BF16), wk.astype(BF16), wv.astype(BF16),
            jnp.asarray(p1, BF16), jnp.asarray(p2, BF16), jnp.asarray(one))


def mla_prep(u, cb0, cos, sin, q_norm, kv_norm, weights, tm, rope_blocks):
    n = u.shape[0]
    wq, wqr, wk, wv, p1, p2, one = weights
    qn = jnp.pad(q_norm, (0, BW - q_norm.shape[0])).reshape(1, BW)
    full = lambda a: pl.BlockSpec(a.shape, lambda i: (0,) * a.ndim)
    tab = pl.BlockSpec((tm, LANES), lambda i: (i % rope_blocks, 0))
    kv_cb = (cb0 + CB_MKV) * (BW // LANES)
    out = jax.ShapeDtypeStruct((n, HEADS * LANES), BF16)
    ospec = pl.BlockSpec((tm, HEADS * LANES), lambda i: (i, 0))
    return pl.pallas_call(
        _mla_prep_kernel,
        grid=(n // tm,),
        in_specs=[
            pl.BlockSpec((tm, BW), lambda i: (i, cb0 + CB_MCQ)),
            pl.BlockSpec((tm, LANES), lambda i: (i, kv_cb)),
            pl.BlockSpec((tm, LANES), lambda i: (i, kv_cb + 1)),
            tab, tab, full(qn), pl.BlockSpec((1, LANES), lambda i: (0, 0)),
            full(wq), full(wqr), full(wk), full(wv), full(p1), full(p2), full(one),
        ],
        out_specs=[ospec, ospec, ospec],
        out_shape=[out, out, out],
        compiler_params=_cp(("arbitrary",)),
        name="mla_prep",
    )(u, u, u, cos, sin, qn, kv_norm.reshape(1, LANES), wq, wqr, wk, wv, p1, p2, one)


def _flash_kernel(*refs, lens, tk):
    q_ref = refs[0]
    kv_refs = refs[1:1 + 2 * len(lens)]
    o_ref = refs[1 + 2 * len(lens)]
    q = q_ref[...]
    tq = q.shape[0]
    m = jnp.full((tq, 1), NEG, F32)
    acc = jnp.zeros((tq, LANES), F32)

    def chunk(kc, vc, m, acc):
        s = _dot_nt(q, kc)
        m_new = jnp.maximum(m, jnp.max(s, axis=-1, keepdims=True))
        p = jnp.exp(s - m_new)
        acc = jnp.exp(m - m_new) * acc + _dot(p.astype(BF16), vc)
        return m_new, acc

    for si, length in enumerate(lens):
        k_ref, v_ref = kv_refs[2 * si], kv_refs[2 * si + 1]
        step = min(tk, length)
        if length == step:
            m, acc = chunk(k_ref[...], v_ref[...], m, acc)
        else:
            def body(j, carry, k_ref=k_ref, v_ref=v_ref, step=step):
                j0 = pl.multiple_of(j * step, step)
                return chunk(k_ref[pl.ds(j0, step), :], v_ref[pl.ds(j0, step), :], *carry)

            m, acc = lax.fori_loop(0, length // step, body, (m, acc))

    lane = lax.broadcasted_iota(jnp.int32, (1, LANES), 1)
    l = jnp.sum(jnp.where(lane == MLA_V, acc, 0.0), axis=-1, keepdims=True)
    o_ref[...] = jnp.where(lane < MLA_V, acc / l, 0.0).astype(o_ref.dtype)


def flash_attention(q, kvs, batch, tq, tk):
    n = q.shape[0]
    nq = n // batch // tq
    lens = tuple(k.shape[0] // batch for k, _ in kvs)
    in_specs = [pl.BlockSpec((tq, LANES), lambda b, h, i: (b * nq + i, h))]
    args = [q]
    for (k, v), length in zip(kvs, lens):
        in_specs += [pl.BlockSpec((length, LANES), lambda b, h, i: (b, h))] * 2
        args += [k, v]
    return pl.pallas_call(
        functools.partial(_flash_kernel, lens=lens, tk=tk),
        grid=(batch, HEADS, nq),
        in_specs=in_specs,
        out_specs=pl.BlockSpec((tq, LANES), lambda b, h, i: (b * nq + i, h)),
        out_shape=jax.ShapeDtypeStruct((n, HEADS * LANES), BF16),
        compiler_params=_cp(("arbitrary", "arbitrary", "arbitrary")),
        name="mla_attention",
    )(*args)


def _merge_kernel(*refs, with_router):
    (gate_ref, yr_f_ref, yr_b_ref, yf_ref, yn_ref, ym_ref, x_ref, g1_ref, wb_ret_ref, wb_f_ref,
     wb_na_ref, wb_mla_ref, wo_ref, gain_ref, sc_ref, sh_ref) = refs[:16]
    rest = refs[16:]
    if with_router:
        rh_ref, rl_ref, x_out_ref, h_out_ref, lg_out_ref = rest
    else:
        x_out_ref, h_out_ref = rest
    d = x_ref.shape[1]

    def gated(k, y):
        g = jax.nn.sigmoid(gate_ref[:, k * d:(k + 1) * d].astype(F32))
        return g * y

    m = gated(0, _dot(yr_f_ref[0] + yr_b_ref[0], wb_ret_ref[...]))
    m = m + gated(1, _dot(yf_ref[...], wb_f_ref[...]))
    m = m + gated(2, _dot(yn_ref[...], wb_na_ref[...]))
    m = m + gated(3, _dot(ym_ref[...], wb_mla_ref[...]))
    y = _dot(m.astype(BF16), wo_ref[...])
    x = x_ref[...] + g1_ref[0] * y
    x_out_ref[...] = x
    hn = x * lax.rsqrt(jnp.mean(x * x, axis=-1, keepdims=True) + NORM_EPS)
    h = (hn * gain_ref[...]) * (1.0 + sc_ref[0]) + sh_ref[0]
    h_out_ref[...] = h.astype(h_out_ref.dtype)
    if with_router:
        h_hi = h.astype(BF16)
        h_lo = (h - h_hi.astype(F32)).astype(BF16)
        lg_out_ref[...] = (_dot(h_hi, rh_ref[...]) + _dot(h_lo, rh_ref[...])) + _dot(h_hi, rl_ref[...])


def merge(u, y_ret, y_four, y_na, y_mla, x, mods, mod_row, gain, wb, w_out, router, tm):
    n, d = x.shape
    wb_ret, wb_f, wb_na, wb_mla = wb
    full = lambda a: pl.BlockSpec(a.shape, lambda i: (0,) * a.ndim)
    br = lambda: pl.BlockSpec((tm, BW), lambda i: (i, 0))
    tok = lambda: pl.BlockSpec((tm, d), lambda i: (i, 0))
    in_specs = [
        pl.BlockSpec((tm, 4 * d), lambda i: (i, 0)),
        pl.BlockSpec((1, tm, BW), lambda i: (0, i, 0)),
        pl.BlockSpec((1, tm, BW), lambda i: (1, i, 0)),
        br(), br(),
        pl.BlockSpec((tm, HEADS * LANES), lambda i: (i, 0)),
        tok(), _mod_spec(d, mod_row, tm, 2),
        full(wb_ret), full(wb_f), full(wb_na), full(wb_mla), full(w_out),
        pl.BlockSpec((1, d), lambda i: (0, 0)), _mod_spec(d, mod_row, tm, 4), _mod_spec(d, mod_row, tm, 3),
    ]
    args = [u, y_ret, y_ret, y_four, y_na, y_mla, x, mods, wb_ret, wb_f, wb_na, wb_mla, w_out,
            gain.reshape(1, d), mods, mods]
    out_specs = [tok(), tok()]
    out_shape = [jax.ShapeDtypeStruct((n, d), F32), jax.ShapeDtypeStruct((n, d), BF16)]
    if router is not None:
        in_specs += [full(router[0]), full(router[1])]
        args += list(router)
        out_specs.append(pl.BlockSpec((tm, LANES), lambda i: (i, 0)))
        out_shape.append(jax.ShapeDtypeStruct((n, LANES), F32))
    return pl.pallas_call(
        functools.partial(_merge_kernel, with_router=router is not None),
        grid=(n // tm,),
        in_specs=in_specs,
        out_specs=out_specs,
        out_shape=out_shape,
        compiler_params=_cp(("arbitrary",)),
        name="merge",
    )(*args)


def _ffn_kernel(h_ref, wg_ref, wu_ref, wd_ref, x_ref, g2_ref, o_ref, acc_ref):
    f = pl.program_id(1)

    @pl.when(f == 0)
    def _():
        acc_ref[...] = jnp.zeros_like(acc_ref)

    h = h_ref[...]
    a = _silu(_dot(h, wg_ref[...])) * _dot(h, wu_ref[...])
    acc_ref[...] += _dot(a.astype(BF16), wd_ref[...])

    @pl.when(f == pl.num_programs(1) - 1)
    def _():
        o_ref[...] = x_ref[...] + g2_ref[0] * acc_ref[...]


def ffn(h, x, mods, mod_row, wg, wu, wd, tm, tf):
    n, d = x.shape
    nf = wg.shape[1] // tf
    return pl.pallas_call(
        _ffn_kernel,
        grid=(n // tm, nf),
        in_specs=[
            pl.BlockSpec((tm, d), lambda i, f: (i, 0)),
            pl.BlockSpec((d, tf), lambda i, f: (0, f)),
            pl.BlockSpec((d, tf), lambda i, f: (0, f)),
            pl.BlockSpec((tf, d), lambda i, f: (f, 0)),
            pl.BlockSpec((tm, d), lambda i, f: (i, 0)),
            _mod_spec(d, mod_row, tm, 5),
        ],
        out_specs=pl.BlockSpec((tm, d), lambda i, f: (i, 0)),
        out_shape=jax.ShapeDtypeStruct((n, d), F32),
        scratch_shapes=[pltpu.VMEM((tm, d), F32)],
        compiler_params=_cp(("arbitrary", "arbitrary")),
        name="ffn",
    )(h, wg, wu, wd, x, mods)


def _moe_kernel(be_ref, nu_ref, x_ref, wg_ref, wu_ref, wd_ref, o_ref, acc_ref):
    i = pl.program_id(0)
    f = pl.program_id(1)
    used = i < nu_ref[0]

    @pl.when(f == 0)
    def _():
        acc_ref[...] = jnp.zeros_like(acc_ref)

    @pl.when(used)
    def _():
        x = x_ref[...]
        a = _silu(_dot(x, wg_ref[0])) * _dot(x, wu_ref[0])
        acc_ref[...] += _dot(a.astype(BF16), wd_ref[0])

    @pl.when(f == pl.num_programs(1) - 1)
    def _():
        o_ref[...] = acc_ref[...]


def moe_ffn(blk_exp, n_used, xb, wg, wu, wd, tm, tf):
    n, d = xb.shape
    nf = wg.shape[2] // tf
    grid_spec = pltpu.PrefetchScalarGridSpec(
        num_scalar_prefetch=2,
        grid=(n // tm, nf),
        in_specs=[
            pl.BlockSpec((tm, d), lambda i, f, be, nu: (i, 0)),
            pl.BlockSpec((1, d, tf), lambda i, f, be, nu: (be[i], 0, f)),
            pl.BlockSpec((1, d, tf), lambda i, f, be, nu: (be[i], 0, f)),
            pl.BlockSpec((1, tf, d), lambda i, f, be, nu: (be[i], f, 0)),
        ],
        out_specs=pl.BlockSpec((tm, d), lambda i, f, be, nu: (i, 0)),
        scratch_shapes=[pltpu.VMEM((tm, d), F32)],
    )
    return pl.pallas_call(
        _moe_kernel,
        grid_spec=grid_spec,
        out_shape=jax.ShapeDtypeStruct((n, d), F32),
        compiler_params=_cp(("arbitrary", "arbitrary")),
        name="moe_ffn",
    )(blk_exp, n_used, xb, wg, wu, wd)


def _combine_kernel(x_ref, ya_ref, yb_ref, w_ref, g2_ref, gain_ref, o_ref, *, final):
    w = w_ref[...]
    y = w[:, 0:1] * ya_ref[...] + w[:, 1:2] * yb_ref[...]
    x = x_ref[...] + g2_ref[0] * y
    if final:
        x = (x * lax.rsqrt(jnp.mean(x * x, axis=-1, keepdims=True) + NORM_EPS)) * gain_ref[...]
    o_ref[...] = x


def moe_combine(x, ya, yb, w, mods, mod_row, gain, final, tm):
    n, d = x.shape
    tok = lambda: pl.BlockSpec((tm, d), lambda i: (i, 0))
    return pl.pallas_call(
        functools.partial(_combine_kernel, final=final),
        grid=(n // tm,),
        in_specs=[tok(), tok(), tok(), pl.BlockSpec((tm, MOE_TOP_K), lambda i: (i, 0)),
                  _mod_spec(d, mod_row, tm, 5), pl.BlockSpec((1, d), lambda i: (0, 0))],
        out_specs=tok(),
        out_shape=jax.ShapeDtypeStruct((n, d), F32),
        compiler_params=_cp(("arbitrary",)),
        name="moe_combine",
    )(x, ya, yb, w, mods, gain.reshape(1, d))


def _rmsnorm_kernel(x_ref, gain_ref, o_ref):
    x = x_ref[...]
    o_ref[...] = (x * lax.rsqrt(jnp.mean(x * x, axis=-1, keepdims=True) + NORM_EPS)) * gain_ref[...]


def rmsnorm_rows(x, gain, tm):
    n, d = x.shape
    return pl.pallas_call(
        _rmsnorm_kernel,
        grid=(n // tm,),
        in_specs=[pl.BlockSpec((tm, d), lambda i: (i, 0)), pl.BlockSpec((1, d), lambda i: (0, 0))],
        out_specs=pl.BlockSpec((tm, d), lambda i: (i, 0)),
        out_shape=jax.ShapeDtypeStruct((n, d), F32),
        compiler_params=_cp(("arbitrary",)),
        name="final_norm",
    )(x, gain.reshape(1, d))


def moe_route(logits, n_experts, tm):
    n_tok = logits.shape[0]
    top_logit, top_idx = lax.top_k(logits[:, :n_experts], MOE_TOP_K)
    top_w = jax.nn.softmax(top_logit, axis=-1)
    e_flat = top_idx.reshape(-1).astype(jnp.int32)
    n_assign = e_flat.shape[0]
    onehot = (e_flat[:, None] == jnp.arange(n_experts, dtype=jnp.int32)[None, :]).astype(jnp.int32)
    rank = jnp.sum((jnp.cumsum(onehot, axis=0) - onehot) * onehot, axis=1)
    counts = jnp.sum(onehot, axis=0)
    padded = (counts + tm - 1) // tm * tm
    pad_end = jnp.cumsum(padded)
    pad_start = pad_end - padded
    dest = pad_start[e_flat] + rank
    n_rows = n_assign + n_experts * tm
    tok = jnp.arange(n_assign, dtype=jnp.int32) // MOE_TOP_K
    row_tok = jnp.zeros((n_rows,), jnp.int32).at[dest].set(tok)
    blk_start = jnp.arange(n_rows // tm, dtype=jnp.int32) * tm
    blk_exp = jnp.minimum(jnp.sum(pad_end[None, :] <= blk_start[:, None], axis=1), n_experts - 1)
    n_used = (pad_end[-1] // tm).reshape(1)
    return row_tok, dest.reshape(n_tok, MOE_TOP_K), top_w, blk_exp.astype(jnp.int32), n_used.astype(jnp.int32)


def _rope_split(wcols):
    d, w = wcols.shape
    half = w // HEADS // 2
    return wcols.reshape(d, HEADS, 2, half).transpose(0, 2, 1, 3).reshape(d, w)


def _inproj_weights(w_in):
    d = w_in.shape[0]
    kv = (BW, BW, BW, BW, MLA_KV_RANK, MLA_ROPE)
    qs = (BW, BW, BW, BW, BW, MLA_Q_RANK, 4 * d)
    offs = np.concatenate([[0], np.cumsum(kv + qs)])
    seg = lambda i: w_in[:, offs[i]:offs[i + 1]]
    r_k, r_v, n_k, n_v, m_ckv, m_kr = (seg(i) for i in range(6))
    r_q, r_gf, r_gb, f_in, n_q, m_cq, gate = (seg(6 + i) for i in range(7))
    z = lambda n: jnp.zeros((d, n), w_in.dtype)
    cols = [gate, _rope_split(r_q), _rope_split(r_k), r_v, r_gf, r_gb, f_in, n_q, n_k, n_v,
            m_cq, z(BW - MLA_Q_RANK), m_ckv, m_kr, z(LANES - MLA_ROPE)]
    return jnp.concatenate(cols, axis=1).astype(BF16)


def _ret_rope_tables(n):
    t = jnp.arange(n)
    row = (t // GRID_W).astype(F32)
    col = (t % GRID_W).astype(F32)
    nf = RET_DK // 4
    inv = ROPE_BASE ** (-jnp.arange(nf, dtype=F32) / nf)
    ang = jnp.concatenate([row[:, None] * inv, col[:, None] * inv], axis=-1)
    return jnp.tile(jnp.cos(ang), (1, HEADS)), jnp.tile(jnp.sin(ang), (1, HEADS))


def _mla_rope_tables(n):
    t = jnp.arange(n)
    row = (t // GRID_W).astype(F32)
    col = (t % GRID_W).astype(F32)
    nf = MLA_ROPE // 4
    inv = ROPE_BASE ** (-jnp.arange(nf, dtype=F32) / nf)
    ang = jnp.concatenate([row[:, None] * inv, col[:, None] * inv], axis=-1)
    pad = jnp.zeros((n, LANES - MLA_NOPE - MLA_ROPE), F32)
    cos = jnp.concatenate([jnp.ones((n, MLA_NOPE), F32), jnp.cos(ang), jnp.cos(ang), pad], axis=-1)
    sin = jnp.concatenate([jnp.zeros((n, MLA_NOPE), F32), jnp.sin(ang), jnp.sin(ang), pad], axis=-1)
    return cos, sin


def _tile_rows(*sizes):
    for tm in (1024, 512, 256, 128):
        if all(s % tm == 0 for s in sizes):
            return tm
    raise ValueError(f"token counts {sizes} need a common multiple-of-128 row tile")


def kernel(x, c, ctx, c_ctx, ada_w, ada_b, norm_mix, norm_ffn, w_in, ret_decay_fwd, ret_decay_bwd,
           mla_q_norm, mla_kv_norm, mla_w_uq, mla_w_ukv, na_rpb, w_branch, w_out,
           ffn_w_gate, ffn_w_up, ffn_w_down, moe_router, moe_w_gate, moe_w_up, moe_w_down, norm_final):
    batch, t, d = x.shape
    tc = ctx.shape[1]
    depth = ada_w.shape[0]
    nl, ncx = batch * t, batch * tc
    assert batch < 8 and t % (16 * GRID_W) == 0 and tc % LANES == 0 and d == 4 * BW
    tm = _tile_rows(t, ncx)
    cb0 = 4 * d // BW

    xl = x.reshape(nl, d)
    xc = ctx.reshape(ncx, d)
    cc = jnp.zeros((8, d), F32).at[:batch].set(c).at[batch].set(c_ctx)
    mods = adaln(cc, ada_w, ada_b).reshape(depth, 8 * 6, 1, d)
    lat_row = lambda r0: r0 // t
    ctx_row = lambda r0: batch

    ret_cos, ret_sin = _ret_rope_tables(t)
    ret_cos_c, ret_sin_c = jnp.ones((tc, LANES), F32), jnp.zeros((tc, LANES), F32)
    mla_cos, mla_sin = _mla_rope_tables(t)
    mla_cos_c = jnp.concatenate([jnp.ones((tm, MLA_NOPE + MLA_ROPE), F32),
                                 jnp.zeros((tm, LANES - MLA_NOPE - MLA_ROPE), F32)], axis=-1)
    mla_sin_c = jnp.zeros((tm, LANES), F32)
    ret_cs = 256

    for i in range(depth):
        ctx_out = i < depth - 1
        md = mods[i]
        w_p = _inproj_weights(w_in[i])
        u = norm_inproj(xl, norm_mix[i], md, lat_row, w_p, tm, 768)
        uc = norm_inproj(xc, norm_mix[i], md, ctx_row, w_p, tm, 768)

        lg = jnp.stack([jax.nn.log_sigmoid(ret_decay_fwd[i].astype(F32)),
                        jax.nn.log_sigmoid(ret_decay_bwd[i].astype(F32))])
        zero_state = jnp.zeros((batch, 2, BW, BW), F32)
        yc_ret, s_ctx = retention(uc, cb0, lg, ret_cos_c, ret_sin_c, zero_state, batch, min(ret_cs, tc))
        y_ret, _ = retention(u, cb0, lg, ret_cos, ret_sin, s_ctx, batch, ret_cs)

        y_four = fourier_long(u, cb0 + CB_F, batch)

        bias_tab = _na_bias_table(na_rpb[i])
        y_na = na_attention(u, uc, cb0, bias_tab, batch, 16)

        mw = _mla_weights(mla_w_uq[i], mla_w_ukv[i])
        q_l, k_l, v_l = mla_prep(u, cb0, mla_cos, mla_sin, mla_q_norm[i], mla_kv_norm[i], mw, tm, t // tm)
        q_c, k_c, v_c = mla_prep(uc, cb0, mla_cos_c, mla_sin_c, mla_q_norm[i], mla_kv_norm[i], mw, tm, 1)
        y_mla = flash_attention(q_l, [(k_l, v_l), (k_c, v_c)], batch, 512, 512)

        wb = w_branch[i].astype(BF16)
        wb_mla = jnp.concatenate(
            [wb[3].reshape(HEADS, MLA_V, d), jnp.zeros((HEADS, LANES - MLA_V, d), BF16)], axis=1
        ).reshape(HEADS * LANES, d)
        wbs = (wb[0], wb[1], wb[2], wb_mla)
        wo = w_out[i].astype(BF16)
        j = i // 2
        if i % 2 == 0:
            xl, h2 = merge(u, y_ret, y_four, y_na, y_mla, xl, md, lat_row, norm_ffn[i], wbs, wo, None, 512)
            wg, wu, wd = ffn_w_gate[j].astype(BF16), ffn_w_up[j].astype(BF16), ffn_w_down[j].astype(BF16)
            tf = wg.shape[1] // 2
            xl = ffn(h2, xl, md, lat_row, wg, wu, wd, 512, tf)
        else:
            n_exp = moe_router.shape[2]
            r = jnp.pad(moe_router[j], ((0, 0), (0, LANES - n_exp)))
            r_hi = r.astype(BF16)
            r_lo = (r - r_hi.astype(F32)).astype(BF16)
            xl, h2, logits = merge(u, y_ret, y_four, y_na, y_mla, xl, md, lat_row, norm_ffn[i], wbs, wo,
                                   (r_hi, r_lo), 512)
            tmm = 512
            row_tok, dest, top_w, blk_exp, n_used = moe_route(logits, n_exp, tmm)
            xb = jnp.take(h2, row_tok, axis=0)
            yb = moe_ffn(blk_exp, n_used, xb, moe_w_gate[j].astype(BF16), moe_w_up[j].astype(BF16),
                         moe_w_down[j].astype(BF16), tmm, 512)
            ya = jnp.take(yb, dest[:, 0], axis=0)
            yb2 = jnp.take(yb, dest[:, 1], axis=0)
            xl = moe_combine(xl, ya, yb2, top_w, md, lat_row, norm_final, i == depth - 1, tm)

        if ctx_out:
            yc_four = fourier_short(uc, cb0 + CB_F, batch)
            yc_na = na_ctx_attention(uc, cb0, batch)
            yc_mla = flash_attention(q_c, [(k_c, v_c)], batch, tc, 512)
            if i % 2 == 0:
                xc, hc2 = merge(uc, yc_ret, yc_four, yc_na, yc_mla, xc, md, ctx_row, norm_ffn[i], wbs, wo,
                                None, 512)
                xc = ffn(hc2, xc, md, ctx_row, wg, wu, wd, 512, tf)
            else:
                raise NotImplementedError("context tokens through the expert mixer")

    if depth % 2 == 1:
        xl = rmsnorm_rows(xl, norm_final, tm)
    return xl.reshape(batch, t, d)
```

```python
import functools

import numpy as np
import jax
import jax.numpy as jnp
from jax import lax
from jax.experimental import pallas as pl
from jax.experimental.pallas import tpu as pltpu

F32 = jnp.float32
BF16 = jnp.bfloat16

GRID_W = 64
ROPE_BASE = 10000.0
NORM_EPS = 1e-6
HEADS = 4
RET_DK = 64
FOURIER_GROUP_DIM = 64
NA_HEAD_DIM = 64
NA_WIN_R = 8
NA_WIN_C = 16
MLA_NOPE = 64
MLA_ROPE = 32
MLA_V = 64
MLA_Q_RANK = 192
MLA_KV_RANK = 128
MOE_TOP_K = 2
BW = 256

COL_GATE = 0
CB_RQ, CB_RK, CB_RV, CB_RGF, CB_RGB, CB_F, CB_NQ, CB_NK, CB_NV, CB_MCQ, CB_MKV = range(11)
LANES = 128
NEG = -1e30

VMEM_LIMIT = 48 * 1024 * 1024


def _cp(sem, vmem=VMEM_LIMIT):
    return pltpu.CompilerParams(dimension_semantics=sem, vmem_limit_bytes=vmem)


def _dot(a, b):
    return jnp.dot(a, b, preferred_element_type=F32)


def _dot_nt(a, b):
    return lax.dot_general(a, b, (((1,), (1,)), ((), ())), preferred_element_type=F32)


def _dot_tn(a, b):
    return lax.dot_general(a, b, (((0,), (0,)), ((), ())), preferred_element_type=F32)


def _silu(x):
    return x * jax.nn.sigmoid(x)


def _adaln_kernel(c_ref, w_ref, b_ref, o_ref):
    s = _silu(c_ref[...])
    o_ref[0] = _dot(s.astype(BF16), w_ref[0].astype(BF16)) + b_ref[0]


def adaln(cc, ada_w, ada_b):
    depth, d, n6 = ada_w.shape
    tn = n6 // 4
    return pl.pallas_call(
        _adaln_kernel,
        grid=(depth, n6 // tn),
        in_specs=[
            pl.BlockSpec((8, d), lambda l, j: (0, 0)),
            pl.BlockSpec((1, d, tn), lambda l, j: (l, 0, j)),
            pl.BlockSpec((1, 1, tn), lambda l, j: (l, 0, j)),
        ],
        out_specs=pl.BlockSpec((1, 8, tn), lambda l, j: (l, 0, j)),
        out_shape=jax.ShapeDtypeStruct((depth, 8, n6), F32),
        compiler_params=_cp(("arbitrary", "arbitrary")),
        name="adaln",
    )(cc, ada_w, ada_b.reshape(depth, 1, n6))


def _inproj_kernel(x_ref, g_ref, sc_ref, sh_ref, w_ref, o_ref, h_ref):
    @pl.when(pl.program_id(1) == 0)
    def _():
        x = x_ref[...]
        y = x * lax.rsqrt(jnp.mean(x * x, axis=-1, keepdims=True) + NORM_EPS)
        h = (y * g_ref[...]) * (1.0 + sc_ref[0]) + sh_ref[0]
        h_ref[...] = h.astype(BF16)

    o_ref[...] = _dot(h_ref[...], w_ref[...]).astype(o_ref.dtype)


def _mod_spec(d, mod_row, tm, k):
    return pl.BlockSpec((1, 1, d), lambda i, *_: (mod_row(i * tm) * 6 + k, 0, 0))


def norm_inproj(x, gain, mods, mod_row, w, tm, tn):
    n, d = x.shape
    nc = w.shape[1]
    return pl.pallas_call(
        _inproj_kernel,
        grid=(n // tm, nc // tn),
        in_specs=[
            pl.BlockSpec((tm, d), lambda i, j: (i, 0)),
            pl.BlockSpec((1, d), lambda i, j: (0, 0)),
            _mod_spec(d, mod_row, tm, 1),
            _mod_spec(d, mod_row, tm, 0),
            pl.BlockSpec((d, tn), lambda i, j: (0, j)),
        ],
        out_specs=pl.BlockSpec((tm, tn), lambda i, j: (i, j)),
        out_shape=jax.ShapeDtypeStruct((n, nc), BF16),
        scratch_shapes=[pltpu.VMEM((tm, d), BF16)],
        compiler_params=_cp(("arbitrary", "arbitrary")),
        name="norm_inproj",
    )(x, gain.reshape(1, d), mods, mods, w)


def _ret_kernel(lg_ref, q_ref, k_ref, v_ref, g_ref, cos_ref, sin_ref, s0_ref, lgq_ref, lgv_ref,
                y_ref, sout_ref, s_ref, *, n_chunks):
    d = pl.program_id(1)
    c = pl.program_id(2)
    rev = d == 1
    cs = q_ref.shape[0]
    w = q_ref.shape[1]
    half = w // 2

    @pl.when(c == 0)
    def _():
        s_ref[...] = s0_ref[0, 0]

    cos = cos_ref[...]
    sin = sin_ref[...]

    def rope(t):
        t1, t2 = t[:, :half], t[:, half:]
        return jnp.concatenate([t1 * cos - t2 * sin, t2 * cos + t1 * sin], axis=-1)

    q = rope(q_ref[...].astype(F32))
    k = rope(k_ref[...].astype(F32)) * (RET_DK ** -0.5)
    vb = v_ref[...]

    pos_i = lax.broadcasted_iota(jnp.int32, (cs, 1), 0).astype(F32)
    pos_j = lax.broadcasted_iota(jnp.int32, (1, cs), 1).astype(F32)
    p_i = jnp.where(rev, cs - 1.0 - pos_i, pos_i)
    p_j = jnp.where(rev, cs - 1.0 - pos_j, pos_j)
    diff = p_i - p_j
    lgq = lgq_ref[0]
    lgv = lgv_ref[0]
    q_w = jnp.exp(lgq * (p_i + 1.0))
    k_w = jnp.exp(lgq * (cs - 1.0 - p_i))

    lane = lax.broadcasted_iota(jnp.int32, (1, w), 1)
    head_q = (lane % half) // (half // HEADS)
    head_v = lane // (w // HEADS)

    s_prev = s_ref[...]
    o = _dot((q * q_w).astype(BF16), s_prev.astype(BF16))
    qb = q.astype(BF16)
    kb = k.astype(BF16)
    zero_b = jnp.zeros_like(qb)
    for h in range(HEADS):
        a = _dot_nt(jnp.where(head_q == h, qb, zero_b), kb)
        decay = jnp.where(diff >= 0, jnp.exp(lg_ref[d, h] * jnp.maximum(diff, 0.0)), 0.0)
        oh = _dot((a * decay).astype(BF16), vb)
        o = o + jnp.where(head_v == h, oh, 0.0)

    ds = _dot_tn((k * k_w).astype(BF16), vb)
    row_head = (lax.broadcasted_iota(jnp.int32, (w, 1), 0) % half) // (half // HEADS)
    s_new = s_prev * jnp.exp(lgv * float(cs)) + jnp.where(row_head == head_v, ds, 0.0)
    s_ref[...] = s_new

    @pl.when(c == n_chunks - 1)
    def _():
        sout_ref[0, 0] = s_new

    hd = w // HEADS
    gi = lax.broadcasted_iota(jnp.int32, (w, w), 0) // hd
    gj = lax.broadcasted_iota(jnp.int32, (w, w), 1) // hd
    avg = jnp.where(gi == gj, 1.0 / hd, 0.0).astype(BF16)
    ms = _dot((o * o).astype(BF16), avg)
    on = o * lax.rsqrt(ms + NORM_EPS)
    y_ref[0] = (_silu(g_ref[...].astype(F32)) * on).astype(y_ref.dtype)


def retention(u, cb0, lg, cos, sin, s0, batch, cs):
    n = u.shape[0]
    t = n // batch
    nch = t // cs
    w = BW
    half = w // 2
    lgq = jnp.tile(jnp.repeat(lg, half // HEADS, axis=1), (1, 2)).reshape(2, 1, w)
    lgv = jnp.repeat(lg, w // HEADS, axis=1).reshape(2, 1, w)

    def rows(b, d, c):
        return b * nch + jnp.where(d == 1, nch - 1 - c, c)

    def col(cb):
        return pl.BlockSpec((cs, w), lambda b, d, c: (rows(b, d, c), cb0 + cb))

    def tab():
        return pl.BlockSpec((cs, half), lambda b, d, c: (jnp.where(d == 1, nch - 1 - c, c), 0))

    y, s_out = pl.pallas_call(
        functools.partial(_ret_kernel, n_chunks=nch),
        grid=(batch, 2, nch),
        in_specs=[
            pl.BlockSpec(memory_space=pltpu.SMEM),
            col(CB_RQ), col(CB_RK), col(CB_RV),
            pl.BlockSpec((cs, w), lambda b, d, c: (rows(b, d, c), cb0 + CB_RGF + d)),
            tab(), tab(),
            pl.BlockSpec((1, 1, w, w), lambda b, d, c: (b, d, 0, 0)),
            pl.BlockSpec((1, 1, w), lambda b, d, c: (d, 0, 0)),
            pl.BlockSpec((1, 1, w), lambda b, d, c: (d, 0, 0)),
        ],
        out_specs=[
            pl.BlockSpec((1, cs, w), lambda b, d, c: (d, rows(b, d, c), 0)),
            pl.BlockSpec((1, 1, w, w), lambda b, d, c: (b, d, 0, 0)),
        ],
        out_shape=[
            jax.ShapeDtypeStruct((2, n, w), BF16),
            jax.ShapeDtypeStruct((batch, 2, w, w), F32),
        ],
        scratch_shapes=[pltpu.VMEM((w, w), F32)],
        compiler_params=_cp(("arbitrary", "arbitrary", "arbitrary")),
        name="retention",
    )(lg, u, u, u, u, cos, sin, s0, lgq, lgv)
    return y, s_out


def _dft_tables(t, t1, t2):
    k1 = jnp.arange(t1, dtype=jnp.int32)
    a = jnp.arange(t1, dtype=jnp.int32)
    m = jnp.arange(t2, dtype=jnp.int32)
    ph1 = (k1[None, :, None] * (a[None, None, :] * t2 + m[:, None, None])) % t
    ang1 = ph1.astype(F32) * (2.0 * np.pi / t)
    ph2 = (m[:, None] * m[None, :]) % t2
    ang2 = ph2.astype(F32) * (2.0 * np.pi / t2)
    return (jnp.cos(ang1).astype(BF16), jnp.sin(ang1).astype(BF16),
            jnp.cos(ang2).astype(BF16), jnp.sin(ang2).astype(BF16))


def _channel_tables(width):
    ch = jnp.arange(width, dtype=jnp.int32)
    same = (ch[:, None] // FOURIER_GROUP_DIM) == (ch[None, :] // FOURIER_GROUP_DIM)
    ph = ((ch[:, None] % FOURIER_GROUP_DIM) * (ch[None, :] % FOURIER_GROUP_DIM)) % FOURIER_GROUP_DIM
    ang = ph.astype(F32) * (2.0 * np.pi / FOURIER_GROUP_DIM)
    return (jnp.where(same, jnp.cos(ang), 0.0).astype(BF16),
            jnp.where(same, jnp.sin(ang), 0.0).astype(BF16))


def _fourier_kernel(x_ref, c1_ref, s1_ref, c2_ref, s2_ref, cc_ref, sc_ref, o_ref, xa, yre, yim,
                    *, t1, t2, norm):
    xa[...] = x_ref[...].astype(F32)

    def stage1(m, carry):
        xs = xa[pl.ds(m, t1, stride=t2), :].astype(BF16)
        r0 = pl.multiple_of(m * t1, t1)
        yre[pl.ds(r0, t1), :] = _dot(c1_ref[m], xs)
        yim[pl.ds(r0, t1), :] = -_dot(s1_ref[m], xs)
        return carry

    lax.fori_loop(0, t2, stage1, 0, unroll=8)

    c2 = c2_ref[...]
    s2 = s2_ref[...]
    cc = cc_ref[...]
    sc = sc_ref[...]

    def stage2(k1, carry):
        yr = yre[pl.ds(k1, t2, stride=t1), :].astype(BF16)
        yi = yim[pl.ds(k1, t2, stride=t1), :].astype(BF16)
        zr = _dot(c2, yr) + _dot(s2, yi)
        zi = _dot(c2, yi) - _dot(s2, yr)
        out = (_dot(zr.astype(BF16), cc) + _dot(zi.astype(BF16), sc)) * norm
        xa[pl.ds(k1, t2, stride=t1), :] = out
        return carry

    lax.fori_loop(0, t1, stage2, 0, unroll=4)
    o_ref[...] = xa[...].astype(o_ref.dtype)


def fourier_long(u, cb, batch, t2=LANES):
    n = u.shape[0]
    t = n // batch
    t1 = t // t2
    c1, s1, c2, s2 = _dft_tables(t, t1, t2)
    cc, sc = _channel_tables(LANES)
    norm = float(1.0 / np.sqrt(t * FOURIER_GROUP_DIM))
    full = lambda shape: pl.BlockSpec(shape, lambda b, hh: (0,) * len(shape))
    return pl.pallas_call(
        functools.partial(_fourier_kernel, t1=t1, t2=t2, norm=norm),
        grid=(batch, BW // LANES),
        in_specs=[
            pl.BlockSpec((t, LANES), lambda b, hh: (b, cb * (BW // LANES) + hh)),
            full((t2, t1, t1)), full((t2, t1, t1)), full((t2, t2)), full((t2, t2)),
            full((LANES, LANES)), full((LANES, LANES)),
        ],
        out_specs=pl.BlockSpec((t, LANES), lambda b, hh: (b, hh)),
        out_shape=jax.ShapeDtypeStruct((n, BW), BF16),
        scratch_shapes=[pltpu.VMEM((t, LANES), F32)] * 3,
        compiler_params=_cp(("arbitrary", "arbitrary")),
        name="fourier",
    )(u, c1, s1, c2, s2, cc, sc)


def _fourier_small_kernel(x_ref, ct_ref, st_ref, cc_ref, sc_ref, o_ref, *, norm):
    x = x_ref[...]
    zr = _dot(ct_ref[...], x)
    zi = -_dot(st_ref[...], x)
    out = (_dot(zr.astype(BF16), cc_ref[...]) + _dot(zi.astype(BF16), sc_ref[...])) * norm
    o_ref[...] = out.astype(o_ref.dtype)


def fourier_short(u, cb, batch):
    n = u.shape[0]
    t = n // batch
    pos = jnp.arange(t, dtype=jnp.int32)
    ang = ((pos[:, None] * pos[None, :]) % t).astype(F32) * (2.0 * np.pi / t)
    ct, st = jnp.cos(ang).astype(BF16), jnp.sin(ang).astype(BF16)
    cc, sc = _channel_tables(BW)
    norm = float(1.0 / np.sqrt(t * FOURIER_GROUP_DIM))
    full = lambda shape: pl.BlockSpec(shape, lambda b: (0,) * len(shape))
    return pl.pallas_call(
        functools.partial(_fourier_small_kernel, norm=norm),
        grid=(batch,),
        in_specs=[pl.BlockSpec((t, BW), lambda b: (b, cb)), full((t, t)), full((t, t)),
                  full((BW, BW)), full((BW, BW))],
        out_specs=pl.BlockSpec((t, BW), lambda b: (b, 0)),
        out_shape=jax.ShapeDtypeStruct((n, BW), BF16),
        compiler_params=_cp(("arbitrary",)),
        name="fourier_ctx",
    )(u, ct, st, cc, sc)


def _na_bias_table(rpb):
    var = jnp.arange(NA_WIN_R)[:, None, None, None]
    kr = jnp.arange(NA_WIN_R)[None, :, None, None]
    qc = jnp.arange(GRID_W)[None, None, :, None]
    kc = jnp.arange(GRID_W)[None, None, None, :]
    ri = jnp.broadcast_to(kr - var + (NA_WIN_R - 1), (NA_WIN_R, NA_WIN_R, GRID_W, GRID_W))
    ci = jnp.broadcast_to(kc - qc + (NA_WIN_C - 1), (NA_WIN_R, NA_WIN_R, GRID_W, GRID_W))
    start = jnp.clip(qc - NA_WIN_C // 2, 0, GRID_W - NA_WIN_C)
    valid = jnp.broadcast_to((kc >= start) & (kc < start + NA_WIN_C), ci.shape)
    tab = rpb.astype(F32)[:, ri, jnp.clip(ci, 0, 2 * NA_WIN_C - 2)]
    tab = jnp.where(valid[None], tab, NEG)
    tab = tab.transpose(1, 0, 3, 2, 4)
    return tab.reshape(NA_WIN_R, HEADS * GRID_W, NA_WIN_R * GRID_W)


def _na_kernel(q_ref, k_ref, v_ref, kc_ref, vc_ref, bias_ref, o_ref, *, rows_per_step, n_rows):
    i = pl.program_id(1)
    w = q_ref.shape[1]
    lane = lax.broadcasted_iota(jnp.int32, (1, w), 1)
    head = lane // (w // HEADS)
    scale = jnp.asarray(NA_HEAD_DIM ** -0.5, q_ref.dtype)
    kc = kc_ref[...]
    vc = vc_ref[...]
    win = NA_WIN_R * GRID_W

    def row(rl, carry):
        r = i * rows_per_step + rl
        rs = jnp.clip(r - NA_WIN_R // 2, 0, n_rows - NA_WIN_R)
        var = r - rs
        q0 = pl.multiple_of(rl * GRID_W, GRID_W)
        k0 = pl.multiple_of(rs * GRID_W, GRID_W)
        q = q_ref[pl.ds(q0, GRID_W), :] * scale
        kw = k_ref[pl.ds(k0, win), :]
        vw = v_ref[pl.ds(k0, win), :]
        zero_b = jnp.zeros_like(q)
        q4 = jnp.concatenate([jnp.where(head == h, q, zero_b) for h in range(HEADS)], axis=0)
        s_loc = _dot_nt(q4, kw) + bias_ref[var]
        s_ctx = _dot_nt(q4, kc)
        m = jnp.maximum(jnp.max(s_loc, axis=-1, keepdims=True), jnp.max(s_ctx, axis=-1, keepdims=True))
        p_loc = jnp.exp(s_loc - m)
        p_ctx = jnp.exp(s_ctx - m)
        l = jnp.sum(p_loc, axis=-1, keepdims=True) + jnp.sum(p_ctx, axis=-1, keepdims=True)
        pv = (_dot(p_loc.astype(BF16), vw) + _dot(p_ctx.astype(BF16), vc)) / l
        acc = jnp.zeros((GRID_W, w), F32)
        for h in range(HEADS):
            acc = acc + jnp.where(head == h, pv[h * GRID_W:(h + 1) * GRID_W], 0.0)
        o_ref[pl.ds(q0, GRID_W), :] = acc.astype(o_ref.dtype)
        return carry

    lax.fori_loop(0, rows_per_step, row, 0, unroll=2)


def na_attention(u, uc, cb0, bias_tab, batch, rows_per_step):
    n = u.shape[0]
    t = n // batch
    tc = uc.shape[0] // batch
    n_rows = t // GRID_W
    steps = n_rows // rows_per_step
    tq = rows_per_step * GRID_W
    return pl.pallas_call(
        functools.partial(_na_kernel, rows_per_step=rows_per_step, n_rows=n_rows),
        grid=(batch, steps),
        in_specs=[
            pl.BlockSpec((tq, BW), lambda b, i: (b * steps + i, cb0 + CB_NQ)),
            pl.BlockSpec((t, BW), lambda b, i: (b, cb0 + CB_NK)),
            pl.BlockSpec((t, BW), lambda b, i: (b, cb0 + CB_NV)),
            pl.BlockSpec((tc, BW), lambda b, i: (b, cb0 + CB_NK)),
            pl.BlockSpec((tc, BW), lambda b, i: (b, cb0 + CB_NV)),
            pl.BlockSpec(bias_tab.shape, lambda b, i: (0, 0, 0)),
        ],
        out_specs=pl.BlockSpec((tq, BW), lambda b, i: (b * steps + i, 0)),
        out_shape=jax.ShapeDtypeStruct((n, BW), BF16),
        compiler_params=_cp(("arbitrary", "arbitrary")),
        name="na_attention",
    )(u, u, u, uc, uc, bias_tab)


def _na_ctx_kernel(q_ref, k_ref, v_ref, o_ref):
    w = q_ref.shape[1]
    lane = lax.broadcasted_iota(jnp.int32, (1, w), 1)
    head = lane // (w // HEADS)
    scale = NA_HEAD_DIM ** -0.5
    q = q_ref[...]
    k = k_ref[...]
    v = v_ref[...]
    zero_b = jnp.zeros_like(q)
    acc = jnp.zeros(q.shape, F32)
    for h in range(HEADS):
        s = _dot_nt(jnp.where(head == h, q, zero_b), k) * scale
        p = jnp.exp(s - jnp.max(s, axis=-1, keepdims=True))
        l = jnp.sum(p, axis=-1, keepdims=True)
        acc = acc + jnp.where(head == h, _dot(p.astype(BF16), v) / l, 0.0)
    o_ref[...] = acc.astype(o_ref.dtype)


def na_ctx_attention(uc, cb0, batch):
    tc = uc.shape[0] // batch
    spec = lambda cb: pl.BlockSpec((tc, BW), lambda b: (b, cb0 + cb))
    return pl.pallas_call(
        _na_ctx_kernel,
        grid=(batch,),
        in_specs=[spec(CB_NQ), spec(CB_NK), spec(CB_NV)],
        out_specs=pl.BlockSpec((tc, BW), lambda b: (b, 0)),
        out_shape=jax.ShapeDtypeStruct((uc.shape[0], BW), BF16),
        compiler_params=_cp(("arbitrary",)),
        name="na_ctx_attention",
    )(uc, uc, uc)


def _mla_prep_kernel(cq_ref, ckv_ref, kr_ref, cos_ref, sin_ref, qn_ref, kvn_ref, wq_ref, wqr_ref,
                     wk_ref, wv_ref, p1_ref, p2_ref, one_ref, q_ref, k_ref, v_ref):
    cos = cos_ref[...]
    sin = sin_ref[...]
    cos4 = jnp.concatenate([cos] * HEADS, axis=-1)
    sin4 = jnp.concatenate([sin] * HEADS, axis=-1)

    cq = cq_ref[...].astype(F32)
    ms = jnp.sum(cq * cq, axis=-1, keepdims=True) * (1.0 / MLA_Q_RANK)
    cqn = ((cq * lax.rsqrt(ms + NORM_EPS)) * qn_ref[...]).astype(BF16)
    q = _dot(cqn, wq_ref[...]) * cos4 + _dot(cqn, wqr_ref[...]) * sin4
    q_ref[...] = (q * float((MLA_NOPE + MLA_ROPE) ** -0.5 * np.log2(np.e))).astype(q_ref.dtype)

    ckv = ckv_ref[...].astype(F32)
    ms = jnp.mean(ckv * ckv, axis=-1, keepdims=True)
    ckvn = ((ckv * lax.rsqrt(ms + NORM_EPS)) * kvn_ref[...]).astype(BF16)
    kr = kr_ref[...]
    k_rot = _dot(kr, p1_ref[...]) * cos + _dot(kr, p2_ref[...]) * sin
    k = _dot(ckvn, wk_ref[...]) + jnp.concatenate([k_rot] * HEADS, axis=-1)
    k_ref[...] = k.astype(k_ref.dtype)
    v_ref[...] = (_dot(ckvn, wv_ref[...]) + one_ref[...]).astype(v_ref.dtype)


def _mla_weights(w_uq, w_ukv):
    qr = w_uq.shape[0]
    dq = MLA_NOPE + MLA_ROPE
    hr = MLA_ROPE // 2
    wq3 = w_uq.reshape(qr, HEADS, dq)
    zq = jnp.zeros((qr, HEADS, LANES - dq), F32)
    wq = jnp.concatenate([wq3, zq], axis=-1)
    x1 = wq3[..., MLA_NOPE:MLA_NOPE + hr]
    x2 = wq3[..., MLA_NOPE + hr:]
    wqr = jnp.concatenate([jnp.zeros((qr, HEADS, MLA_NOPE), F32), -x2, x1, zq], axis=-1)
    pad_rows = lambda m: jnp.pad(m.reshape(qr, HEADS * LANES), ((0, BW - qr), (0, 0)))
    kvr = w_ukv.shape[0]
    wkv3 = w_ukv.reshape(kvr, HEADS, MLA_NOPE + MLA_V)
    zk = jnp.zeros((kvr, HEADS, LANES - MLA_NOPE), F32)
    wk = jnp.concatenate([wkv3[..., :MLA_NOPE], zk], axis=-1).reshape(kvr, HEADS * LANES)
    wv = jnp.concatenate([wkv3[..., MLA_NOPE:], zk], axis=-1).reshape(kvr, HEADS * LANES)
    j = np.arange(hr)
    p1 = np.zeros((LANES, LANES), np.float32)
    p1[np.arange(MLA_ROPE), MLA_NOPE + np.arange(MLA_ROPE)] = 1.0
    p2 = np.zeros((LANES, LANES), np.float32)
    p2[hr + j, MLA_NOPE + j] = -1.0
    p2[j, MLA_NOPE + hr + j] = 1.0
    one = np.zeros((1, HEADS * LANES), np.float32)
    one[0, MLA_V + LANES * np.arange(HEADS)] = 1.0
    return (pad_rows(wq).astype(BF16), pad_rows(wqr).astype(BF16), wk.astype(BF16), wv.astype(BF16),
            jnp.asarray(p1, BF16), jnp.asarray(p2, BF16), jnp.asarray(one))


def mla_prep(u, cb0, cos, sin, q_norm, kv_norm, weights, tm, rope_blocks):
    n = u.shape[0]
    wq, wqr, wk, wv, p1, p2, one = weights
    qn = jnp.pad(q_norm, (0, BW - q_norm.shape[0])).reshape(1, BW)
    full = lambda a: pl.BlockSpec(a.shape, lambda i: (0,) * a.ndim)
    tab = pl.BlockSpec((tm, LANES), lambda i: (i % rope_blocks, 0))
    kv_cb = (cb0 + CB_MKV) * (BW // LANES)
    out = jax.ShapeDtypeStruct((n, HEADS * LANES), BF16)
    ospec = pl.BlockSpec((tm, HEADS * LANES), lambda i: (i, 0))
    return pl.pallas_call(
        _mla_prep_kernel,
        grid=(n // tm,),
        in_specs=[
            pl.BlockSpec((tm, BW), lambda i: (i, cb0 + CB_MCQ)),
            pl.BlockSpec((tm, LANES), lambda i: (i, kv_cb)),
            pl.BlockSpec((tm, LANES), lambda i: (i, kv_cb + 1)),
            tab, tab, full(qn), pl.BlockSpec((1, LANES), lambda i: (0, 0)),
            full(wq), full(wqr), full(wk), full(wv), full(p1), full(p2), full(one),
        ],
        out_specs=[ospec, ospec, ospec],
        out_shape=[out, out, out],
        compiler_params=_cp(("arbitrary",)),
        name="mla_prep",
    )(u, u, u, cos, sin, qn, kv_norm.reshape(1, LANES), wq, wqr, wk, wv, p1, p2, one)


def _flash_kernel(*refs, lens, tk):
    q_ref = refs[0]
    kv_refs = refs[1:1 + 2 * len(lens)]
    o_ref = refs[1 + 2 * len(lens)]
    q = q_ref[...]
    tq = q.shape[0]
    m = jnp.full((tq, 1), NEG, F32)
    acc = jnp.zeros((tq, LANES), F32)

    def chunk(kc, vc, m, acc):
        s = _dot_nt(q, kc)
        m_new = jnp.maximum(m, jnp.max(s, axis=-1, keepdims=True))
        p = jnp.exp2((s - m_new).astype(BF16))
        acc = jnp.exp2(m - m_new) * acc + _dot(p, vc)
        return m_new, acc

    for si, length in enumerate(lens):
        k_ref, v_ref = kv_refs[2 * si], kv_refs[2 * si + 1]
        step = min(tk, length)
        if length == step:
            m, acc = chunk(k_ref[...], v_ref[...], m, acc)
        else:
            def body(j, carry, k_ref=k_ref, v_ref=v_ref, step=step):
                j0 = pl.multiple_of(j * step, step)
                return chunk(k_ref[pl.ds(j0, step), :], v_ref[pl.ds(j0, step), :], *carry)

            m, acc = lax.fori_loop(0, length // step, body, (m, acc), unroll=2)

    lane = lax.broadcasted_iota(jnp.int32, (1, LANES), 1)
    l = jnp.sum(jnp.where(lane == MLA_V, acc, 0.0), axis=-1, keepdims=True)
    o_ref[...] = jnp.where(lane < MLA_V, acc / l, 0.0).astype(o_ref.dtype)


def flash_attention(q, kvs, batch, tq, tk):
    n = q.shape[0]
    nq = n // batch // tq
    lens = tuple(k.shape[0] // batch for k, _ in kvs)
    in_specs = [pl.BlockSpec((tq, LANES), lambda b, h, i: (b * nq + i, h))]
    args = [q]
    for (k, v), length in zip(kvs, lens):
        in_specs += [pl.BlockSpec((length, LANES), lambda b, h, i: (b, h))] * 2
        args += [k, v]
    return pl.pallas_call(
        functools.partial(_flash_kernel, lens=lens, tk=tk),
        grid=(batch, HEADS, nq),
        in_specs=in_specs,
        out_specs=pl.BlockSpec((tq, LANES), lambda b, h, i: (b * nq + i, h)),
        out_shape=jax.ShapeDtypeStruct((n, HEADS * LANES), BF16),
        compiler_params=_cp(("arbitrary", "arbitrary", "arbitrary")),
        name="mla_attention",
    )(*args)


def _merge_kernel(*refs, with_router):
    (gate_ref, yr_f_ref, yr_b_ref, yf_ref, yn_ref, ym_ref, x_ref, g1_ref, wb_ret_ref, wb_f_ref,
     wb_na_ref, wb_mla_ref, wo_ref, gain_ref, sc_ref, sh_ref) = refs[:16]
    rest = refs[16:]
    if with_router:
        rh_ref, rl_ref, x_out_ref, h_out_ref, lg_out_ref = rest
    else:
        x_out_ref, h_out_ref = rest
    d = x_ref.shape[1]

    def gated(k, y):
        g = jax.nn.sigmoid(gate_ref[:, k * d:(k + 1) * d].astype(F32))
        return g * y

    m = gated(0, _dot(yr_f_ref[0] + yr_b_ref[0], wb_ret_ref[...]))
    m = m + gated(1, _dot(yf_ref[...], wb_f_ref[...]))
    m = m + gated(2, _dot(yn_ref[...], wb_na_ref[...]))
    m = m + gated(3, _dot(ym_ref[...], wb_mla_ref[...]))
    y = _dot(m.astype(BF16), wo_ref[...])
    x = x_ref[...] + g1_ref[0] * y
    x_out_ref[...] = x
    hn = x * lax.rsqrt(jnp.mean(x * x, axis=-1, keepdims=True) + NORM_EPS)
    h = (hn * gain_ref[...]) * (1.0 + sc_ref[0]) + sh_ref[0]
    h_out_ref[...] = h.astype(h_out_ref.dtype)
    if with_router:
        h_hi = h.astype(BF16)
        h_lo = (h - h_hi.astype(F32)).astype(BF16)
        lg_out_ref[...] = (_dot(h_hi, rh_ref[...]) + _dot(h_lo, rh_ref[...])) + _dot(h_hi, rl_ref[...])


def merge(u, y_ret, y_four, y_na, y_mla, x, mods, mod_row, gain, wb, w_out, router, tm):
    n, d = x.shape
    wb_ret, wb_f, wb_na, wb_mla = wb
    full = lambda a: pl.BlockSpec(a.shape, lambda i: (0,) * a.ndim)
    br = lambda: pl.BlockSpec((tm, BW), lambda i: (i, 0))
    tok = lambda: pl.BlockSpec((tm, d), lambda i: (i, 0))
    in_specs = [
        pl.BlockSpec((tm, 4 * d), lambda i: (i, 0)),
        pl.BlockSpec((1, tm, BW), lambda i: (0, i, 0)),
        pl.BlockSpec((1, tm, BW), lambda i: (1, i, 0)),
        br(), br(),
        pl.BlockSpec((tm, HEADS * LANES), lambda i: (i, 0)),
        tok(), _mod_spec(d, mod_row, tm, 2),
        full(wb_ret), full(wb_f), full(wb_na), full(wb_mla), full(w_out),
        pl.BlockSpec((1, d), lambda i: (0, 0)), _mod_spec(d, mod_row, tm, 4), _mod_spec(d, mod_row, tm, 3),
    ]
    args = [u, y_ret, y_ret, y_four, y_na, y_mla, x, mods, wb_ret, wb_f, wb_na, wb_mla, w_out,
            gain.reshape(1, d), mods, mods]
    out_specs = [tok(), tok()]
    out_shape = [jax.ShapeDtypeStruct((n, d), F32), jax.ShapeDtypeStruct((n, d), BF16)]
    if router is not None:
        in_specs += [full(router[0]), full(router[1])]
        args += list(router)
        out_specs.append(pl.BlockSpec((tm, LANES), lambda i: (i, 0)))
        out_shape.append(jax.ShapeDtypeStruct((n, LANES), F32))
    return pl.pallas_call(
        functools.partial(_merge_kernel, with_router=router is not None),
        grid=(n // tm,),
        in_specs=in_specs,
        out_specs=out_specs,
        out_shape=out_shape,
        compiler_params=_cp(("arbitrary",)),
        name="merge",
    )(*args)


def _ffn_kernel(h_ref, wg_ref, wu_ref, wd_ref, x_ref, g2_ref, o_ref, acc_ref):
    f = pl.program_id(1)

    @pl.when(f == 0)
    def _():
        acc_ref[...] = jnp.zeros_like(acc_ref)

    h = h_ref[...]
    a = _silu(_dot(h, wg_ref[...])) * _dot(h, wu_ref[...])
    acc_ref[...] += _dot(a.astype(BF16), wd_ref[...])

    @pl.when(f == pl.num_programs(1) - 1)
    def _():
        o_ref[...] = x_ref[...] + g2_ref[0] * acc_ref[...]


def ffn(h, x, mods, mod_row, wg, wu, wd, tm, tf):
    n, d = x.shape
    nf = wg.shape[1] // tf
    return pl.pallas_call(
        _ffn_kernel,
        grid=(n // tm, nf),
        in_specs=[
            pl.BlockSpec((tm, d), lambda i, f: (i, 0)),
            pl.BlockSpec((d, tf), lambda i, f: (0, f)),
            pl.BlockSpec((d, tf), lambda i, f: (0, f)),
            pl.BlockSpec((tf, d), lambda i, f: (f, 0)),
            pl.BlockSpec((tm, d), lambda i, f: (i, 0)),
            _mod_spec(d, mod_row, tm, 5),
        ],
        out_specs=pl.BlockSpec((tm, d), lambda i, f: (i, 0)),
        out_shape=jax.ShapeDtypeStruct((n, d), F32),
        scratch_shapes=[pltpu.VMEM((tm, d), F32)],
        compiler_params=_cp(("arbitrary", "arbitrary")),
        name="ffn",
    )(h, wg, wu, wd, x, mods)


def _moe_kernel(be_ref, nu_ref, x_ref, wg_ref, wu_ref, wd_ref, o_ref, acc_ref):
    i = pl.program_id(0)
    f = pl.program_id(1)
    used = i < nu_ref[0]

    @pl.when(f == 0)
    def _():
        acc_ref[...] = jnp.zeros_like(acc_ref)

    @pl.when(used)
    def _():
        x = x_ref[...]
        a = _silu(_dot(x, wg_ref[0])) * _dot(x, wu_ref[0])
        acc_ref[...] += _dot(a.astype(BF16), wd_ref[0])

    @pl.when(f == pl.num_programs(1) - 1)
    def _():
        o_ref[...] = acc_ref[...].astype(o_ref.dtype)


def moe_ffn(blk_exp, n_used, xb, wg, wu, wd, tm, tf):
    n, d = xb.shape
    nf = wg.shape[2] // tf
    grid_spec = pltpu.PrefetchScalarGridSpec(
        num_scalar_prefetch=2,
        grid=(n // tm, nf),
        in_specs=[
            pl.BlockSpec((tm, d), lambda i, f, be, nu: (i, 0)),
            pl.BlockSpec((1, d, tf), lambda i, f, be, nu: (be[i], 0, f)),
            pl.BlockSpec((1, d, tf), lambda i, f, be, nu: (be[i], 0, f)),
            pl.BlockSpec((1, tf, d), lambda i, f, be, nu: (be[i], f, 0)),
        ],
        out_specs=pl.BlockSpec((tm, d), lambda i, f, be, nu: (i, 0)),
        scratch_shapes=[pltpu.VMEM((tm, d), F32)],
    )
    return pl.pallas_call(
        _moe_kernel,
        grid_spec=grid_spec,
        out_shape=jax.ShapeDtypeStruct((n, d), BF16),
        compiler_params=_cp(("arbitrary", "arbitrary")),
        name="moe_ffn",
    )(blk_exp, n_used, xb, wg, wu, wd)


def _combine_kernel(x_ref, y_ref, w_ref, g2_ref, gain_ref, o_ref, *, final):
    w = w_ref[...]
    d = x_ref.shape[1]
    y = w[:, 0:1] * y_ref[:, :d].astype(F32) + w[:, 1:2] * y_ref[:, d:].astype(F32)
    x = x_ref[...] + g2_ref[0] * y
    if final:
        x = (x * lax.rsqrt(jnp.mean(x * x, axis=-1, keepdims=True) + NORM_EPS)) * gain_ref[...]
    o_ref[...] = x


def moe_combine(x, y2, w, mods, mod_row, gain, final, tm):
    n, d = x.shape
    tok = lambda: pl.BlockSpec((tm, d), lambda i: (i, 0))
    return pl.pallas_call(
        functools.partial(_combine_kernel, final=final),
        grid=(n // tm,),
        in_specs=[tok(), pl.BlockSpec((tm, MOE_TOP_K * d), lambda i: (i, 0)),
                  pl.BlockSpec((tm, MOE_TOP_K), lambda i: (i, 0)),
                  _mod_spec(d, mod_row, tm, 5), pl.BlockSpec((1, d), lambda i: (0, 0))],
        out_specs=tok(),
        out_shape=jax.ShapeDtypeStruct((n, d), F32),
        compiler_params=_cp(("arbitrary",)),
        name="moe_combine",
    )(x, y2, w, mods, gain.reshape(1, d))


def _rmsnorm_kernel(x_ref, gain_ref, o_ref):
    x = x_ref[...]
    o_ref[...] = (x * lax.rsqrt(jnp.mean(x * x, axis=-1, keepdims=True) + NORM_EPS)) * gain_ref[...]


def rmsnorm_rows(x, gain, tm):
    n, d = x.shape
    return pl.pallas_call(
        _rmsnorm_kernel,
        grid=(n // tm,),
        in_specs=[pl.BlockSpec((tm, d), lambda i: (i, 0)), pl.BlockSpec((1, d), lambda i: (0, 0))],
        out_specs=pl.BlockSpec((tm, d), lambda i: (i, 0)),
        out_shape=jax.ShapeDtypeStruct((n, d), F32),
        compiler_params=_cp(("arbitrary",)),
        name="final_norm",
    )(x, gain.reshape(1, d))


def moe_route(logits, n_experts, tm):
    n_tok = logits.shape[0]
    top_logit, top_idx = lax.top_k(logits[:, :n_experts], MOE_TOP_K)
    top_w = jax.nn.softmax(top_logit, axis=-1)
    e_flat = top_idx.reshape(-1).astype(jnp.int32)
    n_assign = e_flat.shape[0]
    onehot = (e_flat[:, None] == jnp.arange(n_experts, dtype=jnp.int32)[None, :]).astype(jnp.int32)
    rank = jnp.sum((jnp.cumsum(onehot, axis=0) - onehot) * onehot, axis=1)
    counts = jnp.sum(onehot, axis=0)
    padded = (counts + tm - 1) // tm * tm
    pad_end = jnp.cumsum(padded)
    pad_start = pad_end - padded
    dest = pad_start[e_flat] + rank
    n_rows = n_assign + n_experts * tm
    tok = jnp.arange(n_assign, dtype=jnp.int32) // MOE_TOP_K
    row_tok = jnp.zeros((n_rows,), jnp.int32).at[dest].set(tok)
    blk_start = jnp.arange(n_rows // tm, dtype=jnp.int32) * tm
    blk_exp = jnp.minimum(jnp.sum(pad_end[None, :] <= blk_start[:, None], axis=1), n_experts - 1)
    n_used = (pad_end[-1] // tm).reshape(1)
    return row_tok, dest.reshape(n_tok, MOE_TOP_K), top_w, blk_exp.astype(jnp.int32), n_used.astype(jnp.int32)


def _rope_split(wcols):
    d, w = wcols.shape
    half = w // HEADS // 2
    return wcols.reshape(d, HEADS, 2, half).transpose(0, 2, 1, 3).reshape(d, w)


def _inproj_weights(w_in):
    d = w_in.shape[0]
    kv = (BW, BW, BW, BW, MLA_KV_RANK, MLA_ROPE)
    qs = (BW, BW, BW, BW, BW, MLA_Q_RANK, 4 * d)
    offs = np.concatenate([[0], np.cumsum(kv + qs)])
    seg = lambda i: w_in[:, offs[i]:offs[i + 1]]
    r_k, r_v, n_k, n_v, m_ckv, m_kr = (seg(i) for i in range(6))
    r_q, r_gf, r_gb, f_in, n_q, m_cq, gate = (seg(6 + i) for i in range(7))
    z = lambda n: jnp.zeros((d, n), w_in.dtype)
    cols = [gate, _rope_split(r_q), _rope_split(r_k), r_v, r_gf, r_gb, f_in, n_q, n_k, n_v,
            m_cq, z(BW - MLA_Q_RANK), m_ckv, m_kr, z(LANES - MLA_ROPE)]
    return jnp.concatenate(cols, axis=1).astype(BF16)


def _ret_rope_tables(n):
    t = jnp.arange(n)
    row = (t // GRID_W).astype(F32)
    col = (t % GRID_W).astype(F32)
    nf = RET_DK // 4
    inv = ROPE_BASE ** (-jnp.arange(nf, dtype=F32) / nf)
    ang = jnp.concatenate([row[:, None] * inv, col[:, None] * inv], axis=-1)
    return jnp.tile(jnp.cos(ang), (1, HEADS)), jnp.tile(jnp.sin(ang), (1, HEADS))


def _mla_rope_tables(n):
    t = jnp.arange(n)
    row = (t // GRID_W).astype(F32)
    col = (t % GRID_W).astype(F32)
    nf = MLA_ROPE // 4
    inv = ROPE_BASE ** (-jnp.arange(nf, dtype=F32) / nf)
    ang = jnp.concatenate([row[:, None] * inv, col[:, None] * inv], axis=-1)
    pad = jnp.zeros((n, LANES - MLA_NOPE - MLA_ROPE), F32)
    cos = jnp.concatenate([jnp.ones((n, MLA_NOPE), F32), jnp.cos(ang), jnp.cos(ang), pad], axis=-1)
    sin = jnp.concatenate([jnp.zeros((n, MLA_NOPE), F32), jnp.sin(ang), jnp.sin(ang), pad], axis=-1)
    return cos, sin


def _tile_rows(*sizes):
    for tm in (1024, 512, 256, 128):
        if all(s % tm == 0 for s in sizes):
            return tm
    raise ValueError(f"token counts {sizes} need a common multiple-of-128 row tile")


def kernel(x, c, ctx, c_ctx, ada_w, ada_b, norm_mix, norm_ffn, w_in, ret_decay_fwd, ret_decay_bwd,
           mla_q_norm, mla_kv_norm, mla_w_uq, mla_w_ukv, na_rpb, w_branch, w_out,
           ffn_w_gate, ffn_w_up, ffn_w_down, moe_router, moe_w_gate, moe_w_up, moe_w_down, norm_final):
    batch, t, d = x.shape
    tc = ctx.shape[1]
    depth = ada_w.shape[0]
    nl, ncx = batch * t, batch * tc
    assert batch < 8 and t % (16 * GRID_W) == 0 and tc % LANES == 0 and d == 4 * BW
    tm = _tile_rows(t, ncx)
    cb0 = 4 * d // BW

    xl = x.reshape(nl, d)
    xc = ctx.reshape(ncx, d)
    cc = jnp.zeros((8, d), F32).at[:batch].set(c).at[batch].set(c_ctx)
    mods = adaln(cc, ada_w, ada_b).reshape(depth, 8 * 6, 1, d)
    lat_row = lambda r0: r0 // t
    ctx_row = lambda r0: batch

    ret_cos, ret_sin = _ret_rope_tables(t)
    ret_cos_c, ret_sin_c = jnp.ones((tc, LANES), F32), jnp.zeros((tc, LANES), F32)
    mla_cos, mla_sin = _mla_rope_tables(t)
    mla_cos_c = jnp.concatenate([jnp.ones((tm, MLA_NOPE + MLA_ROPE), F32),
                                 jnp.zeros((tm, LANES - MLA_NOPE - MLA_ROPE), F32)], axis=-1)
    mla_sin_c = jnp.zeros((tm, LANES), F32)
    ret_cs = 256

    for i in range(depth):
        ctx_out = i < depth - 1
        md = mods[i]
        w_p = _inproj_weights(w_in[i])
        u = norm_inproj(xl, norm_mix[i], md, lat_row, w_p, tm, 768)
        uc = norm_inproj(xc, norm_mix[i], md, ctx_row, w_p, tm, 768)

        lg = jnp.stack([jax.nn.log_sigmoid(ret_decay_fwd[i].astype(F32)),
                        jax.nn.log_sigmoid(ret_decay_bwd[i].astype(F32))])
        zero_state = jnp.zeros((batch, 2, BW, BW), F32)
        yc_ret, s_ctx = retention(uc, cb0, lg, ret_cos_c, ret_sin_c, zero_state, batch, min(ret_cs, tc))
        y_ret, _ = retention(u, cb0, lg, ret_cos, ret_sin, s_ctx, batch, ret_cs)

        y_four = fourier_long(u, cb0 + CB_F, batch)

        bias_tab = _na_bias_table(na_rpb[i])
        y_na = na_attention(u, uc, cb0, bias_tab, batch, 16)

        mw = _mla_weights(mla_w_uq[i], mla_w_ukv[i])
        q_l, k_l, v_l = mla_prep(u, cb0, mla_cos, mla_sin, mla_q_norm[i], mla_kv_norm[i], mw, tm, t // tm)
        q_c, k_c, v_c = mla_prep(uc, cb0, mla_cos_c, mla_sin_c, mla_q_norm[i], mla_kv_norm[i], mw, tm, 1)
        y_mla = flash_attention(q_l, [(k_l, v_l), (k_c, v_c)], batch, 512, 512)

        wb = w_branch[i].astype(BF16)
        wb_mla = jnp.concatenate(
            [wb[3].reshape(HEADS, MLA_V, d), jnp.zeros((HEADS, LANES - MLA_V, d), BF16)], axis=1
        ).reshape(HEADS * LANES, d)
        wbs = (wb[0], wb[1], wb[2], wb_mla)
        wo = w_out[i].astype(BF16)
        j = i // 2
        if i % 2 == 0:
            xl, h2 = merge(u, y_ret, y_four, y_na, y_mla, xl, md, lat_row, norm_ffn[i], wbs, wo, None, 512)
            wg, wu, wd = ffn_w_gate[j].astype(BF16), ffn_w_up[j].astype(BF16), ffn_w_down[j].astype(BF16)
            tf = wg.shape[1] // 2
            xl = ffn(h2, xl, md, lat_row, wg, wu, wd, 512, tf)
        else:
            n_exp = moe_router.shape[2]
            r = jnp.pad(moe_router[j], ((0, 0), (0, LANES - n_exp)))
            r_hi = r.astype(BF16)
            r_lo = (r - r_hi.astype(F32)).astype(BF16)
            xl, h2, logits = merge(u, y_ret, y_four, y_na, y_mla, xl, md, lat_row, norm_ffn[i], wbs, wo,
                                   (r_hi, r_lo), 512)
            tmm = 512
            row_tok, dest, top_w, blk_exp, n_used = moe_route(logits, n_exp, tmm)
            xb = jnp.take(h2, row_tok, axis=0)
            yb = moe_ffn(blk_exp, n_used, xb, moe_w_gate[j].astype(BF16), moe_w_up[j].astype(BF16),
                         moe_w_down[j].astype(BF16), tmm, 512)
            y2 = jnp.take(yb, dest.reshape(-1), axis=0).reshape(nl, MOE_TOP_K * d)
            xl = moe_combine(xl, y2, top_w, md, lat_row, norm_final, i == depth - 1, 512)

        if ctx_out:
            yc_four = fourier_short(uc, cb0 + CB_F, batch)
            yc_na = na_ctx_attention(uc, cb0, batch)
            yc_mla = flash_attention(q_c, [(k_c, v_c)], batch, tc, 512)
            if i % 2 == 0:
                xc, hc2 = merge(uc, yc_ret, yc_four, yc_na, yc_mla, xc, md, ctx_row, norm_ffn[i], wbs, wo,
                                None, 512)
                xc = ffn(hc2, xc, md, ctx_row, wg, wu, wd, 512, tf)
            else:
                raise NotImplementedError("context tokens through the expert mixer")

    if depth % 2 == 1:
        xl = rmsnorm_rows(xl, norm_final, tm)
    return xl.reshape(batch, t, d)
```

```python
import functools

import numpy as np
import jax
import jax.numpy as jnp
from jax import lax
from jax.experimental import pallas as pl
from jax.experimental.pallas import tpu as pltpu

F32 = jnp.float32
BF16 = jnp.bfloat16

GRID_W = 64
ROPE_BASE = 10000.0
NORM_EPS = 1e-6
HEADS = 4
RET_DK = 64
FOURIER_GROUP_DIM = 64
NA_HEAD_DIM = 64
NA_WIN_R = 8
NA_WIN_C = 16
MLA_NOPE = 64
MLA_ROPE = 32
MLA_V = 64
MLA_Q_RANK = 192
MLA_KV_RANK = 128
MOE_TOP_K = 2
BW = 256

COL_GATE = 0
CB_RQ, CB_RK, CB_RV, CB_RGF, CB_RGB, CB_F, CB_NQ, CB_NK, CB_NV, CB_MCQ, CB_MKV = range(11)
LANES = 128
NEG = -1e30

VMEM_LIMIT = 48 * 1024 * 1024


def _cp(sem, vmem=VMEM_LIMIT):
    return pltpu.CompilerParams(dimension_semantics=sem, vmem_limit_bytes=vmem)


def _dot(a, b):
    return jnp.dot(a, b, preferred_element_type=F32)


def _dot_nt(a, b):
    return lax.dot_general(a, b, (((1,), (1,)), ((), ())), preferred_element_type=F32)


def _dot_tn(a, b):
    return lax.dot_general(a, b, (((0,), (0,)), ((), ())), preferred_element_type=F32)


def _silu(x):
    return x * jax.nn.sigmoid(x)


def _adaln_kernel(c_ref, w_ref, b_ref, o_ref):
    s = _silu(c_ref[...])
    o_ref[0] = _dot(s.astype(BF16), w_ref[0].astype(BF16)) + b_ref[0]


def adaln(cc, ada_w, ada_b):
    depth, d, n6 = ada_w.shape
    tn = n6 // 4
    return pl.pallas_call(
        _adaln_kernel,
        grid=(depth, n6 // tn),
        in_specs=[
            pl.BlockSpec((8, d), lambda l, j: (0, 0)),
            pl.BlockSpec((1, d, tn), lambda l, j: (l, 0, j)),
            pl.BlockSpec((1, 1, tn), lambda l, j: (l, 0, j)),
        ],
        out_specs=pl.BlockSpec((1, 8, tn), lambda l, j: (l, 0, j)),
        out_shape=jax.ShapeDtypeStruct((depth, 8, n6), F32),
        compiler_params=_cp(("arbitrary", "arbitrary")),
        name="adaln",
    )(cc, ada_w, ada_b.reshape(depth, 1, n6))


def _inproj_kernel(x_ref, g_ref, sc_ref, sh_ref, w_ref, o_ref, h_ref):
    @pl.when(pl.program_id(1) == 0)
    def _():
        x = x_ref[...]
        y = x * lax.rsqrt(jnp.mean(x * x, axis=-1, keepdims=True) + NORM_EPS)
        h = (y * g_ref[...]) * (1.0 + sc_ref[0]) + sh_ref[0]
        h_ref[...] = h.astype(BF16)

    o_ref[...] = _dot(h_ref[...], w_ref[...]).astype(o_ref.dtype)


def _mod_spec(d, mod_row, tm, k):
    return pl.BlockSpec((1, 1, d), lambda i, *_: (mod_row(i * tm) * 6 + k, 0, 0))


def norm_inproj(x, gain, mods, mod_row, w, tm, tn):
    n, d = x.shape
    nc = w.shape[1]
    return pl.pallas_call(
        _inproj_kernel,
        grid=(n // tm, nc // tn),
        in_specs=[
            pl.BlockSpec((tm, d), lambda i, j: (i, 0)),
            pl.BlockSpec((1, d), lambda i, j: (0, 0)),
            _mod_spec(d, mod_row, tm, 1),
            _mod_spec(d, mod_row, tm, 0),
            pl.BlockSpec((d, tn), lambda i, j: (0, j)),
        ],
        out_specs=pl.BlockSpec((tm, tn), lambda i, j: (i, j)),
        out_shape=jax.ShapeDtypeStruct((n, nc), BF16),
        scratch_shapes=[pltpu.VMEM((tm, d), BF16)],
        compiler_params=_cp(("arbitrary", "arbitrary")),
        name="norm_inproj",
    )(x, gain.reshape(1, d), mods, mods, w)


def _ret_kernel(lg_ref, q_ref, k_ref, v_ref, g_ref, cos_ref, sin_ref, s0_ref, lgq_ref, lgv_ref,
                y_ref, sout_ref, s_ref, *, n_chunks):
    d = pl.program_id(1)
    c = pl.program_id(2)
    rev = d == 1
    cs = q_ref.shape[0]
    w = q_ref.shape[1]
    half = w // 2

    @pl.when(c == 0)
    def _():
        s_ref[...] = s0_ref[0, 0]

    cos = cos_ref[...]
    sin = sin_ref[...]

    def rope(t):
        t1, t2 = t[:, :half], t[:, half:]
        return jnp.concatenate([t1 * cos - t2 * sin, t2 * cos + t1 * sin], axis=-1)

    q = rope(q_ref[...].astype(F32))
    k = rope(k_ref[...].astype(F32)) * (RET_DK ** -0.5)
    vb = v_ref[...]

    pos_i = lax.broadcasted_iota(jnp.int32, (cs, 1), 0).astype(F32)
    pos_j = lax.broadcasted_iota(jnp.int32, (1, cs), 1).astype(F32)
    p_i = jnp.where(rev, cs - 1.0 - pos_i, pos_i)
    p_j = jnp.where(rev, cs - 1.0 - pos_j, pos_j)
    diff = p_i - p_j
    lgq = lgq_ref[0]
    lgv = lgv_ref[0]
    q_w = jnp.exp(lgq * (p_i + 1.0))
    k_w = jnp.exp(lgq * (cs - 1.0 - p_i))

    lane = lax.broadcasted_iota(jnp.int32, (1, w), 1)
    head_q = (lane % half) // (half // HEADS)
    head_v = lane // (w // HEADS)

    s_prev = s_ref[...]
    o = _dot((q * q_w).astype(BF16), s_prev.astype(BF16))
    qb = q.astype(BF16)
    kb = k.astype(BF16)
    zero_b = jnp.zeros_like(qb)
    for h in range(HEADS):
        a = _dot_nt(jnp.where(head_q == h, qb, zero_b), kb)
        decay = jnp.where(diff >= 0, jnp.exp(lg_ref[d, h] * jnp.maximum(diff, 0.0)), 0.0)
        oh = _dot((a * decay).astype(BF16), vb)
        o = o + jnp.where(head_v == h, oh, 0.0)

    ds = _dot_tn((k * k_w).astype(BF16), vb)
    row_head = (lax.broadcasted_iota(jnp.int32, (w, 1), 0) % half) // (half // HEADS)
    s_new = s_prev * jnp.exp(lgv * float(cs)) + jnp.where(row_head == head_v, ds, 0.0)
    s_ref[...] = s_new

    @pl.when(c == n_chunks - 1)
    def _():
        sout_ref[0, 0] = s_new

    hd = w // HEADS
    gi = lax.broadcasted_iota(jnp.int32, (w, w), 0) // hd
    gj = lax.broadcasted_iota(jnp.int32, (w, w), 1) // hd
    avg = jnp.where(gi == gj, 1.0 / hd, 0.0).astype(BF16)
    ms = _dot((o * o).astype(BF16), avg)
    on = o * lax.rsqrt(ms + NORM_EPS)
    y_ref[0] = (_silu(g_ref[...].astype(F32)) * on).astype(y_ref.dtype)


def retention(u, cb0, lg, cos, sin, s0, batch, cs):
    n = u.shape[0]
    t = n // batch
    nch = t // cs
    w = BW
    half = w // 2
    lgq = jnp.tile(jnp.repeat(lg, half // HEADS, axis=1), (1, 2)).reshape(2, 1, w)
    lgv = jnp.repeat(lg, w // HEADS, axis=1).reshape(2, 1, w)

    def rows(b, d, c):
        return b * nch + jnp.where(d == 1, nch - 1 - c, c)

    def col(cb):
        return pl.BlockSpec((cs, w), lambda b, d, c: (rows(b, d, c), cb0 + cb))

    def tab():
        return pl.BlockSpec((cs, half), lambda b, d, c: (jnp.where(d == 1, nch - 1 - c, c), 0))

    y, s_out = pl.pallas_call(
        functools.partial(_ret_kernel, n_chunks=nch),
        grid=(batch, 2, nch),
        in_specs=[
            pl.BlockSpec(memory_space=pltpu.SMEM),
            col(CB_RQ), col(CB_RK), col(CB_RV),
            pl.BlockSpec((cs, w), lambda b, d, c: (rows(b, d, c), cb0 + CB_RGF + d)),
            tab(), tab(),
            pl.BlockSpec((1, 1, w, w), lambda b, d, c: (b, d, 0, 0)),
            pl.BlockSpec((1, 1, w), lambda b, d, c: (d, 0, 0)),
            pl.BlockSpec((1, 1, w), lambda b, d, c: (d, 0, 0)),
        ],
        out_specs=[
            pl.BlockSpec((1, cs, w), lambda b, d, c: (d, rows(b, d, c), 0)),
            pl.BlockSpec((1, 1, w, w), lambda b, d, c: (b, d, 0, 0)),
        ],
        out_shape=[
            jax.ShapeDtypeStruct((2, n, w), BF16),
            jax.ShapeDtypeStruct((batch, 2, w, w), F32),
        ],
        scratch_shapes=[pltpu.VMEM((w, w), F32)],
        compiler_params=_cp(("arbitrary", "arbitrary", "arbitrary")),
        name="retention",
    )(lg, u, u, u, u, cos, sin, s0, lgq, lgv)
    return y, s_out


def _dft_tables(t, t1, t2):
    k1 = jnp.arange(t1, dtype=jnp.int32)
    a = jnp.arange(t1, dtype=jnp.int32)
    m = jnp.arange(t2, dtype=jnp.int32)
    ph1 = (k1[None, :, None] * (a[None, None, :] * t2 + m[:, None, None])) % t
    ang1 = ph1.astype(F32) * (2.0 * np.pi / t)
    ph2 = (m[:, None] * m[None, :]) % t2
    ang2 = ph2.astype(F32) * (2.0 * np.pi / t2)
    return (jnp.cos(ang1).astype(BF16), jnp.sin(ang1).astype(BF16),
            jnp.cos(ang2).astype(BF16), jnp.sin(ang2).astype(BF16))


def _channel_tables(width):
    ch = jnp.arange(width, dtype=jnp.int32)
    same = (ch[:, None] // FOURIER_GROUP_DIM) == (ch[None, :] // FOURIER_GROUP_DIM)
    ph = ((ch[:, None] % FOURIER_GROUP_DIM) * (ch[None, :] % FOURIER_GROUP_DIM)) % FOURIER_GROUP_DIM
    ang = ph.astype(F32) * (2.0 * np.pi / FOURIER_GROUP_DIM)
    return (jnp.where(same, jnp.cos(ang), 0.0).astype(BF16),
            jnp.where(same, jnp.sin(ang), 0.0).astype(BF16))


def _fourier_kernel(x_ref, c1_ref, s1_ref, c2_ref, s2_ref, cc_ref, sc_ref, o_ref, xa, yre, yim,
                    *, t1, t2, norm):
    xa[...] = x_ref[...].astype(F32)

    def stage1(m, carry):
        xs = xa[pl.ds(m, t1, stride=t2), :].astype(BF16)
        r0 = pl.multiple_of(m * t1, t1)
        yre[pl.ds(r0, t1), :] = _dot(c1_ref[m], xs)
        yim[pl.ds(r0, t1), :] = -_dot(s1_ref[m], xs)
        return carry

    lax.fori_loop(0, t2, stage1, 0, unroll=8)

    c2 = c2_ref[...]
    s2 = s2_ref[...]
    cc = cc_ref[...]
    sc = sc_ref[...]

    def stage2(k1, carry):
        yr = yre[pl.ds(k1, t2, stride=t1), :].astype(BF16)
        yi = yim[pl.ds(k1, t2, stride=t1), :].astype(BF16)
        zr = _dot(c2, yr) + _dot(s2, yi)
        zi = _dot(c2, yi) - _dot(s2, yr)
        out = (_dot(zr.astype(BF16), cc) + _dot(zi.astype(BF16), sc)) * norm
        xa[pl.ds(k1, t2, stride=t1), :] = out
        return carry

    lax.fori_loop(0, t1, stage2, 0, unroll=4)
    o_ref[...] = xa[...].astype(o_ref.dtype)


def fourier_long(u, cb, batch, t2=LANES):
    n = u.shape[0]
    t = n // batch
    t1 = t // t2
    c1, s1, c2, s2 = _dft_tables(t, t1, t2)
    cc, sc = _channel_tables(LANES)
    norm = float(1.0 / np.sqrt(t * FOURIER_GROUP_DIM))
    full = lambda shape: pl.BlockSpec(shape, lambda b, hh: (0,) * len(shape))
    return pl.pallas_call(
        functools.partial(_fourier_kernel, t1=t1, t2=t2, norm=norm),
        grid=(batch, BW // LANES),
        in_specs=[
            pl.BlockSpec((t, LANES), lambda b, hh: (b, cb * (BW // LANES) + hh)),
            full((t2, t1, t1)), full((t2, t1, t1)), full((t2, t2)), full((t2, t2)),
            full((LANES, LANES)), full((LANES, LANES)),
        ],
        out_specs=pl.BlockSpec((t, LANES), lambda b, hh: (b, hh)),
        out_shape=jax.ShapeDtypeStruct((n, BW), BF16),
        scratch_shapes=[pltpu.VMEM((t, LANES), F32)] * 3,
        compiler_params=_cp(("arbitrary", "arbitrary")),
        name="fourier",
    )(u, c1, s1, c2, s2, cc, sc)


def _fourier_small_kernel(x_ref, ct_ref, st_ref, cc_ref, sc_ref, o_ref, *, norm):
    x = x_ref[...]
    zr = _dot(ct_ref[...], x)
    zi = -_dot(st_ref[...], x)
    out = (_dot(zr.astype(BF16), cc_ref[...]) + _dot(zi.astype(BF16), sc_ref[...])) * norm
    o_ref[...] = out.astype(o_ref.dtype)


def fourier_short(u, cb, batch):
    n = u.shape[0]
    t = n // batch
    pos = jnp.arange(t, dtype=jnp.int32)
    ang = ((pos[:, None] * pos[None, :]) % t).astype(F32) * (2.0 * np.pi / t)
    ct, st = jnp.cos(ang).astype(BF16), jnp.sin(ang).astype(BF16)
    cc, sc = _channel_tables(BW)
    norm = float(1.0 / np.sqrt(t * FOURIER_GROUP_DIM))
    full = lambda shape: pl.BlockSpec(shape, lambda b: (0,) * len(shape))
    return pl.pallas_call(
        functools.partial(_fourier_small_kernel, norm=norm),
        grid=(batch,),
        in_specs=[pl.BlockSpec((t, BW), lambda b: (b, cb)), full((t, t)), full((t, t)),
                  full((BW, BW)), full((BW, BW))],
        out_specs=pl.BlockSpec((t, BW), lambda b: (b, 0)),
        out_shape=jax.ShapeDtypeStruct((n, BW), BF16),
        compiler_params=_cp(("arbitrary",)),
        name="fourier_ctx",
    )(u, ct, st, cc, sc)


def _na_bias_table(rpb):
    n_r, n_c = rpb.shape[1], rpb.shape[2]
    span = 2 * GRID_W
    left = GRID_W - NA_WIN_C
    vp = jnp.pad(rpb.astype(F32), ((0, 0), (0, 0), (left, span - n_c - left)))
    rep = jnp.broadcast_to(vp[:, :, None, :], (HEADS, n_r, GRID_W, span)).reshape(HEADS, n_r, GRID_W * span)
    skew = rep[..., :GRID_W * (span - 1)].reshape(HEADS, n_r, GRID_W, span - 1)
    toep = skew[..., GRID_W - 1:]
    qc = np.arange(GRID_W)[:, None]
    kc = np.arange(GRID_W)[None, :]
    start = np.clip(qc - NA_WIN_C // 2, 0, GRID_W - NA_WIN_C)
    valid = (kc >= start) & (kc < start + NA_WIN_C)
    toep = jnp.where(valid, toep, NEG)
    tab = jnp.stack([toep[:, NA_WIN_R - 1 - v:2 * NA_WIN_R - 1 - v] for v in range(NA_WIN_R)])
    tab = tab.transpose(0, 1, 3, 2, 4)
    return tab.reshape(NA_WIN_R, HEADS * GRID_W, NA_WIN_R * GRID_W)


def _na_kernel(q_ref, k_ref, v_ref, kc_ref, vc_ref, bias_ref, o_ref, *, rows_per_step, n_rows):
    i = pl.program_id(1)
    w = q_ref.shape[1]
    lane = lax.broadcasted_iota(jnp.int32, (1, w), 1)
    head = lane // (w // HEADS)
    scale = jnp.asarray(NA_HEAD_DIM ** -0.5, q_ref.dtype)
    kc = kc_ref[...]
    vc = vc_ref[...]
    win = NA_WIN_R * GRID_W

    def row(rl, carry):
        r = i * rows_per_step + rl
        rs = jnp.clip(r - NA_WIN_R // 2, 0, n_rows - NA_WIN_R)
        var = r - rs
        q0 = pl.multiple_of(rl * GRID_W, GRID_W)
        k0 = pl.multiple_of(rs * GRID_W, GRID_W)
        q = q_ref[pl.ds(q0, GRID_W), :] * scale
        kw = k_ref[pl.ds(k0, win), :]
        vw = v_ref[pl.ds(k0, win), :]
        zero_b = jnp.zeros_like(q)
        q4 = jnp.concatenate([jnp.where(head == h, q, zero_b) for h in range(HEADS)], axis=0)
        s_loc = _dot_nt(q4, kw) + bias_ref[var]
        s_ctx = _dot_nt(q4, kc)
        m = jnp.maximum(jnp.max(s_loc, axis=-1, keepdims=True), jnp.max(s_ctx, axis=-1, keepdims=True))
        p_loc = jnp.exp(s_loc - m)
        p_ctx = jnp.exp(s_ctx - m)
        l = jnp.sum(p_loc, axis=-1, keepdims=True) + jnp.sum(p_ctx, axis=-1, keepdims=True)
        pv = (_dot(p_loc.astype(BF16), vw) + _dot(p_ctx.astype(BF16), vc)) / l
        acc = jnp.zeros((GRID_W, w), F32)
        for h in range(HEADS):
            acc = acc + jnp.where(head == h, pv[h * GRID_W:(h + 1) * GRID_W], 0.0)
        o_ref[pl.ds(q0, GRID_W), :] = acc.astype(o_ref.dtype)
        return carry

    lax.fori_loop(0, rows_per_step, row, 0, unroll=2)


def na_attention(u, uc, cb0, bias_tab, batch, rows_per_step):
    n = u.shape[0]
    t = n // batch
    tc = uc.shape[0] // batch
    n_rows = t // GRID_W
    steps = n_rows // rows_per_step
    tq = rows_per_step * GRID_W
    return pl.pallas_call(
        functools.partial(_na_kernel, rows_per_step=rows_per_step, n_rows=n_rows),
        grid=(batch, steps),
        in_specs=[
            pl.BlockSpec((tq, BW), lambda b, i: (b * steps + i, cb0 + CB_NQ)),
            pl.BlockSpec((t, BW), lambda b, i: (b, cb0 + CB_NK)),
            pl.BlockSpec((t, BW), lambda b, i: (b, cb0 + CB_NV)),
            pl.BlockSpec((tc, BW), lambda b, i: (b, cb0 + CB_NK)),
            pl.BlockSpec((tc, BW), lambda b, i: (b, cb0 + CB_NV)),
            pl.BlockSpec(bias_tab.shape, lambda b, i: (0, 0, 0)),
        ],
        out_specs=pl.BlockSpec((tq, BW), lambda b, i: (b * steps + i, 0)),
        out_shape=jax.ShapeDtypeStruct((n, BW), BF16),
        compiler_params=_cp(("arbitrary", "arbitrary")),
        name="na_attention",
    )(u, u, u, uc, uc, bias_tab)


def _na_ctx_kernel(q_ref, k_ref, v_ref, o_ref):
    w = q_ref.shape[1]
    lane = lax.broadcasted_iota(jnp.int32, (1, w), 1)
    head = lane // (w // HEADS)
    scale = NA_HEAD_DIM ** -0.5
    q = q_ref[...]
    k = k_ref[...]
    v = v_ref[...]
    zero_b = jnp.zeros_like(q)
    acc = jnp.zeros(q.shape, F32)
    for h in range(HEADS):
        s = _dot_nt(jnp.where(head == h, q, zero_b), k) * scale
        p = jnp.exp(s - jnp.max(s, axis=-1, keepdims=True))
        l = jnp.sum(p, axis=-1, keepdims=True)
        acc = acc + jnp.where(head == h, _dot(p.astype(BF16), v) / l, 0.0)
    o_ref[...] = acc.astype(o_ref.dtype)


def na_ctx_attention(uc, cb0, batch):
    tc = uc.shape[0] // batch
    spec = lambda cb: pl.BlockSpec((tc, BW), lambda b: (b, cb0 + cb))
    return pl.pallas_call(
        _na_ctx_kernel,
        grid=(batch,),
        in_specs=[spec(CB_NQ), spec(CB_NK), spec(CB_NV)],
        out_specs=pl.BlockSpec((tc, BW), lambda b: (b, 0)),
        out_shape=jax.ShapeDtypeStruct((uc.shape[0], BW), BF16),
        compiler_params=_cp(("arbitrary",)),
        name="na_ctx_attention",
    )(uc, uc, uc)


def _mla_prep_kernel(cq_ref, ckv_ref, kr_ref, cos_ref, sin_ref, qn_ref, kvn_ref, wq_ref, wqr_ref,
                     wk_ref, wv_ref, p1_ref, p2_ref, one_ref, q_ref, k_ref, v_ref):
    cos = cos_ref[...]
    sin = sin_ref[...]
    cos4 = jnp.concatenate([cos] * HEADS, axis=-1)
    sin4 = jnp.concatenate([sin] * HEADS, axis=-1)

    cq = cq_ref[...].astype(F32)
    ms = jnp.sum(cq * cq, axis=-1, keepdims=True) * (1.0 / MLA_Q_RANK)
    cqn = ((cq * lax.rsqrt(ms + NORM_EPS)) * qn_ref[...]).astype(BF16)
    q = _dot(cqn, wq_ref[...]) * cos4 + _dot(cqn, wqr_ref[...]) * sin4
    q_ref[...] = (q * float((MLA_NOPE + MLA_ROPE) ** -0.5 * np.log2(np.e))).astype(q_ref.dtype)

    ckv = ckv_ref[...].astype(F32)
    ms = jnp.mean(ckv * ckv, axis=-1, keepdims=True)
    ckvn = ((ckv * lax.rsqrt(ms + NORM_EPS)) * kvn_ref[...]).astype(BF16)
    kr = kr_ref[...]
    k_rot = _dot(kr, p1_ref[...]) * cos + _dot(kr, p2_ref[...]) * sin
    k = _dot(ckvn, wk_ref[...]) + jnp.concatenate([k_rot] * HEADS, axis=-1)
    k_ref[...] = k.astype(k_ref.dtype)
    v_ref[...] = (_dot(ckvn, wv_ref[...]) + one_ref[...]).astype(v_ref.dtype)


def _mla_weights(w_uq, w_ukv):
    qr = w_uq.shape[0]
    dq = MLA_NOPE + MLA_ROPE
    hr = MLA_ROPE // 2
    wq3 = w_uq.reshape(qr, HEADS, dq)
    zq = jnp.zeros((qr, HEADS, LANES - dq), F32)
    wq = jnp.concatenate([wq3, zq], axis=-1)
    x1 = wq3[..., MLA_NOPE:MLA_NOPE + hr]
    x2 = wq3[..., MLA_NOPE + hr:]
    wqr = jnp.concatenate([jnp.zeros((qr, HEADS, MLA_NOPE), F32), -x2, x1, zq], axis=-1)
    pad_rows = lambda m: jnp.pad(m.reshape(qr, HEADS * LANES), ((0, BW - qr), (0, 0)))
    kvr = w_ukv.shape[0]
    wkv3 = w_ukv.reshape(kvr, HEADS, MLA_NOPE + MLA_V)
    zk = jnp.zeros((kvr, HEADS, LANES - MLA_NOPE), F32)
    wk = jnp.concatenate([wkv3[..., :MLA_NOPE], zk], axis=-1).reshape(kvr, HEADS * LANES)
    wv = jnp.concatenate([wkv3[..., MLA_NOPE:], zk], axis=-1).reshape(kvr, HEADS * LANES)
    j = np.arange(hr)
    p1 = np.zeros((LANES, LANES), np.float32)
    p1[np.arange(MLA_ROPE), MLA_NOPE + np.arange(MLA_ROPE)] = 1.0
    p2 = np.zeros((LANES, LANES), np.float32)
    p2[hr + j, MLA_NOPE + j] = -1.0
    p2[j, MLA_NOPE + hr + j] = 1.0
    one = np.zeros((1, HEADS * LANES), np.float32)
    one[0, MLA_V + LANES * np.arange(HEADS)] = 1.0
    return (pad_rows(wq).astype(BF16), pad_rows(wqr).astype(BF16), wk.astype(BF16), wv.astype(BF16),
            jnp.asarray(p1, BF16), jnp.asarray(p2, BF16), jnp.asarray(one))


def mla_prep(u, cb0, cos, sin, q_norm, kv_norm, weights, tm, rope_blocks):
    n = u.shape[0]
    wq, wqr, wk, wv, p1, p2, one = weights
    qn = jnp.pad(q_norm, (0, BW - q_norm.shape[0])).reshape(1, BW)
    full = lambda a: pl.BlockSpec(a.shape, lambda i: (0,) * a.ndim)
    tab = pl.BlockSpec((tm, LANES), lambda i: (i % rope_blocks, 0))
    kv_cb = (cb0 + CB_MKV) * (BW // LANES)
    out = jax.ShapeDtypeStruct((n, HEADS * LANES), BF16)
    ospec = pl.BlockSpec((tm, HEADS * LANES), lambda i: (i, 0))
    return pl.pallas_call(
        _mla_prep_kernel,
        grid=(n // tm,),
        in_specs=[
            pl.BlockSpec((tm, BW), lambda i: (i, cb0 + CB_MCQ)),
            pl.BlockSpec((tm, LANES), lambda i: (i, kv_cb)),
            pl.BlockSpec((tm, LANES), lambda i: (i, kv_cb + 1)),
            tab, tab, full(qn), pl.BlockSpec((1, LANES), lambda i: (0, 0)),
            full(wq), full(wqr), full(wk), full(wv), full(p1), full(p2), full(one),
        ],
        out_specs=[ospec, ospec, ospec],
        out_shape=[out, out, out],
        compiler_params=_cp(("arbitrary",)),
        name="mla_prep",
    )(u, u, u, cos, sin, qn, kv_norm.reshape(1, LANES), wq, wqr, wk, wv, p1, p2, one)


def _flash_kernel(*refs, lens, tk):
    q_ref = refs[0]
    kv_refs = refs[1:1 + 2 * len(lens)]
    o_ref = refs[1 + 2 * len(lens)]
    q = q_ref[...]
    tq = q.shape[0]
    m = jnp.full((tq, 1), NEG, F32)
    acc = jnp.zeros((tq, LANES), F32)

    def chunk(kc, vc, m, acc):
        s = _dot_nt(q, kc)
        m_new = jnp.maximum(m, jnp.max(s, axis=-1, keepdims=True))
        p = jnp.exp2((s - m_new).astype(BF16))
        acc = jnp.exp2(m - m_new) * acc + _dot(p, vc)
        return m_new, acc

    for si, length in enumerate(lens):
        k_ref, v_ref = kv_refs[2 * si], kv_refs[2 * si + 1]
        step = min(tk, length)
        if length == step:
            m, acc = chunk(k_ref[...], v_ref[...], m, acc)
        else:
            def body(j, carry, k_ref=k_ref, v_ref=v_ref, step=step):
                j0 = pl.multiple_of(j * step, step)
                return chunk(k_ref[pl.ds(j0, step), :], v_ref[pl.ds(j0, step), :], *carry)

            m, acc = lax.fori_loop(0, length // step, body, (m, acc), unroll=2)

    lane = lax.broadcasted_iota(jnp.int32, (1, LANES), 1)
    l = jnp.sum(jnp.where(lane == MLA_V, acc, 0.0), axis=-1, keepdims=True)
    o_ref[...] = jnp.where(lane < MLA_V, acc / l, 0.0).astype(o_ref.dtype)


def flash_attention(q, kvs, batch, tq, tk):
    n = q.shape[0]
    nq = n // batch // tq
    lens = tuple(k.shape[0] // batch for k, _ in kvs)
    in_specs = [pl.BlockSpec((tq, LANES), lambda b, h, i: (b * nq + i, h))]
    args = [q]
    for (k, v), length in zip(kvs, lens):
        in_specs += [pl.BlockSpec((length, LANES), lambda b, h, i: (b, h))] * 2
        args += [k, v]
    return pl.pallas_call(
        functools.partial(_flash_kernel, lens=lens, tk=tk),
        grid=(batch, HEADS, nq),
        in_specs=in_specs,
        out_specs=pl.BlockSpec((tq, LANES), lambda b, h, i: (b * nq + i, h)),
        out_shape=jax.ShapeDtypeStruct((n, HEADS * LANES), BF16),
        compiler_params=_cp(("arbitrary", "arbitrary", "arbitrary")),
        name="mla_attention",
    )(*args)


def _merge_kernel(*refs, with_router):
    (gate_ref, yr_f_ref, yr_b_ref, yf_ref, yn_ref, ym_ref, x_ref, g1_ref, wb_ret_ref, wb_f_ref,
     wb_na_ref, wb_mla_ref, wo_ref, gain_ref, sc_ref, sh_ref) = refs[:16]
    rest = refs[16:]
    if with_router:
        rh_ref, rl_ref, x_out_ref, h_out_ref, lg_out_ref = rest
    else:
        x_out_ref, h_out_ref = rest
    d = x_ref.shape[1]

    def gated(k, y):
        g = jax.nn.sigmoid(gate_ref[:, k * d:(k + 1) * d].astype(F32))
        return g * y

    m = gated(0, _dot(yr_f_ref[0] + yr_b_ref[0], wb_ret_ref[...]))
    m = m + gated(1, _dot(yf_ref[...], wb_f_ref[...]))
    m = m + gated(2, _dot(yn_ref[...], wb_na_ref[...]))
    m = m + gated(3, _dot(ym_ref[...], wb_mla_ref[...]))
    y = _dot(m.astype(BF16), wo_ref[...])
    x = x_ref[...] + g1_ref[0] * y
    x_out_ref[...] = x
    hn = x * lax.rsqrt(jnp.mean(x * x, axis=-1, keepdims=True) + NORM_EPS)
    h = (hn * gain_ref[...]) * (1.0 + sc_ref[0]) + sh_ref[0]
    h_out_ref[...] = h.astype(h_out_ref.dtype)
    if with_router:
        h_hi = h.astype(BF16)
        h_lo = (h - h_hi.astype(F32)).astype(BF16)
        lg_out_ref[...] = (_dot(h_hi, rh_ref[...]) + _dot(h_lo, rh_ref[...])) + _dot(h_hi, rl_ref[...])


def merge(u, y_ret, y_four, y_na, y_mla, x, mods, mod_row, gain, wb, w_out, router, tm):
    n, d = x.shape
    wb_ret, wb_f, wb_na, wb_mla = wb
    full = lambda a: pl.BlockSpec(a.shape, lambda i: (0,) * a.ndim)
    br = lambda: pl.BlockSpec((tm, BW), lambda i: (i, 0))
    tok = lambda: pl.BlockSpec((tm, d), lambda i: (i, 0))
    in_specs = [
        pl.BlockSpec((tm, 4 * d), lambda i: (i, 0)),
        pl.BlockSpec((1, tm, BW), lambda i: (0, i, 0)),
        pl.BlockSpec((1, tm, BW), lambda i: (1, i, 0)),
        br(), br(),
        pl.BlockSpec((tm, HEADS * LANES), lambda i: (i, 0)),
        tok(), _mod_spec(d, mod_row, tm, 2),
        full(wb_ret), full(wb_f), full(wb_na), full(wb_mla), full(w_out),
        pl.BlockSpec((1, d), lambda i: (0, 0)), _mod_spec(d, mod_row, tm, 4), _mod_spec(d, mod_row, tm, 3),
    ]
    args = [u, y_ret, y_ret, y_four, y_na, y_mla, x, mods, wb_ret, wb_f, wb_na, wb_mla, w_out,
            gain.reshape(1, d), mods, mods]
    out_specs = [tok(), tok()]
    out_shape = [jax.ShapeDtypeStruct((n, d), F32), jax.ShapeDtypeStruct((n, d), BF16)]
    if router is not None:
        in_specs += [full(router[0]), full(router[1])]
        args += list(router)
        out_specs.append(pl.BlockSpec((tm, LANES), lambda i: (i, 0)))
        out_shape.append(jax.ShapeDtypeStruct((n, LANES), F32))
    return pl.pallas_call(
        functools.partial(_merge_kernel, with_router=router is not None),
        grid=(n // tm,),
        in_specs=in_specs,
        out_specs=out_specs,
        out_shape=out_shape,
        compiler_params=_cp(("arbitrary",)),
        name="merge",
    )(*args)


def _ffn_kernel(h_ref, wg_ref, wu_ref, wd_ref, x_ref, g2_ref, o_ref, acc_ref):
    f = pl.program_id(1)

    @pl.when(f == 0)
    def _():
        acc_ref[...] = jnp.zeros_like(acc_ref)

    h = h_ref[...]
    a = _silu(_dot(h, wg_ref[...])) * _dot(h, wu_ref[...])
    acc_ref[...] += _dot(a.astype(BF16), wd_ref[...])

    @pl.when(f == pl.num_programs(1) - 1)
    def _():
        o_ref[...] = x_ref[...] + g2_ref[0] * acc_ref[...]


def ffn(h, x, mods, mod_row, wg, wu, wd, tm, tf):
    n, d = x.shape
    nf = wg.shape[1] // tf
    return pl.pallas_call(
        _ffn_kernel,
        grid=(n // tm, nf),
        in_specs=[
            pl.BlockSpec((tm, d), lambda i, f: (i, 0)),
            pl.BlockSpec((d, tf), lambda i, f: (0, f)),
            pl.BlockSpec((d, tf), lambda i, f: (0, f)),
            pl.BlockSpec((tf, d), lambda i, f: (f, 0)),
            pl.BlockSpec((tm, d), lambda i, f: (i, 0)),
            _mod_spec(d, mod_row, tm, 5),
        ],
        out_specs=pl.BlockSpec((tm, d), lambda i, f: (i, 0)),
        out_shape=jax.ShapeDtypeStruct((n, d), F32),
        scratch_shapes=[pltpu.VMEM((tm, d), F32)],
        compiler_params=_cp(("arbitrary", "arbitrary")),
        name="ffn",
    )(h, wg, wu, wd, x, mods)


def _moe_kernel(be_ref, nu_ref, x_ref, wg_ref, wu_ref, wd_ref, o_ref, acc_ref):
    i = pl.program_id(0)
    f = pl.program_id(1)
    used = i < nu_ref[0]

    @pl.when(f == 0)
    def _():
        acc_ref[...] = jnp.zeros_like(acc_ref)

    @pl.when(used)
    def _():
        x = x_ref[...]
        a = _silu(_dot(x, wg_ref[0])) * _dot(x, wu_ref[0])
        acc_ref[...] += _dot(a.astype(BF16), wd_ref[0])

    @pl.when(f == pl.num_programs(1) - 1)
    def _():
        o_ref[...] = acc_ref[...].astype(o_ref.dtype)


def moe_ffn(blk_exp, n_used, xb, wg, wu, wd, tm, tf):
    n, d = xb.shape
    nf = wg.shape[2] // tf
    grid_spec = pltpu.PrefetchScalarGridSpec(
        num_scalar_prefetch=2,
        grid=(n // tm, nf),
        in_specs=[
            pl.BlockSpec((tm, d), lambda i, f, be, nu: (i, 0)),
            pl.BlockSpec((1, d, tf), lambda i, f, be, nu: (be[i], 0, f)),
            pl.BlockSpec((1, d, tf), lambda i, f, be, nu: (be[i], 0, f)),
            pl.BlockSpec((1, tf, d), lambda i, f, be, nu: (be[i], f, 0)),
        ],
        out_specs=pl.BlockSpec((tm, d), lambda i, f, be, nu: (i, 0)),
        scratch_shapes=[pltpu.VMEM((tm, d), F32)],
    )
    return pl.pallas_call(
        _moe_kernel,
        grid_spec=grid_spec,
        out_shape=jax.ShapeDtypeStruct((n, d), BF16),
        compiler_params=_cp(("arbitrary", "arbitrary")),
        name="moe_ffn",
    )(blk_exp, n_used, xb, wg, wu, wd)


def _combine_kernel(x_ref, ya_ref, yb_ref, w_ref, g2_ref, gain_ref, o_ref, *, final):
    w = w_ref[...]
    y = w[:, 0:1] * ya_ref[...].astype(F32) + w[:, 1:2] * yb_ref[...].astype(F32)
    x = x_ref[...] + g2_ref[0] * y
    if final:
        x = (x * lax.rsqrt(jnp.mean(x * x, axis=-1, keepdims=True) + NORM_EPS)) * gain_ref[...]
    o_ref[...] = x


def moe_combine(x, y2, w, mods, mod_row, gain, final, tm):
    n, d = x.shape
    tok = lambda: pl.BlockSpec((tm, d), lambda i: (i, 0))
    return pl.pallas_call(
        functools.partial(_combine_kernel, final=final),
        grid=(n // tm,),
        in_specs=[tok(), tok(), pl.BlockSpec((tm, d), lambda i: (i + n // tm, 0)),
                  pl.BlockSpec((tm, MOE_TOP_K), lambda i: (i, 0)),
                  _mod_spec(d, mod_row, tm, 5), pl.BlockSpec((1, d), lambda i: (0, 0))],
        out_specs=tok(),
        out_shape=jax.ShapeDtypeStruct((n, d), F32),
        compiler_params=_cp(("arbitrary",)),
        name="moe_combine",
    )(x, y2, y2, w, mods, gain.reshape(1, d))


def _rmsnorm_kernel(x_ref, gain_ref, o_ref):
    x = x_ref[...]
    o_ref[...] = (x * lax.rsqrt(jnp.mean(x * x, axis=-1, keepdims=True) + NORM_EPS)) * gain_ref[...]


def rmsnorm_rows(x, gain, tm):
    n, d = x.shape
    return pl.pallas_call(
        _rmsnorm_kernel,
        grid=(n // tm,),
        in_specs=[pl.BlockSpec((tm, d), lambda i: (i, 0)), pl.BlockSpec((1, d), lambda i: (0, 0))],
        out_specs=pl.BlockSpec((tm, d), lambda i: (i, 0)),
        out_shape=jax.ShapeDtypeStruct((n, d), F32),
        compiler_params=_cp(("arbitrary",)),
        name="final_norm",
    )(x, gain.reshape(1, d))


def moe_route(logits, n_experts, tm):
    n_tok = logits.shape[0]
    top_logit, top_idx = lax.top_k(logits[:, :n_experts], MOE_TOP_K)
    top_w = jax.nn.softmax(top_logit, axis=-1)
    e_flat = top_idx.reshape(-1).astype(jnp.int32)
    n_assign = e_flat.shape[0]
    onehot = (e_flat[:, None] == jnp.arange(n_experts, dtype=jnp.int32)[None, :]).astype(jnp.int32)
    rank = jnp.sum((jnp.cumsum(onehot, axis=0) - onehot) * onehot, axis=1)
    counts = jnp.sum(onehot, axis=0)
    padded = (counts + tm - 1) // tm * tm
    pad_end = jnp.cumsum(padded)
    pad_start = pad_end - padded
    dest = pad_start[e_flat] + rank
    n_rows = n_assign + n_experts * tm
    tok = jnp.arange(n_assign, dtype=jnp.int32) // MOE_TOP_K
    row_tok = jnp.zeros((n_rows,), jnp.int32).at[dest].set(tok)
    blk_start = jnp.arange(n_rows // tm, dtype=jnp.int32) * tm
    blk_exp = jnp.minimum(jnp.sum(pad_end[None, :] <= blk_start[:, None], axis=1), n_experts - 1)
    n_used = (pad_end[-1] // tm).reshape(1)
    return row_tok, dest.reshape(n_tok, MOE_TOP_K), top_w, blk_exp.astype(jnp.int32), n_used.astype(jnp.int32)


def _rope_split(wcols):
    d, w = wcols.shape
    half = w // HEADS // 2
    return wcols.reshape(d, HEADS, 2, half).transpose(0, 2, 1, 3).reshape(d, w)


def _inproj_weights(w_in):
    d = w_in.shape[0]
    kv = (BW, BW, BW, BW, MLA_KV_RANK, MLA_ROPE)
    qs = (BW, BW, BW, BW, BW, MLA_Q_RANK, 4 * d)
    offs = np.concatenate([[0], np.cumsum(kv + qs)])
    seg = lambda i: w_in[:, offs[i]:offs[i + 1]]
    r_k, r_v, n_k, n_v, m_ckv, m_kr = (seg(i) for i in range(6))
    r_q, r_gf, r_gb, f_in, n_q, m_cq, gate = (seg(6 + i) for i in range(7))
    z = lambda n: jnp.zeros((d, n), w_in.dtype)
    cols = [gate, _rope_split(r_q), _rope_split(r_k), r_v, r_gf, r_gb, f_in, n_q, n_k, n_v,
            m_cq, z(BW - MLA_Q_RANK), m_ckv, m_kr, z(LANES - MLA_ROPE)]
    return jnp.concatenate(cols, axis=1).astype(BF16)


def _ret_rope_tables(n):
    t = jnp.arange(n)
    row = (t // GRID_W).astype(F32)
    col = (t % GRID_W).astype(F32)
    nf = RET_DK // 4
    inv = ROPE_BASE ** (-jnp.arange(nf, dtype=F32) / nf)
    ang = jnp.concatenate([row[:, None] * inv, col[:, None] * inv], axis=-1)
    return jnp.tile(jnp.cos(ang), (1, HEADS)), jnp.tile(jnp.sin(ang), (1, HEADS))


def _mla_rope_tables(n):
    t = jnp.arange(n)
    row = (t // GRID_W).astype(F32)
    col = (t % GRID_W).astype(F32)
    nf = MLA_ROPE // 4
    inv = ROPE_BASE ** (-jnp.arange(nf, dtype=F32) / nf)
    ang = jnp.concatenate([row[:, None] * inv, col[:, None] * inv], axis=-1)
    pad = jnp.zeros((n, LANES - MLA_NOPE - MLA_ROPE), F32)
    cos = jnp.concatenate([jnp.ones((n, MLA_NOPE), F32), jnp.cos(ang), jnp.cos(ang), pad], axis=-1)
    sin = jnp.concatenate([jnp.zeros((n, MLA_NOPE), F32), jnp.sin(ang), jnp.sin(ang), pad], axis=-1)
    return cos, sin


def _tile_rows(*sizes):
    for tm in (1024, 512, 256, 128):
        if all(s % tm == 0 for s in sizes):
            return tm
    raise ValueError(f"token counts {sizes} need a common multiple-of-128 row tile")


def kernel(x, c, ctx, c_ctx, ada_w, ada_b, norm_mix, norm_ffn, w_in, ret_decay_fwd, ret_decay_bwd,
           mla_q_norm, mla_kv_norm, mla_w_uq, mla_w_ukv, na_rpb, w_branch, w_out,
           ffn_w_gate, ffn_w_up, ffn_w_down, moe_router, moe_w_gate, moe_w_up, moe_w_down, norm_final):
    batch, t, d = x.shape
    tc = ctx.shape[1]
    depth = ada_w.shape[0]
    nl, ncx = batch * t, batch * tc
    assert batch < 8 and t % (16 * GRID_W) == 0 and tc % LANES == 0 and d == 4 * BW
    tm = _tile_rows(t, ncx)
    cb0 = 4 * d // BW

    xl = x.reshape(nl, d)
    xc = ctx.reshape(ncx, d)
    cc = jnp.zeros((8, d), F32).at[:batch].set(c).at[batch].set(c_ctx)
    mods = adaln(cc, ada_w, ada_b).reshape(depth, 8 * 6, 1, d)
    lat_row = lambda r0: r0 // t
    ctx_row = lambda r0: batch

    ret_cos, ret_sin = _ret_rope_tables(t)
    ret_cos_c, ret_sin_c = jnp.ones((tc, LANES), F32), jnp.zeros((tc, LANES), F32)
    mla_cos, mla_sin = _mla_rope_tables(t)
    mla_cos_c = jnp.concatenate([jnp.ones((tm, MLA_NOPE + MLA_ROPE), F32),
                                 jnp.zeros((tm, LANES - MLA_NOPE - MLA_ROPE), F32)], axis=-1)
    mla_sin_c = jnp.zeros((tm, LANES), F32)
    ret_cs = 256

    for i in range(depth):
        ctx_out = i < depth - 1
        md = mods[i]
        w_p = _inproj_weights(w_in[i])
        u = norm_inproj(xl, norm_mix[i], md, lat_row, w_p, tm, 2304)
        uc = norm_inproj(xc, norm_mix[i], md, ctx_row, w_p, tm, 2304)

        lg = jnp.stack([jax.nn.log_sigmoid(ret_decay_fwd[i].astype(F32)),
                        jax.nn.log_sigmoid(ret_decay_bwd[i].astype(F32))])
        zero_state = jnp.zeros((batch, 2, BW, BW), F32)
        yc_ret, s_ctx = retention(uc, cb0, lg, ret_cos_c, ret_sin_c, zero_state, batch, min(ret_cs, tc))
        y_ret, _ = retention(u, cb0, lg, ret_cos, ret_sin, s_ctx, batch, ret_cs)

        y_four = fourier_long(u, cb0 + CB_F, batch)

        bias_tab = _na_bias_table(na_rpb[i])
        y_na = na_attention(u, uc, cb0, bias_tab, batch, 16)

        mw = _mla_weights(mla_w_uq[i], mla_w_ukv[i])
        q_l, k_l, v_l = mla_prep(u, cb0, mla_cos, mla_sin, mla_q_norm[i], mla_kv_norm[i], mw, tm, t // tm)
        q_c, k_c, v_c = mla_prep(uc, cb0, mla_cos_c, mla_sin_c, mla_q_norm[i], mla_kv_norm[i], mw, tm, 1)
        y_mla = flash_attention(q_l, [(k_l, v_l), (k_c, v_c)], batch, 1024, 512)

        wb = w_branch[i].astype(BF16)
        wb_mla = jnp.concatenate(
            [wb[3].reshape(HEADS, MLA_V, d), jnp.zeros((HEADS, LANES - MLA_V, d), BF16)], axis=1
        ).reshape(HEADS * LANES, d)
        wbs = (wb[0], wb[1], wb[2], wb_mla)
        wo = w_out[i].astype(BF16)
        j = i // 2
        if i % 2 == 0:
            xl, h2 = merge(u, y_ret, y_four, y_na, y_mla, xl, md, lat_row, norm_ffn[i], wbs, wo, None, 512)
            wg, wu, wd = ffn_w_gate[j].astype(BF16), ffn_w_up[j].astype(BF16), ffn_w_down[j].astype(BF16)
            tf = wg.shape[1] // 2
            xl = ffn(h2, xl, md, lat_row, wg, wu, wd, 512, tf)
        else:
            n_exp = moe_router.shape[2]
            r = jnp.pad(moe_router[j], ((0, 0), (0, LANES - n_exp)))
            r_hi = r.astype(BF16)
            r_lo = (r - r_hi.astype(F32)).astype(BF16)
            xl, h2, logits = merge(u, y_ret, y_four, y_na, y_mla, xl, md, lat_row, norm_ffn[i], wbs, wo,
                                   (r_hi, r_lo), 512)
            tmm = 512
            row_tok, dest, top_w, blk_exp, n_used = moe_route(logits, n_exp, tmm)
            xb = jnp.take(h2, row_tok, axis=0)
            yb = moe_ffn(blk_exp, n_used, xb, moe_w_gate[j].astype(BF16), moe_w_up[j].astype(BF16),
                         moe_w_down[j].astype(BF16), tmm, 1792)
            y2 = jnp.take(yb, dest.T.reshape(-1), axis=0)
            xl = moe_combine(xl, y2, top_w, md, lat_row, norm_final, i == depth - 1, 512)

        if ctx_out:
            yc_four = fourier_short(uc, cb0 + CB_F, batch)
            yc_na = na_ctx_attention(uc, cb0, batch)
            yc_mla = flash_attention(q_c, [(k_c, v_c)], batch, tc, 512)
            if i % 2 == 0:
                xc, hc2 = merge(uc, yc_ret, yc_four, yc_na, yc_mla, xc, md, ctx_row, norm_ffn[i], wbs, wo,
                                None, 512)
                xc = ffn(hc2, xc, md, ctx_row, wg, wu, wd, 512, tf)
            else:
                raise NotImplementedError("context tokens through the expert mixer")

    if depth % 2 == 1:
        xl = rmsnorm_rows(xl, norm_final, tm)
    return xl.reshape(batch, t, d)
```

```python
import functools

import numpy as np
import jax
import jax.numpy as jnp
from jax import lax
from jax.experimental import pallas as pl
from jax.experimental.pallas import tpu as pltpu

F32 = jnp.float32
BF16 = jnp.bfloat16

GRID_W = 64
ROPE_BASE = 10000.0
NORM_EPS = 1e-6
HEADS = 4
RET_DK = 64
FOURIER_GROUP_DIM = 64
NA_HEAD_DIM = 64
NA_WIN_R = 8
NA_WIN_C = 16
MLA_NOPE = 64
MLA_ROPE = 32
MLA_V = 64
MLA_Q_RANK = 192
MLA_KV_RANK = 128
MOE_TOP_K = 2
BW = 256

COL_GATE = 0
CB_RQ, CB_RK, CB_RV, CB_RGF, CB_RGB, CB_F, CB_NQ, CB_NK, CB_NV, CB_MCQ, CB_MKV = range(11)
LANES = 128
NEG = -1e30

VMEM_LIMIT = 48 * 1024 * 1024


def _cp(sem, vmem=VMEM_LIMIT):
    return pltpu.CompilerParams(dimension_semantics=sem, vmem_limit_bytes=vmem)


def _dot(a, b):
    return jnp.dot(a, b, preferred_element_type=F32)


def _dot_nt(a, b):
    return lax.dot_general(a, b, (((1,), (1,)), ((), ())), preferred_element_type=F32)


def _dot_tn(a, b):
    return lax.dot_general(a, b, (((0,), (0,)), ((), ())), preferred_element_type=F32)


def _silu(x):
    return x * jax.nn.sigmoid(x)


def _adaln_kernel(c_ref, w_ref, b_ref, o_ref):
    s = _silu(c_ref[...])
    o_ref[0] = _dot(s.astype(BF16), w_ref[0].astype(BF16)) + b_ref[0]


def adaln(cc, ada_w, ada_b):
    depth, d, n6 = ada_w.shape
    tn = n6 // 4
    return pl.pallas_call(
        _adaln_kernel,
        grid=(depth, n6 // tn),
        in_specs=[
            pl.BlockSpec((8, d), lambda l, j: (0, 0)),
            pl.BlockSpec((1, d, tn), lambda l, j: (l, 0, j)),
            pl.BlockSpec((1, 1, tn), lambda l, j: (l, 0, j)),
        ],
        out_specs=pl.BlockSpec((1, 8, tn), lambda l, j: (l, 0, j)),
        out_shape=jax.ShapeDtypeStruct((depth, 8, n6), F32),
        compiler_params=_cp(("arbitrary", "arbitrary")),
        name="adaln",
    )(cc, ada_w, ada_b.reshape(depth, 1, n6))


def _inproj_kernel(x_ref, g_ref, sc_ref, sh_ref, w_ref, o_ref, h_ref):
    @pl.when(pl.program_id(1) == 0)
    def _():
        x = x_ref[...]
        y = x * lax.rsqrt(jnp.mean(x * x, axis=-1, keepdims=True) + NORM_EPS)
        h = (y * g_ref[...]) * (1.0 + sc_ref[0]) + sh_ref[0]
        h_ref[...] = h.astype(BF16)

    o_ref[...] = _dot(h_ref[...], w_ref[...]).astype(o_ref.dtype)


def _mod_spec(d, mod_row, tm, k):
    return pl.BlockSpec((1, 1, d), lambda i, *_: (mod_row(i * tm) * 6 + k, 0, 0))


def norm_inproj(x, gain, mods, mod_row, w, tm, tn):
    n, d = x.shape
    nc = w.shape[1]
    return pl.pallas_call(
        _inproj_kernel,
        grid=(n // tm, nc // tn),
        in_specs=[
            pl.BlockSpec((tm, d), lambda i, j: (i, 0)),
            pl.BlockSpec((1, d), lambda i, j: (0, 0)),
            _mod_spec(d, mod_row, tm, 1),
            _mod_spec(d, mod_row, tm, 0),
            pl.BlockSpec((d, tn), lambda i, j: (0, j)),
        ],
        out_specs=pl.BlockSpec((tm, tn), lambda i, j: (i, j)),
        out_shape=jax.ShapeDtypeStruct((n, nc), BF16),
        scratch_shapes=[pltpu.VMEM((tm, d), BF16)],
        compiler_params=_cp(("arbitrary", "arbitrary")),
        name="norm_inproj",
    )(x, gain.reshape(1, d), mods, mods, w)


def _ret_kernel(lg_ref, q_ref, k_ref, v_ref, g_ref, cos_ref, sin_ref, s0_ref, lgq_ref, lgv_ref,
                y_ref, sout_ref, s_ref, decay_ref, qw_ref, kw_ref, avg_ref, *, n_chunks):
    d = pl.program_id(1)
    c = pl.program_id(2)
    rev = d == 1
    cs = q_ref.shape[0]
    w = q_ref.shape[1]
    half = w // 2
    lgv = lgv_ref[0]

    @pl.when(c == 0)
    def _():
        s_ref[...] = s0_ref[0, 0]
        pos_i = lax.broadcasted_iota(jnp.int32, (cs, 1), 0).astype(F32)
        pos_j = lax.broadcasted_iota(jnp.int32, (1, cs), 1).astype(F32)
        p_i = jnp.where(rev, cs - 1.0 - pos_i, pos_i)
        p_j = jnp.where(rev, cs - 1.0 - pos_j, pos_j)
        diff = p_i - p_j
        for h in range(HEADS):
            decay_ref[h] = jnp.where(diff >= 0, jnp.exp(lg_ref[d, h] * jnp.maximum(diff, 0.0)), 0.0)
        lgq = lgq_ref[0]
        qw_ref[...] = jnp.exp(lgq * (p_i + 1.0))
        kw_ref[...] = jnp.exp(lgq * (cs - 1.0 - p_i))
        hd = w // HEADS
        gi = lax.broadcasted_iota(jnp.int32, (w, w), 0) // hd
        gj = lax.broadcasted_iota(jnp.int32, (w, w), 1) // hd
        avg_ref[...] = jnp.where(gi == gj, 1.0 / hd, 0.0).astype(BF16)

    cos = cos_ref[...]
    sin = sin_ref[...]

    def rope(t):
        t1, t2 = t[:, :half], t[:, half:]
        return jnp.concatenate([t1 * cos - t2 * sin, t2 * cos + t1 * sin], axis=-1)

    q = rope(q_ref[...].astype(F32))
    k = rope(k_ref[...].astype(F32)) * (RET_DK ** -0.5)
    vb = v_ref[...]

    lane = lax.broadcasted_iota(jnp.int32, (1, w), 1)
    head_q = (lane % half) // (half // HEADS)
    head_v = lane // (w // HEADS)

    s_prev = s_ref[...]
    o = _dot((q * qw_ref[...]).astype(BF16), s_prev.astype(BF16))
    qb = q.astype(BF16)
    kb = k.astype(BF16)
    zero_b = jnp.zeros_like(qb)
    for h in range(HEADS):
        a = _dot_nt(jnp.where(head_q == h, qb, zero_b), kb)
        oh = _dot((a * decay_ref[h]).astype(BF16), vb)
        o = o + jnp.where(head_v == h, oh, 0.0)

    ds = _dot_tn((k * kw_ref[...]).astype(BF16), vb)
    row_head = (lax.broadcasted_iota(jnp.int32, (w, 1), 0) % half) // (half // HEADS)
    s_new = s_prev * jnp.exp(lgv * float(cs)) + jnp.where(row_head == head_v, ds, 0.0)
    s_ref[...] = s_new

    @pl.when(c == n_chunks - 1)
    def _():
        sout_ref[0, 0] = s_new

    ms = _dot((o * o).astype(BF16), avg_ref[...])
    on = o * lax.rsqrt(ms + NORM_EPS)
    y_ref[0] = (_silu(g_ref[...].astype(F32)) * on).astype(y_ref.dtype)


def retention(u, cb0, lg, cos, sin, s0, batch, cs):
    n = u.shape[0]
    t = n // batch
    nch = t // cs
    w = BW
    half = w // 2
    lgq = jnp.tile(jnp.repeat(lg, half // HEADS, axis=1), (1, 2)).reshape(2, 1, w)
    lgv = jnp.repeat(lg, w // HEADS, axis=1).reshape(2, 1, w)

    def rows(b, d, c):
        return b * nch + jnp.where(d == 1, nch - 1 - c, c)

    def col(cb):
        return pl.BlockSpec((cs, w), lambda b, d, c: (rows(b, d, c), cb0 + cb))

    def tab():
        return pl.BlockSpec((cs, half), lambda b, d, c: (jnp.where(d == 1, nch - 1 - c, c), 0))

    y, s_out = pl.pallas_call(
        functools.partial(_ret_kernel, n_chunks=nch),
        grid=(batch, 2, nch),
        in_specs=[
            pl.BlockSpec(memory_space=pltpu.SMEM),
            col(CB_RQ), col(CB_RK), col(CB_RV),
            pl.BlockSpec((cs, w), lambda b, d, c: (rows(b, d, c), cb0 + CB_RGF + d)),
            tab(), tab(),
            pl.BlockSpec((1, 1, w, w), lambda b, d, c: (b, d, 0, 0)),
            pl.BlockSpec((1, 1, w), lambda b, d, c: (d, 0, 0)),
            pl.BlockSpec((1, 1, w), lambda b, d, c: (d, 0, 0)),
        ],
        out_specs=[
            pl.BlockSpec((1, cs, w), lambda b, d, c: (d, rows(b, d, c), 0)),
            pl.BlockSpec((1, 1, w, w), lambda b, d, c: (b, d, 0, 0)),
        ],
        out_shape=[
            jax.ShapeDtypeStruct((2, n, w), BF16),
            jax.ShapeDtypeStruct((batch, 2, w, w), F32),
        ],
        scratch_shapes=[pltpu.VMEM((w, w), F32), pltpu.VMEM((HEADS, cs, cs), F32),
                        pltpu.VMEM((cs, w), F32), pltpu.VMEM((cs, w), F32), pltpu.VMEM((w, w), BF16)],
        compiler_params=_cp(("arbitrary", "arbitrary", "arbitrary")),
        name="retention",
    )(lg, u, u, u, u, cos, sin, s0, lgq, lgv)
    return y, s_out


def _dft_tables(t, t1, t2):
    k1 = jnp.arange(t1, dtype=jnp.int32)
    a = jnp.arange(t1, dtype=jnp.int32)
    m = jnp.arange(t2, dtype=jnp.int32)
    ph1 = (k1[None, :, None] * (a[None, None, :] * t2 + m[:, None, None])) % t
    ang1 = ph1.astype(F32) * (2.0 * np.pi / t)
    ph2 = (m[:, None] * m[None, :]) % t2
    ang2 = ph2.astype(F32) * (2.0 * np.pi / t2)
    return (jnp.cos(ang1).astype(BF16), jnp.sin(ang1).astype(BF16),
            jnp.cos(ang2).astype(BF16), jnp.sin(ang2).astype(BF16))


def _channel_tables(width):
    ch = jnp.arange(width, dtype=jnp.int32)
    same = (ch[:, None] // FOURIER_GROUP_DIM) == (ch[None, :] // FOURIER_GROUP_DIM)
    ph = ((ch[:, None] % FOURIER_GROUP_DIM) * (ch[None, :] % FOURIER_GROUP_DIM)) % FOURIER_GROUP_DIM
    ang = ph.astype(F32) * (2.0 * np.pi / FOURIER_GROUP_DIM)
    return (jnp.where(same, jnp.cos(ang), 0.0).astype(BF16),
            jnp.where(same, jnp.sin(ang), 0.0).astype(BF16))


def _fourier_kernel(x_ref, c1_ref, s1_ref, c2_ref, s2_ref, cc_ref, sc_ref, o_ref, xa, yre, yim,
                    *, t1, t2, norm):
    xa[...] = x_ref[...].astype(F32)

    def stage1(m, carry):
        xs = xa[pl.ds(m, t1, stride=t2), :].astype(BF16)
        r0 = pl.multiple_of(m * t1, t1)
        yre[pl.ds(r0, t1), :] = _dot(c1_ref[m], xs)
        yim[pl.ds(r0, t1), :] = -_dot(s1_ref[m], xs)
        return carry

    lax.fori_loop(0, t2, stage1, 0, unroll=8)

    c2 = c2_ref[...]
    s2 = s2_ref[...]
    cc = cc_ref[...]
    sc = sc_ref[...]

    def stage2(k1, carry):
        yr = yre[pl.ds(k1, t2, stride=t1), :].astype(BF16)
        yi = yim[pl.ds(k1, t2, stride=t1), :].astype(BF16)
        zr = _dot(c2, yr) + _dot(s2, yi)
        zi = _dot(c2, yi) - _dot(s2, yr)
        out = (_dot(zr.astype(BF16), cc) + _dot(zi.astype(BF16), sc)) * norm
        xa[pl.ds(k1, t2, stride=t1), :] = out
        return carry

    lax.fori_loop(0, t1, stage2, 0, unroll=4)
    o_ref[...] = xa[...].astype(o_ref.dtype)


def fourier_long(u, cb, batch, t2=LANES):
    n = u.shape[0]
    t = n // batch
    t1 = t // t2
    c1, s1, c2, s2 = _dft_tables(t, t1, t2)
    cc, sc = _channel_tables(LANES)
    norm = float(1.0 / np.sqrt(t * FOURIER_GROUP_DIM))
    full = lambda shape: pl.BlockSpec(shape, lambda b, hh: (0,) * len(shape))
    return pl.pallas_call(
        functools.partial(_fourier_kernel, t1=t1, t2=t2, norm=norm),
        grid=(batch, BW // LANES),
        in_specs=[
            pl.BlockSpec((t, LANES), lambda b, hh: (b, cb * (BW // LANES) + hh)),
            full((t2, t1, t1)), full((t2, t1, t1)), full((t2, t2)), full((t2, t2)),
            full((LANES, LANES)), full((LANES, LANES)),
        ],
        out_specs=pl.BlockSpec((t, LANES), lambda b, hh: (b, hh)),
        out_shape=jax.ShapeDtypeStruct((n, BW), BF16),
        scratch_shapes=[pltpu.VMEM((t, LANES), F32)] * 3,
        compiler_params=_cp(("arbitrary", "arbitrary")),
        name="fourier",
    )(u, c1, s1, c2, s2, cc, sc)


def _fourier_small_kernel(x_ref, ct_ref, st_ref, cc_ref, sc_ref, o_ref, *, norm):
    x = x_ref[...]
    zr = _dot(ct_ref[...], x)
    zi = -_dot(st_ref[...], x)
    out = (_dot(zr.astype(BF16), cc_ref[...]) + _dot(zi.astype(BF16), sc_ref[...])) * norm
    o_ref[...] = out.astype(o_ref.dtype)


def fourier_short(u, cb, batch):
    n = u.shape[0]
    t = n // batch
    pos = jnp.arange(t, dtype=jnp.int32)
    ang = ((pos[:, None] * pos[None, :]) % t).astype(F32) * (2.0 * np.pi / t)
    ct, st = jnp.cos(ang).astype(BF16), jnp.sin(ang).astype(BF16)
    cc, sc = _channel_tables(BW)
    norm = float(1.0 / np.sqrt(t * FOURIER_GROUP_DIM))
    full = lambda shape: pl.BlockSpec(shape, lambda b: (0,) * len(shape))
    return pl.pallas_call(
        functools.partial(_fourier_small_kernel, norm=norm),
        grid=(batch,),
        in_specs=[pl.BlockSpec((t, BW), lambda b: (b, cb)), full((t, t)), full((t, t)),
                  full((BW, BW)), full((BW, BW))],
        out_specs=pl.BlockSpec((t, BW), lambda b: (b, 0)),
        out_shape=jax.ShapeDtypeStruct((n, BW), BF16),
        compiler_params=_cp(("arbitrary",)),
        name="fourier_ctx",
    )(u, ct, st, cc, sc)


def _na_bias_table(rpb):
    n_r, n_c = rpb.shape[1], rpb.shape[2]
    span = 2 * GRID_W
    left = GRID_W - NA_WIN_C
    vp = jnp.pad(rpb.astype(F32), ((0, 0), (0, 0), (left, span - n_c - left)))
    rep = jnp.broadcast_to(vp[:, :, None, :], (HEADS, n_r, GRID_W, span)).reshape(HEADS, n_r, GRID_W * span)
    skew = rep[..., :GRID_W * (span - 1)].reshape(HEADS, n_r, GRID_W, span - 1)
    toep = skew[..., GRID_W - 1:]
    qc = np.arange(GRID_W)[:, None]
    kc = np.arange(GRID_W)[None, :]
    start = np.clip(qc - NA_WIN_C // 2, 0, GRID_W - NA_WIN_C)
    valid = (kc >= start) & (kc < start + NA_WIN_C)
    toep = jnp.where(valid, toep, NEG)
    tab = jnp.stack([toep[:, NA_WIN_R - 1 - v:2 * NA_WIN_R - 1 - v] for v in range(NA_WIN_R)])
    tab = tab.transpose(0, 1, 3, 2, 4)
    return tab.reshape(NA_WIN_R, HEADS * GRID_W, NA_WIN_R * GRID_W)


def _na_kernel(q_ref, k_ref, v_ref, kc_ref, vc_ref, bias_ref, o_ref, *, rows_per_step, n_rows):
    i = pl.program_id(1)
    w = q_ref.shape[1]
    lane = lax.broadcasted_iota(jnp.int32, (1, w), 1)
    head = lane // (w // HEADS)
    scale = jnp.asarray(NA_HEAD_DIM ** -0.5, q_ref.dtype)
    kc = kc_ref[...]
    vc = vc_ref[...]
    win = NA_WIN_R * GRID_W

    def row(rl, carry):
        r = i * rows_per_step + rl
        rs = jnp.clip(r - NA_WIN_R // 2, 0, n_rows - NA_WIN_R)
        var = r - rs
        q0 = pl.multiple_of(rl * GRID_W, GRID_W)
        k0 = pl.multiple_of(rs * GRID_W, GRID_W)
        q = q_ref[pl.ds(q0, GRID_W), :] * scale
        kw = k_ref[pl.ds(k0, win), :]
        vw = v_ref[pl.ds(k0, win), :]
        zero_b = jnp.zeros_like(q)
        q4 = jnp.concatenate([jnp.where(head == h, q, zero_b) for h in range(HEADS)], axis=0)
        s_loc = _dot_nt(q4, kw) + bias_ref[var]
        s_ctx = _dot_nt(q4, kc)
        m = jnp.maximum(jnp.max(s_loc, axis=-1, keepdims=True), jnp.max(s_ctx, axis=-1, keepdims=True))
        p_loc = jnp.exp(s_loc - m)
        p_ctx = jnp.exp(s_ctx - m)
        l = jnp.sum(p_loc, axis=-1, keepdims=True) + jnp.sum(p_ctx, axis=-1, keepdims=True)
        pv = (_dot(p_loc.astype(BF16), vw) + _dot(p_ctx.astype(BF16), vc)) / l
        acc = jnp.zeros((GRID_W, w), F32)
        for h in range(HEADS):
            acc = acc + jnp.where(head == h, pv[h * GRID_W:(h + 1) * GRID_W], 0.0)
        o_ref[pl.ds(q0, GRID_W), :] = acc.astype(o_ref.dtype)
        return carry

    lax.fori_loop(0, rows_per_step, row, 0, unroll=2)


def na_attention(u, uc, cb0, bias_tab, batch, rows_per_step):
    n = u.shape[0]
    t = n // batch
    tc = uc.shape[0] // batch
    n_rows = t // GRID_W
    steps = n_rows // rows_per_step
    tq = rows_per_step * GRID_W
    return pl.pallas_call(
        functools.partial(_na_kernel, rows_per_step=rows_per_step, n_rows=n_rows),
        grid=(batch, steps),
        in_specs=[
            pl.BlockSpec((tq, BW), lambda b, i: (b * steps + i, cb0 + CB_NQ)),
            pl.BlockSpec((t, BW), lambda b, i: (b, cb0 + CB_NK)),
            pl.BlockSpec((t, BW), lambda b, i: (b, cb0 + CB_NV)),
            pl.BlockSpec((tc, BW), lambda b, i: (b, cb0 + CB_NK)),
            pl.BlockSpec((tc, BW), lambda b, i: (b, cb0 + CB_NV)),
            pl.BlockSpec(bias_tab.shape, lambda b, i: (0, 0, 0)),
        ],
        out_specs=pl.BlockSpec((tq, BW), lambda b, i: (b * steps + i, 0)),
        out_shape=jax.ShapeDtypeStruct((n, BW), BF16),
        compiler_params=_cp(("arbitrary", "arbitrary")),
        name="na_attention",
    )(u, u, u, uc, uc, bias_tab)


def _na_ctx_kernel(q_ref, k_ref, v_ref, o_ref):
    w = q_ref.shape[1]
    lane = lax.broadcasted_iota(jnp.int32, (1, w), 1)
    head = lane // (w // HEADS)
    scale = NA_HEAD_DIM ** -0.5
    q = q_ref[...]
    k = k_ref[...]
    v = v_ref[...]
    zero_b = jnp.zeros_like(q)
    acc = jnp.zeros(q.shape, F32)
    for h in range(HEADS):
        s = _dot_nt(jnp.where(head == h, q, zero_b), k) * scale
        p = jnp.exp(s - jnp.max(s, axis=-1, keepdims=True))
        l = jnp.sum(p, axis=-1, keepdims=True)
        acc = acc + jnp.where(head == h, _dot(p.astype(BF16), v) / l, 0.0)
    o_ref[...] = acc.astype(o_ref.dtype)


def na_ctx_attention(uc, cb0, batch):
    tc = uc.shape[0] // batch
    spec = lambda cb: pl.BlockSpec((tc, BW), lambda b: (b, cb0 + cb))
    return pl.pallas_call(
        _na_ctx_kernel,
        grid=(batch,),
        in_specs=[spec(CB_NQ), spec(CB_NK), spec(CB_NV)],
        out_specs=pl.BlockSpec((tc, BW), lambda b: (b, 0)),
        out_shape=jax.ShapeDtypeStruct((uc.shape[0], BW), BF16),
        compiler_params=_cp(("arbitrary",)),
        name="na_ctx_attention",
    )(uc, uc, uc)


def _mla_prep_kernel(cq_ref, ckv_ref, kr_ref, cos_ref, sin_ref, qn_ref, kvn_ref, wq_ref, wqr_ref,
                     wk_ref, wv_ref, p1_ref, p2_ref, one_ref, q_ref, k_ref, v_ref):
    cos = cos_ref[...]
    sin = sin_ref[...]
    cos4 = jnp.concatenate([cos] * HEADS, axis=-1)
    sin4 = jnp.concatenate([sin] * HEADS, axis=-1)

    cq = cq_ref[...].astype(F32)
    ms = jnp.sum(cq * cq, axis=-1, keepdims=True) * (1.0 / MLA_Q_RANK)
    cqn = ((cq * lax.rsqrt(ms + NORM_EPS)) * qn_ref[...]).astype(BF16)
    q = _dot(cqn, wq_ref[...]) * cos4 + _dot(cqn, wqr_ref[...]) * sin4
    q_ref[...] = (q * float((MLA_NOPE + MLA_ROPE) ** -0.5 * np.log2(np.e))).astype(q_ref.dtype)

    ckv = ckv_ref[...].astype(F32)
    ms = jnp.mean(ckv * ckv, axis=-1, keepdims=True)
    ckvn = ((ckv * lax.rsqrt(ms + NORM_EPS)) * kvn_ref[...]).astype(BF16)
    kr = kr_ref[...]
    k_rot = _dot(kr, p1_ref[...]) * cos + _dot(kr, p2_ref[...]) * sin
    k = _dot(ckvn, wk_ref[...]) + jnp.concatenate([k_rot] * HEADS, axis=-1)
    k_ref[...] = k.astype(k_ref.dtype)
    v_ref[...] = (_dot(ckvn, wv_ref[...]) + one_ref[...]).astype(v_ref.dtype)


def _mla_weights(w_uq, w_ukv):
    qr = w_uq.shape[0]
    dq = MLA_NOPE + MLA_ROPE
    hr = MLA_ROPE // 2
    wq3 = w_uq.reshape(qr, HEADS, dq)
    zq = jnp.zeros((qr, HEADS, LANES - dq), F32)
    wq = jnp.concatenate([wq3, zq], axis=-1)
    x1 = wq3[..., MLA_NOPE:MLA_NOPE + hr]
    x2 = wq3[..., MLA_NOPE + hr:]
    wqr = jnp.concatenate([jnp.zeros((qr, HEADS, MLA_NOPE), F32), -x2, x1, zq], axis=-1)
    pad_rows = lambda m: jnp.pad(m.reshape(qr, HEADS * LANES), ((0, BW - qr), (0, 0)))
    kvr = w_ukv.shape[0]
    wkv3 = w_ukv.reshape(kvr, HEADS, MLA_NOPE + MLA_V)
    zk = jnp.zeros((kvr, HEADS, LANES - MLA_NOPE), F32)
    wk = jnp.concatenate([wkv3[..., :MLA_NOPE], zk], axis=-1).reshape(kvr, HEADS * LANES)
    wv = jnp.concatenate([wkv3[..., MLA_NOPE:], zk], axis=-1).reshape(kvr, HEADS * LANES)
    j = np.arange(hr)
    p1 = np.zeros((LANES, LANES), np.float32)
    p1[np.arange(MLA_ROPE), MLA_NOPE + np.arange(MLA_ROPE)] = 1.0
    p2 = np.zeros((LANES, LANES), np.float32)
    p2[hr + j, MLA_NOPE + j] = -1.0
    p2[j, MLA_NOPE + hr + j] = 1.0
    one = np.zeros((1, HEADS * LANES), np.float32)
    one[0, MLA_V + LANES * np.arange(HEADS)] = 1.0
    return (pad_rows(wq).astype(BF16), pad_rows(wqr).astype(BF16), wk.astype(BF16), wv.astype(BF16),
            jnp.asarray(p1, BF16), jnp.asarray(p2, BF16), jnp.asarray(one))


def mla_prep(u, cb0, cos, sin, q_norm, kv_norm, weights, tm, rope_blocks):
    n = u.shape[0]
    wq, wqr, wk, wv, p1, p2, one = weights
    qn = jnp.pad(q_norm, (0, BW - q_norm.shape[0])).reshape(1, BW)
    full = lambda a: pl.BlockSpec(a.shape, lambda i: (0,) * a.ndim)
    tab = pl.BlockSpec((tm, LANES), lambda i: (i % rope_blocks, 0))
    kv_cb = (cb0 + CB_MKV) * (BW // LANES)
    out = jax.ShapeDtypeStruct((n, HEADS * LANES), BF16)
    ospec = pl.BlockSpec((tm, HEADS * LANES), lambda i: (i, 0))
    return pl.pallas_call(
        _mla_prep_kernel,
        grid=(n // tm,),
        in_specs=[
            pl.BlockSpec((tm, BW), lambda i: (i, cb0 + CB_MCQ)),
            pl.BlockSpec((tm, LANES), lambda i: (i, kv_cb)),
            pl.BlockSpec((tm, LANES), lambda i: (i, kv_cb + 1)),
            tab, tab, full(qn), pl.BlockSpec((1, LANES), lambda i: (0, 0)),
            full(wq), full(wqr), full(wk), full(wv), full(p1), full(p2), full(one),
        ],
        out_specs=[ospec, ospec, ospec],
        out_shape=[out, out, out],
        compiler_params=_cp(("arbitrary",)),
        name="mla_prep",
    )(u, u, u, cos, sin, qn, kv_norm.reshape(1, LANES), wq, wqr, wk, wv, p1, p2, one)


def _flash_kernel(*refs, lens, tk):
    q_ref = refs[0]
    kv_refs = refs[1:1 + 2 * len(lens)]
    o_ref = refs[1 + 2 * len(lens)]
    q = q_ref[...]
    tq = q.shape[0]
    m = jnp.full((tq, 1), NEG, F32)
    acc = jnp.zeros((tq, LANES), F32)

    def chunk(kc, vc, m, acc):
        s = _dot_nt(q, kc)
        m_new = jnp.maximum(m, jnp.max(s, axis=-1, keepdims=True))
        p = jnp.exp2((s - m_new).astype(BF16))
        acc = jnp.exp2(m - m_new) * acc + _dot(p, vc)
        return m_new, acc

    for si, length in enumerate(lens):
        k_ref, v_ref = kv_refs[2 * si], kv_refs[2 * si + 1]
        step = min(tk, length)
        if length == step:
            m, acc = chunk(k_ref[...], v_ref[...], m, acc)
        else:
            def body(j, carry, k_ref=k_ref, v_ref=v_ref, step=step):
                j0 = pl.multiple_of(j * step, step)
                return chunk(k_ref[pl.ds(j0, step), :], v_ref[pl.ds(j0, step), :], *carry)

            m, acc = lax.fori_loop(0, length // step, body, (m, acc), unroll=4)

    lane = lax.broadcasted_iota(jnp.int32, (1, LANES), 1)
    l = jnp.sum(jnp.where(lane == MLA_V, acc, 0.0), axis=-1, keepdims=True)
    o_ref[...] = jnp.where(lane < MLA_V, acc / l, 0.0).astype(o_ref.dtype)


def flash_attention(q, kvs, batch, tq, tk):
    n = q.shape[0]
    nq = n // batch // tq
    lens = tuple(k.shape[0] // batch for k, _ in kvs)
    in_specs = [pl.BlockSpec((tq, LANES), lambda b, h, i: (b * nq + i, h))]
    args = [q]
    for (k, v), length in zip(kvs, lens):
        in_specs += [pl.BlockSpec((length, LANES), lambda b, h, i: (b, h))] * 2
        args += [k, v]
    return pl.pallas_call(
        functools.partial(_flash_kernel, lens=lens, tk=tk),
        grid=(batch, HEADS, nq),
        in_specs=in_specs,
        out_specs=pl.BlockSpec((tq, LANES), lambda b, h, i: (b * nq + i, h)),
        out_shape=jax.ShapeDtypeStruct((n, HEADS * LANES), BF16),
        compiler_params=_cp(("arbitrary", "arbitrary", "arbitrary")),
        name="mla_attention",
    )(*args)


def _merge_kernel(*refs, with_router):
    (gate_ref, yr_f_ref, yr_b_ref, yf_ref, yn_ref, ym_ref, x_ref, g1_ref, wb_ret_ref, wb_f_ref,
     wb_na_ref, wb_mla_ref, wo_ref, gain_ref, sc_ref, sh_ref) = refs[:16]
    rest = refs[16:]
    if with_router:
        rh_ref, rl_ref, x_out_ref, h_out_ref, lg_out_ref = rest
    else:
        x_out_ref, h_out_ref = rest
    d = x_ref.shape[1]

    def gated(k, y):
        g = jax.nn.sigmoid(gate_ref[:, k * d:(k + 1) * d].astype(F32))
        return g * y

    m = gated(0, _dot(yr_f_ref[0] + yr_b_ref[0], wb_ret_ref[...]))
    m = m + gated(1, _dot(yf_ref[...], wb_f_ref[...]))
    m = m + gated(2, _dot(yn_ref[...], wb_na_ref[...]))
    m = m + gated(3, _dot(ym_ref[...], wb_mla_ref[...]))
    y = _dot(m.astype(BF16), wo_ref[...])
    x = x_ref[...] + g1_ref[0] * y
    x_out_ref[...] = x
    hn = x * lax.rsqrt(jnp.mean(x * x, axis=-1, keepdims=True) + NORM_EPS)
    h = (hn * gain_ref[...]) * (1.0 + sc_ref[0]) + sh_ref[0]
    h_out_ref[...] = h.astype(h_out_ref.dtype)
    if with_router:
        h_hi = h.astype(BF16)
        h_lo = (h - h_hi.astype(F32)).astype(BF16)
        lg_out_ref[...] = (_dot(h_hi, rh_ref[...]) + _dot(h_lo, rh_ref[...])) + _dot(h_hi, rl_ref[...])


def merge(u, y_ret, y_four, y_na, y_mla, x, mods, mod_row, gain, wb, w_out, router, tm):
    n, d = x.shape
    wb_ret, wb_f, wb_na, wb_mla = wb
    full = lambda a: pl.BlockSpec(a.shape, lambda i: (0,) * a.ndim)
    br = lambda: pl.BlockSpec((tm, BW), lambda i: (i, 0))
    tok = lambda: pl.BlockSpec((tm, d), lambda i: (i, 0))
    in_specs = [
        pl.BlockSpec((tm, 4 * d), lambda i: (i, 0)),
        pl.BlockSpec((1, tm, BW), lambda i: (0, i, 0)),
        pl.BlockSpec((1, tm, BW), lambda i: (1, i, 0)),
        br(), br(),
        pl.BlockSpec((tm, HEADS * LANES), lambda i: (i, 0)),
        tok(), _mod_spec(d, mod_row, tm, 2),
        full(wb_ret), full(wb_f), full(wb_na), full(wb_mla), full(w_out),
        pl.BlockSpec((1, d), lambda i: (0, 0)), _mod_spec(d, mod_row, tm, 4), _mod_spec(d, mod_row, tm, 3),
    ]
    args = [u, y_ret, y_ret, y_four, y_na, y_mla, x, mods, wb_ret, wb_f, wb_na, wb_mla, w_out,
            gain.reshape(1, d), mods, mods]
    out_specs = [tok(), tok()]
    out_shape = [jax.ShapeDtypeStruct((n, d), F32), jax.ShapeDtypeStruct((n, d), BF16)]
    if router is not None:
        in_specs += [full(router[0]), full(router[1])]
        args += list(router)
        out_specs.append(pl.BlockSpec((tm, LANES), lambda i: (i, 0)))
        out_shape.append(jax.ShapeDtypeStruct((n, LANES), F32))
    return pl.pallas_call(
        functools.partial(_merge_kernel, with_router=router is not None),
        grid=(n // tm,),
        in_specs=in_specs,
        out_specs=out_specs,
        out_shape=out_shape,
        compiler_params=_cp(("arbitrary",)),
        name="merge",
    )(*args)


def _ffn_kernel(h_ref, wg_ref, wu_ref, wd_ref, x_ref, g2_ref, o_ref, acc_ref):
    f = pl.program_id(1)

    @pl.when(f == 0)
    def _():
        acc_ref[...] = jnp.zeros_like(acc_ref)

    h = h_ref[...]
    a = _silu(_dot(h, wg_ref[...])) * _dot(h, wu_ref[...])
    acc_ref[...] += _dot(a.astype(BF16), wd_ref[...])

    @pl.when(f == pl.num_programs(1) - 1)
    def _():
        o_ref[...] = x_ref[...] + g2_ref[0] * acc_ref[...]


def ffn(h, x, mods, mod_row, wg, wu, wd, tm, tf):
    n, d = x.shape
    nf = wg.shape[1] // tf
    return pl.pallas_call(
        _ffn_kernel,
        grid=(n // tm, nf),
        in_specs=[
            pl.BlockSpec((tm, d), lambda i, f: (i, 0)),
            pl.BlockSpec((d, tf), lambda i, f: (0, f)),
            pl.BlockSpec((d, tf), lambda i, f: (0, f)),
            pl.BlockSpec((tf, d), lambda i, f: (f, 0)),
            pl.BlockSpec((tm, d), lambda i, f: (i, 0)),
            _mod_spec(d, mod_row, tm, 5),
        ],
        out_specs=pl.BlockSpec((tm, d), lambda i, f: (i, 0)),
        out_shape=jax.ShapeDtypeStruct((n, d), F32),
        scratch_shapes=[pltpu.VMEM((tm, d), F32)],
        compiler_params=_cp(("arbitrary", "arbitrary")),
        name="ffn",
    )(h, wg, wu, wd, x, mods)


def _moe_kernel(be_ref, nu_ref, x_ref, wg_ref, wu_ref, wd_ref, o_ref, acc_ref):
    i = pl.program_id(0)
    f = pl.program_id(1)
    used = i < nu_ref[0]

    @pl.when(f == 0)
    def _():
        acc_ref[...] = jnp.zeros_like(acc_ref)

    @pl.when(used)
    def _():
        x = x_ref[...]
        a = _silu(_dot(x, wg_ref[0])) * _dot(x, wu_ref[0])
        acc_ref[...] += _dot(a.astype(BF16), wd_ref[0])

    @pl.when(f == pl.num_programs(1) - 1)
    def _():
        o_ref[...] = acc_ref[...].astype(o_ref.dtype)


def _cast_kernel(*refs):
    n = len(refs) // 2
    for src, dst in zip(refs[:n], refs[n:]):
        dst[...] = src[...].astype(dst.dtype)


def cast_expert_weights(ws, splits):
    spec = lambda w: pl.BlockSpec((1, w.shape[1] // splits, w.shape[2]), lambda e, r: (e, r, 0))
    return pl.pallas_call(
        _cast_kernel,
        grid=(ws[0].shape[0], splits),
        in_specs=[spec(w) for w in ws],
        out_specs=[spec(w) for w in ws],
        out_shape=[jax.ShapeDtypeStruct(w.shape, BF16) for w in ws],
        compiler_params=_cp(("arbitrary", "arbitrary")),
        name="cast_expert_weights",
    )(*ws)


def moe_ffn(blk_exp, n_used, xb, wg, wu, wd, tm, tf):
    n, d = xb.shape
    nf = wg.shape[2] // tf
    grid_spec = pltpu.PrefetchScalarGridSpec(
        num_scalar_prefetch=2,
        grid=(n // tm, nf),
        in_specs=[
            pl.BlockSpec((tm, d), lambda i, f, be, nu: (i, 0)),
            pl.BlockSpec((1, d, tf), lambda i, f, be, nu: (be[i], 0, f)),
            pl.BlockSpec((1, d, tf), lambda i, f, be, nu: (be[i], 0, f)),
            pl.BlockSpec((1, tf, d), lambda i, f, be, nu: (be[i], f, 0)),
        ],
        out_specs=pl.BlockSpec((tm, d), lambda i, f, be, nu: (i, 0)),
        scratch_shapes=[pltpu.VMEM((tm, d), F32)],
    )
    return pl.pallas_call(
        _moe_kernel,
        grid_spec=grid_spec,
        out_shape=jax.ShapeDtypeStruct((n, d), BF16),
        compiler_params=_cp(("arbitrary", "arbitrary")),
        name="moe_ffn",
    )(blk_exp, n_used, xb, wg, wu, wd)


def _combine_kernel(x_ref, ya_ref, yb_ref, w_ref, g2_ref, gain_ref, o_ref, *, final):
    w = w_ref[...]
    y = w[:, 0:1] * ya_ref[...].astype(F32) + w[:, 1:2] * yb_ref[...].astype(F32)
    x = x_ref[...] + g2_ref[0] * y
    if final:
        x = (x * lax.rsqrt(jnp.mean(x * x, axis=-1, keepdims=True) + NORM_EPS)) * gain_ref[...]
    o_ref[...] = x


def moe_combine(x, y2, w, mods, mod_row, gain, final, tm):
    n, d = x.shape
    tok = lambda: pl.BlockSpec((tm, d), lambda i: (i, 0))
    return pl.pallas_call(
        functools.partial(_combine_kernel, final=final),
        grid=(n // tm,),
        in_specs=[tok(), tok(), pl.BlockSpec((tm, d), lambda i: (i + n // tm, 0)),
                  pl.BlockSpec((tm, MOE_TOP_K), lambda i: (i, 0)),
                  _mod_spec(d, mod_row, tm, 5), pl.BlockSpec((1, d), lambda i: (0, 0))],
        out_specs=tok(),
        out_shape=jax.ShapeDtypeStruct((n, d), F32),
        compiler_params=_cp(("arbitrary",)),
        name="moe_combine",
    )(x, y2, y2, w, mods, gain.reshape(1, d))


def _rmsnorm_kernel(x_ref, gain_ref, o_ref):
    x = x_ref[...]
    o_ref[...] = (x * lax.rsqrt(jnp.mean(x * x, axis=-1, keepdims=True) + NORM_EPS)) * gain_ref[...]


def rmsnorm_rows(x, gain, tm):
    n, d = x.shape
    return pl.pallas_call(
        _rmsnorm_kernel,
        grid=(n // tm,),
        in_specs=[pl.BlockSpec((tm, d), lambda i: (i, 0)), pl.BlockSpec((1, d), lambda i: (0, 0))],
        out_specs=pl.BlockSpec((tm, d), lambda i: (i, 0)),
        out_shape=jax.ShapeDtypeStruct((n, d), F32),
        compiler_params=_cp(("arbitrary",)),
        name="final_norm",
    )(x, gain.reshape(1, d))


def moe_route(logits, n_experts, tm):
    n_tok = logits.shape[0]
    top_logit, top_idx = lax.top_k(logits[:, :n_experts], MOE_TOP_K)
    top_w = jax.nn.softmax(top_logit, axis=-1)
    e_flat = top_idx.reshape(-1).astype(jnp.int32)
    n_assign = e_flat.shape[0]
    onehot = (e_flat[:, None] == jnp.arange(n_experts, dtype=jnp.int32)[None, :]).astype(jnp.int32)
    rank = jnp.sum((jnp.cumsum(onehot, axis=0) - onehot) * onehot, axis=1)
    counts = jnp.sum(onehot, axis=0)
    padded = (counts + tm - 1) // tm * tm
    pad_end = jnp.cumsum(padded)
    pad_start = pad_end - padded
    dest = pad_start[e_flat] + rank
    n_rows = n_assign + n_experts * tm
    tok = jnp.arange(n_assign, dtype=jnp.int32) // MOE_TOP_K
    row_tok = jnp.zeros((n_rows,), jnp.int32).at[dest].set(tok)
    blk_start = jnp.arange(n_rows // tm, dtype=jnp.int32) * tm
    blk_exp = jnp.minimum(jnp.sum(pad_end[None, :] <= blk_start[:, None], axis=1), n_experts - 1)
    n_used = (pad_end[-1] // tm).reshape(1)
    return row_tok, dest.reshape(n_tok, MOE_TOP_K), top_w, blk_exp.astype(jnp.int32), n_used.astype(jnp.int32)


def _rope_split(wcols):
    d, w = wcols.shape
    half = w // HEADS // 2
    return wcols.reshape(d, HEADS, 2, half).transpose(0, 2, 1, 3).reshape(d, w)


def _inproj_weights(w_in):
    d = w_in.shape[0]
    kv = (BW, BW, BW, BW, MLA_KV_RANK, MLA_ROPE)
    qs = (BW, BW, BW, BW, BW, MLA_Q_RANK, 4 * d)
    offs = np.concatenate([[0], np.cumsum(kv + qs)])
    seg = lambda i: w_in[:, offs[i]:offs[i + 1]]
    r_k, r_v, n_k, n_v, m_ckv, m_kr = (seg(i) for i in range(6))
    r_q, r_gf, r_gb, f_in, n_q, m_cq, gate = (seg(6 + i) for i in range(7))
    z = lambda n: jnp.zeros((d, n), w_in.dtype)
    cols = [gate, _rope_split(r_q), _rope_split(r_k), r_v, r_gf, r_gb, f_in, n_q, n_k, n_v,
            m_cq, z(BW - MLA_Q_RANK), m_ckv, m_kr, z(LANES - MLA_ROPE)]
    return jnp.concatenate(cols, axis=1).astype(BF16)


def _ret_rope_tables(n):
    t = jnp.arange(n)
    row = (t // GRID_W).astype(F32)
    col = (t % GRID_W).astype(F32)
    nf = RET_DK // 4
    inv = ROPE_BASE ** (-jnp.arange(nf, dtype=F32) / nf)
    ang = jnp.concatenate([row[:, None] * inv, col[:, None] * inv], axis=-1)
    return jnp.tile(jnp.cos(ang), (1, HEADS)), jnp.tile(jnp.sin(ang), (1, HEADS))


def _mla_rope_tables(n):
    t = jnp.arange(n)
    row = (t // GRID_W).astype(F32)
    col = (t % GRID_W).astype(F32)
    nf = MLA_ROPE // 4
    inv = ROPE_BASE ** (-jnp.arange(nf, dtype=F32) / nf)
    ang = jnp.concatenate([row[:, None] * inv, col[:, None] * inv], axis=-1)
    pad = jnp.zeros((n, LANES - MLA_NOPE - MLA_ROPE), F32)
    cos = jnp.concatenate([jnp.ones((n, MLA_NOPE), F32), jnp.cos(ang), jnp.cos(ang), pad], axis=-1)
    sin = jnp.concatenate([jnp.zeros((n, MLA_NOPE), F32), jnp.sin(ang), jnp.sin(ang), pad], axis=-1)
    return cos, sin


def _tile_rows(*sizes):
    for tm in (1024, 512, 256, 128):
        if all(s % tm == 0 for s in sizes):
            return tm
    raise ValueError(f"token counts {sizes} need a common multiple-of-128 row tile")


def kernel(x, c, ctx, c_ctx, ada_w, ada_b, norm_mix, norm_ffn, w_in, ret_decay_fwd, ret_decay_bwd,
           mla_q_norm, mla_kv_norm, mla_w_uq, mla_w_ukv, na_rpb, w_branch, w_out,
           ffn_w_gate, ffn_w_up, ffn_w_down, moe_router, moe_w_gate, moe_w_up, moe_w_down, norm_final):
    batch, t, d = x.shape
    tc = ctx.shape[1]
    depth = ada_w.shape[0]
    nl, ncx = batch * t, batch * tc
    assert batch < 8 and t % (16 * GRID_W) == 0 and tc % LANES == 0 and d == 4 * BW
    tm = _tile_rows(t, ncx)
    cb0 = 4 * d // BW

    xl = x.reshape(nl, d)
    xc = ctx.reshape(ncx, d)
    cc = jnp.zeros((8, d), F32).at[:batch].set(c).at[batch].set(c_ctx)
    mods = adaln(cc, ada_w, ada_b).reshape(depth, 8 * 6, 1, d)
    lat_row = lambda r0: r0 // t
    ctx_row = lambda r0: batch

    ret_cos, ret_sin = _ret_rope_tables(t)
    ret_cos_c, ret_sin_c = jnp.ones((tc, LANES), F32), jnp.zeros((tc, LANES), F32)
    mla_cos, mla_sin = _mla_rope_tables(t)
    mla_cos_c = jnp.concatenate([jnp.ones((tm, MLA_NOPE + MLA_ROPE), F32),
                                 jnp.zeros((tm, LANES - MLA_NOPE - MLA_ROPE), F32)], axis=-1)
    mla_sin_c = jnp.zeros((tm, LANES), F32)
    ret_cs = 256

    for i in range(depth):
        ctx_out = i < depth - 1
        md = mods[i]
        w_p = _inproj_weights(w_in[i])
        u = norm_inproj(xl, norm_mix[i], md, lat_row, w_p, tm, 2304)
        uc = norm_inproj(xc, norm_mix[i], md, ctx_row, w_p, tm, 2304)

        lg = jnp.stack([jax.nn.log_sigmoid(ret_decay_fwd[i].astype(F32)),
                        jax.nn.log_sigmoid(ret_decay_bwd[i].astype(F32))])
        zero_state = jnp.zeros((batch, 2, BW, BW), F32)
        yc_ret, s_ctx = retention(uc, cb0, lg, ret_cos_c, ret_sin_c, zero_state, batch, min(ret_cs, tc))
        y_ret, _ = retention(u, cb0, lg, ret_cos, ret_sin, s_ctx, batch, ret_cs)

        y_four = fourier_long(u, cb0 + CB_F, batch)

        bias_tab = _na_bias_table(na_rpb[i])
        y_na = na_attention(u, uc, cb0, bias_tab, batch, 16)

        mw = _mla_weights(mla_w_uq[i], mla_w_ukv[i])
        q_l, k_l, v_l = mla_prep(u, cb0, mla_cos, mla_sin, mla_q_norm[i], mla_kv_norm[i], mw, tm, t // tm)
        q_c, k_c, v_c = mla_prep(uc, cb0, mla_cos_c, mla_sin_c, mla_q_norm[i], mla_kv_norm[i], mw, tm, 1)
        y_mla = flash_attention(q_l, [(k_l, v_l), (k_c, v_c)], batch, 1024, 512)

        wb = w_branch[i].astype(BF16)
        wb_mla = jnp.concatenate(
            [wb[3].reshape(HEADS, MLA_V, d), jnp.zeros((HEADS, LANES - MLA_V, d), BF16)], axis=1
        ).reshape(HEADS * LANES, d)
        wbs = (wb[0], wb[1], wb[2], wb_mla)
        wo = w_out[i].astype(BF16)
        j = i // 2
        if i % 2 == 0:
            xl, h2 = merge(u, y_ret, y_four, y_na, y_mla, xl, md, lat_row, norm_ffn[i], wbs, wo, None, 512)
            wg, wu, wd = ffn_w_gate[j].astype(BF16), ffn_w_up[j].astype(BF16), ffn_w_down[j].astype(BF16)
            tf = wg.shape[1] // 2
            xl = ffn(h2, xl, md, lat_row, wg, wu, wd, 512, tf)
        else:
            n_exp = moe_router.shape[2]
            r = jnp.pad(moe_router[j], ((0, 0), (0, LANES - n_exp)))
            r_hi = r.astype(BF16)
            r_lo = (r - r_hi.astype(F32)).astype(BF16)
            xl, h2, logits = merge(u, y_ret, y_four, y_na, y_mla, xl, md, lat_row, norm_ffn[i], wbs, wo,
                                   (r_hi, r_lo), 512)
            tmm = 512
            row_tok, dest, top_w, blk_exp, n_used = moe_route(logits, n_exp, tmm)
            xb = jnp.take(h2, row_tok, axis=0)
            ewg, ewu, ewd = cast_expert_weights([moe_w_gate[j], moe_w_up[j], moe_w_down[j]], 4)
            yb = moe_ffn(blk_exp, n_used, xb, ewg, ewu, ewd, tmm, 1792)
            y2 = jnp.take(yb, dest.T.reshape(-1), axis=0)
            xl = moe_combine(xl, y2, top_w, md, lat_row, norm_final, i == depth - 1, 512)

        if ctx_out:
            yc_four = fourier_short(uc, cb0 + CB_F, batch)
            yc_na = na_ctx_attention(uc, cb0, batch)
            yc_mla = flash_attention(q_c, [(k_c, v_c)], batch, tc, 512)
            if i % 2 == 0:
                xc, hc2 = merge(uc, yc_ret, yc_four, yc_na, yc_mla, xc, md, ctx_row, norm_ffn[i], wbs, wo,
                                None, 512)
                xc = ffn(hc2, xc, md, ctx_row, wg, wu, wd, 512, tf)
            else:
                raise NotImplementedError("context tokens through the expert mixer")

    if depth % 2 == 1:
        xl = rmsnorm_rows(xl, norm_final, tm)
    return xl.reshape(batch, t, d)
```

```python
import functools

import numpy as np
import jax
import jax.numpy as jnp
from jax import lax
from jax.experimental import pallas as pl
from jax.experimental.pallas import tpu as pltpu
from jax.experimental.pallas import tpu_sc as plsc

F32 = jnp.float32
BF16 = jnp.bfloat16

GRID_W = 64
ROPE_BASE = 10000.0
NORM_EPS = 1e-6
HEADS = 4
RET_DK = 64
FOURIER_GROUP_DIM = 64
NA_HEAD_DIM = 64
NA_WIN_R = 8
NA_WIN_C = 16
MLA_NOPE = 64
MLA_ROPE = 32
MLA_V = 64
MLA_Q_RANK = 192
MLA_KV_RANK = 128
MOE_TOP_K = 2
BW = 256

COL_GATE = 0
CB_RQ, CB_RK, CB_RV, CB_RGF, CB_RGB, CB_F, CB_NQ, CB_NK, CB_NV, CB_MCQ, CB_MKV = range(11)
LANES = 128
NEG = -1e30

VMEM_LIMIT = 48 * 1024 * 1024


def _cp(sem, vmem=VMEM_LIMIT):
    return pltpu.CompilerParams(dimension_semantics=sem, vmem_limit_bytes=vmem)


def _dot(a, b):
    return jnp.dot(a, b, preferred_element_type=F32)


def _dot_nt(a, b):
    return lax.dot_general(a, b, (((1,), (1,)), ((), ())), preferred_element_type=F32)


def _dot_tn(a, b):
    return lax.dot_general(a, b, (((0,), (0,)), ((), ())), preferred_element_type=F32)


def _silu(x):
    return x * jax.nn.sigmoid(x)


def _pack_bf16_pairs(x):
    k = x.shape[1] // 2
    lo = lax.bitcast_convert_type(x[:, :k].astype(BF16).astype(F32), jnp.uint32) >> 16
    hi = lax.bitcast_convert_type(x[:, k:].astype(BF16).astype(F32), jnp.uint32) & jnp.uint32(0xFFFF0000)
    return lax.bitcast_convert_type(lo | hi, jnp.int32)


def _unpack_bf16_pairs(p):
    u = lax.bitcast_convert_type(p, jnp.uint32)
    lo = lax.bitcast_convert_type(u << 16, F32)
    hi = lax.bitcast_convert_type(u & jnp.uint32(0xFFFF0000), F32)
    return jnp.concatenate([lo, hi], axis=-1)


def _adaln_kernel(c_ref, w_ref, b_ref, o_ref):
    s = _silu(c_ref[...])
    o_ref[0] = _dot(s.astype(BF16), w_ref[0].astype(BF16)) + b_ref[0]


def adaln(cc, ada_w, ada_b):
    depth, d, n6 = ada_w.shape
    tn = n6 // 4
    return pl.pallas_call(
        _adaln_kernel,
        grid=(depth, n6 // tn),
        in_specs=[
            pl.BlockSpec((8, d), lambda l, j: (0, 0)),
            pl.BlockSpec((1, d, tn), lambda l, j: (l, 0, j)),
            pl.BlockSpec((1, 1, tn), lambda l, j: (l, 0, j)),
        ],
        out_specs=pl.BlockSpec((1, 8, tn), lambda l, j: (l, 0, j)),
        out_shape=jax.ShapeDtypeStruct((depth, 8, n6), F32),
        compiler_params=_cp(("arbitrary", "arbitrary")),
        name="adaln",
    )(cc, ada_w, ada_b.reshape(depth, 1, n6))


def _inproj_kernel(x_ref, g_ref, sc_ref, sh_ref, w_ref, o_ref, h_ref):
    @pl.when(pl.program_id(1) == 0)
    def _():
        x = x_ref[...]
        y = x * lax.rsqrt(jnp.mean(x * x, axis=-1, keepdims=True) + NORM_EPS)
        h = (y * g_ref[...]) * (1.0 + sc_ref[0]) + sh_ref[0]
        h_ref[...] = h.astype(BF16)

    o_ref[...] = _dot(h_ref[...], w_ref[...]).astype(o_ref.dtype)


def _mod_spec(d, mod_row, tm, k):
    return pl.BlockSpec((1, 1, d), lambda i, *_: (mod_row(i * tm) * 6 + k, 0, 0))


def norm_inproj(x, gain, mods, mod_row, w, tm, tn):
    n, d = x.shape
    nc = w.shape[1]
    return pl.pallas_call(
        _inproj_kernel,
        grid=(n // tm, nc // tn),
        in_specs=[
            pl.BlockSpec((tm, d), lambda i, j: (i, 0)),
            pl.BlockSpec((1, d), lambda i, j: (0, 0)),
            _mod_spec(d, mod_row, tm, 1),
            _mod_spec(d, mod_row, tm, 0),
            pl.BlockSpec((d, tn), lambda i, j: (0, j)),
        ],
        out_specs=pl.BlockSpec((tm, tn), lambda i, j: (i, j)),
        out_shape=jax.ShapeDtypeStruct((n, nc), BF16),
        scratch_shapes=[pltpu.VMEM((tm, d), BF16)],
        compiler_params=_cp(("arbitrary", "arbitrary")),
        name="norm_inproj",
    )(x, gain.reshape(1, d), mods, mods, w)


def _ret_init(d, cs, w, lg_ref, s0_ref, lgq_ref, s_ref, decay_ref, qw_ref, kw_ref, avg_ref):
    rev = d == 1
    s_ref[d] = s0_ref[0, d]
    pos_i = lax.broadcasted_iota(jnp.int32, (cs, 1), 0).astype(F32)
    pos_j = lax.broadcasted_iota(jnp.int32, (1, cs), 1).astype(F32)
    p_i = jnp.where(rev, cs - 1.0 - pos_i, pos_i)
    p_j = jnp.where(rev, cs - 1.0 - pos_j, pos_j)
    diff = p_i - p_j
    for h in range(HEADS):
        decay_ref[d, h] = jnp.where(diff >= 0, jnp.exp(lg_ref[d, h] * jnp.maximum(diff, 0.0)), 0.0)
    lgq = lgq_ref[d]
    qw_ref[d] = jnp.exp(lgq * (p_i + 1.0))
    kw_ref[d] = jnp.exp(lgq * (cs - 1.0 - p_i))
    hd = w // HEADS
    gi = lax.broadcasted_iota(jnp.int32, (w, w), 0) // hd
    gj = lax.broadcasted_iota(jnp.int32, (w, w), 1) // hd
    avg_ref[...] = jnp.where(gi == gj, 1.0 / hd, 0.0).astype(BF16)


def _ret_chunk(d, q_ref, k_ref, v_ref, g_ref, cos_ref, sin_ref, lgv_ref, y_ref, s_ref, decay_ref, qw_ref, kw_ref,
               avg_ref):
    cs = q_ref.shape[0]
    w = q_ref.shape[1]
    half = w // 2
    lgv = lgv_ref[d]
    s_ref, decay_ref, qw_ref, kw_ref = s_ref.at[d], decay_ref.at[d], qw_ref.at[d], kw_ref.at[d]

    cos = cos_ref[...]
    sin = sin_ref[...]

    def rope(t):
        t1, t2 = t[:, :half], t[:, half:]
        return jnp.concatenate([t1 * cos - t2 * sin, t2 * cos + t1 * sin], axis=-1)

    q = rope(q_ref[...].astype(F32))
    k = rope(k_ref[...].astype(F32)) * (RET_DK ** -0.5)
    vb = v_ref[...]

    lane = lax.broadcasted_iota(jnp.int32, (1, w), 1)
    head_q = (lane % half) // (half // HEADS)
    head_v = lane // (w // HEADS)

    s_prev = s_ref[...]
    o = _dot((q * qw_ref[...]).astype(BF16), s_prev.astype(BF16))
    qb = q.astype(BF16)
    kb = k.astype(BF16)
    zero_b = jnp.zeros_like(qb)
    for h in range(HEADS):
        a = _dot_nt(jnp.where(head_q == h, qb, zero_b), kb)
        oh = _dot((a * decay_ref[h]).astype(BF16), vb)
        o = o + jnp.where(head_v == h, oh, 0.0)

    ds = _dot_tn((k * kw_ref[...]).astype(BF16), vb)
    row_head = (lax.broadcasted_iota(jnp.int32, (w, 1), 0) % half) // (half // HEADS)
    s_new = s_prev * jnp.exp(lgv * float(cs)) + jnp.where(row_head == head_v, ds, 0.0)
    s_ref[...] = s_new

    ms = _dot((o * o).astype(BF16), avg_ref[...])
    on = o * lax.rsqrt(ms + NORM_EPS)
    y_ref[...] = (_silu(g_ref[...].astype(F32)) * on).astype(y_ref.dtype)
    return s_new


def _ret_kernel(lg_ref, qf, kf, vf, gf, cosf, sinf, qb, kb, vb, gb, cosb, sinb, s0_ref, lgq_ref, lgv_ref,
                yf_ref, yb_ref, sout_ref, s_ref, decay_ref, qw_ref, kw_ref, avg_ref, *, n_chunks):
    c = pl.program_id(1)
    scratch = (s_ref, decay_ref, qw_ref, kw_ref, avg_ref)

    @pl.when(c == 0)
    def _():
        for d in range(2):
            _ret_init(d, qf.shape[0], qf.shape[1], lg_ref, s0_ref, lgq_ref, *scratch)

    s_f = _ret_chunk(0, qf, kf, vf, gf, cosf, sinf, lgv_ref, yf_ref, *scratch)
    s_b = _ret_chunk(1, qb, kb, vb, gb, cosb, sinb, lgv_ref, yb_ref, *scratch)

    @pl.when(c == n_chunks - 1)
    def _():
        sout_ref[0, 0] = s_f
        sout_ref[0, 1] = s_b


def retention(u, cb0, lg, cos, sin, s0, batch, cs):
    n = u.shape[0]
    t = n // batch
    nch = t // cs
    w = BW
    half = w // 2
    lgq = jnp.tile(jnp.repeat(lg, half // HEADS, axis=1), (1, 2)).reshape(2, 1, w)
    lgv = jnp.repeat(lg, w // HEADS, axis=1).reshape(2, 1, w)

    def chunk(d, c):
        return nch - 1 - c if d else c

    def direction(d):
        col = lambda cb: pl.BlockSpec((cs, w), lambda b, c: (b * nch + chunk(d, c), cb0 + cb))
        tab = lambda: pl.BlockSpec((cs, half), lambda b, c: (chunk(d, c), 0))
        return [col(CB_RQ), col(CB_RK), col(CB_RV), col(CB_RGF + d), tab(), tab()]

    y_spec = lambda d: pl.BlockSpec((cs, w), lambda b, c: (b * nch + chunk(d, c), 0))
    state_spec = pl.BlockSpec((1, 2, w, w), lambda b, c: (b, 0, 0, 0))
    lane_spec = pl.BlockSpec((2, 1, w), lambda b, c: (0, 0, 0))
    y_f, y_b, s_out = pl.pallas_call(
        functools.partial(_ret_kernel, n_chunks=nch),
        grid=(batch, nch),
        in_specs=[pl.BlockSpec(memory_space=pltpu.SMEM)] + direction(0) + direction(1)
        + [state_spec, lane_spec, lane_spec],
        out_specs=[y_spec(0), y_spec(1), state_spec],
        out_shape=[
            jax.ShapeDtypeStruct((n, w), BF16),
            jax.ShapeDtypeStruct((n, w), BF16),
            jax.ShapeDtypeStruct((batch, 2, w, w), F32),
        ],
        scratch_shapes=[pltpu.VMEM((2, w, w), F32), pltpu.VMEM((2, HEADS, cs, cs), F32),
                        pltpu.VMEM((2, cs, w), F32), pltpu.VMEM((2, cs, w), F32), pltpu.VMEM((w, w), BF16)],
        compiler_params=_cp(("arbitrary", "arbitrary")),
        name="retention",
    )(lg, u, u, u, u, cos, sin, u, u, u, u, cos, sin, s0, lgq, lgv)
    return (y_f, y_b), s_out


def _dft_tables(t, t1, t2):
    k1 = jnp.arange(t1, dtype=jnp.int32)
    a = jnp.arange(t1, dtype=jnp.int32)
    m = jnp.arange(t2, dtype=jnp.int32)
    ph1 = (k1[None, :, None] * (a[None, None, :] * t2 + m[:, None, None])) % t
    ang1 = ph1.astype(F32) * (2.0 * np.pi / t)
    ph2 = (m[:, None] * m[None, :]) % t2
    ang2 = ph2.astype(F32) * (2.0 * np.pi / t2)
    return (jnp.cos(ang1).astype(BF16), jnp.sin(ang1).astype(BF16),
            jnp.cos(ang2).astype(BF16), jnp.sin(ang2).astype(BF16))


def _channel_tables(width):
    ch = jnp.arange(width, dtype=jnp.int32)
    same = (ch[:, None] // FOURIER_GROUP_DIM) == (ch[None, :] // FOURIER_GROUP_DIM)
    ph = ((ch[:, None] % FOURIER_GROUP_DIM) * (ch[None, :] % FOURIER_GROUP_DIM)) % FOURIER_GROUP_DIM
    ang = ph.astype(F32) * (2.0 * np.pi / FOURIER_GROUP_DIM)
    return (jnp.where(same, jnp.cos(ang), 0.0).astype(BF16),
            jnp.where(same, jnp.sin(ang), 0.0).astype(BF16))


def _fourier_kernel(x_ref, c1_ref, s1_ref, c2_ref, s2_ref, cc_ref, sc_ref, o_ref, xa, yre, yim,
                    *, t1, t2, norm):
    xa[...] = x_ref[...].astype(F32)

    def stage1(m, carry):
        xs = xa[pl.ds(m, t1, stride=t2), :].astype(BF16)
        r0 = pl.multiple_of(m * t1, t1)
        yre[pl.ds(r0, t1), :] = _dot(c1_ref[m], xs)
        yim[pl.ds(r0, t1), :] = -_dot(s1_ref[m], xs)
        return carry

    lax.fori_loop(0, t2, stage1, 0, unroll=8)

    c2 = c2_ref[...]
    s2 = s2_ref[...]
    cc = cc_ref[...]
    sc = sc_ref[...]

    def stage2(k1, carry):
        yr = yre[pl.ds(k1, t2, stride=t1), :].astype(BF16)
        yi = yim[pl.ds(k1, t2, stride=t1), :].astype(BF16)
        zr = _dot(c2, yr) + _dot(s2, yi)
        zi = _dot(c2, yi) - _dot(s2, yr)
        out = (_dot(zr.astype(BF16), cc) + _dot(zi.astype(BF16), sc)) * norm
        xa[pl.ds(k1, t2, stride=t1), :] = out
        return carry

    lax.fori_loop(0, t1, stage2, 0, unroll=4)
    o_ref[...] = xa[...].astype(o_ref.dtype)


def fourier_long(u, cb, batch, t2=LANES):
    n = u.shape[0]
    t = n // batch
    t1 = t // t2
    c1, s1, c2, s2 = _dft_tables(t, t1, t2)
    cc, sc = _channel_tables(LANES)
    norm = float(1.0 / np.sqrt(t * FOURIER_GROUP_DIM))
    full = lambda shape: pl.BlockSpec(shape, lambda b, hh: (0,) * len(shape))
    return pl.pallas_call(
        functools.partial(_fourier_kernel, t1=t1, t2=t2, norm=norm),
        grid=(batch, BW // LANES),
        in_specs=[
            pl.BlockSpec((t, LANES), lambda b, hh: (b, cb * (BW // LANES) + hh)),
            full((t2, t1, t1)), full((t2, t1, t1)), full((t2, t2)), full((t2, t2)),
            full((LANES, LANES)), full((LANES, LANES)),
        ],
        out_specs=pl.BlockSpec((t, LANES), lambda b, hh: (b, hh)),
        out_shape=jax.ShapeDtypeStruct((n, BW), BF16),
        scratch_shapes=[pltpu.VMEM((t, LANES), F32)] * 3,
        compiler_params=_cp(("arbitrary", "arbitrary")),
        name="fourier",
    )(u, c1, s1, c2, s2, cc, sc)


def _fourier_small_kernel(x_ref, ct_ref, st_ref, cc_ref, sc_ref, o_ref, *, norm):
    x = x_ref[...]
    zr = _dot(ct_ref[...], x)
    zi = -_dot(st_ref[...], x)
    out = (_dot(zr.astype(BF16), cc_ref[...]) + _dot(zi.astype(BF16), sc_ref[...])) * norm
    o_ref[...] = out.astype(o_ref.dtype)


def fourier_short(u, cb, batch):
    n = u.shape[0]
    t = n // batch
    pos = jnp.arange(t, dtype=jnp.int32)
    ang = ((pos[:, None] * pos[None, :]) % t).astype(F32) * (2.0 * np.pi / t)
    ct, st = jnp.cos(ang).astype(BF16), jnp.sin(ang).astype(BF16)
    cc, sc = _channel_tables(BW)
    norm = float(1.0 / np.sqrt(t * FOURIER_GROUP_DIM))
    full = lambda shape: pl.BlockSpec(shape, lambda b: (0,) * len(shape))
    return pl.pallas_call(
        functools.partial(_fourier_small_kernel, norm=norm),
        grid=(batch,),
        in_specs=[pl.BlockSpec((t, BW), lambda b: (b, cb)), full((t, t)), full((t, t)),
                  full((BW, BW)), full((BW, BW))],
        out_specs=pl.BlockSpec((t, BW), lambda b: (b, 0)),
        out_shape=jax.ShapeDtypeStruct((n, BW), BF16),
        compiler_params=_cp(("arbitrary",)),
        name="fourier_ctx",
    )(u, ct, st, cc, sc)


def _na_bias_table(rpb):
    n_r, n_c = rpb.shape[1], rpb.shape[2]
    span = 2 * GRID_W
    left = GRID_W - NA_WIN_C
    vp = jnp.pad(rpb.astype(F32), ((0, 0), (0, 0), (left, span - n_c - left)))
    rep = jnp.broadcast_to(vp[:, :, None, :], (HEADS, n_r, GRID_W, span)).reshape(HEADS, n_r, GRID_W * span)
    skew = rep[..., :GRID_W * (span - 1)].reshape(HEADS, n_r, GRID_W, span - 1)
    toep = skew[..., GRID_W - 1:]
    qc = np.arange(GRID_W)[:, None]
    kc = np.arange(GRID_W)[None, :]
    start = np.clip(qc - NA_WIN_C // 2, 0, GRID_W - NA_WIN_C)
    valid = (kc >= start) & (kc < start + NA_WIN_C)
    toep = jnp.where(valid, toep, NEG)
    tab = jnp.stack([toep[:, NA_WIN_R - 1 - v:2 * NA_WIN_R - 1 - v] for v in range(NA_WIN_R)])
    tab = tab.transpose(0, 1, 3, 2, 4)
    return tab.reshape(NA_WIN_R, HEADS * GRID_W, NA_WIN_R * GRID_W)


def _na_kernel(q_ref, k_ref, v_ref, kc_ref, vc_ref, bias_ref, o_ref, *, rows_per_step, n_rows):
    i = pl.program_id(1)
    w = q_ref.shape[1]
    lane = lax.broadcasted_iota(jnp.int32, (1, w), 1)
    head = lane // (w // HEADS)
    scale = jnp.asarray(NA_HEAD_DIM ** -0.5, q_ref.dtype)
    kc = kc_ref[...]
    vc = vc_ref[...]
    win = NA_WIN_R * GRID_W

    def row(rl, carry):
        r = i * rows_per_step + rl
        rs = jnp.clip(r - NA_WIN_R // 2, 0, n_rows - NA_WIN_R)
        var = r - rs
        q0 = pl.multiple_of(rl * GRID_W, GRID_W)
        k0 = pl.multiple_of(rs * GRID_W, GRID_W)
        q = q_ref[pl.ds(q0, GRID_W), :] * scale
        kw = k_ref[pl.ds(k0, win), :]
        vw = v_ref[pl.ds(k0, win), :]
        zero_b = jnp.zeros_like(q)
        q4 = jnp.concatenate([jnp.where(head == h, q, zero_b) for h in range(HEADS)], axis=0)
        s_loc = _dot_nt(q4, kw) + bias_ref[var]
        s_ctx = _dot_nt(q4, kc)
        m = jnp.maximum(jnp.max(s_loc, axis=-1, keepdims=True), jnp.max(s_ctx, axis=-1, keepdims=True))
        p_loc = jnp.exp(s_loc - m)
        p_ctx = jnp.exp(s_ctx - m)
        l = jnp.sum(p_loc, axis=-1, keepdims=True) + jnp.sum(p_ctx, axis=-1, keepdims=True)
        pv = (_dot(p_loc.astype(BF16), vw) + _dot(p_ctx.astype(BF16), vc)) / l
        acc = jnp.zeros((GRID_W, w), F32)
        for h in range(HEADS):
            acc = acc + jnp.where(head == h, pv[h * GRID_W:(h + 1) * GRID_W], 0.0)
        o_ref[pl.ds(q0, GRID_W), :] = acc.astype(o_ref.dtype)
        return carry

    lax.fori_loop(0, rows_per_step, row, 0, unroll=2)


def na_attention(u, uc, cb0, bias_tab, batch, rows_per_step):
    n = u.shape[0]
    t = n // batch
    tc = uc.shape[0] // batch
    n_rows = t // GRID_W
    steps = n_rows // rows_per_step
    tq = rows_per_step * GRID_W
    return pl.pallas_call(
        functools.partial(_na_kernel, rows_per_step=rows_per_step, n_rows=n_rows),
        grid=(batch, steps),
        in_specs=[
            pl.BlockSpec((tq, BW), lambda b, i: (b * steps + i, cb0 + CB_NQ)),
            pl.BlockSpec((t, BW), lambda b, i: (b, cb0 + CB_NK)),
            pl.BlockSpec((t, BW), lambda b, i: (b, cb0 + CB_NV)),
            pl.BlockSpec((tc, BW), lambda b, i: (b, cb0 + CB_NK)),
            pl.BlockSpec((tc, BW), lambda b, i: (b, cb0 + CB_NV)),
            pl.BlockSpec(bias_tab.shape, lambda b, i: (0, 0, 0)),
        ],
        out_specs=pl.BlockSpec((tq, BW), lambda b, i: (b * steps + i, 0)),
        out_shape=jax.ShapeDtypeStruct((n, BW), BF16),
        compiler_params=_cp(("arbitrary", "arbitrary")),
        name="na_attention",
    )(u, u, u, uc, uc, bias_tab)


def _na_ctx_kernel(q_ref, k_ref, v_ref, o_ref):
    w = q_ref.shape[1]
    lane = lax.broadcasted_iota(jnp.int32, (1, w), 1)
    head = lane // (w // HEADS)
    scale = NA_HEAD_DIM ** -0.5
    q = q_ref[...]
    k = k_ref[...]
    v = v_ref[...]
    zero_b = jnp.zeros_like(q)
    acc = jnp.zeros(q.shape, F32)
    for h in range(HEADS):
        s = _dot_nt(jnp.where(head == h, q, zero_b), k) * scale
        p = jnp.exp(s - jnp.max(s, axis=-1, keepdims=True))
        l = jnp.sum(p, axis=-1, keepdims=True)
        acc = acc + jnp.where(head == h, _dot(p.astype(BF16), v) / l, 0.0)
    o_ref[...] = acc.astype(o_ref.dtype)


def na_ctx_attention(uc, cb0, batch):
    tc = uc.shape[0] // batch
    spec = lambda cb: pl.BlockSpec((tc, BW), lambda b: (b, cb0 + cb))
    return pl.pallas_call(
        _na_ctx_kernel,
        grid=(batch,),
        in_specs=[spec(CB_NQ), spec(CB_NK), spec(CB_NV)],
        out_specs=pl.BlockSpec((tc, BW), lambda b: (b, 0)),
        out_shape=jax.ShapeDtypeStruct((uc.shape[0], BW), BF16),
        compiler_params=_cp(("arbitrary",)),
        name="na_ctx_attention",
    )(uc, uc, uc)


def _mla_prep_kernel(cq_ref, ckv_ref, kr_ref, cos_ref, sin_ref, qn_ref, kvn_ref, wq_ref, wqr_ref,
                     wk_ref, wv_ref, p1_ref, p2_ref, one_ref, q_ref, k_ref, v_ref):
    cos = cos_ref[...]
    sin = sin_ref[...]
    cos4 = jnp.concatenate([cos] * HEADS, axis=-1)
    sin4 = jnp.concatenate([sin] * HEADS, axis=-1)

    cq = cq_ref[...].astype(F32)
    ms = jnp.sum(cq * cq, axis=-1, keepdims=True) * (1.0 / MLA_Q_RANK)
    cqn = ((cq * lax.rsqrt(ms + NORM_EPS)) * qn_ref[...]).astype(BF16)
    q = _dot(cqn, wq_ref[...]) * cos4 + _dot(cqn, wqr_ref[...]) * sin4
    q_ref[...] = (q * float((MLA_NOPE + MLA_ROPE) ** -0.5 * np.log2(np.e))).astype(q_ref.dtype)

    ckv = ckv_ref[...].astype(F32)
    ms = jnp.mean(ckv * ckv, axis=-1, keepdims=True)
    ckvn = ((ckv * lax.rsqrt(ms + NORM_EPS)) * kvn_ref[...]).astype(BF16)
    kr = kr_ref[...]
    k_rot = _dot(kr, p1_ref[...]) * cos + _dot(kr, p2_ref[...]) * sin
    k = _dot(ckvn, wk_ref[...]) + jnp.concatenate([k_rot] * HEADS, axis=-1)
    k_ref[...] = k.astype(k_ref.dtype)
    v_ref[...] = (_dot(ckvn, wv_ref[...]) + one_ref[...]).astype(v_ref.dtype)


def _mla_weights(w_uq, w_ukv):
    qr = w_uq.shape[0]
    dq = MLA_NOPE + MLA_ROPE
    hr = MLA_ROPE // 2
    wq3 = w_uq.reshape(qr, HEADS, dq)
    zq = jnp.zeros((qr, HEADS, LANES - dq), F32)
    wq = jnp.concatenate([wq3, zq], axis=-1)
    x1 = wq3[..., MLA_NOPE:MLA_NOPE + hr]
    x2 = wq3[..., MLA_NOPE + hr:]
    wqr = jnp.concatenate([jnp.zeros((qr, HEADS, MLA_NOPE), F32), -x2, x1, zq], axis=-1)
    pad_rows = lambda m: jnp.pad(m.reshape(qr, HEADS * LANES), ((0, BW - qr), (0, 0)))
    kvr = w_ukv.shape[0]
    wkv3 = w_ukv.reshape(kvr, HEADS, MLA_NOPE + MLA_V)
    zk = jnp.zeros((kvr, HEADS, LANES - MLA_NOPE), F32)
    wk = jnp.concatenate([wkv3[..., :MLA_NOPE], zk], axis=-1).reshape(kvr, HEADS * LANES)
    wv = jnp.concatenate([wkv3[..., MLA_NOPE:], zk], axis=-1).reshape(kvr, HEADS * LANES)
    j = np.arange(hr)
    p1 = np.zeros((LANES, LANES), np.float32)
    p1[np.arange(MLA_ROPE), MLA_NOPE + np.arange(MLA_ROPE)] = 1.0
    p2 = np.zeros((LANES, LANES), np.float32)
    p2[hr + j, MLA_NOPE + j] = -1.0
    p2[j, MLA_NOPE + hr + j] = 1.0
    one = np.zeros((1, HEADS * LANES), np.float32)
    one[0, MLA_V + LANES * np.arange(HEADS)] = 1.0
    return (pad_rows(wq).astype(BF16), pad_rows(wqr).astype(BF16), wk.astype(BF16), wv.astype(BF16),
            jnp.asarray(p1, BF16), jnp.asarray(p2, BF16), jnp.asarray(one))


def mla_prep(u, cb0, cos, sin, q_norm, kv_norm, weights, tm, rope_blocks):
    n = u.shape[0]
    wq, wqr, wk, wv, p1, p2, one = weights
    qn = jnp.pad(q_norm, (0, BW - q_norm.shape[0])).reshape(1, BW)
    full = lambda a: pl.BlockSpec(a.shape, lambda i: (0,) * a.ndim)
    tab = pl.BlockSpec((tm, LANES), lambda i: (i % rope_blocks, 0))
    kv_cb = (cb0 + CB_MKV) * (BW // LANES)
    out = jax.ShapeDtypeStruct((n, HEADS * LANES), BF16)
    ospec = pl.BlockSpec((tm, HEADS * LANES), lambda i: (i, 0))
    return pl.pallas_call(
        _mla_prep_kernel,
        grid=(n // tm,),
        in_specs=[
            pl.BlockSpec((tm, BW), lambda i: (i, cb0 + CB_MCQ)),
            pl.BlockSpec((tm, LANES), lambda i: (i, kv_cb)),
            pl.BlockSpec((tm, LANES), lambda i: (i, kv_cb + 1)),
            tab, tab, full(qn), pl.BlockSpec((1, LANES), lambda i: (0, 0)),
            full(wq), full(wqr), full(wk), full(wv), full(p1), full(p2), full(one),
        ],
        out_specs=[ospec, ospec, ospec],
        out_shape=[out, out, out],
        compiler_params=_cp(("arbitrary",)),
        name="mla_prep",
    )(u, u, u, cos, sin, qn, kv_norm.reshape(1, LANES), wq, wqr, wk, wv, p1, p2, one)


def _flash_kernel(*refs, lens, tk):
    q_ref = refs[0]
    kv_refs = refs[1:1 + 2 * len(lens)]
    o_ref = refs[1 + 2 * len(lens)]
    q = q_ref[...]
    tq = q.shape[0]
    m = jnp.full((tq, 1), NEG, F32)
    acc = jnp.zeros((tq, LANES), F32)

    def chunk(kc, vc, m, acc):
        s = _dot_nt(q, kc)
        m_new = jnp.maximum(m, jnp.max(s, axis=-1, keepdims=True))
        p = jnp.exp2((s - m_new).astype(BF16))
        acc = jnp.exp2(m - m_new) * acc + _dot(p, vc)
        return m_new, acc

    for si, length in enumerate(lens):
        k_ref, v_ref = kv_refs[2 * si], kv_refs[2 * si + 1]
        step = min(tk, length)
        if length == step:
            m, acc = chunk(k_ref[...], v_ref[...], m, acc)
        else:
            def body(j, carry, k_ref=k_ref, v_ref=v_ref, step=step):
                j0 = pl.multiple_of(j * step, step)
                return chunk(k_ref[pl.ds(j0, step), :], v_ref[pl.ds(j0, step), :], *carry)

            m, acc = lax.fori_loop(0, length // step, body, (m, acc), unroll=4)

    lane = lax.broadcasted_iota(jnp.int32, (1, LANES), 1)
    l = jnp.sum(jnp.where(lane == MLA_V, acc, 0.0), axis=-1, keepdims=True)
    o_ref[...] = jnp.where(lane < MLA_V, acc / l, 0.0).astype(o_ref.dtype)


def flash_attention(q, kvs, batch, tq, tk):
    n = q.shape[0]
    nq = n // batch // tq
    lens = tuple(k.shape[0] // batch for k, _ in kvs)
    in_specs = [pl.BlockSpec((tq, LANES), lambda b, h, i: (b * nq + i, h))]
    args = [q]
    for (k, v), length in zip(kvs, lens):
        in_specs += [pl.BlockSpec((length, LANES), lambda b, h, i: (b, h))] * 2
        args += [k, v]
    return pl.pallas_call(
        functools.partial(_flash_kernel, lens=lens, tk=tk),
        grid=(batch, HEADS, nq),
        in_specs=in_specs,
        out_specs=pl.BlockSpec((tq, LANES), lambda b, h, i: (b * nq + i, h)),
        out_shape=jax.ShapeDtypeStruct((n, HEADS * LANES), BF16),
        compiler_params=_cp(("arbitrary", "arbitrary", "arbitrary")),
        name="mla_attention",
    )(*args)


def _merge_kernel(*refs, with_router):
    (gate_ref, yr_f_ref, yr_b_ref, yf_ref, yn_ref, ym_ref, x_ref, g1_ref, wb_ret_ref, wb_f_ref,
     wb_na_ref, wb_mla_ref, wo_ref, gain_ref, sc_ref, sh_ref) = refs[:16]
    rest = refs[16:]
    if with_router:
        rh_ref, rl_ref, x_out_ref, lg_out_ref, hp_out_ref = rest
    else:
        x_out_ref, h_out_ref = rest
    d = x_ref.shape[1]

    def gated(k, y):
        g = jax.nn.sigmoid(gate_ref[:, k * d:(k + 1) * d].astype(F32))
        return g * y

    m = gated(0, _dot(yr_f_ref[...] + yr_b_ref[...], wb_ret_ref[...]))
    m = m + gated(1, _dot(yf_ref[...], wb_f_ref[...]))
    m = m + gated(2, _dot(yn_ref[...], wb_na_ref[...]))
    m = m + gated(3, _dot(ym_ref[...], wb_mla_ref[...]))
    y = _dot(m.astype(BF16), wo_ref[...])
    x = x_ref[...] + g1_ref[0] * y
    x_out_ref[...] = x
    hn = x * lax.rsqrt(jnp.mean(x * x, axis=-1, keepdims=True) + NORM_EPS)
    h = (hn * gain_ref[...]) * (1.0 + sc_ref[0]) + sh_ref[0]
    if with_router:
        hp_out_ref[...] = _pack_bf16_pairs(h)
        h_hi = h.astype(BF16)
        h_lo = (h - h_hi.astype(F32)).astype(BF16)
        lg_out_ref[...] = (_dot(h_hi, rh_ref[...]) + _dot(h_lo, rh_ref[...])) + _dot(h_hi, rl_ref[...])
    else:
        h_out_ref[...] = h.astype(h_out_ref.dtype)


def merge(u, y_ret, y_four, y_na, y_mla, x, mods, mod_row, gain, wb, w_out, router, tm):
    n, d = x.shape
    wb_ret, wb_f, wb_na, wb_mla = wb
    full = lambda a: pl.BlockSpec(a.shape, lambda i: (0,) * a.ndim)
    br = lambda: pl.BlockSpec((tm, BW), lambda i: (i, 0))
    tok = lambda: pl.BlockSpec((tm, d), lambda i: (i, 0))
    in_specs = [
        pl.BlockSpec((tm, 4 * d), lambda i: (i, 0)),
        br(), br(), br(), br(),
        pl.BlockSpec((tm, HEADS * LANES), lambda i: (i, 0)),
        tok(), _mod_spec(d, mod_row, tm, 2),
        full(wb_ret), full(wb_f), full(wb_na), full(wb_mla), full(w_out),
        pl.BlockSpec((1, d), lambda i: (0, 0)), _mod_spec(d, mod_row, tm, 4), _mod_spec(d, mod_row, tm, 3),
    ]
    args = [u, y_ret[0], y_ret[1], y_four, y_na, y_mla, x, mods, wb_ret, wb_f, wb_na, wb_mla, w_out,
            gain.reshape(1, d), mods, mods]
    out_specs = [tok()]
    out_shape = [jax.ShapeDtypeStruct((n, d), F32)]
    if router is None:
        out_specs.append(tok())
        out_shape.append(jax.ShapeDtypeStruct((n, d), BF16))
    else:
        in_specs += [full(router[0]), full(router[1])]
        args += list(router)
        out_specs += [pl.BlockSpec((tm, LANES), lambda i: (i, 0)), pl.BlockSpec((tm, d // 2), lambda i: (i, 0))]
        out_shape += [jax.ShapeDtypeStruct((n, LANES), F32), jax.ShapeDtypeStruct((n, d // 2), jnp.int32)]
    return pl.pallas_call(
        functools.partial(_merge_kernel, with_router=router is not None),
        grid=(n // tm,),
        in_specs=in_specs,
        out_specs=out_specs,
        out_shape=out_shape,
        compiler_params=_cp(("arbitrary",)),
        name="merge",
    )(*args)


def _ffn_kernel(h_ref, wg_ref, wu_ref, wd_ref, x_ref, g2_ref, o_ref, acc_ref):
    f = pl.program_id(1)

    @pl.when(f == 0)
    def _():
        acc_ref[...] = jnp.zeros_like(acc_ref)

    h = h_ref[...]
    a = _silu(_dot(h, wg_ref[...])) * _dot(h, wu_ref[...])
    acc_ref[...] += _dot(a.astype(BF16), wd_ref[...])

    @pl.when(f == pl.num_programs(1) - 1)
    def _():
        o_ref[...] = x_ref[...] + g2_ref[0] * acc_ref[...]


def ffn(h, x, mods, mod_row, wg, wu, wd, tm, tf):
    n, d = x.shape
    nf = wg.shape[1] // tf
    return pl.pallas_call(
        _ffn_kernel,
        grid=(n // tm, nf),
        in_specs=[
            pl.BlockSpec((tm, d), lambda i, f: (i, 0)),
            pl.BlockSpec((d, tf), lambda i, f: (0, f)),
            pl.BlockSpec((d, tf), lambda i, f: (0, f)),
            pl.BlockSpec((tf, d), lambda i, f: (f, 0)),
            pl.BlockSpec((tm, d), lambda i, f: (i, 0)),
            _mod_spec(d, mod_row, tm, 5),
        ],
        out_specs=pl.BlockSpec((tm, d), lambda i, f: (i, 0)),
        out_shape=jax.ShapeDtypeStruct((n, d), F32),
        scratch_shapes=[pltpu.VMEM((tm, d), F32)],
        compiler_params=_cp(("arbitrary", "arbitrary")),
        name="ffn",
    )(h, wg, wu, wd, x, mods)


def _moe_kernel(be_ref, nu_ref, xp_ref, wg_ref, wu_ref, wd_ref, o_ref, acc_ref, x_ref):
    i = pl.program_id(0)
    f = pl.program_id(1)
    used = i < nu_ref[0]

    @pl.when(f == 0)
    def _():
        acc_ref[...] = jnp.zeros_like(acc_ref)
        x_ref[...] = _unpack_bf16_pairs(xp_ref[...]).astype(BF16)

    @pl.when(used)
    def _():
        x = x_ref[...]
        a = _silu(_dot(x, wg_ref[0])) * _dot(x, wu_ref[0])
        acc_ref[...] += _dot(a.astype(BF16), wd_ref[0])

    @pl.when(f == pl.num_programs(1) - 1)
    def _():
        o_ref[...] = _pack_bf16_pairs(acc_ref[...])


SC_CAST_BLOCK = (16, 512)


def sc_cast_bf16(w):
    e, a, b = w.shape
    br, bc = SC_CAST_BLOCK
    assert (e * a) % br == 0 and b % bc == 0
    mesh = plsc.VectorSubcoreMesh(core_axis_name="c", subcore_axis_name="s")

    @functools.partial(pl.kernel, mesh=mesh, out_type=jax.ShapeDtypeStruct((e * a, b), BF16), scratch_types=[])
    def cast(x_hbm, o_hbm):
        def body(in_v, out_v):
            @pl.loop(0, br, step=2)
            def _(r):
                @pl.loop(0, bc, step=16)
                def _(c):
                    top = in_v.at[pl.ds(r, 1), pl.ds(c, 16)][...]
                    bot = in_v.at[pl.ds(r + 1, 1), pl.ds(c, 16)][...]
                    out_v.at[pl.ds(r, 2), pl.ds(c, 16)][...] = jnp.concatenate([top, bot], axis=0).astype(BF16)

        pltpu.emit_pipeline(
            body,
            grid=(e * a // br, b // bc),
            in_specs=[pl.BlockSpec((br, bc), lambda i, j: (i, j))],
            out_specs=[pl.BlockSpec((br, bc), lambda i, j: (i, j))],
            core_axis_name=("c", "s"),
            dimension_semantics=(pltpu.PARALLEL, pltpu.PARALLEL),
        )(x_hbm, o_hbm)

    return cast(w.reshape(e * a, b)).reshape(e, a, b)


def moe_ffn(blk_exp, n_used, xb, wg, wu, wd, tm, tf):
    n, dp = xb.shape
    d = 2 * dp
    nf = wg.shape[2] // tf
    grid_spec = pltpu.PrefetchScalarGridSpec(
        num_scalar_prefetch=2,
        grid=(n // tm, nf),
        in_specs=[
            pl.BlockSpec((tm, dp), lambda i, f, be, nu: (i, 0)),
            pl.BlockSpec((1, d, tf), lambda i, f, be, nu: (be[i], 0, f)),
            pl.BlockSpec((1, d, tf), lambda i, f, be, nu: (be[i], 0, f)),
            pl.BlockSpec((1, tf, d), lambda i, f, be, nu: (be[i], f, 0)),
        ],
        out_specs=pl.BlockSpec((tm, dp), lambda i, f, be, nu: (i, 0)),
        scratch_shapes=[pltpu.VMEM((tm, d), F32), pltpu.VMEM((tm, d), BF16)],
    )
    return pl.pallas_call(
        _moe_kernel,
        grid_spec=grid_spec,
        out_shape=jax.ShapeDtypeStruct((n, dp), jnp.int32),
        compiler_params=_cp(("arbitrary", "arbitrary")),
        name="moe_ffn",
    )(blk_exp, n_used, xb, wg, wu, wd)


def _combine_kernel(x_ref, ya_ref, yb_ref, w_ref, g2_ref, gain_ref, o_ref, *, final):
    w = w_ref[...]
    y = w[:, 0:1] * _unpack_bf16_pairs(ya_ref[...]) + w[:, 1:2] * _unpack_bf16_pairs(yb_ref[...])
    x = x_ref[...] + g2_ref[0] * y
    if final:
        x = (x * lax.rsqrt(jnp.mean(x * x, axis=-1, keepdims=True) + NORM_EPS)) * gain_ref[...]
    o_ref[...] = x


def moe_combine(x, y2, w, mods, mod_row, gain, final, tm):
    n, d = x.shape
    tok = lambda: pl.BlockSpec((tm, d), lambda i: (i, 0))
    half = lambda off: pl.BlockSpec((tm, d // 2), lambda i: (i + off, 0))
    return pl.pallas_call(
        functools.partial(_combine_kernel, final=final),
        grid=(n // tm,),
        in_specs=[tok(), half(0), half(n // tm),
                  pl.BlockSpec((tm, MOE_TOP_K), lambda i: (i, 0)),
                  _mod_spec(d, mod_row, tm, 5), pl.BlockSpec((1, d), lambda i: (0, 0))],
        out_specs=tok(),
        out_shape=jax.ShapeDtypeStruct((n, d), F32),
        compiler_params=_cp(("arbitrary",)),
        name="moe_combine",
    )(x, y2, y2, w, mods, gain.reshape(1, d))


def _rmsnorm_kernel(x_ref, gain_ref, o_ref):
    x = x_ref[...]
    o_ref[...] = (x * lax.rsqrt(jnp.mean(x * x, axis=-1, keepdims=True) + NORM_EPS)) * gain_ref[...]


def rmsnorm_rows(x, gain, tm):
    n, d = x.shape
    return pl.pallas_call(
        _rmsnorm_kernel,
        grid=(n // tm,),
        in_specs=[pl.BlockSpec((tm, d), lambda i: (i, 0)), pl.BlockSpec((1, d), lambda i: (0, 0))],
        out_specs=pl.BlockSpec((tm, d), lambda i: (i, 0)),
        out_shape=jax.ShapeDtypeStruct((n, d), F32),
        compiler_params=_cp(("arbitrary",)),
        name="final_norm",
    )(x, gain.reshape(1, d))


SC_CORES = 2
SC_SUBCORES = 16
SC_CHUNK = 64


def sc_gather_rows(table, idx):
    n_out = idx.shape[0]
    width = table.shape[1]
    workers = SC_CORES * SC_SUBCORES
    per_worker = n_out // workers
    assert n_out == per_worker * workers and per_worker % SC_CHUNK == 0 and table.dtype == jnp.int32
    mesh = plsc.VectorSubcoreMesh(core_axis_name="c", subcore_axis_name="s")

    @functools.partial(
        pl.kernel, mesh=mesh,
        out_type=jax.ShapeDtypeStruct((n_out, width), table.dtype),
        scratch_types=[pltpu.VMEM((SC_CHUNK,), jnp.int32), pltpu.VMEM((SC_CHUNK, width), table.dtype),
                       pltpu.SemaphoreType.DMA],
    )
    def gather(table_hbm, idx_hbm, out_hbm, idx_v, rows_v, sem):
        base = (lax.axis_index("s") * SC_CORES + lax.axis_index("c")) * per_worker

        @pl.loop(0, per_worker // SC_CHUNK)
        def _(j):
            off = pl.multiple_of(base + j * SC_CHUNK, SC_CHUNK)
            pltpu.sync_copy(idx_hbm.at[pl.ds(off, SC_CHUNK)], idx_v)
            pltpu.async_copy(table_hbm.at[idx_v], rows_v, sem).wait()
            pltpu.sync_copy(rows_v, out_hbm.at[pl.ds(off, SC_CHUNK)])

    return gather(table, idx)


def moe_route(logits, n_experts, tm):
    n_tok = logits.shape[0]
    top_logit, top_idx = lax.top_k(logits[:, :n_experts], MOE_TOP_K)
    top_w = jax.nn.softmax(top_logit, axis=-1)
    e_flat = top_idx.reshape(-1).astype(jnp.int32)
    n_assign = e_flat.shape[0]
    onehot = (e_flat[:, None] == jnp.arange(n_experts, dtype=jnp.int32)[None, :]).astype(jnp.int32)
    rank = jnp.sum((jnp.cumsum(onehot, axis=0) - onehot) * onehot, axis=1)
    counts = jnp.sum(onehot, axis=0)
    padded = (counts + tm - 1) // tm * tm
    pad_end = jnp.cumsum(padded)
    pad_start = pad_end - padded
    dest = pad_start[e_flat] + rank
    n_rows = n_assign + n_experts * tm
    tok = jnp.arange(n_assign, dtype=jnp.int32) // MOE_TOP_K
    row_tok = (jnp.arange(n_rows, dtype=jnp.int32) % n_tok).at[dest].set(tok)
    blk_start = jnp.arange(n_rows // tm, dtype=jnp.int32) * tm
    blk_exp = jnp.minimum(jnp.sum(pad_end[None, :] <= blk_start[:, None], axis=1), n_experts - 1)
    n_used = (pad_end[-1] // tm).reshape(1)
    return row_tok, dest.reshape(n_tok, MOE_TOP_K), top_w, blk_exp.astype(jnp.int32), n_used.astype(jnp.int32)


def _rope_split(wcols):
    d, w = wcols.shape
    half = w // HEADS // 2
    return wcols.reshape(d, HEADS, 2, half).transpose(0, 2, 1, 3).reshape(d, w)


def _inproj_weights(w_in):
    d = w_in.shape[0]
    kv = (BW, BW, BW, BW, MLA_KV_RANK, MLA_ROPE)
    qs = (BW, BW, BW, BW, BW, MLA_Q_RANK, 4 * d)
    offs = np.concatenate([[0], np.cumsum(kv + qs)])
    seg = lambda i: w_in[:, offs[i]:offs[i + 1]]
    r_k, r_v, n_k, n_v, m_ckv, m_kr = (seg(i) for i in range(6))
    r_q, r_gf, r_gb, f_in, n_q, m_cq, gate = (seg(6 + i) for i in range(7))
    z = lambda n: jnp.zeros((d, n), w_in.dtype)
    cols = [gate, _rope_split(r_q), _rope_split(r_k), r_v, r_gf, r_gb, f_in, n_q, n_k, n_v,
            m_cq, z(BW - MLA_Q_RANK), m_ckv, m_kr, z(LANES - MLA_ROPE)]
    return jnp.concatenate(cols, axis=1).astype(BF16)


def _ret_rope_tables(n):
    t = jnp.arange(n)
    row = (t // GRID_W).astype(F32)
    col = (t % GRID_W).astype(F32)
    nf = RET_DK // 4
    inv = ROPE_BASE ** (-jnp.arange(nf, dtype=F32) / nf)
    ang = jnp.concatenate([row[:, None] * inv, col[:, None] * inv], axis=-1)
    return jnp.tile(jnp.cos(ang), (1, HEADS)), jnp.tile(jnp.sin(ang), (1, HEADS))


def _mla_rope_tables(n):
    t = jnp.arange(n)
    row = (t // GRID_W).astype(F32)
    col = (t % GRID_W).astype(F32)
    nf = MLA_ROPE // 4
    inv = ROPE_BASE ** (-jnp.arange(nf, dtype=F32) / nf)
    ang = jnp.concatenate([row[:, None] * inv, col[:, None] * inv], axis=-1)
    pad = jnp.zeros((n, LANES - MLA_NOPE - MLA_ROPE), F32)
    cos = jnp.concatenate([jnp.ones((n, MLA_NOPE), F32), jnp.cos(ang), jnp.cos(ang), pad], axis=-1)
    sin = jnp.concatenate([jnp.zeros((n, MLA_NOPE), F32), jnp.sin(ang), jnp.sin(ang), pad], axis=-1)
    return cos, sin


def _tile_rows(*sizes):
    for tm in (1024, 512, 256, 128):
        if all(s % tm == 0 for s in sizes):
            return tm
    raise ValueError(f"token counts {sizes} need a common multiple-of-128 row tile")


def kernel(x, c, ctx, c_ctx, ada_w, ada_b, norm_mix, norm_ffn, w_in, ret_decay_fwd, ret_decay_bwd,
           mla_q_norm, mla_kv_norm, mla_w_uq, mla_w_ukv, na_rpb, w_branch, w_out,
           ffn_w_gate, ffn_w_up, ffn_w_down, moe_router, moe_w_gate, moe_w_up, moe_w_down, norm_final):
    batch, t, d = x.shape
    tc = ctx.shape[1]
    depth = ada_w.shape[0]
    nl, ncx = batch * t, batch * tc
    assert batch < 8 and t % (16 * GRID_W) == 0 and tc % LANES == 0 and d == 4 * BW
    tm = _tile_rows(t, ncx)
    cb0 = 4 * d // BW

    xl = x.reshape(nl, d)
    xc = ctx.reshape(ncx, d)
    cc = jnp.zeros((8, d), F32).at[:batch].set(c).at[batch].set(c_ctx)
    mods = adaln(cc, ada_w, ada_b).reshape(depth, 8 * 6, 1, d)
    lat_row = lambda r0: r0 // t
    ctx_row = lambda r0: batch

    ret_cos, ret_sin = _ret_rope_tables(t)
    ret_cos_c, ret_sin_c = jnp.ones((tc, LANES), F32), jnp.zeros((tc, LANES), F32)
    mla_cos, mla_sin = _mla_rope_tables(t)
    mla_cos_c = jnp.concatenate([jnp.ones((tm, MLA_NOPE + MLA_ROPE), F32),
                                 jnp.zeros((tm, LANES - MLA_NOPE - MLA_ROPE), F32)], axis=-1)
    mla_sin_c = jnp.zeros((tm, LANES), F32)
    ret_cs = 256

    for i in range(depth):
        ctx_out = i < depth - 1
        md = mods[i]
        w_p = _inproj_weights(w_in[i])
        u = norm_inproj(xl, norm_mix[i], md, lat_row, w_p, tm, 2304)
        uc = norm_inproj(xc, norm_mix[i], md, ctx_row, w_p, tm, 2304)

        lg = jnp.stack([jax.nn.log_sigmoid(ret_decay_fwd[i].astype(F32)),
                        jax.nn.log_sigmoid(ret_decay_bwd[i].astype(F32))])
        zero_state = jnp.zeros((batch, 2, BW, BW), F32)
        yc_ret, s_ctx = retention(uc, cb0, lg, ret_cos_c, ret_sin_c, zero_state, batch, min(ret_cs, tc))
        y_ret, _ = retention(u, cb0, lg, ret_cos, ret_sin, s_ctx, batch, ret_cs)

        y_four = fourier_long(u, cb0 + CB_F, batch)

        bias_tab = _na_bias_table(na_rpb[i])
        y_na = na_attention(u, uc, cb0, bias_tab, batch, 16)

        mw = _mla_weights(mla_w_uq[i], mla_w_ukv[i])
        q_l, k_l, v_l = mla_prep(u, cb0, mla_cos, mla_sin, mla_q_norm[i], mla_kv_norm[i], mw, tm, t // tm)
        q_c, k_c, v_c = mla_prep(uc, cb0, mla_cos_c, mla_sin_c, mla_q_norm[i], mla_kv_norm[i], mw, tm, 1)
        y_mla = flash_attention(q_l, [(k_l, v_l), (k_c, v_c)], batch, 1024, 512)

        wb = w_branch[i].astype(BF16)
        wb_mla = jnp.concatenate(
            [wb[3].reshape(HEADS, MLA_V, d), jnp.zeros((HEADS, LANES - MLA_V, d), BF16)], axis=1
        ).reshape(HEADS * LANES, d)
        wbs = (wb[0], wb[1], wb[2], wb_mla)
        wo = w_out[i].astype(BF16)
        j = i // 2
        if i % 2 == 0:
            xl, h2 = merge(u, y_ret, y_four, y_na, y_mla, xl, md, lat_row, norm_ffn[i], wbs, wo, None, 512)
            wg, wu, wd = ffn_w_gate[j].astype(BF16), ffn_w_up[j].astype(BF16), ffn_w_down[j].astype(BF16)
            tf = wg.shape[1] // 2
            xl = ffn(h2, xl, md, lat_row, wg, wu, wd, 512, tf)
        else:
            n_exp = moe_router.shape[2]
            r = jnp.pad(moe_router[j], ((0, 0), (0, LANES - n_exp)))
            r_hi = r.astype(BF16)
            r_lo = (r - r_hi.astype(F32)).astype(BF16)
            xl, logits, h2p = merge(u, y_ret, y_four, y_na, y_mla, xl, md, lat_row, norm_ffn[i], wbs, wo,
                                   (r_hi, r_lo), 512)
            tmm = 512
            row_tok, dest, top_w, blk_exp, n_used = moe_route(logits, n_exp, tmm)
            xb = sc_gather_rows(h2p, row_tok)
            ewg, ewu, ewd = (sc_cast_bf16(w) for w in (moe_w_gate[j], moe_w_up[j], moe_w_down[j]))
            yb = moe_ffn(blk_exp, n_used, xb, ewg, ewu, ewd, tmm, 1792)
            y2 = sc_gather_rows(yb, dest.T.reshape(-1))
            xl = moe_combine(xl, y2, top_w, md, lat_row, norm_final, i == depth - 1, 512)

        if ctx_out:
            yc_four = fourier_short(uc, cb0 + CB_F, batch)
            yc_na = na_ctx_attention(uc, cb0, batch)
            yc_mla = flash_attention(q_c, [(k_c, v_c)], batch, tc, 512)
            if i % 2 == 0:
                xc, hc2 = merge(uc, yc_ret, yc_four, yc_na, yc_mla, xc, md, ctx_row, norm_ffn[i], wbs, wo,
                                None, 512)
                xc = ffn(hc2, xc, md, ctx_row, wg, wu, wd, 512, tf)
            else:
                raise NotImplementedError("context tokens through the expert mixer")

    if depth % 2 == 1:
        xl = rmsnorm_rows(xl, norm_final, tm)
    return xl.reshape(batch, t, d)
```

```python
import functools

import numpy as np
import jax
import jax.numpy as jnp
from jax import lax
from jax.experimental import pallas as pl
from jax.experimental.pallas import tpu as pltpu
from jax.experimental.pallas import tpu_sc as plsc

F32 = jnp.float32
BF16 = jnp.bfloat16

GRID_W = 64
ROPE_BASE = 10000.0
NORM_EPS = 1e-6
HEADS = 4
RET_DK = 64
FOURIER_GROUP_DIM = 64
NA_HEAD_DIM = 64
NA_WIN_R = 8
NA_WIN_C = 16
MLA_NOPE = 64
MLA_ROPE = 32
MLA_V = 64
MLA_Q_RANK = 192
MLA_KV_RANK = 128
MOE_TOP_K = 2
BW = 256

COL_GATE = 0
CB_RQ, CB_RK, CB_RV, CB_RGF, CB_RGB, CB_F, CB_NQ, CB_NK, CB_NV, CB_MCQ, CB_MKV = range(11)
LANES = 128
NEG = -1e30

VMEM_LIMIT = 48 * 1024 * 1024


def _cp(sem, vmem=VMEM_LIMIT):
    return pltpu.CompilerParams(dimension_semantics=sem, vmem_limit_bytes=vmem)


def _dot(a, b):
    return jnp.dot(a, b, preferred_element_type=F32)


def _dot_nt(a, b):
    return lax.dot_general(a, b, (((1,), (1,)), ((), ())), preferred_element_type=F32)


def _dot_tn(a, b):
    return lax.dot_general(a, b, (((0,), (0,)), ((), ())), preferred_element_type=F32)


def _silu(x):
    return x * jax.nn.sigmoid(x)


def _pack_bf16_pairs(x):
    k = x.shape[1] // 2
    lo = lax.bitcast_convert_type(x[:, :k].astype(BF16).astype(F32), jnp.uint32) >> 16
    hi = lax.bitcast_convert_type(x[:, k:].astype(BF16).astype(F32), jnp.uint32) & jnp.uint32(0xFFFF0000)
    return lax.bitcast_convert_type(lo | hi, jnp.int32)


def _unpack_bf16_pairs(p):
    u = lax.bitcast_convert_type(p, jnp.uint32)
    lo = lax.bitcast_convert_type(u << 16, F32)
    hi = lax.bitcast_convert_type(u & jnp.uint32(0xFFFF0000), F32)
    return jnp.concatenate([lo, hi], axis=-1)


def _adaln_kernel(c_ref, w_ref, b_ref, o_ref):
    s = _silu(c_ref[...])
    o_ref[0] = _dot(s.astype(BF16), w_ref[0].astype(BF16)) + b_ref[0]


def adaln(cc, ada_w, ada_b):
    depth, d, n6 = ada_w.shape
    tn = n6 // 4
    return pl.pallas_call(
        _adaln_kernel,
        grid=(depth, n6 // tn),
        in_specs=[
            pl.BlockSpec((8, d), lambda l, j: (0, 0)),
            pl.BlockSpec((1, d, tn), lambda l, j: (l, 0, j)),
            pl.BlockSpec((1, 1, tn), lambda l, j: (l, 0, j)),
        ],
        out_specs=pl.BlockSpec((1, 8, tn), lambda l, j: (l, 0, j)),
        out_shape=jax.ShapeDtypeStruct((depth, 8, n6), F32),
        compiler_params=_cp(("arbitrary", "arbitrary")),
        name="adaln",
    )(cc, ada_w, ada_b.reshape(depth, 1, n6))


def _inproj_kernel(x_ref, g_ref, sc_ref, sh_ref, w_ref, o_ref, h_ref):
    @pl.when(pl.program_id(1) == 0)
    def _():
        x = x_ref[...]
        y = x * lax.rsqrt(jnp.mean(x * x, axis=-1, keepdims=True) + NORM_EPS)
        h = (y * g_ref[...]) * (1.0 + sc_ref[0]) + sh_ref[0]
        h_ref[...] = h.astype(BF16)

    o_ref[...] = _dot(h_ref[...], w_ref[...]).astype(o_ref.dtype)


def _mod_spec(d, mod_row, tm, k):
    return pl.BlockSpec((1, 1, d), lambda i, *_: (mod_row(i * tm) * 6 + k, 0, 0))


def norm_inproj(x, gain, mods, mod_row, w, tm, tn):
    n, d = x.shape
    nc = w.shape[1]
    return pl.pallas_call(
        _inproj_kernel,
        grid=(n // tm, nc // tn),
        in_specs=[
            pl.BlockSpec((tm, d), lambda i, j: (i, 0)),
            pl.BlockSpec((1, d), lambda i, j: (0, 0)),
            _mod_spec(d, mod_row, tm, 1),
            _mod_spec(d, mod_row, tm, 0),
            pl.BlockSpec((d, tn), lambda i, j: (0, j)),
        ],
        out_specs=pl.BlockSpec((tm, tn), lambda i, j: (i, j)),
        out_shape=jax.ShapeDtypeStruct((n, nc), BF16),
        scratch_shapes=[pltpu.VMEM((tm, d), BF16)],
        compiler_params=_cp(("arbitrary", "arbitrary")),
        name="norm_inproj",
    )(x, gain.reshape(1, d), mods, mods, w)


def _ret_init(d, cs, w, lg_ref, s0_ref, lgq_ref, s_ref, decay_ref, qw_ref, kw_ref, avg_ref):
    rev = d == 1
    s_ref[d] = s0_ref[0, d]
    pos_i = lax.broadcasted_iota(jnp.int32, (cs, 1), 0).astype(F32)
    pos_j = lax.broadcasted_iota(jnp.int32, (1, cs), 1).astype(F32)
    p_i = jnp.where(rev, cs - 1.0 - pos_i, pos_i)
    p_j = jnp.where(rev, cs - 1.0 - pos_j, pos_j)
    diff = p_i - p_j
    for h in range(HEADS):
        decay_ref[d, h] = jnp.where(diff >= 0, jnp.exp(lg_ref[d, h] * jnp.maximum(diff, 0.0)), 0.0)
    lgq = lgq_ref[d]
    qw_ref[d] = jnp.exp(lgq * (p_i + 1.0))
    kw_ref[d] = jnp.exp(lgq * (cs - 1.0 - p_i))
    hd = w // HEADS
    gi = lax.broadcasted_iota(jnp.int32, (w, w), 0) // hd
    gj = lax.broadcasted_iota(jnp.int32, (w, w), 1) // hd
    avg_ref[...] = jnp.where(gi == gj, 1.0 / hd, 0.0).astype(BF16)


def _ret_chunk(d, q_ref, k_ref, v_ref, g_ref, cos_ref, sin_ref, lgv_ref, y_ref, s_ref, decay_ref, qw_ref, kw_ref,
               avg_ref):
    cs = q_ref.shape[0]
    w = q_ref.shape[1]
    half = w // 2
    lgv = lgv_ref[d]
    s_ref, decay_ref, qw_ref, kw_ref = s_ref.at[d], decay_ref.at[d], qw_ref.at[d], kw_ref.at[d]

    cos = cos_ref[...]
    sin = sin_ref[...]

    def rope(t):
        t1, t2 = t[:, :half], t[:, half:]
        return jnp.concatenate([t1 * cos - t2 * sin, t2 * cos + t1 * sin], axis=-1)

    q = rope(q_ref[...].astype(F32))
    k = rope(k_ref[...].astype(F32)) * (RET_DK ** -0.5)
    vb = v_ref[...]

    lane = lax.broadcasted_iota(jnp.int32, (1, w), 1)
    head_q = (lane % half) // (half // HEADS)
    head_v = lane // (w // HEADS)

    s_prev = s_ref[...]
    o = _dot((q * qw_ref[...]).astype(BF16), s_prev.astype(BF16))
    qb = q.astype(BF16)
    kb = k.astype(BF16)
    zero_b = jnp.zeros_like(qb)
    for h in range(HEADS):
        a = _dot_nt(jnp.where(head_q == h, qb, zero_b), kb)
        oh = _dot((a * decay_ref[h]).astype(BF16), vb)
        o = o + jnp.where(head_v == h, oh, 0.0)

    ds = _dot_tn((k * kw_ref[...]).astype(BF16), vb)
    row_head = (lax.broadcasted_iota(jnp.int32, (w, 1), 0) % half) // (half // HEADS)
    s_new = s_prev * jnp.exp(lgv * float(cs)) + jnp.where(row_head == head_v, ds, 0.0)
    s_ref[...] = s_new

    ms = _dot((o * o).astype(BF16), avg_ref[...])
    on = o * lax.rsqrt(ms + NORM_EPS)
    y_ref[...] = (_silu(g_ref[...].astype(F32)) * on).astype(y_ref.dtype)
    return s_new


def _ret_kernel(lg_ref, qf, kf, vf, gf, cosf, sinf, qb, kb, vb, gb, cosb, sinb, s0_ref, lgq_ref, lgv_ref,
                yf_ref, yb_ref, sout_ref, s_ref, decay_ref, qw_ref, kw_ref, avg_ref, *, n_chunks):
    c = pl.program_id(1)
    scratch = (s_ref, decay_ref, qw_ref, kw_ref, avg_ref)

    @pl.when(c == 0)
    def _():
        for d in range(2):
            _ret_init(d, qf.shape[0], qf.shape[1], lg_ref, s0_ref, lgq_ref, *scratch)

    s_f = _ret_chunk(0, qf, kf, vf, gf, cosf, sinf, lgv_ref, yf_ref, *scratch)
    s_b = _ret_chunk(1, qb, kb, vb, gb, cosb, sinb, lgv_ref, yb_ref, *scratch)

    @pl.when(c == n_chunks - 1)
    def _():
        sout_ref[0, 0] = s_f
        sout_ref[0, 1] = s_b


def retention(u, cb0, lg, cos, sin, s0, batch, cs):
    n = u.shape[0]
    t = n // batch
    nch = t // cs
    w = BW
    half = w // 2
    lgq = jnp.tile(jnp.repeat(lg, half // HEADS, axis=1), (1, 2)).reshape(2, 1, w)
    lgv = jnp.repeat(lg, w // HEADS, axis=1).reshape(2, 1, w)

    def chunk(d, c):
        return nch - 1 - c if d else c

    def direction(d):
        col = lambda cb: pl.BlockSpec((cs, w), lambda b, c: (b * nch + chunk(d, c), cb0 + cb))
        tab = lambda: pl.BlockSpec((cs, half), lambda b, c: (chunk(d, c), 0))
        return [col(CB_RQ), col(CB_RK), col(CB_RV), col(CB_RGF + d), tab(), tab()]

    y_spec = lambda d: pl.BlockSpec((cs, w), lambda b, c: (b * nch + chunk(d, c), 0))
    state_spec = pl.BlockSpec((1, 2, w, w), lambda b, c: (b, 0, 0, 0))
    lane_spec = pl.BlockSpec((2, 1, w), lambda b, c: (0, 0, 0))
    y_f, y_b, s_out = pl.pallas_call(
        functools.partial(_ret_kernel, n_chunks=nch),
        grid=(batch, nch),
        in_specs=[pl.BlockSpec(memory_space=pltpu.SMEM)] + direction(0) + direction(1)
        + [state_spec, lane_spec, lane_spec],
        out_specs=[y_spec(0), y_spec(1), state_spec],
        out_shape=[
            jax.ShapeDtypeStruct((n, w), BF16),
            jax.ShapeDtypeStruct((n, w), BF16),
            jax.ShapeDtypeStruct((batch, 2, w, w), F32),
        ],
        scratch_shapes=[pltpu.VMEM((2, w, w), F32), pltpu.VMEM((2, HEADS, cs, cs), F32),
                        pltpu.VMEM((2, cs, w), F32), pltpu.VMEM((2, cs, w), F32), pltpu.VMEM((w, w), BF16)],
        compiler_params=_cp(("arbitrary", "arbitrary")),
        name="retention",
    )(lg, u, u, u, u, cos, sin, u, u, u, u, cos, sin, s0, lgq, lgv)
    return (y_f, y_b), s_out


def _dft_tables(t, t1, t2):
    k1 = jnp.arange(t1, dtype=jnp.int32)
    a = jnp.arange(t1, dtype=jnp.int32)
    m = jnp.arange(t2, dtype=jnp.int32)
    ph1 = (k1[None, :, None] * (a[None, None, :] * t2 + m[:, None, None])) % t
    ang1 = ph1.astype(F32) * (2.0 * np.pi / t)
    ph2 = (m[:, None] * m[None, :]) % t2
    ang2 = ph2.astype(F32) * (2.0 * np.pi / t2)
    return (jnp.cos(ang1).astype(BF16), jnp.sin(ang1).astype(BF16),
            jnp.cos(ang2).astype(BF16), jnp.sin(ang2).astype(BF16))


def _channel_tables(width):
    ch = jnp.arange(width, dtype=jnp.int32)
    same = (ch[:, None] // FOURIER_GROUP_DIM) == (ch[None, :] // FOURIER_GROUP_DIM)
    ph = ((ch[:, None] % FOURIER_GROUP_DIM) * (ch[None, :] % FOURIER_GROUP_DIM)) % FOURIER_GROUP_DIM
    ang = ph.astype(F32) * (2.0 * np.pi / FOURIER_GROUP_DIM)
    return (jnp.where(same, jnp.cos(ang), 0.0).astype(BF16),
            jnp.where(same, jnp.sin(ang), 0.0).astype(BF16))


def _fourier_kernel(x_ref, c1_ref, s1_ref, c2_ref, s2_ref, cc_ref, sc_ref, o_ref, xa, yre, yim,
                    *, t1, t2, norm):
    xa[...] = x_ref[...].astype(F32)

    def stage1(m, carry):
        xs = xa[pl.ds(m, t1, stride=t2), :].astype(BF16)
        r0 = pl.multiple_of(m * t1, t1)
        yre[pl.ds(r0, t1), :] = _dot(c1_ref[m], xs)
        yim[pl.ds(r0, t1), :] = -_dot(s1_ref[m], xs)
        return carry

    lax.fori_loop(0, t2, stage1, 0, unroll=8)

    c2 = c2_ref[...]
    s2 = s2_ref[...]
    cc = cc_ref[...]
    sc = sc_ref[...]

    def stage2(k1, carry):
        yr = yre[pl.ds(k1, t2, stride=t1), :].astype(BF16)
        yi = yim[pl.ds(k1, t2, stride=t1), :].astype(BF16)
        zr = _dot(c2, yr) + _dot(s2, yi)
        zi = _dot(c2, yi) - _dot(s2, yr)
        out = (_dot(zr.astype(BF16), cc) + _dot(zi.astype(BF16), sc)) * norm
        xa[pl.ds(k1, t2, stride=t1), :] = out
        return carry

    lax.fori_loop(0, t1, stage2, 0, unroll=4)
    o_ref[...] = xa[...].astype(o_ref.dtype)


def fourier_long(u, cb, batch, t2=LANES):
    n = u.shape[0]
    t = n // batch
    t1 = t // t2
    c1, s1, c2, s2 = _dft_tables(t, t1, t2)
    cc, sc = _channel_tables(LANES)
    norm = float(1.0 / np.sqrt(t * FOURIER_GROUP_DIM))
    full = lambda shape: pl.BlockSpec(shape, lambda b, hh: (0,) * len(shape))
    return pl.pallas_call(
        functools.partial(_fourier_kernel, t1=t1, t2=t2, norm=norm),
        grid=(batch, BW // LANES),
        in_specs=[
            pl.BlockSpec((t, LANES), lambda b, hh: (b, cb * (BW // LANES) + hh)),
            full((t2, t1, t1)), full((t2, t1, t1)), full((t2, t2)), full((t2, t2)),
            full((LANES, LANES)), full((LANES, LANES)),
        ],
        out_specs=pl.BlockSpec((t, LANES), lambda b, hh: (b, hh)),
        out_shape=jax.ShapeDtypeStruct((n, BW), BF16),
        scratch_shapes=[pltpu.VMEM((t, LANES), F32)] * 3,
        compiler_params=_cp(("arbitrary", "arbitrary")),
        name="fourier",
    )(u, c1, s1, c2, s2, cc, sc)


def _fourier_small_kernel(x_ref, ct_ref, st_ref, cc_ref, sc_ref, o_ref, *, norm):
    x = x_ref[...]
    zr = _dot(ct_ref[...], x)
    zi = -_dot(st_ref[...], x)
    out = (_dot(zr.astype(BF16), cc_ref[...]) + _dot(zi.astype(BF16), sc_ref[...])) * norm
    o_ref[...] = out.astype(o_ref.dtype)


def fourier_short(u, cb, batch):
    n = u.shape[0]
    t = n // batch
    pos = jnp.arange(t, dtype=jnp.int32)
    ang = ((pos[:, None] * pos[None, :]) % t).astype(F32) * (2.0 * np.pi / t)
    ct, st = jnp.cos(ang).astype(BF16), jnp.sin(ang).astype(BF16)
    cc, sc = _channel_tables(BW)
    norm = float(1.0 / np.sqrt(t * FOURIER_GROUP_DIM))
    full = lambda shape: pl.BlockSpec(shape, lambda b: (0,) * len(shape))
    return pl.pallas_call(
        functools.partial(_fourier_small_kernel, norm=norm),
        grid=(batch,),
        in_specs=[pl.BlockSpec((t, BW), lambda b: (b, cb)), full((t, t)), full((t, t)),
                  full((BW, BW)), full((BW, BW))],
        out_specs=pl.BlockSpec((t, BW), lambda b: (b, 0)),
        out_shape=jax.ShapeDtypeStruct((n, BW), BF16),
        compiler_params=_cp(("arbitrary",)),
        name="fourier_ctx",
    )(u, ct, st, cc, sc)


def _na_bias_table(rpb):
    n_r, n_c = rpb.shape[1], rpb.shape[2]
    span = 2 * GRID_W
    left = GRID_W - NA_WIN_C
    vp = jnp.pad(rpb.astype(F32), ((0, 0), (0, 0), (left, span - n_c - left)))
    rep = jnp.broadcast_to(vp[:, :, None, :], (HEADS, n_r, GRID_W, span)).reshape(HEADS, n_r, GRID_W * span)
    skew = rep[..., :GRID_W * (span - 1)].reshape(HEADS, n_r, GRID_W, span - 1)
    toep = skew[..., GRID_W - 1:]
    qc = np.arange(GRID_W)[:, None]
    kc = np.arange(GRID_W)[None, :]
    start = np.clip(qc - NA_WIN_C // 2, 0, GRID_W - NA_WIN_C)
    valid = (kc >= start) & (kc < start + NA_WIN_C)
    toep = jnp.where(valid, toep, NEG)
    tab = jnp.stack([toep[:, NA_WIN_R - 1 - v:2 * NA_WIN_R - 1 - v] for v in range(NA_WIN_R)])
    tab = tab.transpose(0, 1, 3, 2, 4)
    return tab.reshape(NA_WIN_R, HEADS * GRID_W, NA_WIN_R * GRID_W)


def _na_kernel(q_ref, k_ref, v_ref, kc_ref, vc_ref, bias_ref, o_ref, *, rows_per_step, n_rows):
    i = pl.program_id(1)
    w = q_ref.shape[1]
    lane = lax.broadcasted_iota(jnp.int32, (1, w), 1)
    head = lane // (w // HEADS)
    scale = jnp.asarray(NA_HEAD_DIM ** -0.5, q_ref.dtype)
    kc = kc_ref[...]
    vc = vc_ref[...]
    win = NA_WIN_R * GRID_W

    def row(rl, carry):
        r = i * rows_per_step + rl
        rs = jnp.clip(r - NA_WIN_R // 2, 0, n_rows - NA_WIN_R)
        var = r - rs
        q0 = pl.multiple_of(rl * GRID_W, GRID_W)
        k0 = pl.multiple_of(rs * GRID_W, GRID_W)
        q = q_ref[pl.ds(q0, GRID_W), :] * scale
        kw = k_ref[pl.ds(k0, win), :]
        vw = v_ref[pl.ds(k0, win), :]
        zero_b = jnp.zeros_like(q)
        q4 = jnp.concatenate([jnp.where(head == h, q, zero_b) for h in range(HEADS)], axis=0)
        s_loc = _dot_nt(q4, kw) + bias_ref[var]
        s_ctx = _dot_nt(q4, kc)
        m = jnp.maximum(jnp.max(s_loc, axis=-1, keepdims=True), jnp.max(s_ctx, axis=-1, keepdims=True))
        p_loc = jnp.exp(s_loc - m)
        p_ctx = jnp.exp(s_ctx - m)
        l = jnp.sum(p_loc, axis=-1, keepdims=True) + jnp.sum(p_ctx, axis=-1, keepdims=True)
        pv = (_dot(p_loc.astype(BF16), vw) + _dot(p_ctx.astype(BF16), vc)) / l
        acc = jnp.zeros((GRID_W, w), F32)
        for h in range(HEADS):
            acc = acc + jnp.where(head == h, pv[h * GRID_W:(h + 1) * GRID_W], 0.0)
        o_ref[pl.ds(q0, GRID_W), :] = acc.astype(o_ref.dtype)
        return carry

    lax.fori_loop(0, rows_per_step, row, 0, unroll=2)


def na_attention(u, uc, cb0, bias_tab, batch, rows_per_step):
    n = u.shape[0]
    t = n // batch
    tc = uc.shape[0] // batch
    n_rows = t // GRID_W
    steps = n_rows // rows_per_step
    tq = rows_per_step * GRID_W
    return pl.pallas_call(
        functools.partial(_na_kernel, rows_per_step=rows_per_step, n_rows=n_rows),
        grid=(batch, steps),
        in_specs=[
            pl.BlockSpec((tq, BW), lambda b, i: (b * steps + i, cb0 + CB_NQ)),
            pl.BlockSpec((t, BW), lambda b, i: (b, cb0 + CB_NK)),
            pl.BlockSpec((t, BW), lambda b, i: (b, cb0 + CB_NV)),
            pl.BlockSpec((tc, BW), lambda b, i: (b, cb0 + CB_NK)),
            pl.BlockSpec((tc, BW), lambda b, i: (b, cb0 + CB_NV)),
            pl.BlockSpec(bias_tab.shape, lambda b, i: (0, 0, 0)),
        ],
        out_specs=pl.BlockSpec((tq, BW), lambda b, i: (b * steps + i, 0)),
        out_shape=jax.ShapeDtypeStruct((n, BW), BF16),
        compiler_params=_cp(("arbitrary", "arbitrary")),
        name="na_attention",
    )(u, u, u, uc, uc, bias_tab)


def _na_ctx_kernel(q_ref, k_ref, v_ref, o_ref):
    w = q_ref.shape[1]
    lane = lax.broadcasted_iota(jnp.int32, (1, w), 1)
    head = lane // (w // HEADS)
    scale = NA_HEAD_DIM ** -0.5
    q = q_ref[...]
    k = k_ref[...]
    v = v_ref[...]
    zero_b = jnp.zeros_like(q)
    acc = jnp.zeros(q.shape, F32)
    for h in range(HEADS):
        s = _dot_nt(jnp.where(head == h, q, zero_b), k) * scale
        p = jnp.exp(s - jnp.max(s, axis=-1, keepdims=True))
        l = jnp.sum(p, axis=-1, keepdims=True)
        acc = acc + jnp.where(head == h, _dot(p.astype(BF16), v) / l, 0.0)
    o_ref[...] = acc.astype(o_ref.dtype)


def na_ctx_attention(uc, cb0, batch):
    tc = uc.shape[0] // batch
    spec = lambda cb: pl.BlockSpec((tc, BW), lambda b: (b, cb0 + cb))
    return pl.pallas_call(
        _na_ctx_kernel,
        grid=(batch,),
        in_specs=[spec(CB_NQ), spec(CB_NK), spec(CB_NV)],
        out_specs=pl.BlockSpec((tc, BW), lambda b: (b, 0)),
        out_shape=jax.ShapeDtypeStruct((uc.shape[0], BW), BF16),
        compiler_params=_cp(("arbitrary",)),
        name="na_ctx_attention",
    )(uc, uc, uc)


def _mla_prep_kernel(cq_ref, ckv_ref, kr_ref, cos_ref, sin_ref, qn_ref, kvn_ref, wq_ref, wqr_ref,
                     wk_ref, wv_ref, p1_ref, p2_ref, one_ref, q_ref, k_ref, v_ref):
    cos = cos_ref[...]
    sin = sin_ref[...]
    cos4 = jnp.concatenate([cos] * HEADS, axis=-1)
    sin4 = jnp.concatenate([sin] * HEADS, axis=-1)

    cq = cq_ref[...].astype(F32)
    ms = jnp.sum(cq * cq, axis=-1, keepdims=True) * (1.0 / MLA_Q_RANK)
    cqn = ((cq * lax.rsqrt(ms + NORM_EPS)) * qn_ref[...]).astype(BF16)
    q = _dot(cqn, wq_ref[...]) * cos4 + _dot(cqn, wqr_ref[...]) * sin4
    q_ref[...] = (q * float((MLA_NOPE + MLA_ROPE) ** -0.5 * np.log2(np.e))).astype(q_ref.dtype)

    ckv = ckv_ref[...].astype(F32)
    ms = jnp.mean(ckv * ckv, axis=-1, keepdims=True)
    ckvn = ((ckv * lax.rsqrt(ms + NORM_EPS)) * kvn_ref[...]).astype(BF16)
    kr = kr_ref[...]
    k_rot = _dot(kr, p1_ref[...]) * cos + _dot(kr, p2_ref[...]) * sin
    k = _dot(ckvn, wk_ref[...]) + jnp.concatenate([k_rot] * HEADS, axis=-1)
    k_ref[...] = k.astype(k_ref.dtype)
    v_ref[...] = (_dot(ckvn, wv_ref[...]) + one_ref[...]).astype(v_ref.dtype)


def _mla_weights(w_uq, w_ukv):
    qr = w_uq.shape[0]
    dq = MLA_NOPE + MLA_ROPE
    hr = MLA_ROPE // 2
    wq3 = w_uq.reshape(qr, HEADS, dq)
    zq = jnp.zeros((qr, HEADS, LANES - dq), F32)
    wq = jnp.concatenate([wq3, zq], axis=-1)
    x1 = wq3[..., MLA_NOPE:MLA_NOPE + hr]
    x2 = wq3[..., MLA_NOPE + hr:]
    wqr = jnp.concatenate([jnp.zeros((qr, HEADS, MLA_NOPE), F32), -x2, x1, zq], axis=-1)
    pad_rows = lambda m: jnp.pad(m.reshape(qr, HEADS * LANES), ((0, BW - qr), (0, 0)))
    kvr = w_ukv.shape[0]
    wkv3 = w_ukv.reshape(kvr, HEADS, MLA_NOPE + MLA_V)
    zk = jnp.zeros((kvr, HEADS, LANES - MLA_NOPE), F32)
    wk = jnp.concatenate([wkv3[..., :MLA_NOPE], zk], axis=-1).reshape(kvr, HEADS * LANES)
    wv = jnp.concatenate([wkv3[..., MLA_NOPE:], zk], axis=-1).reshape(kvr, HEADS * LANES)
    j = np.arange(hr)
    p1 = np.zeros((LANES, LANES), np.float32)
    p1[np.arange(MLA_ROPE), MLA_NOPE + np.arange(MLA_ROPE)] = 1.0
    p2 = np.zeros((LANES, LANES), np.float32)
    p2[hr + j, MLA_NOPE + j] = -1.0
    p2[j, MLA_NOPE + hr + j] = 1.0
    one = np.zeros((1, HEADS * LANES), np.float32)
    one[0, MLA_V + LANES * np.arange(HEADS)] = 1.0
    return (pad_rows(wq).astype(BF16), pad_rows(wqr).astype(BF16), wk.astype(BF16), wv.astype(BF16),
            jnp.asarray(p1, BF16), jnp.asarray(p2, BF16), jnp.asarray(one))


def mla_prep(u, cb0, cos, sin, q_norm, kv_norm, weights, tm, rope_blocks):
    n = u.shape[0]
    wq, wqr, wk, wv, p1, p2, one = weights
    qn = jnp.pad(q_norm, (0, BW - q_norm.shape[0])).reshape(1, BW)
    full = lambda a: pl.BlockSpec(a.shape, lambda i: (0,) * a.ndim)
    tab = pl.BlockSpec((tm, LANES), lambda i: (i % rope_blocks, 0))
    kv_cb = (cb0 + CB_MKV) * (BW // LANES)
    out = jax.ShapeDtypeStruct((n, HEADS * LANES), BF16)
    ospec = pl.BlockSpec((tm, HEADS * LANES), lambda i: (i, 0))
    return pl.pallas_call(
        _mla_prep_kernel,
        grid=(n // tm,),
        in_specs=[
            pl.BlockSpec((tm, BW), lambda i: (i, cb0 + CB_MCQ)),
            pl.BlockSpec((tm, LANES), lambda i: (i, kv_cb)),
            pl.BlockSpec((tm, LANES), lambda i: (i, kv_cb + 1)),
            tab, tab, full(qn), pl.BlockSpec((1, LANES), lambda i: (0, 0)),
            full(wq), full(wqr), full(wk), full(wv), full(p1), full(p2), full(one),
        ],
        out_specs=[ospec, ospec, ospec],
        out_shape=[out, out, out],
        compiler_params=_cp(("arbitrary",)),
        name="mla_prep",
    )(u, u, u, cos, sin, qn, kv_norm.reshape(1, LANES), wq, wqr, wk, wv, p1, p2, one)


def _flash_kernel(*refs, lens, tk):
    q_ref = refs[0]
    kv_refs = refs[1:1 + 2 * len(lens)]
    o_ref = refs[1 + 2 * len(lens)]
    q = q_ref[...]
    tq = q.shape[0]
    m = jnp.full((tq, 1), NEG, F32)
    acc = jnp.zeros((tq, LANES), F32)

    def chunk(kc, vc, m, acc):
        s = _dot_nt(q, kc)
        m_new = jnp.maximum(m, jnp.max(s, axis=-1, keepdims=True))
        p = jnp.exp2((s - m_new).astype(BF16))
        acc = jnp.exp2(m - m_new) * acc + _dot(p, vc)
        return m_new, acc

    for si, length in enumerate(lens):
        k_ref, v_ref = kv_refs[2 * si], kv_refs[2 * si + 1]
        step = min(tk, length)
        if length == step:
            m, acc = chunk(k_ref[...], v_ref[...], m, acc)
        else:
            def body(j, carry, k_ref=k_ref, v_ref=v_ref, step=step):
                j0 = pl.multiple_of(j * step, step)
                return chunk(k_ref[pl.ds(j0, step), :], v_ref[pl.ds(j0, step), :], *carry)

            m, acc = lax.fori_loop(0, length // step, body, (m, acc), unroll=4)

    lane = lax.broadcasted_iota(jnp.int32, (1, LANES), 1)
    l = jnp.sum(jnp.where(lane == MLA_V, acc, 0.0), axis=-1, keepdims=True)
    o_ref[...] = jnp.where(lane < MLA_V, acc / l, 0.0).astype(o_ref.dtype)


def flash_attention(q, kvs, batch, tq, tk):
    n = q.shape[0]
    nq = n // batch // tq
    lens = tuple(k.shape[0] // batch for k, _ in kvs)
    in_specs = [pl.BlockSpec((tq, LANES), lambda b, h, i: (b * nq + i, h))]
    args = [q]
    for (k, v), length in zip(kvs, lens):
        in_specs += [pl.BlockSpec((length, LANES), lambda b, h, i: (b, h))] * 2
        args += [k, v]
    return pl.pallas_call(
        functools.partial(_flash_kernel, lens=lens, tk=tk),
        grid=(batch, HEADS, nq),
        in_specs=in_specs,
        out_specs=pl.BlockSpec((tq, LANES), lambda b, h, i: (b * nq + i, h)),
        out_shape=jax.ShapeDtypeStruct((n, HEADS * LANES), BF16),
        compiler_params=_cp(("arbitrary", "arbitrary", "arbitrary")),
        name="mla_attention",
    )(*args)


def _merge_kernel(*refs, with_router):
    (gate_ref, yr_f_ref, yr_b_ref, yf_ref, yn_ref, ym_ref, x_ref, g1_ref, wb_ret_ref, wb_f_ref,
     wb_na_ref, wb_mla_ref, wo_ref, gain_ref, sc_ref, sh_ref) = refs[:16]
    rest = refs[16:]
    if with_router:
        rh_ref, rl_ref, x_out_ref, lg_out_ref, hp_out_ref = rest
    else:
        x_out_ref, h_out_ref = rest
    d = x_ref.shape[1]

    def gated(k, y):
        g = jax.nn.sigmoid(gate_ref[:, k * d:(k + 1) * d].astype(F32))
        return g * y

    m = gated(0, _dot(yr_f_ref[...] + yr_b_ref[...], wb_ret_ref[...]))
    m = m + gated(1, _dot(yf_ref[...], wb_f_ref[...]))
    m = m + gated(2, _dot(yn_ref[...], wb_na_ref[...]))
    m = m + gated(3, _dot(ym_ref[...], wb_mla_ref[...]))
    y = _dot(m.astype(BF16), wo_ref[...])
    x = x_ref[...] + g1_ref[0] * y
    x_out_ref[...] = x
    hn = x * lax.rsqrt(jnp.mean(x * x, axis=-1, keepdims=True) + NORM_EPS)
    h = (hn * gain_ref[...]) * (1.0 + sc_ref[0]) + sh_ref[0]
    if with_router:
        hp_out_ref[...] = _pack_bf16_pairs(h)
        h_hi = h.astype(BF16)
        h_lo = (h - h_hi.astype(F32)).astype(BF16)
        lg_out_ref[...] = (_dot(h_hi, rh_ref[...]) + _dot(h_lo, rh_ref[...])) + _dot(h_hi, rl_ref[...])
    else:
        h_out_ref[...] = h.astype(h_out_ref.dtype)


def merge(u, y_ret, y_four, y_na, y_mla, x, mods, mod_row, gain, wb, w_out, router, tm):
    n, d = x.shape
    wb_ret, wb_f, wb_na, wb_mla = wb
    full = lambda a: pl.BlockSpec(a.shape, lambda i: (0,) * a.ndim)
    br = lambda: pl.BlockSpec((tm, BW), lambda i: (i, 0))
    tok = lambda: pl.BlockSpec((tm, d), lambda i: (i, 0))
    in_specs = [
        pl.BlockSpec((tm, 4 * d), lambda i: (i, 0)),
        br(), br(), br(), br(),
        pl.BlockSpec((tm, HEADS * LANES), lambda i: (i, 0)),
        tok(), _mod_spec(d, mod_row, tm, 2),
        full(wb_ret), full(wb_f), full(wb_na), full(wb_mla), full(w_out),
        pl.BlockSpec((1, d), lambda i: (0, 0)), _mod_spec(d, mod_row, tm, 4), _mod_spec(d, mod_row, tm, 3),
    ]
    args = [u, y_ret[0], y_ret[1], y_four, y_na, y_mla, x, mods, wb_ret, wb_f, wb_na, wb_mla, w_out,
            gain.reshape(1, d), mods, mods]
    out_specs = [tok()]
    out_shape = [jax.ShapeDtypeStruct((n, d), F32)]
    if router is None:
        out_specs.append(tok())
        out_shape.append(jax.ShapeDtypeStruct((n, d), BF16))
    else:
        in_specs += [full(router[0]), full(router[1])]
        args += list(router)
        out_specs += [pl.BlockSpec((tm, LANES), lambda i: (i, 0)), pl.BlockSpec((tm, d // 2), lambda i: (i, 0))]
        out_shape += [jax.ShapeDtypeStruct((n, LANES), F32), jax.ShapeDtypeStruct((n, d // 2), jnp.int32)]
    return pl.pallas_call(
        functools.partial(_merge_kernel, with_router=router is not None),
        grid=(n // tm,),
        in_specs=in_specs,
        out_specs=out_specs,
        out_shape=out_shape,
        compiler_params=_cp(("arbitrary",)),
        name="merge",
    )(*args)


def _ffn_kernel(h_ref, wg_ref, wu_ref, wd_ref, x_ref, g2_ref, o_ref, acc_ref):
    f = pl.program_id(1)

    @pl.when(f == 0)
    def _():
        acc_ref[...] = jnp.zeros_like(acc_ref)

    h = h_ref[...]
    a = _silu(_dot(h, wg_ref[...])) * _dot(h, wu_ref[...])
    acc_ref[...] += _dot(a.astype(BF16), wd_ref[...])

    @pl.when(f == pl.num_programs(1) - 1)
    def _():
        o_ref[...] = x_ref[...] + g2_ref[0] * acc_ref[...]


def ffn(h, x, mods, mod_row, wg, wu, wd, tm, tf):
    n, d = x.shape
    nf = wg.shape[1] // tf
    return pl.pallas_call(
        _ffn_kernel,
        grid=(n // tm, nf),
        in_specs=[
            pl.BlockSpec((tm, d), lambda i, f: (i, 0)),
            pl.BlockSpec((d, tf), lambda i, f: (0, f)),
            pl.BlockSpec((d, tf), lambda i, f: (0, f)),
            pl.BlockSpec((tf, d), lambda i, f: (f, 0)),
            pl.BlockSpec((tm, d), lambda i, f: (i, 0)),
            _mod_spec(d, mod_row, tm, 5),
        ],
        out_specs=pl.BlockSpec((tm, d), lambda i, f: (i, 0)),
        out_shape=jax.ShapeDtypeStruct((n, d), F32),
        scratch_shapes=[pltpu.VMEM((tm, d), F32)],
        compiler_params=_cp(("arbitrary", "arbitrary")),
        name="ffn",
    )(h, wg, wu, wd, x, mods)


def _moe_kernel(be_ref, nu_ref, xp_ref, wg_ref, wu_ref, wd_ref, o_ref, acc_ref, x_ref):
    i = pl.program_id(0)
    f = pl.program_id(1)
    used = i < nu_ref[0]

    @pl.when(f == 0)
    def _():
        acc_ref[...] = jnp.zeros_like(acc_ref)
        x_ref[...] = _unpack_bf16_pairs(xp_ref[...]).astype(BF16)

    @pl.when(used)
    def _():
        x = x_ref[...]
        a = _silu(_dot(x, wg_ref[0])) * _dot(x, wu_ref[0])
        acc_ref[...] += _dot(a.astype(BF16), wd_ref[0])

    @pl.when(f == pl.num_programs(1) - 1)
    def _():
        o_ref[...] = _pack_bf16_pairs(acc_ref[...])


SC_CAST_BLOCK = (16, 512)


def sc_cast_bf16(w):
    e, a, b = w.shape
    br, bc = SC_CAST_BLOCK
    assert (e * a) % br == 0 and b % bc == 0
    mesh = plsc.VectorSubcoreMesh(core_axis_name="c", subcore_axis_name="s")

    @functools.partial(pl.kernel, mesh=mesh, out_type=jax.ShapeDtypeStruct((e * a, b), BF16), scratch_types=[])
    def cast(x_hbm, o_hbm):
        def body(in_v, out_v):
            @pl.loop(0, br, step=2)
            def _(r):
                @pl.loop(0, bc, step=16)
                def _(c):
                    top = in_v.at[pl.ds(r, 1), pl.ds(c, 16)][...]
                    bot = in_v.at[pl.ds(r + 1, 1), pl.ds(c, 16)][...]
                    out_v.at[pl.ds(r, 2), pl.ds(c, 16)][...] = jnp.concatenate([top, bot], axis=0).astype(BF16)

        pltpu.emit_pipeline(
            body,
            grid=(e * a // br, b // bc),
            in_specs=[pl.BlockSpec((br, bc), lambda i, j: (i, j))],
            out_specs=[pl.BlockSpec((br, bc), lambda i, j: (i, j))],
            core_axis_name=("c", "s"),
            dimension_semantics=(pltpu.PARALLEL, pltpu.PARALLEL),
        )(x_hbm, o_hbm)

    return cast(w.reshape(e * a, b)).reshape(e, a, b)


def moe_ffn(blk_exp, n_used, xb, wg, wu, wd, tm, tf):
    n, dp = xb.shape
    d = 2 * dp
    nf = wg.shape[2] // tf
    grid_spec = pltpu.PrefetchScalarGridSpec(
        num_scalar_prefetch=2,
        grid=(n // tm, nf),
        in_specs=[
            pl.BlockSpec((tm, dp), lambda i, f, be, nu: (i, 0)),
            pl.BlockSpec((1, d, tf), lambda i, f, be, nu: (be[i], 0, f)),
            pl.BlockSpec((1, d, tf), lambda i, f, be, nu: (be[i], 0, f)),
            pl.BlockSpec((1, tf, d), lambda i, f, be, nu: (be[i], f, 0)),
        ],
        out_specs=pl.BlockSpec((tm, dp), lambda i, f, be, nu: (i, 0)),
        scratch_shapes=[pltpu.VMEM((tm, d), F32), pltpu.VMEM((tm, d), BF16)],
    )
    return pl.pallas_call(
        _moe_kernel,
        grid_spec=grid_spec,
        out_shape=jax.ShapeDtypeStruct((n, dp), jnp.int32),
        compiler_params=_cp(("arbitrary", "arbitrary")),
        name="moe_ffn",
    )(blk_exp, n_used, xb, wg, wu, wd)


def _combine_kernel(x_ref, ya_ref, yb_ref, w_ref, g2_ref, gain_ref, o_ref, *, final):
    w = w_ref[...]
    y = w[:, 0:1] * _unpack_bf16_pairs(ya_ref[...]) + w[:, 1:2] * _unpack_bf16_pairs(yb_ref[...])
    x = x_ref[...] + g2_ref[0] * y
    if final:
        x = (x * lax.rsqrt(jnp.mean(x * x, axis=-1, keepdims=True) + NORM_EPS)) * gain_ref[...]
    o_ref[...] = x


def moe_combine(x, y2, w, mods, mod_row, gain, final, tm):
    n, d = x.shape
    tok = lambda: pl.BlockSpec((tm, d), lambda i: (i, 0))
    half = lambda off: pl.BlockSpec((tm, d // 2), lambda i: (i + off, 0))
    return pl.pallas_call(
        functools.partial(_combine_kernel, final=final),
        grid=(n // tm,),
        in_specs=[tok(), half(0), half(n // tm),
                  pl.BlockSpec((tm, MOE_TOP_K), lambda i: (i, 0)),
                  _mod_spec(d, mod_row, tm, 5), pl.BlockSpec((1, d), lambda i: (0, 0))],
        out_specs=tok(),
        out_shape=jax.ShapeDtypeStruct((n, d), F32),
        compiler_params=_cp(("arbitrary",)),
        name="moe_combine",
    )(x, y2, y2, w, mods, gain.reshape(1, d))


def _rmsnorm_kernel(x_ref, gain_ref, o_ref):
    x = x_ref[...]
    o_ref[...] = (x * lax.rsqrt(jnp.mean(x * x, axis=-1, keepdims=True) + NORM_EPS)) * gain_ref[...]


def rmsnorm_rows(x, gain, tm):
    n, d = x.shape
    return pl.pallas_call(
        _rmsnorm_kernel,
        grid=(n // tm,),
        in_specs=[pl.BlockSpec((tm, d), lambda i: (i, 0)), pl.BlockSpec((1, d), lambda i: (0, 0))],
        out_specs=pl.BlockSpec((tm, d), lambda i: (i, 0)),
        out_shape=jax.ShapeDtypeStruct((n, d), F32),
        compiler_params=_cp(("arbitrary",)),
        name="final_norm",
    )(x, gain.reshape(1, d))


SC_CORES = 2
SC_SUBCORES = 16
SC_CHUNK = 64


def sc_gather_rows(table, idx):
    n_out = idx.shape[0]
    width = table.shape[1]
    workers = SC_CORES * SC_SUBCORES
    per_worker = n_out // workers
    assert n_out == per_worker * workers and per_worker % SC_CHUNK == 0 and table.dtype == jnp.int32
    mesh = plsc.VectorSubcoreMesh(core_axis_name="c", subcore_axis_name="s")

    @functools.partial(
        pl.kernel, mesh=mesh,
        out_type=jax.ShapeDtypeStruct((n_out, width), table.dtype),
        scratch_types=[pltpu.VMEM((SC_CHUNK,), jnp.int32), pltpu.VMEM((SC_CHUNK, width), table.dtype),
                       pltpu.SemaphoreType.DMA],
    )
    def gather(table_hbm, idx_hbm, out_hbm, idx_v, rows_v, sem):
        base = (lax.axis_index("s") * SC_CORES + lax.axis_index("c")) * per_worker

        @pl.loop(0, per_worker // SC_CHUNK)
        def _(j):
            off = pl.multiple_of(base + j * SC_CHUNK, SC_CHUNK)
            pltpu.sync_copy(idx_hbm.at[pl.ds(off, SC_CHUNK)], idx_v)
            pltpu.async_copy(table_hbm.at[idx_v], rows_v, sem).wait()
            pltpu.sync_copy(rows_v, out_hbm.at[pl.ds(off, SC_CHUNK)])

    return gather(table, idx)


def moe_route(logits, n_experts, tm):
    n_tok = logits.shape[0]
    top_logit, top_idx = lax.top_k(logits[:, :n_experts], MOE_TOP_K)
    top_w = jax.nn.softmax(top_logit, axis=-1)
    e_flat = top_idx.reshape(-1).astype(jnp.int32)
    n_assign = e_flat.shape[0]
    onehot = (e_flat[:, None] == jnp.arange(n_experts, dtype=jnp.int32)[None, :]).astype(jnp.int32)
    rank = jnp.sum((jnp.cumsum(onehot, axis=0) - onehot) * onehot, axis=1)
    counts = jnp.sum(onehot, axis=0)
    padded = (counts + tm - 1) // tm * tm
    pad_end = jnp.cumsum(padded)
    pad_start = pad_end - padded
    dest = pad_start[e_flat] + rank
    n_rows = n_assign + n_experts * tm
    tok = jnp.arange(n_assign, dtype=jnp.int32) // MOE_TOP_K
    blk_start = jnp.arange(n_rows // tm, dtype=jnp.int32) * tm
    blk_exp = jnp.minimum(jnp.sum(pad_end[None, :] <= blk_start[:, None], axis=1), n_experts - 1)
    n_used = (pad_end[-1] // tm).reshape(1)
    _, tok_by_row = lax.sort_key_val(dest, tok)
    max_shift = n_experts * tm
    filler = jnp.arange(max_shift, dtype=jnp.int32)
    compact = jnp.concatenate([filler, tok_by_row, filler])
    shift = pad_start - (jnp.cumsum(counts) - counts)
    row_exp = jnp.repeat(blk_exp, tm)
    row_tok = jnp.zeros((n_rows,), jnp.int32)
    for e in range(n_experts):
        shifted = lax.dynamic_slice(compact, (max_shift - shift[e],), (n_rows,))
        row_tok = jnp.where(row_exp == e, shifted, row_tok)
    return row_tok, dest.reshape(n_tok, MOE_TOP_K), top_w, blk_exp.astype(jnp.int32), n_used.astype(jnp.int32)


def _rope_split(wcols):
    d, w = wcols.shape
    half = w // HEADS // 2
    return wcols.reshape(d, HEADS, 2, half).transpose(0, 2, 1, 3).reshape(d, w)


def _inproj_weights(w_in):
    d = w_in.shape[0]
    kv = (BW, BW, BW, BW, MLA_KV_RANK, MLA_ROPE)
    qs = (BW, BW, BW, BW, BW, MLA_Q_RANK, 4 * d)
    offs = np.concatenate([[0], np.cumsum(kv + qs)])
    seg = lambda i: w_in[:, offs[i]:offs[i + 1]]
    r_k, r_v, n_k, n_v, m_ckv, m_kr = (seg(i) for i in range(6))
    r_q, r_gf, r_gb, f_in, n_q, m_cq, gate = (seg(6 + i) for i in range(7))
    z = lambda n: jnp.zeros((d, n), w_in.dtype)
    cols = [gate, _rope_split(r_q), _rope_split(r_k), r_v, r_gf, r_gb, f_in, n_q, n_k, n_v,
            m_cq, z(BW - MLA_Q_RANK), m_ckv, m_kr, z(LANES - MLA_ROPE)]
    return jnp.concatenate(cols, axis=1).astype(BF16)


def _ret_rope_tables(n):
    t = jnp.arange(n)
    row = (t // GRID_W).astype(F32)
    col = (t % GRID_W).astype(F32)
    nf = RET_DK // 4
    inv = ROPE_BASE ** (-jnp.arange(nf, dtype=F32) / nf)
    ang = jnp.concatenate([row[:, None] * inv, col[:, None] * inv], axis=-1)
    return jnp.tile(jnp.cos(ang), (1, HEADS)), jnp.tile(jnp.sin(ang), (1, HEADS))


def _mla_rope_tables(n):
    t = jnp.arange(n)
    row = (t // GRID_W).astype(F32)
    col = (t % GRID_W).astype(F32)
    nf = MLA_ROPE // 4
    inv = ROPE_BASE ** (-jnp.arange(nf, dtype=F32) / nf)
    ang = jnp.concatenate([row[:, None] * inv, col[:, None] * inv], axis=-1)
    pad = jnp.zeros((n, LANES - MLA_NOPE - MLA_ROPE), F32)
    cos = jnp.concatenate([jnp.ones((n, MLA_NOPE), F32), jnp.cos(ang), jnp.cos(ang), pad], axis=-1)
    sin = jnp.concatenate([jnp.zeros((n, MLA_NOPE), F32), jnp.sin(ang), jnp.sin(ang), pad], axis=-1)
    return cos, sin


def _tile_rows(*sizes):
    for tm in (1024, 512, 256, 128):
        if all(s % tm == 0 for s in sizes):
            return tm
    raise ValueError(f"token counts {sizes} need a common multiple-of-128 row tile")


def kernel(x, c, ctx, c_ctx, ada_w, ada_b, norm_mix, norm_ffn, w_in, ret_decay_fwd, ret_decay_bwd,
           mla_q_norm, mla_kv_norm, mla_w_uq, mla_w_ukv, na_rpb, w_branch, w_out,
           ffn_w_gate, ffn_w_up, ffn_w_down, moe_router, moe_w_gate, moe_w_up, moe_w_down, norm_final):
    batch, t, d = x.shape
    tc = ctx.shape[1]
    depth = ada_w.shape[0]
    nl, ncx = batch * t, batch * tc
    assert batch < 8 and t % (16 * GRID_W) == 0 and tc % LANES == 0 and d == 4 * BW
    tm = _tile_rows(t, ncx)
    cb0 = 4 * d // BW

    xl = x.reshape(nl, d)
    xc = ctx.reshape(ncx, d)
    cc = jnp.zeros((8, d), F32).at[:batch].set(c).at[batch].set(c_ctx)
    mods = adaln(cc, ada_w, ada_b).reshape(depth, 8 * 6, 1, d)
    lat_row = lambda r0: r0 // t
    ctx_row = lambda r0: batch

    ret_cos, ret_sin = _ret_rope_tables(t)
    ret_cos_c, ret_sin_c = jnp.ones((tc, LANES), F32), jnp.zeros((tc, LANES), F32)
    mla_cos, mla_sin = _mla_rope_tables(t)
    mla_cos_c = jnp.concatenate([jnp.ones((tm, MLA_NOPE + MLA_ROPE), F32),
                                 jnp.zeros((tm, LANES - MLA_NOPE - MLA_ROPE), F32)], axis=-1)
    mla_sin_c = jnp.zeros((tm, LANES), F32)
    ret_cs = 256

    for i in range(depth):
        ctx_out = i < depth - 1
        md = mods[i]
        w_p = _inproj_weights(w_in[i])
        u = norm_inproj(xl, norm_mix[i], md, lat_row, w_p, tm, 2304)
        uc = norm_inproj(xc, norm_mix[i], md, ctx_row, w_p, tm, 2304)

        lg = jnp.stack([jax.nn.log_sigmoid(ret_decay_fwd[i].astype(F32)),
                        jax.nn.log_sigmoid(ret_decay_bwd[i].astype(F32))])
        zero_state = jnp.zeros((batch, 2, BW, BW), F32)
        yc_ret, s_ctx = retention(uc, cb0, lg, ret_cos_c, ret_sin_c, zero_state, batch, min(ret_cs, tc))
        y_ret, _ = retention(u, cb0, lg, ret_cos, ret_sin, s_ctx, batch, ret_cs)

        y_four = fourier_long(u, cb0 + CB_F, batch)

        bias_tab = _na_bias_table(na_rpb[i])
        y_na = na_attention(u, uc, cb0, bias_tab, batch, 16)

        mw = _mla_weights(mla_w_uq[i], mla_w_ukv[i])
        q_l, k_l, v_l = mla_prep(u, cb0, mla_cos, mla_sin, mla_q_norm[i], mla_kv_norm[i], mw, tm, t // tm)
        q_c, k_c, v_c = mla_prep(uc, cb0, mla_cos_c, mla_sin_c, mla_q_norm[i], mla_kv_norm[i], mw, tm, 1)
        y_mla = flash_attention(q_l, [(k_l, v_l), (k_c, v_c)], batch, 1024, 512)

        wb = w_branch[i].astype(BF16)
        wb_mla = jnp.concatenate(
            [wb[3].reshape(HEADS, MLA_V, d), jnp.zeros((HEADS, LANES - MLA_V, d), BF16)], axis=1
        ).reshape(HEADS * LANES, d)
        wbs = (wb[0], wb[1], wb[2], wb_mla)
        wo = w_out[i].astype(BF16)
        j = i // 2
        if i % 2 == 0:
            xl, h2 = merge(u, y_ret, y_four, y_na, y_mla, xl, md, lat_row, norm_ffn[i], wbs, wo, None, 512)
            wg, wu, wd = ffn_w_gate[j].astype(BF16), ffn_w_up[j].astype(BF16), ffn_w_down[j].astype(BF16)
            tf = wg.shape[1] // 2
            xl = ffn(h2, xl, md, lat_row, wg, wu, wd, 512, tf)
        else:
            n_exp = moe_router.shape[2]
            r = jnp.pad(moe_router[j], ((0, 0), (0, LANES - n_exp)))
            r_hi = r.astype(BF16)
            r_lo = (r - r_hi.astype(F32)).astype(BF16)
            xl, logits, h2p = merge(u, y_ret, y_four, y_na, y_mla, xl, md, lat_row, norm_ffn[i], wbs, wo,
                                   (r_hi, r_lo), 512)
            tmm = 512
            row_tok, dest, top_w, blk_exp, n_used = moe_route(logits, n_exp, tmm)
            xb = sc_gather_rows(h2p, row_tok)
            ewg, ewu, ewd = (sc_cast_bf16(w) for w in (moe_w_gate[j], moe_w_up[j], moe_w_down[j]))
            yb = moe_ffn(blk_exp, n_used, xb, ewg, ewu, ewd, tmm, 1792)
            y2 = sc_gather_rows(yb, dest.T.reshape(-1))
            xl = moe_combine(xl, y2, top_w, md, lat_row, norm_final, i == depth - 1, 512)

        if ctx_out:
            yc_four = fourier_short(uc, cb0 + CB_F, batch)
            yc_na = na_ctx_attention(uc, cb0, batch)
            yc_mla = flash_attention(q_c, [(k_c, v_c)], batch, tc, 512)
            if i % 2 == 0:
                xc, hc2 = merge(uc, yc_ret, yc_four, yc_na, yc_mla, xc, md, ctx_row, norm_ffn[i], wbs, wo,
                                None, 512)
                xc = ffn(hc2, xc, md, ctx_row, wg, wu, wd, 512, tf)
            else:
                raise NotImplementedError("context tokens through the expert mixer")

    if depth % 2 == 1:
        xl = rmsnorm_rows(xl, norm_final, tm)
    return xl.reshape(batch, t, d)
```

```python
import functools

import numpy as np
import jax
import jax.numpy as jnp
from jax import lax
from jax.experimental import pallas as pl
from jax.experimental.pallas import tpu as pltpu
from jax.experimental.pallas import tpu_sc as plsc

F32 = jnp.float32
BF16 = jnp.bfloat16

GRID_W = 64
ROPE_BASE = 10000.0
NORM_EPS = 1e-6
HEADS = 4
RET_DK = 64
FOURIER_GROUP_DIM = 64
NA_HEAD_DIM = 64
NA_WIN_R = 8
NA_WIN_C = 16
MLA_NOPE = 64
MLA_ROPE = 32
MLA_V = 64
MLA_Q_RANK = 192
MLA_KV_RANK = 128
MOE_TOP_K = 2
BW = 256

COL_GATE = 0
CB_RQ, CB_RK, CB_RV, CB_RGF, CB_RGB, CB_F, CB_NQ, CB_NK, CB_NV, CB_MCQ, CB_MKV = range(11)
LANES = 128
NEG = -1e30

VMEM_LIMIT = 48 * 1024 * 1024
VMEM_LIMIT_LARGE = 58 * 1024 * 1024


def _cp(sem, vmem=VMEM_LIMIT):
    return pltpu.CompilerParams(dimension_semantics=sem, vmem_limit_bytes=vmem)


def _dot(a, b):
    return jnp.dot(a, b, preferred_element_type=F32)


def _dot_nt(a, b):
    return lax.dot_general(a, b, (((1,), (1,)), ((), ())), preferred_element_type=F32)


def _dot_tn(a, b):
    return lax.dot_general(a, b, (((0,), (0,)), ((), ())), preferred_element_type=F32)


def _silu(x):
    return x * jax.nn.sigmoid(x)


def _pack_bf16_pairs(x):
    k = x.shape[1] // 2
    lo = lax.bitcast_convert_type(x[:, :k].astype(BF16).astype(F32), jnp.uint32) >> 16
    hi = lax.bitcast_convert_type(x[:, k:].astype(BF16).astype(F32), jnp.uint32) & jnp.uint32(0xFFFF0000)
    return lax.bitcast_convert_type(lo | hi, jnp.int32)


def _unpack_bf16_pairs(p):
    u = lax.bitcast_convert_type(p, jnp.uint32)
    lo = lax.bitcast_convert_type(u << 16, F32)
    hi = lax.bitcast_convert_type(u & jnp.uint32(0xFFFF0000), F32)
    return jnp.concatenate([lo, hi], axis=-1)


def _adaln_kernel(c_ref, w_ref, b_ref, o_ref):
    s = _silu(c_ref[...])
    o_ref[0] = _dot(s.astype(BF16), w_ref[0].astype(BF16)) + b_ref[0]


def adaln(cc, ada_w, ada_b):
    depth, d, n6 = ada_w.shape
    tn = n6 // 4
    return pl.pallas_call(
        _adaln_kernel,
        grid=(depth, n6 // tn),
        in_specs=[
            pl.BlockSpec((8, d), lambda l, j: (0, 0)),
            pl.BlockSpec((1, d, tn), lambda l, j: (l, 0, j)),
            pl.BlockSpec((1, 1, tn), lambda l, j: (l, 0, j)),
        ],
        out_specs=pl.BlockSpec((1, 8, tn), lambda l, j: (l, 0, j)),
        out_shape=jax.ShapeDtypeStruct((depth, 8, n6), F32),
        compiler_params=_cp(("arbitrary", "arbitrary")),
        name="adaln",
    )(cc, ada_w, ada_b.reshape(depth, 1, n6))


def _inproj_kernel(x_ref, g_ref, sc_ref, sh_ref, w_ref, o_ref, h_ref):
    @pl.when(pl.program_id(1) == 0)
    def _():
        x = x_ref[...]
        y = x * lax.rsqrt(jnp.mean(x * x, axis=-1, keepdims=True) + NORM_EPS)
        h = (y * g_ref[...]) * (1.0 + sc_ref[0]) + sh_ref[0]
        h_ref[...] = h.astype(BF16)

    o_ref[...] = _dot(h_ref[...], w_ref[...]).astype(o_ref.dtype)


def _mod_spec(d, mod_row, tm, k):
    return pl.BlockSpec((1, 1, d), lambda i, *_: (mod_row(i * tm) * 6 + k, 0, 0))


def norm_inproj(x, gain, mods, mod_row, w, tm, tn):
    n, d = x.shape
    nc = w.shape[1]
    return pl.pallas_call(
        _inproj_kernel,
        grid=(n // tm, nc // tn),
        in_specs=[
            pl.BlockSpec((tm, d), lambda i, j: (i, 0)),
            pl.BlockSpec((1, d), lambda i, j: (0, 0)),
            _mod_spec(d, mod_row, tm, 1),
            _mod_spec(d, mod_row, tm, 0),
            pl.BlockSpec((d, tn), lambda i, j: (0, j)),
        ],
        out_specs=pl.BlockSpec((tm, tn), lambda i, j: (i, j)),
        out_shape=jax.ShapeDtypeStruct((n, nc), BF16),
        scratch_shapes=[pltpu.VMEM((tm, d), BF16)],
        compiler_params=_cp(("arbitrary", "arbitrary")),
        name="norm_inproj",
    )(x, gain.reshape(1, d), mods, mods, w)


def _ret_init(d, cs, w, lg_ref, s0_ref, lgq_ref, s_ref, decay_ref, qw_ref, kw_ref, avg_ref):
    rev = d == 1
    s_ref[d] = s0_ref[0, d]
    pos_i = lax.broadcasted_iota(jnp.int32, (cs, 1), 0).astype(F32)
    pos_j = lax.broadcasted_iota(jnp.int32, (1, cs), 1).astype(F32)
    p_i = jnp.where(rev, cs - 1.0 - pos_i, pos_i)
    p_j = jnp.where(rev, cs - 1.0 - pos_j, pos_j)
    diff = p_i - p_j
    for h in range(HEADS):
        decay_ref[d, h] = jnp.where(diff >= 0, jnp.exp(lg_ref[d, h] * jnp.maximum(diff, 0.0)), 0.0)
    lgq = lgq_ref[d]
    qw_ref[d] = jnp.exp(lgq * (p_i + 1.0))
    kw_ref[d] = jnp.exp(lgq * (cs - 1.0 - p_i))
    hd = w // HEADS
    gi = lax.broadcasted_iota(jnp.int32, (w, w), 0) // hd
    gj = lax.broadcasted_iota(jnp.int32, (w, w), 1) // hd
    avg_ref[...] = jnp.where(gi == gj, 1.0 / hd, 0.0).astype(BF16)


def _ret_chunk(d, q_ref, k_ref, v_ref, g_ref, cos_ref, sin_ref, lgv_ref, y_ref, s_ref, decay_ref, qw_ref, kw_ref,
               avg_ref):
    cs = q_ref.shape[0]
    w = q_ref.shape[1]
    half = w // 2
    lgv = lgv_ref[d]
    s_ref, decay_ref, qw_ref, kw_ref = s_ref.at[d], decay_ref.at[d], qw_ref.at[d], kw_ref.at[d]

    cos = cos_ref[...]
    sin = sin_ref[...]

    def rope(t):
        t1, t2 = t[:, :half], t[:, half:]
        return jnp.concatenate([t1 * cos - t2 * sin, t2 * cos + t1 * sin], axis=-1)

    q = rope(q_ref[...].astype(F32))
    k = rope(k_ref[...].astype(F32)) * (RET_DK ** -0.5)
    vb = v_ref[...]

    lane = lax.broadcasted_iota(jnp.int32, (1, w), 1)
    head_q = (lane % half) // (half // HEADS)
    head_v = lane // (w // HEADS)

    s_prev = s_ref[...]
    o = _dot((q * qw_ref[...]).astype(BF16), s_prev.astype(BF16))
    qb = q.astype(BF16)
    kb = k.astype(BF16)
    zero_b = jnp.zeros_like(qb)
    for h in range(HEADS):
        a = _dot_nt(jnp.where(head_q == h, qb, zero_b), kb)
        oh = _dot((a * decay_ref[h]).astype(BF16), vb)
        o = o + jnp.where(head_v == h, oh, 0.0)

    ds = _dot_tn((k * kw_ref[...]).astype(BF16), vb)
    row_head = (lax.broadcasted_iota(jnp.int32, (w, 1), 0) % half) // (half // HEADS)
    s_new = s_prev * jnp.exp(lgv * float(cs)) + jnp.where(row_head == head_v, ds, 0.0)
    s_ref[...] = s_new

    ms = _dot((o * o).astype(BF16), avg_ref[...])
    on = o * lax.rsqrt(ms + NORM_EPS)
    y_ref[...] = (_silu(g_ref[...].astype(F32)) * on).astype(y_ref.dtype)
    return s_new


def _ret_kernel(lg_ref, qf, kf, vf, gf, cosf, sinf, qb, kb, vb, gb, cosb, sinb, s0_ref, lgq_ref, lgv_ref,
                yf_ref, yb_ref, sout_ref, s_ref, decay_ref, qw_ref, kw_ref, avg_ref, *, n_chunks):
    c = pl.program_id(1)
    scratch = (s_ref, decay_ref, qw_ref, kw_ref, avg_ref)

    @pl.when(c == 0)
    def _():
        for d in range(2):
            _ret_init(d, qf.shape[0], qf.shape[1], lg_ref, s0_ref, lgq_ref, *scratch)

    s_f = _ret_chunk(0, qf, kf, vf, gf, cosf, sinf, lgv_ref, yf_ref, *scratch)
    s_b = _ret_chunk(1, qb, kb, vb, gb, cosb, sinb, lgv_ref, yb_ref, *scratch)

    @pl.when(c == n_chunks - 1)
    def _():
        sout_ref[0, 0] = s_f
        sout_ref[0, 1] = s_b


def retention(u, cb0, lg, cos, sin, s0, batch, cs):
    n = u.shape[0]
    t = n // batch
    nch = t // cs
    w = BW
    half = w // 2
    lgq = jnp.tile(jnp.repeat(lg, half // HEADS, axis=1), (1, 2)).reshape(2, 1, w)
    lgv = jnp.repeat(lg, w // HEADS, axis=1).reshape(2, 1, w)

    def chunk(d, c):
        return nch - 1 - c if d else c

    def direction(d):
        col = lambda cb: pl.BlockSpec((cs, w), lambda b, c: (b * nch + chunk(d, c), cb0 + cb))
        tab = lambda: pl.BlockSpec((cs, half), lambda b, c: (chunk(d, c), 0))
        return [col(CB_RQ), col(CB_RK), col(CB_RV), col(CB_RGF + d), tab(), tab()]

    y_spec = lambda d: pl.BlockSpec((cs, w), lambda b, c: (b * nch + chunk(d, c), 0))
    state_spec = pl.BlockSpec((1, 2, w, w), lambda b, c: (b, 0, 0, 0))
    lane_spec = pl.BlockSpec((2, 1, w), lambda b, c: (0, 0, 0))
    y_f, y_b, s_out = pl.pallas_call(
        functools.partial(_ret_kernel, n_chunks=nch),
        grid=(batch, nch),
        in_specs=[pl.BlockSpec(memory_space=pltpu.SMEM)] + direction(0) + direction(1)
        + [state_spec, lane_spec, lane_spec],
        out_specs=[y_spec(0), y_spec(1), state_spec],
        out_shape=[
            jax.ShapeDtypeStruct((n, w), BF16),
            jax.ShapeDtypeStruct((n, w), BF16),
            jax.ShapeDtypeStruct((batch, 2, w, w), F32),
        ],
        scratch_shapes=[pltpu.VMEM((2, w, w), F32), pltpu.VMEM((2, HEADS, cs, cs), F32),
                        pltpu.VMEM((2, cs, w), F32), pltpu.VMEM((2, cs, w), F32), pltpu.VMEM((w, w), BF16)],
        compiler_params=_cp(("arbitrary", "arbitrary")),
        name="retention",
    )(lg, u, u, u, u, cos, sin, u, u, u, u, cos, sin, s0, lgq, lgv)
    return (y_f, y_b), s_out


def _dft_tables(t, t1, t2):
    k1 = jnp.arange(t1, dtype=jnp.int32)
    a = jnp.arange(t1, dtype=jnp.int32)
    m = jnp.arange(t2, dtype=jnp.int32)
    ph1 = (k1[None, :, None] * (a[None, None, :] * t2 + m[:, None, None])) % t
    ang1 = ph1.astype(F32) * (2.0 * np.pi / t)
    ph2 = (m[:, None] * m[None, :]) % t2
    ang2 = ph2.astype(F32) * (2.0 * np.pi / t2)
    return (jnp.cos(ang1).astype(BF16), jnp.sin(ang1).astype(BF16),
            jnp.cos(ang2).astype(BF16), jnp.sin(ang2).astype(BF16))


def _channel_tables(width):
    ch = jnp.arange(width, dtype=jnp.int32)
    same = (ch[:, None] // FOURIER_GROUP_DIM) == (ch[None, :] // FOURIER_GROUP_DIM)
    ph = ((ch[:, None] % FOURIER_GROUP_DIM) * (ch[None, :] % FOURIER_GROUP_DIM)) % FOURIER_GROUP_DIM
    ang = ph.astype(F32) * (2.0 * np.pi / FOURIER_GROUP_DIM)
    return (jnp.where(same, jnp.cos(ang), 0.0).astype(BF16),
            jnp.where(same, jnp.sin(ang), 0.0).astype(BF16))


def _fourier_kernel(x_ref, c1_ref, s1_ref, c2_ref, s2_ref, cc_ref, sc_ref, o_ref, xa, yre, yim,
                    *, t1, t2, norm):
    xa[...] = x_ref[...].astype(F32)

    def stage1(m, carry):
        xs = xa[pl.ds(m, t1, stride=t2), :].astype(BF16)
        r0 = pl.multiple_of(m * t1, t1)
        yre[pl.ds(r0, t1), :] = _dot(c1_ref[m], xs)
        yim[pl.ds(r0, t1), :] = -_dot(s1_ref[m], xs)
        return carry

    lax.fori_loop(0, t2, stage1, 0, unroll=8)

    c2 = c2_ref[...]
    s2 = s2_ref[...]
    cc = cc_ref[...]
    sc = sc_ref[...]

    def stage2(k1, carry):
        yr = yre[pl.ds(k1, t2, stride=t1), :].astype(BF16)
        yi = yim[pl.ds(k1, t2, stride=t1), :].astype(BF16)
        zr = _dot(c2, yr) + _dot(s2, yi)
        zi = _dot(c2, yi) - _dot(s2, yr)
        out = (_dot(zr.astype(BF16), cc) + _dot(zi.astype(BF16), sc)) * norm
        xa[pl.ds(k1, t2, stride=t1), :] = out
        return carry

    lax.fori_loop(0, t1, stage2, 0, unroll=4)
    o_ref[...] = xa[...].astype(o_ref.dtype)


def fourier_long(u, cb, batch, t2=LANES):
    n = u.shape[0]
    t = n // batch
    t1 = t // t2
    c1, s1, c2, s2 = _dft_tables(t, t1, t2)
    cc, sc = _channel_tables(LANES)
    norm = float(1.0 / np.sqrt(t * FOURIER_GROUP_DIM))
    full = lambda shape: pl.BlockSpec(shape, lambda b, hh: (0,) * len(shape))
    return pl.pallas_call(
        functools.partial(_fourier_kernel, t1=t1, t2=t2, norm=norm),
        grid=(batch, BW // LANES),
        in_specs=[
            pl.BlockSpec((t, LANES), lambda b, hh: (b, cb * (BW // LANES) + hh)),
            full((t2, t1, t1)), full((t2, t1, t1)), full((t2, t2)), full((t2, t2)),
            full((LANES, LANES)), full((LANES, LANES)),
        ],
        out_specs=pl.BlockSpec((t, LANES), lambda b, hh: (b, hh)),
        out_shape=jax.ShapeDtypeStruct((n, BW), BF16),
        scratch_shapes=[pltpu.VMEM((t, LANES), F32)] * 3,
        compiler_params=_cp(("arbitrary", "arbitrary")),
        name="fourier",
    )(u, c1, s1, c2, s2, cc, sc)


def _fourier_small_kernel(x_ref, ct_ref, st_ref, cc_ref, sc_ref, o_ref, *, norm):
    x = x_ref[...]
    zr = _dot(ct_ref[...], x)
    zi = -_dot(st_ref[...], x)
    out = (_dot(zr.astype(BF16), cc_ref[...]) + _dot(zi.astype(BF16), sc_ref[...])) * norm
    o_ref[...] = out.astype(o_ref.dtype)


def fourier_short(u, cb, batch):
    n = u.shape[0]
    t = n // batch
    pos = jnp.arange(t, dtype=jnp.int32)
    ang = ((pos[:, None] * pos[None, :]) % t).astype(F32) * (2.0 * np.pi / t)
    ct, st = jnp.cos(ang).astype(BF16), jnp.sin(ang).astype(BF16)
    cc, sc = _channel_tables(BW)
    norm = float(1.0 / np.sqrt(t * FOURIER_GROUP_DIM))
    full = lambda shape: pl.BlockSpec(shape, lambda b: (0,) * len(shape))
    return pl.pallas_call(
        functools.partial(_fourier_small_kernel, norm=norm),
        grid=(batch,),
        in_specs=[pl.BlockSpec((t, BW), lambda b: (b, cb)), full((t, t)), full((t, t)),
                  full((BW, BW)), full((BW, BW))],
        out_specs=pl.BlockSpec((t, BW), lambda b: (b, 0)),
        out_shape=jax.ShapeDtypeStruct((n, BW), BF16),
        compiler_params=_cp(("arbitrary",)),
        name="fourier_ctx",
    )(u, ct, st, cc, sc)


def _na_bias_table(rpb):
    n_r, n_c = rpb.shape[1], rpb.shape[2]
    span = 2 * GRID_W
    left = GRID_W - NA_WIN_C
    vp = jnp.pad(rpb.astype(F32), ((0, 0), (0, 0), (left, span - n_c - left)))
    rep = jnp.broadcast_to(vp[:, :, None, :], (HEADS, n_r, GRID_W, span)).reshape(HEADS, n_r, GRID_W * span)
    skew = rep[..., :GRID_W * (span - 1)].reshape(HEADS, n_r, GRID_W, span - 1)
    toep = skew[..., GRID_W - 1:]
    qc = np.arange(GRID_W)[:, None]
    kc = np.arange(GRID_W)[None, :]
    start = np.clip(qc - NA_WIN_C // 2, 0, GRID_W - NA_WIN_C)
    valid = (kc >= start) & (kc < start + NA_WIN_C)
    toep = jnp.where(valid, toep, NEG)
    tab = jnp.stack([toep[:, NA_WIN_R - 1 - v:2 * NA_WIN_R - 1 - v] for v in range(NA_WIN_R)])
    tab = tab.transpose(0, 1, 3, 2, 4)
    return tab.reshape(NA_WIN_R, HEADS * GRID_W, NA_WIN_R * GRID_W)


def _na_kernel(q_ref, k_ref, v_ref, kc_ref, vc_ref, bias_ref, o_ref, *, rows_per_step, n_rows):
    i = pl.program_id(1)
    w = q_ref.shape[1]
    lane = lax.broadcasted_iota(jnp.int32, (1, w), 1)
    head = lane // (w // HEADS)
    scale = jnp.asarray(NA_HEAD_DIM ** -0.5, q_ref.dtype)
    kc = kc_ref[...]
    vc = vc_ref[...]
    win = NA_WIN_R * GRID_W

    def row(rl, carry):
        r = i * rows_per_step + rl
        rs = jnp.clip(r - NA_WIN_R // 2, 0, n_rows - NA_WIN_R)
        var = r - rs
        q0 = pl.multiple_of(rl * GRID_W, GRID_W)
        k0 = pl.multiple_of(rs * GRID_W, GRID_W)
        q = q_ref[pl.ds(q0, GRID_W), :] * scale
        kw = k_ref[pl.ds(k0, win), :]
        vw = v_ref[pl.ds(k0, win), :]
        zero_b = jnp.zeros_like(q)
        q4 = jnp.concatenate([jnp.where(head == h, q, zero_b) for h in range(HEADS)], axis=0)
        s_loc = _dot_nt(q4, kw) + bias_ref[var]
        s_ctx = _dot_nt(q4, kc)
        m = jnp.maximum(jnp.max(s_loc, axis=-1, keepdims=True), jnp.max(s_ctx, axis=-1, keepdims=True))
        p_loc = jnp.exp(s_loc - m)
        p_ctx = jnp.exp(s_ctx - m)
        l = jnp.sum(p_loc, axis=-1, keepdims=True) + jnp.sum(p_ctx, axis=-1, keepdims=True)
        pv = (_dot(p_loc.astype(BF16), vw) + _dot(p_ctx.astype(BF16), vc)) / l
        acc = jnp.zeros((GRID_W, w), F32)
        for h in range(HEADS):
            acc = acc + jnp.where(head == h, pv[h * GRID_W:(h + 1) * GRID_W], 0.0)
        o_ref[pl.ds(q0, GRID_W), :] = acc.astype(o_ref.dtype)
        return carry

    lax.fori_loop(0, rows_per_step, row, 0, unroll=2)


def na_attention(u, uc, cb0, bias_tab, batch, rows_per_step):
    n = u.shape[0]
    t = n // batch
    tc = uc.shape[0] // batch
    n_rows = t // GRID_W
    steps = n_rows // rows_per_step
    tq = rows_per_step * GRID_W
    return pl.pallas_call(
        functools.partial(_na_kernel, rows_per_step=rows_per_step, n_rows=n_rows),
        grid=(batch, steps),
        in_specs=[
            pl.BlockSpec((tq, BW), lambda b, i: (b * steps + i, cb0 + CB_NQ)),
            pl.BlockSpec((t, BW), lambda b, i: (b, cb0 + CB_NK)),
            pl.BlockSpec((t, BW), lambda b, i: (b, cb0 + CB_NV)),
            pl.BlockSpec((tc, BW), lambda b, i: (b, cb0 + CB_NK)),
            pl.BlockSpec((tc, BW), lambda b, i: (b, cb0 + CB_NV)),
            pl.BlockSpec(bias_tab.shape, lambda b, i: (0, 0, 0)),
        ],
        out_specs=pl.BlockSpec((tq, BW), lambda b, i: (b * steps + i, 0)),
        out_shape=jax.ShapeDtypeStruct((n, BW), BF16),
        compiler_params=_cp(("arbitrary", "arbitrary")),
        name="na_attention",
    )(u, u, u, uc, uc, bias_tab)


def _na_ctx_kernel(q_ref, k_ref, v_ref, o_ref):
    w = q_ref.shape[1]
    lane = lax.broadcasted_iota(jnp.int32, (1, w), 1)
    head = lane // (w // HEADS)
    scale = NA_HEAD_DIM ** -0.5
    q = q_ref[...]
    k = k_ref[...]
    v = v_ref[...]
    zero_b = jnp.zeros_like(q)
    acc = jnp.zeros(q.shape, F32)
    for h in range(HEADS):
        s = _dot_nt(jnp.where(head == h, q, zero_b), k) * scale
        p = jnp.exp(s - jnp.max(s, axis=-1, keepdims=True))
        l = jnp.sum(p, axis=-1, keepdims=True)
        acc = acc + jnp.where(head == h, _dot(p.astype(BF16), v) / l, 0.0)
    o_ref[...] = acc.astype(o_ref.dtype)


def na_ctx_attention(uc, cb0, batch):
    tc = uc.shape[0] // batch
    spec = lambda cb: pl.BlockSpec((tc, BW), lambda b: (b, cb0 + cb))
    return pl.pallas_call(
        _na_ctx_kernel,
        grid=(batch,),
        in_specs=[spec(CB_NQ), spec(CB_NK), spec(CB_NV)],
        out_specs=pl.BlockSpec((tc, BW), lambda b: (b, 0)),
        out_shape=jax.ShapeDtypeStruct((uc.shape[0], BW), BF16),
        compiler_params=_cp(("arbitrary",)),
        name="na_ctx_attention",
    )(uc, uc, uc)


def _mla_prep_kernel(cq_ref, ckv_ref, kr_ref, cos_ref, sin_ref, qn_ref, kvn_ref, wq_ref, wqr_ref,
                     wk_ref, wv_ref, p1_ref, p2_ref, one_ref, q_ref, k_ref, v_ref):
    cos = cos_ref[...]
    sin = sin_ref[...]
    cos4 = jnp.concatenate([cos] * HEADS, axis=-1)
    sin4 = jnp.concatenate([sin] * HEADS, axis=-1)

    cq = cq_ref[...].astype(F32)
    ms = jnp.sum(cq * cq, axis=-1, keepdims=True) * (1.0 / MLA_Q_RANK)
    cqn = ((cq * lax.rsqrt(ms + NORM_EPS)) * qn_ref[...]).astype(BF16)
    q = _dot(cqn, wq_ref[...]) * cos4 + _dot(cqn, wqr_ref[...]) * sin4
    q_ref[...] = (q * float((MLA_NOPE + MLA_ROPE) ** -0.5 * np.log2(np.e))).astype(q_ref.dtype)

    ckv = ckv_ref[...].astype(F32)
    ms = jnp.mean(ckv * ckv, axis=-1, keepdims=True)
    ckvn = ((ckv * lax.rsqrt(ms + NORM_EPS)) * kvn_ref[...]).astype(BF16)
    kr = kr_ref[...]
    k_rot = _dot(kr, p1_ref[...]) * cos + _dot(kr, p2_ref[...]) * sin
    k = _dot(ckvn, wk_ref[...]) + jnp.concatenate([k_rot] * HEADS, axis=-1)
    k_ref[...] = k.astype(k_ref.dtype)
    v_ref[...] = (_dot(ckvn, wv_ref[...]) + one_ref[...]).astype(v_ref.dtype)


def _mla_weights(w_uq, w_ukv):
    qr = w_uq.shape[0]
    dq = MLA_NOPE + MLA_ROPE
    hr = MLA_ROPE // 2
    wq3 = w_uq.reshape(qr, HEADS, dq)
    zq = jnp.zeros((qr, HEADS, LANES - dq), F32)
    wq = jnp.concatenate([wq3, zq], axis=-1)
    x1 = wq3[..., MLA_NOPE:MLA_NOPE + hr]
    x2 = wq3[..., MLA_NOPE + hr:]
    wqr = jnp.concatenate([jnp.zeros((qr, HEADS, MLA_NOPE), F32), -x2, x1, zq], axis=-1)
    pad_rows = lambda m: jnp.pad(m.reshape(qr, HEADS * LANES), ((0, BW - qr), (0, 0)))
    kvr = w_ukv.shape[0]
    wkv3 = w_ukv.reshape(kvr, HEADS, MLA_NOPE + MLA_V)
    zk = jnp.zeros((kvr, HEADS, LANES - MLA_NOPE), F32)
    wk = jnp.concatenate([wkv3[..., :MLA_NOPE], zk], axis=-1).reshape(kvr, HEADS * LANES)
    wv = jnp.concatenate([wkv3[..., MLA_NOPE:], zk], axis=-1).reshape(kvr, HEADS * LANES)
    j = np.arange(hr)
    p1 = np.zeros((LANES, LANES), np.float32)
    p1[np.arange(MLA_ROPE), MLA_NOPE + np.arange(MLA_ROPE)] = 1.0
    p2 = np.zeros((LANES, LANES), np.float32)
    p2[hr + j, MLA_NOPE + j] = -1.0
    p2[j, MLA_NOPE + hr + j] = 1.0
    one = np.zeros((1, HEADS * LANES), np.float32)
    one[0, MLA_V + LANES * np.arange(HEADS)] = 1.0
    return (pad_rows(wq).astype(BF16), pad_rows(wqr).astype(BF16), wk.astype(BF16), wv.astype(BF16),
            jnp.asarray(p1, BF16), jnp.asarray(p2, BF16), jnp.asarray(one))


def mla_prep(u, cb0, cos, sin, q_norm, kv_norm, weights, tm, rope_blocks):
    n = u.shape[0]
    wq, wqr, wk, wv, p1, p2, one = weights
    qn = jnp.pad(q_norm, (0, BW - q_norm.shape[0])).reshape(1, BW)
    full = lambda a: pl.BlockSpec(a.shape, lambda i: (0,) * a.ndim)
    tab = pl.BlockSpec((tm, LANES), lambda i: (i % rope_blocks, 0))
    kv_cb = (cb0 + CB_MKV) * (BW // LANES)
    out = jax.ShapeDtypeStruct((n, HEADS * LANES), BF16)
    ospec = pl.BlockSpec((tm, HEADS * LANES), lambda i: (i, 0))
    return pl.pallas_call(
        _mla_prep_kernel,
        grid=(n // tm,),
        in_specs=[
            pl.BlockSpec((tm, BW), lambda i: (i, cb0 + CB_MCQ)),
            pl.BlockSpec((tm, LANES), lambda i: (i, kv_cb)),
            pl.BlockSpec((tm, LANES), lambda i: (i, kv_cb + 1)),
            tab, tab, full(qn), pl.BlockSpec((1, LANES), lambda i: (0, 0)),
            full(wq), full(wqr), full(wk), full(wv), full(p1), full(p2), full(one),
        ],
        out_specs=[ospec, ospec, ospec],
        out_shape=[out, out, out],
        compiler_params=_cp(("arbitrary",)),
        name="mla_prep",
    )(u, u, u, cos, sin, qn, kv_norm.reshape(1, LANES), wq, wqr, wk, wv, p1, p2, one)


def _flash_kernel(*refs, lens, tk):
    q_ref = refs[0]
    kv_refs = refs[1:1 + 2 * len(lens)]
    o_ref = refs[1 + 2 * len(lens)]
    q = q_ref[...]
    tq = q.shape[0]
    m = jnp.full((tq, 1), NEG, F32)
    acc = jnp.zeros((tq, LANES), F32)

    def chunk(kc, vc, m, acc):
        s = _dot_nt(q, kc)
        m_new = jnp.maximum(m, jnp.max(s, axis=-1, keepdims=True))
        p = jnp.exp2((s - m_new).astype(BF16))
        acc = jnp.exp2(m - m_new) * acc + _dot(p, vc)
        return m_new, acc

    for si, length in enumerate(lens):
        k_ref, v_ref = kv_refs[2 * si], kv_refs[2 * si + 1]
        step = min(tk, length)
        if length == step:
            m, acc = chunk(k_ref[...], v_ref[...], m, acc)
        else:
            def body(j, carry, k_ref=k_ref, v_ref=v_ref, step=step):
                j0 = pl.multiple_of(j * step, step)
                return chunk(k_ref[pl.ds(j0, step), :], v_ref[pl.ds(j0, step), :], *carry)

            m, acc = lax.fori_loop(0, length // step, body, (m, acc), unroll=4)

    lane = lax.broadcasted_iota(jnp.int32, (1, LANES), 1)
    l = jnp.sum(jnp.where(lane == MLA_V, acc, 0.0), axis=-1, keepdims=True)
    o_ref[...] = jnp.where(lane < MLA_V, acc / l, 0.0).astype(o_ref.dtype)


def flash_attention(q, kvs, batch, tq, tk):
    n = q.shape[0]
    nq = n // batch // tq
    lens = tuple(k.shape[0] // batch for k, _ in kvs)
    in_specs = [pl.BlockSpec((tq, LANES), lambda b, h, i: (b * nq + i, h))]
    args = [q]
    for (k, v), length in zip(kvs, lens):
        in_specs += [pl.BlockSpec((length, LANES), lambda b, h, i: (b, h))] * 2
        args += [k, v]
    return pl.pallas_call(
        functools.partial(_flash_kernel, lens=lens, tk=tk),
        grid=(batch, HEADS, nq),
        in_specs=in_specs,
        out_specs=pl.BlockSpec((tq, LANES), lambda b, h, i: (b * nq + i, h)),
        out_shape=jax.ShapeDtypeStruct((n, HEADS * LANES), BF16),
        compiler_params=_cp(("arbitrary", "arbitrary", "arbitrary")),
        name="mla_attention",
    )(*args)


def _merge_kernel(*refs, with_router):
    (gate_ref, yr_f_ref, yr_b_ref, yf_ref, yn_ref, ym_ref, x_ref, g1_ref, wb_ret_ref, wb_f_ref,
     wb_na_ref, wb_mla_ref, wo_ref, gain_ref, sc_ref, sh_ref) = refs[:16]
    rest = refs[16:]
    if with_router:
        rh_ref, rl_ref, x_out_ref, lg_out_ref, hp_out_ref = rest
    else:
        x_out_ref, h_out_ref = rest
    d = x_ref.shape[1]

    def gated(k, y):
        g = jax.nn.sigmoid(gate_ref[:, k * d:(k + 1) * d].astype(F32))
        return g * y

    m = gated(0, _dot(yr_f_ref[...] + yr_b_ref[...], wb_ret_ref[...]))
    m = m + gated(1, _dot(yf_ref[...], wb_f_ref[...]))
    m = m + gated(2, _dot(yn_ref[...], wb_na_ref[...]))
    m = m + gated(3, _dot(ym_ref[...], wb_mla_ref[...]))
    y = _dot(m.astype(BF16), wo_ref[...])
    x = x_ref[...] + g1_ref[0] * y
    x_out_ref[...] = x
    hn = x * lax.rsqrt(jnp.mean(x * x, axis=-1, keepdims=True) + NORM_EPS)
    h = (hn * gain_ref[...]) * (1.0 + sc_ref[0]) + sh_ref[0]
    if with_router:
        hp_out_ref[...] = _pack_bf16_pairs(h)
        h_hi = h.astype(BF16)
        h_lo = (h - h_hi.astype(F32)).astype(BF16)
        lg_out_ref[...] = (_dot(h_hi, rh_ref[...]) + _dot(h_lo, rh_ref[...])) + _dot(h_hi, rl_ref[...])
    else:
        h_out_ref[...] = h.astype(h_out_ref.dtype)


def merge(u, y_ret, y_four, y_na, y_mla, x, mods, mod_row, gain, wb, w_out, router, tm):
    n, d = x.shape
    wb_ret, wb_f, wb_na, wb_mla = wb
    full = lambda a: pl.BlockSpec(a.shape, lambda i: (0,) * a.ndim)
    br = lambda: pl.BlockSpec((tm, BW), lambda i: (i, 0))
    tok = lambda: pl.BlockSpec((tm, d), lambda i: (i, 0))
    in_specs = [
        pl.BlockSpec((tm, 4 * d), lambda i: (i, 0)),
        br(), br(), br(), br(),
        pl.BlockSpec((tm, HEADS * LANES), lambda i: (i, 0)),
        tok(), _mod_spec(d, mod_row, tm, 2),
        full(wb_ret), full(wb_f), full(wb_na), full(wb_mla), full(w_out),
        pl.BlockSpec((1, d), lambda i: (0, 0)), _mod_spec(d, mod_row, tm, 4), _mod_spec(d, mod_row, tm, 3),
    ]
    args = [u, y_ret[0], y_ret[1], y_four, y_na, y_mla, x, mods, wb_ret, wb_f, wb_na, wb_mla, w_out,
            gain.reshape(1, d), mods, mods]
    out_specs = [tok()]
    out_shape = [jax.ShapeDtypeStruct((n, d), F32)]
    if router is None:
        out_specs.append(tok())
        out_shape.append(jax.ShapeDtypeStruct((n, d), BF16))
    else:
        in_specs += [full(router[0]), full(router[1])]
        args += list(router)
        out_specs += [pl.BlockSpec((tm, LANES), lambda i: (i, 0)), pl.BlockSpec((tm, d // 2), lambda i: (i, 0))]
        out_shape += [jax.ShapeDtypeStruct((n, LANES), F32), jax.ShapeDtypeStruct((n, d // 2), jnp.int32)]
    return pl.pallas_call(
        functools.partial(_merge_kernel, with_router=router is not None),
        grid=(n // tm,),
        in_specs=in_specs,
        out_specs=out_specs,
        out_shape=out_shape,
        compiler_params=_cp(("arbitrary",)),
        name="merge",
    )(*args)


def _ffn_kernel(h_ref, wg_ref, wu_ref, wd_ref, x_ref, g2_ref, o_ref, acc_ref):
    f = pl.program_id(1)

    @pl.when(f == 0)
    def _():
        acc_ref[...] = jnp.zeros_like(acc_ref)

    h = h_ref[...]
    a = _silu(_dot(h, wg_ref[...])) * _dot(h, wu_ref[...])
    acc_ref[...] += _dot(a.astype(BF16), wd_ref[...])

    @pl.when(f == pl.num_programs(1) - 1)
    def _():
        o_ref[...] = x_ref[...] + g2_ref[0] * acc_ref[...]


def ffn(h, x, mods, mod_row, wg, wu, wd, tm, tf):
    n, d = x.shape
    nf = wg.shape[1] // tf
    return pl.pallas_call(
        _ffn_kernel,
        grid=(n // tm, nf),
        in_specs=[
            pl.BlockSpec((tm, d), lambda i, f: (i, 0)),
            pl.BlockSpec((d, tf), lambda i, f: (0, f)),
            pl.BlockSpec((d, tf), lambda i, f: (0, f)),
            pl.BlockSpec((tf, d), lambda i, f: (f, 0)),
            pl.BlockSpec((tm, d), lambda i, f: (i, 0)),
            _mod_spec(d, mod_row, tm, 5),
        ],
        out_specs=pl.BlockSpec((tm, d), lambda i, f: (i, 0)),
        out_shape=jax.ShapeDtypeStruct((n, d), F32),
        scratch_shapes=[pltpu.VMEM((tm, d), F32)],
        compiler_params=_cp(("arbitrary", "arbitrary"), VMEM_LIMIT_LARGE),
        name="ffn",
    )(h, wg, wu, wd, x, mods)


def _moe_kernel(be_ref, nu_ref, xp_ref, wg_ref, wu_ref, wd_ref, o_ref, acc_ref, x_ref):
    i = pl.program_id(0)
    f = pl.program_id(1)
    used = i < nu_ref[0]

    @pl.when(f == 0)
    def _():
        acc_ref[...] = jnp.zeros_like(acc_ref)
        x_ref[...] = _unpack_bf16_pairs(xp_ref[...]).astype(BF16)

    @pl.when(used)
    def _():
        x = x_ref[...]
        a = _silu(_dot(x, wg_ref[0])) * _dot(x, wu_ref[0])
        acc_ref[...] += _dot(a.astype(BF16), wd_ref[0])

    @pl.when(f == pl.num_programs(1) - 1)
    def _():
        o_ref[...] = _pack_bf16_pairs(acc_ref[...])


SC_CAST_BLOCK = (16, 512)


def sc_cast_bf16(w):
    e, a, b = w.shape
    br, bc = SC_CAST_BLOCK
    assert (e * a) % br == 0 and b % bc == 0
    mesh = plsc.VectorSubcoreMesh(core_axis_name="c", subcore_axis_name="s")

    @functools.partial(pl.kernel, mesh=mesh, out_type=jax.ShapeDtypeStruct((e * a, b), BF16), scratch_types=[])
    def cast(x_hbm, o_hbm):
        def body(in_v, out_v):
            @pl.loop(0, br, step=2)
            def _(r):
                @pl.loop(0, bc, step=16)
                def _(c):
                    top = in_v.at[pl.ds(r, 1), pl.ds(c, 16)][...]
                    bot = in_v.at[pl.ds(r + 1, 1), pl.ds(c, 16)][...]
                    out_v.at[pl.ds(r, 2), pl.ds(c, 16)][...] = jnp.concatenate([top, bot], axis=0).astype(BF16)

        pltpu.emit_pipeline(
            body,
            grid=(e * a // br, b // bc),
            in_specs=[pl.BlockSpec((br, bc), lambda i, j: (i, j))],
            out_specs=[pl.BlockSpec((br, bc), lambda i, j: (i, j))],
            core_axis_name=("c", "s"),
            dimension_semantics=(pltpu.PARALLEL, pltpu.PARALLEL),
        )(x_hbm, o_hbm)

    return cast(w.reshape(e * a, b)).reshape(e, a, b)


def moe_ffn(blk_exp, n_used, xb, wg, wu, wd, tm, tf):
    n, dp = xb.shape
    d = 2 * dp
    nf = wg.shape[2] // tf
    grid_spec = pltpu.PrefetchScalarGridSpec(
        num_scalar_prefetch=2,
        grid=(n // tm, nf),
        in_specs=[
            pl.BlockSpec((tm, dp), lambda i, f, be, nu: (i, 0)),
            pl.BlockSpec((1, d, tf), lambda i, f, be, nu: (be[i], 0, f)),
            pl.BlockSpec((1, d, tf), lambda i, f, be, nu: (be[i], 0, f)),
            pl.BlockSpec((1, tf, d), lambda i, f, be, nu: (be[i], f, 0)),
        ],
        out_specs=pl.BlockSpec((tm, dp), lambda i, f, be, nu: (i, 0)),
        scratch_shapes=[pltpu.VMEM((tm, d), F32), pltpu.VMEM((tm, d), BF16)],
    )
    return pl.pallas_call(
        _moe_kernel,
        grid_spec=grid_spec,
        out_shape=jax.ShapeDtypeStruct((n, dp), jnp.int32),
        compiler_params=_cp(("arbitrary", "arbitrary")),
        name="moe_ffn",
    )(blk_exp, n_used, xb, wg, wu, wd)


def _combine_kernel(x_ref, ya_ref, yb_ref, w_ref, g2_ref, gain_ref, o_ref, *, final):
    w = w_ref[...]
    y = w[:, 0:1] * _unpack_bf16_pairs(ya_ref[...]) + w[:, 1:2] * _unpack_bf16_pairs(yb_ref[...])
    x = x_ref[...] + g2_ref[0] * y
    if final:
        x = (x * lax.rsqrt(jnp.mean(x * x, axis=-1, keepdims=True) + NORM_EPS)) * gain_ref[...]
    o_ref[...] = x


def moe_combine(x, y2, w, mods, mod_row, gain, final, tm):
    n, d = x.shape
    tok = lambda: pl.BlockSpec((tm, d), lambda i: (i, 0))
    half = lambda off: pl.BlockSpec((tm, d // 2), lambda i: (i + off, 0))
    return pl.pallas_call(
        functools.partial(_combine_kernel, final=final),
        grid=(n // tm,),
        in_specs=[tok(), half(0), half(n // tm),
                  pl.BlockSpec((tm, MOE_TOP_K), lambda i: (i, 0)),
                  _mod_spec(d, mod_row, tm, 5), pl.BlockSpec((1, d), lambda i: (0, 0))],
        out_specs=tok(),
        out_shape=jax.ShapeDtypeStruct((n, d), F32),
        compiler_params=_cp(("arbitrary",)),
        name="moe_combine",
    )(x, y2, y2, w, mods, gain.reshape(1, d))


def _rmsnorm_kernel(x_ref, gain_ref, o_ref):
    x = x_ref[...]
    o_ref[...] = (x * lax.rsqrt(jnp.mean(x * x, axis=-1, keepdims=True) + NORM_EPS)) * gain_ref[...]


def rmsnorm_rows(x, gain, tm):
    n, d = x.shape
    return pl.pallas_call(
        _rmsnorm_kernel,
        grid=(n // tm,),
        in_specs=[pl.BlockSpec((tm, d), lambda i: (i, 0)), pl.BlockSpec((1, d), lambda i: (0, 0))],
        out_specs=pl.BlockSpec((tm, d), lambda i: (i, 0)),
        out_shape=jax.ShapeDtypeStruct((n, d), F32),
        compiler_params=_cp(("arbitrary",)),
        name="final_norm",
    )(x, gain.reshape(1, d))


SC_CORES = 2
SC_SUBCORES = 16
SC_CHUNK = 64


def sc_gather_rows(table, idx):
    n_out = idx.shape[0]
    width = table.shape[1]
    workers = SC_CORES * SC_SUBCORES
    per_worker = n_out // workers
    assert n_out == per_worker * workers and per_worker % SC_CHUNK == 0 and table.dtype == jnp.int32
    mesh = plsc.VectorSubcoreMesh(core_axis_name="c", subcore_axis_name="s")

    @functools.partial(
        pl.kernel, mesh=mesh,
        out_type=jax.ShapeDtypeStruct((n_out, width), table.dtype),
        scratch_types=[pltpu.VMEM((SC_CHUNK,), jnp.int32), pltpu.VMEM((SC_CHUNK, width), table.dtype),
                       pltpu.SemaphoreType.DMA],
    )
    def gather(table_hbm, idx_hbm, out_hbm, idx_v, rows_v, sem):
        base = (lax.axis_index("s") * SC_CORES + lax.axis_index("c")) * per_worker

        @pl.loop(0, per_worker // SC_CHUNK)
        def _(j):
            off = pl.multiple_of(base + j * SC_CHUNK, SC_CHUNK)
            pltpu.sync_copy(idx_hbm.at[pl.ds(off, SC_CHUNK)], idx_v)
            pltpu.async_copy(table_hbm.at[idx_v], rows_v, sem).wait()
            pltpu.sync_copy(rows_v, out_hbm.at[pl.ds(off, SC_CHUNK)])

    return gather(table, idx)


def moe_route(logits, n_experts, tm):
    n_tok = logits.shape[0]
    top_logit, top_idx = lax.top_k(logits[:, :n_experts], MOE_TOP_K)
    top_w = jax.nn.softmax(top_logit, axis=-1)
    e_flat = top_idx.reshape(-1).astype(jnp.int32)
    n_assign = e_flat.shape[0]
    onehot = (e_flat[:, None] == jnp.arange(n_experts, dtype=jnp.int32)[None, :]).astype(jnp.int32)
    rank = jnp.sum((jnp.cumsum(onehot, axis=0) - onehot) * onehot, axis=1)
    counts = jnp.sum(onehot, axis=0)
    padded = (counts + tm - 1) // tm * tm
    pad_end = jnp.cumsum(padded)
    pad_start = pad_end - padded
    dest = pad_start[e_flat] + rank
    n_rows = n_assign + n_experts * tm
    tok = jnp.arange(n_assign, dtype=jnp.int32) // MOE_TOP_K
    blk_start = jnp.arange(n_rows // tm, dtype=jnp.int32) * tm
    blk_exp = jnp.minimum(jnp.sum(pad_end[None, :] <= blk_start[:, None], axis=1), n_experts - 1)
    n_used = (pad_end[-1] // tm).reshape(1)
    _, tok_by_row = lax.sort_key_val(dest, tok)
    max_shift = n_experts * tm
    filler = jnp.arange(max_shift, dtype=jnp.int32)
    compact = jnp.concatenate([filler, tok_by_row, filler])
    shift = pad_start - (jnp.cumsum(counts) - counts)
    row_exp = jnp.repeat(blk_exp, tm)
    row_tok = jnp.zeros((n_rows,), jnp.int32)
    for e in range(n_experts):
        shifted = lax.dynamic_slice(compact, (max_shift - shift[e],), (n_rows,))
        row_tok = jnp.where(row_exp == e, shifted, row_tok)
    return row_tok, dest.reshape(n_tok, MOE_TOP_K), top_w, blk_exp.astype(jnp.int32), n_used.astype(jnp.int32)


def _rope_split(wcols):
    d, w = wcols.shape
    half = w // HEADS // 2
    return wcols.reshape(d, HEADS, 2, half).transpose(0, 2, 1, 3).reshape(d, w)


def _inproj_weights(w_in):
    d = w_in.shape[0]
    kv = (BW, BW, BW, BW, MLA_KV_RANK, MLA_ROPE)
    qs = (BW, BW, BW, BW, BW, MLA_Q_RANK, 4 * d)
    offs = np.concatenate([[0], np.cumsum(kv + qs)])
    seg = lambda i: w_in[:, offs[i]:offs[i + 1]]
    r_k, r_v, n_k, n_v, m_ckv, m_kr = (seg(i) for i in range(6))
    r_q, r_gf, r_gb, f_in, n_q, m_cq, gate = (seg(6 + i) for i in range(7))
    z = lambda n: jnp.zeros((d, n), w_in.dtype)
    cols = [gate, _rope_split(r_q), _rope_split(r_k), r_v, r_gf, r_gb, f_in, n_q, n_k, n_v,
            m_cq, z(BW - MLA_Q_RANK), m_ckv, m_kr, z(LANES - MLA_ROPE)]
    return jnp.concatenate(cols, axis=1).astype(BF16)


def _ret_rope_tables(n):
    t = jnp.arange(n)
    row = (t // GRID_W).astype(F32)
    col = (t % GRID_W).astype(F32)
    nf = RET_DK // 4
    inv = ROPE_BASE ** (-jnp.arange(nf, dtype=F32) / nf)
    ang = jnp.concatenate([row[:, None] * inv, col[:, None] * inv], axis=-1)
    return jnp.tile(jnp.cos(ang), (1, HEADS)), jnp.tile(jnp.sin(ang), (1, HEADS))


def _mla_rope_tables(n):
    t = jnp.arange(n)
    row = (t // GRID_W).astype(F32)
    col = (t % GRID_W).astype(F32)
    nf = MLA_ROPE // 4
    inv = ROPE_BASE ** (-jnp.arange(nf, dtype=F32) / nf)
    ang = jnp.concatenate([row[:, None] * inv, col[:, None] * inv], axis=-1)
    pad = jnp.zeros((n, LANES - MLA_NOPE - MLA_ROPE), F32)
    cos = jnp.concatenate([jnp.ones((n, MLA_NOPE), F32), jnp.cos(ang), jnp.cos(ang), pad], axis=-1)
    sin = jnp.concatenate([jnp.zeros((n, MLA_NOPE), F32), jnp.sin(ang), jnp.sin(ang), pad], axis=-1)
    return cos, sin


def _tile_rows(*sizes):
    for tm in (1024, 512, 256, 128):
        if all(s % tm == 0 for s in sizes):
            return tm
    raise ValueError(f"token counts {sizes} need a common multiple-of-128 row tile")


def kernel(x, c, ctx, c_ctx, ada_w, ada_b, norm_mix, norm_ffn, w_in, ret_decay_fwd, ret_decay_bwd,
           mla_q_norm, mla_kv_norm, mla_w_uq, mla_w_ukv, na_rpb, w_branch, w_out,
           ffn_w_gate, ffn_w_up, ffn_w_down, moe_router, moe_w_gate, moe_w_up, moe_w_down, norm_final):
    batch, t, d = x.shape
    tc = ctx.shape[1]
    depth = ada_w.shape[0]
    nl, ncx = batch * t, batch * tc
    assert batch < 8 and t % (16 * GRID_W) == 0 and tc % LANES == 0 and d == 4 * BW
    tm = _tile_rows(t, ncx)
    cb0 = 4 * d // BW

    xl = x.reshape(nl, d)
    xc = ctx.reshape(ncx, d)
    cc = jnp.zeros((8, d), F32).at[:batch].set(c).at[batch].set(c_ctx)
    mods = adaln(cc, ada_w, ada_b).reshape(depth, 8 * 6, 1, d)
    lat_row = lambda r0: r0 // t
    ctx_row = lambda r0: batch

    ret_cos, ret_sin = _ret_rope_tables(t)
    ret_cos_c, ret_sin_c = jnp.ones((tc, LANES), F32), jnp.zeros((tc, LANES), F32)
    mla_cos, mla_sin = _mla_rope_tables(t)
    mla_cos_c = jnp.concatenate([jnp.ones((tm, MLA_NOPE + MLA_ROPE), F32),
                                 jnp.zeros((tm, LANES - MLA_NOPE - MLA_ROPE), F32)], axis=-1)
    mla_sin_c = jnp.zeros((tm, LANES), F32)
    ret_cs = 256

    for i in range(depth):
        ctx_out = i < depth - 1
        md = mods[i]
        w_p = _inproj_weights(w_in[i])
        u = norm_inproj(xl, norm_mix[i], md, lat_row, w_p, tm, 2304)
        uc = norm_inproj(xc, norm_mix[i], md, ctx_row, w_p, tm, 2304)

        lg = jnp.stack([jax.nn.log_sigmoid(ret_decay_fwd[i].astype(F32)),
                        jax.nn.log_sigmoid(ret_decay_bwd[i].astype(F32))])
        zero_state = jnp.zeros((batch, 2, BW, BW), F32)
        yc_ret, s_ctx = retention(uc, cb0, lg, ret_cos_c, ret_sin_c, zero_state, batch, min(ret_cs, tc))
        y_ret, _ = retention(u, cb0, lg, ret_cos, ret_sin, s_ctx, batch, ret_cs)

        y_four = fourier_long(u, cb0 + CB_F, batch)

        bias_tab = _na_bias_table(na_rpb[i])
        y_na = na_attention(u, uc, cb0, bias_tab, batch, 16)

        mw = _mla_weights(mla_w_uq[i], mla_w_ukv[i])
        q_l, k_l, v_l = mla_prep(u, cb0, mla_cos, mla_sin, mla_q_norm[i], mla_kv_norm[i], mw, tm, t // tm)
        q_c, k_c, v_c = mla_prep(uc, cb0, mla_cos_c, mla_sin_c, mla_q_norm[i], mla_kv_norm[i], mw, tm, 1)
        y_mla = flash_attention(q_l, [(k_l, v_l), (k_c, v_c)], batch, min(2048, t), 512)

        wb = w_branch[i].astype(BF16)
        wb_mla = jnp.concatenate(
            [wb[3].reshape(HEADS, MLA_V, d), jnp.zeros((HEADS, LANES - MLA_V, d), BF16)], axis=1
        ).reshape(HEADS * LANES, d)
        wbs = (wb[0], wb[1], wb[2], wb_mla)
        wo = w_out[i].astype(BF16)
        j = i // 2
        if i % 2 == 0:
            xl, h2 = merge(u, y_ret, y_four, y_na, y_mla, xl, md, lat_row, norm_ffn[i], wbs, wo, None, 512)
            wg, wu, wd = ffn_w_gate[j].astype(BF16), ffn_w_up[j].astype(BF16), ffn_w_down[j].astype(BF16)
            tf = wg.shape[1] // 2
            xl = ffn(h2, xl, md, lat_row, wg, wu, wd, tm, tf)
        else:
            n_exp = moe_router.shape[2]
            r = jnp.pad(moe_router[j], ((0, 0), (0, LANES - n_exp)))
            r_hi = r.astype(BF16)
            r_lo = (r - r_hi.astype(F32)).astype(BF16)
            xl, logits, h2p = merge(u, y_ret, y_four, y_na, y_mla, xl, md, lat_row, norm_ffn[i], wbs, wo,
                                   (r_hi, r_lo), 512)
            tmm = 512
            row_tok, dest, top_w, blk_exp, n_used = moe_route(logits, n_exp, tmm)
            xb = sc_gather_rows(h2p, row_tok)
            ewg, ewu, ewd = (sc_cast_bf16(w) for w in (moe_w_gate[j], moe_w_up[j], moe_w_down[j]))
            yb = moe_ffn(blk_exp, n_used, xb, ewg, ewu, ewd, tmm, 1792)
            y2 = sc_gather_rows(yb, dest.T.reshape(-1))
            xl = moe_combine(xl, y2, top_w, md, lat_row, norm_final, i == depth - 1, 512)

        if ctx_out:
            yc_four = fourier_short(uc, cb0 + CB_F, batch)
            yc_na = na_ctx_attention(uc, cb0, batch)
            yc_mla = flash_attention(q_c, [(k_c, v_c)], batch, tc, 512)
            if i % 2 == 0:
                xc, hc2 = merge(uc, yc_ret, yc_four, yc_na, yc_mla, xc, md, ctx_row, norm_ffn[i], wbs, wo,
                                None, 512)
                xc = ffn(hc2, xc, md, ctx_row, wg, wu, wd, tm, tf)
            else:
                raise NotImplementedError("context tokens through the expert mixer")

    if depth % 2 == 1:
        xl = rmsnorm_rows(xl, norm_final, tm)
    return xl.reshape(batch, t, d)
```

```python
import functools

import numpy as np
import jax
import jax.numpy as jnp
from jax import lax
from jax.experimental import pallas as pl
from jax.experimental.pallas import tpu as pltpu
from jax.experimental.pallas import tpu_sc as plsc

F32 = jnp.float32
BF16 = jnp.bfloat16

GRID_W = 64
ROPE_BASE = 10000.0
NORM_EPS = 1e-6
HEADS = 4
RET_DK = 64
FOURIER_GROUP_DIM = 64
NA_HEAD_DIM = 64
NA_WIN_R = 8
NA_WIN_C = 16
MLA_NOPE = 64
MLA_ROPE = 32
MLA_V = 64
MLA_Q_RANK = 192
MLA_KV_RANK = 128
MOE_TOP_K = 2
BW = 256

COL_GATE = 0
CB_RQ, CB_RK, CB_RV, CB_RGF, CB_RGB, CB_F, CB_NQ, CB_NK, CB_NV, CB_MCQ, CB_MKV = range(11)
LANES = 128
NEG = -1e30

VMEM_LIMIT = 48 * 1024 * 1024
VMEM_LIMIT_LARGE = 58 * 1024 * 1024

INPROJ_COLS = 2304
RET_CHUNK = 256
NA_ROWS_PER_STEP = 16
FLASH_Q, FLASH_K = 4096, 512
MERGE_ROWS = 512
MOE_ROWS, MOE_FFN_COLS = 512, 1792


def _cp(sem, vmem=VMEM_LIMIT):
    return pltpu.CompilerParams(dimension_semantics=sem, vmem_limit_bytes=vmem)


def _dot(a, b):
    return jnp.dot(a, b, preferred_element_type=F32)


def _dot_nt(a, b):
    return lax.dot_general(a, b, (((1,), (1,)), ((), ())), preferred_element_type=F32)


def _dot_tn(a, b):
    return lax.dot_general(a, b, (((0,), (0,)), ((), ())), preferred_element_type=F32)


def _silu(x):
    return x * jax.nn.sigmoid(x)


def _pack_bf16_pairs(x):
    k = x.shape[1] // 2
    lo = lax.bitcast_convert_type(x[:, :k].astype(BF16).astype(F32), jnp.uint32) >> 16
    hi = lax.bitcast_convert_type(x[:, k:].astype(BF16).astype(F32), jnp.uint32) & jnp.uint32(0xFFFF0000)
    return lax.bitcast_convert_type(lo | hi, jnp.int32)


def _unpack_bf16_pairs(p):
    u = lax.bitcast_convert_type(p, jnp.uint32)
    lo = lax.bitcast_convert_type(u << 16, F32)
    hi = lax.bitcast_convert_type(u & jnp.uint32(0xFFFF0000), F32)
    return jnp.concatenate([lo, hi], axis=-1)


def _adaln_kernel(c_ref, w_ref, b_ref, o_ref):
    s = _silu(c_ref[...])
    o_ref[0] = _dot(s.astype(BF16), w_ref[0].astype(BF16)) + b_ref[0]


def adaln(cc, ada_w, ada_b):
    depth, d, n6 = ada_w.shape
    tn = n6 // 4
    return pl.pallas_call(
        _adaln_kernel,
        grid=(depth, n6 // tn),
        in_specs=[
            pl.BlockSpec((8, d), lambda l, j: (0, 0)),
            pl.BlockSpec((1, d, tn), lambda l, j: (l, 0, j)),
            pl.BlockSpec((1, 1, tn), lambda l, j: (l, 0, j)),
        ],
        out_specs=pl.BlockSpec((1, 8, tn), lambda l, j: (l, 0, j)),
        out_shape=jax.ShapeDtypeStruct((depth, 8, n6), F32),
        compiler_params=_cp(("arbitrary", "arbitrary")),
        name="adaln",
    )(cc, ada_w, ada_b.reshape(depth, 1, n6))


def _inproj_kernel(x_ref, g_ref, sc_ref, sh_ref, w_ref, o_ref, h_ref):
    @pl.when(pl.program_id(1) == 0)
    def _():
        x = x_ref[...]
        y = x * lax.rsqrt(jnp.mean(x * x, axis=-1, keepdims=True) + NORM_EPS)
        h = (y * g_ref[...]) * (1.0 + sc_ref[0]) + sh_ref[0]
        h_ref[...] = h.astype(BF16)

    o_ref[...] = _dot(h_ref[...], w_ref[...]).astype(o_ref.dtype)


def _mod_spec(d, mod_row, tm, k):
    return pl.BlockSpec((1, 1, d), lambda i, *_: (mod_row(i * tm) * 6 + k, 0, 0))


def norm_inproj(x, gain, mods, mod_row, w, tm, tn):
    n, d = x.shape
    nc = w.shape[1]
    return pl.pallas_call(
        _inproj_kernel,
        grid=(n // tm, nc // tn),
        in_specs=[
            pl.BlockSpec((tm, d), lambda i, j: (i, 0)),
            pl.BlockSpec((1, d), lambda i, j: (0, 0)),
            _mod_spec(d, mod_row, tm, 1),
            _mod_spec(d, mod_row, tm, 0),
            pl.BlockSpec((d, tn), lambda i, j: (0, j)),
        ],
        out_specs=pl.BlockSpec((tm, tn), lambda i, j: (i, j)),
        out_shape=jax.ShapeDtypeStruct((n, nc), BF16),
        scratch_shapes=[pltpu.VMEM((tm, d), BF16)],
        compiler_params=_cp(("arbitrary", "arbitrary")),
        name="norm_inproj",
    )(x, gain.reshape(1, d), mods, mods, w)


def _ret_init(d, cs, w, lg_ref, s0_ref, lgq_ref, s_ref, decay_ref, qw_ref, kw_ref, avg_ref):
    rev = d == 1
    s_ref[d] = s0_ref[0, d]
    pos_i = lax.broadcasted_iota(jnp.int32, (cs, 1), 0).astype(F32)
    pos_j = lax.broadcasted_iota(jnp.int32, (1, cs), 1).astype(F32)
    p_i = jnp.where(rev, cs - 1.0 - pos_i, pos_i)
    p_j = jnp.where(rev, cs - 1.0 - pos_j, pos_j)
    diff = p_i - p_j
    for h in range(HEADS):
        decay_ref[d, h] = jnp.where(diff >= 0, jnp.exp(lg_ref[d, h] * jnp.maximum(diff, 0.0)), 0.0)
    lgq = lgq_ref[d]
    qw_ref[d] = jnp.exp(lgq * (p_i + 1.0))
    kw_ref[d] = jnp.exp(lgq * (cs - 1.0 - p_i))
    hd = w // HEADS
    gi = lax.broadcasted_iota(jnp.int32, (w, w), 0) // hd
    gj = lax.broadcasted_iota(jnp.int32, (w, w), 1) // hd
    avg_ref[...] = jnp.where(gi == gj, 1.0 / hd, 0.0).astype(BF16)


def _ret_chunk(d, q_ref, k_ref, v_ref, g_ref, cos_ref, sin_ref, lgv_ref, y_ref, s_ref, decay_ref, qw_ref, kw_ref,
               avg_ref):
    cs = q_ref.shape[0]
    w = q_ref.shape[1]
    half = w // 2
    lgv = lgv_ref[d]
    s_ref, decay_ref, qw_ref, kw_ref = s_ref.at[d], decay_ref.at[d], qw_ref.at[d], kw_ref.at[d]

    cos = cos_ref[...]
    sin = sin_ref[...]

    def rope(t):
        t1, t2 = t[:, :half], t[:, half:]
        return jnp.concatenate([t1 * cos - t2 * sin, t2 * cos + t1 * sin], axis=-1)

    q = rope(q_ref[...].astype(F32))
    k = rope(k_ref[...].astype(F32)) * (RET_DK ** -0.5)
    vb = v_ref[...]

    lane = lax.broadcasted_iota(jnp.int32, (1, w), 1)
    head_q = (lane % half) // (half // HEADS)
    head_v = lane // (w // HEADS)

    s_prev = s_ref[...]
    o = _dot((q * qw_ref[...]).astype(BF16), s_prev.astype(BF16))
    qb = q.astype(BF16)
    kb = k.astype(BF16)
    zero_b = jnp.zeros_like(qb)
    for h in range(HEADS):
        a = _dot_nt(jnp.where(head_q == h, qb, zero_b), kb)
        oh = _dot((a * decay_ref[h]).astype(BF16), vb)
        o = o + jnp.where(head_v == h, oh, 0.0)

    ds = _dot_tn((k * kw_ref[...]).astype(BF16), vb)
    row_head = (lax.broadcasted_iota(jnp.int32, (w, 1), 0) % half) // (half // HEADS)
    s_new = s_prev * jnp.exp(lgv * float(cs)) + jnp.where(row_head == head_v, ds, 0.0)
    s_ref[...] = s_new

    ms = _dot((o * o).astype(BF16), avg_ref[...])
    on = o * lax.rsqrt(ms + NORM_EPS)
    y_ref[...] = (_silu(g_ref[...].astype(F32)) * on).astype(y_ref.dtype)
    return s_new


def _ret_kernel(lg_ref, qf, kf, vf, gf, cosf, sinf, qb, kb, vb, gb, cosb, sinb, s0_ref, lgq_ref, lgv_ref,
                yf_ref, yb_ref, sout_ref, s_ref, decay_ref, qw_ref, kw_ref, avg_ref, *, n_chunks):
    c = pl.program_id(1)
    scratch = (s_ref, decay_ref, qw_ref, kw_ref, avg_ref)

    @pl.when(c == 0)
    def _():
        for d in range(2):
            _ret_init(d, qf.shape[0], qf.shape[1], lg_ref, s0_ref, lgq_ref, *scratch)

    s_f = _ret_chunk(0, qf, kf, vf, gf, cosf, sinf, lgv_ref, yf_ref, *scratch)
    s_b = _ret_chunk(1, qb, kb, vb, gb, cosb, sinb, lgv_ref, yb_ref, *scratch)

    @pl.when(c == n_chunks - 1)
    def _():
        sout_ref[0, 0] = s_f
        sout_ref[0, 1] = s_b


def retention(u, cb0, lg, cos, sin, s0, batch, cs):
    n = u.shape[0]
    t = n // batch
    nch = t // cs
    w = BW
    half = w // 2
    lgq = jnp.tile(jnp.repeat(lg, half // HEADS, axis=1), (1, 2)).reshape(2, 1, w)
    lgv = jnp.repeat(lg, w // HEADS, axis=1).reshape(2, 1, w)

    def chunk(d, c):
        return nch - 1 - c if d else c

    def direction(d):
        col = lambda cb: pl.BlockSpec((cs, w), lambda b, c: (b * nch + chunk(d, c), cb0 + cb))
        tab = lambda: pl.BlockSpec((cs, half), lambda b, c: (chunk(d, c), 0))
        return [col(CB_RQ), col(CB_RK), col(CB_RV), col(CB_RGF + d), tab(), tab()]

    y_spec = lambda d: pl.BlockSpec((cs, w), lambda b, c: (b * nch + chunk(d, c), 0))
    state_spec = pl.BlockSpec((1, 2, w, w), lambda b, c: (b, 0, 0, 0))
    lane_spec = pl.BlockSpec((2, 1, w), lambda b, c: (0, 0, 0))
    y_f, y_b, s_out = pl.pallas_call(
        functools.partial(_ret_kernel, n_chunks=nch),
        grid=(batch, nch),
        in_specs=[pl.BlockSpec(memory_space=pltpu.SMEM)] + direction(0) + direction(1)
        + [state_spec, lane_spec, lane_spec],
        out_specs=[y_spec(0), y_spec(1), state_spec],
        out_shape=[
            jax.ShapeDtypeStruct((n, w), BF16),
            jax.ShapeDtypeStruct((n, w), BF16),
            jax.ShapeDtypeStruct((batch, 2, w, w), F32),
        ],
        scratch_shapes=[pltpu.VMEM((2, w, w), F32), pltpu.VMEM((2, HEADS, cs, cs), F32),
                        pltpu.VMEM((2, cs, w), F32), pltpu.VMEM((2, cs, w), F32), pltpu.VMEM((w, w), BF16)],
        compiler_params=_cp(("arbitrary", "arbitrary")),
        name="retention",
    )(lg, u, u, u, u, cos, sin, u, u, u, u, cos, sin, s0, lgq, lgv)
    return (y_f, y_b), s_out


def _dft_tables(t, t1, t2):
    k1 = jnp.arange(t1, dtype=jnp.int32)
    a = jnp.arange(t1, dtype=jnp.int32)
    m = jnp.arange(t2, dtype=jnp.int32)
    ph1 = (k1[None, :, None] * (a[None, None, :] * t2 + m[:, None, None])) % t
    ang1 = ph1.astype(F32) * (2.0 * np.pi / t)
    ph2 = (m[:, None] * m[None, :]) % t2
    ang2 = ph2.astype(F32) * (2.0 * np.pi / t2)
    return (jnp.cos(ang1).astype(BF16), jnp.sin(ang1).astype(BF16),
            jnp.cos(ang2).astype(BF16), jnp.sin(ang2).astype(BF16))


def _channel_tables(width):
    ch = jnp.arange(width, dtype=jnp.int32)
    same = (ch[:, None] // FOURIER_GROUP_DIM) == (ch[None, :] // FOURIER_GROUP_DIM)
    ph = ((ch[:, None] % FOURIER_GROUP_DIM) * (ch[None, :] % FOURIER_GROUP_DIM)) % FOURIER_GROUP_DIM
    ang = ph.astype(F32) * (2.0 * np.pi / FOURIER_GROUP_DIM)
    return (jnp.where(same, jnp.cos(ang), 0.0).astype(BF16),
            jnp.where(same, jnp.sin(ang), 0.0).astype(BF16))


def _fourier_kernel(x_ref, c1_ref, s1_ref, c2_ref, s2_ref, cc_ref, sc_ref, o_ref, xa, yre, yim,
                    *, t1, t2, norm):
    xa[...] = x_ref[...].astype(F32)

    def stage1(m, carry):
        xs = xa[pl.ds(m, t1, stride=t2), :].astype(BF16)
        r0 = pl.multiple_of(m * t1, t1)
        yre[pl.ds(r0, t1), :] = _dot(c1_ref[m], xs)
        yim[pl.ds(r0, t1), :] = -_dot(s1_ref[m], xs)
        return carry

    lax.fori_loop(0, t2, stage1, 0, unroll=8)

    c2 = c2_ref[...]
    s2 = s2_ref[...]
    w2 = jnp.concatenate([jnp.concatenate([c2, s2], axis=1), jnp.concatenate([-s2, c2], axis=1)], axis=0)
    cs = jnp.concatenate([cc_ref[...], sc_ref[...]], axis=0)
    group = 8

    def stage2(j, carry):
        k1 = j * group
        rows = [pl.ds(k1 + g, t2, stride=t1) for g in range(group)]
        y = jnp.concatenate([jnp.concatenate([yre[r, :] for r in rows], axis=1),
                             jnp.concatenate([yim[r, :] for r in rows], axis=1)], axis=0).astype(BF16)
        z = _dot(w2, y).astype(BF16)
        w = z.shape[1] // group
        zz = jnp.concatenate([jnp.concatenate([z[:t2, g * w:(g + 1) * w], z[t2:, g * w:(g + 1) * w]], axis=1)
                              for g in range(group)], axis=0)
        out = _dot(zz, cs) * norm
        for g in range(group):
            xa[rows[g], :] = out[g * t2:(g + 1) * t2]
        return carry

    lax.fori_loop(0, t1 // group, stage2, 0)
    o_ref[...] = xa[...].astype(o_ref.dtype)


def fourier_long(u, cb, batch, t2=LANES):
    n = u.shape[0]
    t = n // batch
    t1 = t // t2
    c1, s1, c2, s2 = _dft_tables(t, t1, t2)
    cc, sc = _channel_tables(LANES)
    norm = float(1.0 / np.sqrt(t * FOURIER_GROUP_DIM))
    full = lambda shape: pl.BlockSpec(shape, lambda b, hh: (0,) * len(shape))
    return pl.pallas_call(
        functools.partial(_fourier_kernel, t1=t1, t2=t2, norm=norm),
        grid=(batch, BW // LANES),
        in_specs=[
            pl.BlockSpec((t, LANES), lambda b, hh: (b, cb * (BW // LANES) + hh)),
            full((t2, t1, t1)), full((t2, t1, t1)), full((t2, t2)), full((t2, t2)),
            full((LANES, LANES)), full((LANES, LANES)),
        ],
        out_specs=pl.BlockSpec((t, LANES), lambda b, hh: (b, hh)),
        out_shape=jax.ShapeDtypeStruct((n, BW), BF16),
        scratch_shapes=[pltpu.VMEM((t, LANES), F32)] * 3,
        compiler_params=_cp(("arbitrary", "arbitrary")),
        name="fourier",
    )(u, c1, s1, c2, s2, cc, sc)


def _fourier_small_kernel(x_ref, ct_ref, st_ref, cc_ref, sc_ref, o_ref, *, norm):
    x = x_ref[...]
    zr = _dot(ct_ref[...], x)
    zi = -_dot(st_ref[...], x)
    out = (_dot(zr.astype(BF16), cc_ref[...]) + _dot(zi.astype(BF16), sc_ref[...])) * norm
    o_ref[...] = out.astype(o_ref.dtype)


def fourier_short(u, cb, batch):
    n = u.shape[0]
    t = n // batch
    pos = jnp.arange(t, dtype=jnp.int32)
    ang = ((pos[:, None] * pos[None, :]) % t).astype(F32) * (2.0 * np.pi / t)
    ct, st = jnp.cos(ang).astype(BF16), jnp.sin(ang).astype(BF16)
    cc, sc = _channel_tables(BW)
    norm = float(1.0 / np.sqrt(t * FOURIER_GROUP_DIM))
    full = lambda shape: pl.BlockSpec(shape, lambda b: (0,) * len(shape))
    return pl.pallas_call(
        functools.partial(_fourier_small_kernel, norm=norm),
        grid=(batch,),
        in_specs=[pl.BlockSpec((t, BW), lambda b: (b, cb)), full((t, t)), full((t, t)),
                  full((BW, BW)), full((BW, BW))],
        out_specs=pl.BlockSpec((t, BW), lambda b: (b, 0)),
        out_shape=jax.ShapeDtypeStruct((n, BW), BF16),
        compiler_params=_cp(("arbitrary",)),
        name="fourier_ctx",
    )(u, ct, st, cc, sc)


def _na_bias_table(rpb):
    n_r, n_c = rpb.shape[1], rpb.shape[2]
    span = 2 * GRID_W
    left = GRID_W - NA_WIN_C
    vp = jnp.pad(rpb.astype(F32), ((0, 0), (0, 0), (left, span - n_c - left)))
    rep = jnp.broadcast_to(vp[:, :, None, :], (HEADS, n_r, GRID_W, span)).reshape(HEADS, n_r, GRID_W * span)
    skew = rep[..., :GRID_W * (span - 1)].reshape(HEADS, n_r, GRID_W, span - 1)
    toep = skew[..., GRID_W - 1:]
    qc = np.arange(GRID_W)[:, None]
    kc = np.arange(GRID_W)[None, :]
    start = np.clip(qc - NA_WIN_C // 2, 0, GRID_W - NA_WIN_C)
    valid = (kc >= start) & (kc < start + NA_WIN_C)
    toep = jnp.where(valid, toep, NEG)
    tab = jnp.stack([toep[:, NA_WIN_R - 1 - v:2 * NA_WIN_R - 1 - v] for v in range(NA_WIN_R)])
    tab = tab.transpose(0, 1, 3, 2, 4)
    return tab.reshape(NA_WIN_R, HEADS * GRID_W, NA_WIN_R * GRID_W)


def _na_kernel(q_ref, k_ref, v_ref, kc_ref, vc_ref, bias_ref, o_ref, *, rows_per_step, n_rows):
    i = pl.program_id(1)
    w = q_ref.shape[1]
    lane = lax.broadcasted_iota(jnp.int32, (1, w), 1)
    head = lane // (w // HEADS)
    scale = jnp.asarray(NA_HEAD_DIM ** -0.5, q_ref.dtype)
    kc = kc_ref[...]
    vc = vc_ref[...]
    win = NA_WIN_R * GRID_W

    def row(rl, carry):
        r = i * rows_per_step + rl
        rs = jnp.clip(r - NA_WIN_R // 2, 0, n_rows - NA_WIN_R)
        var = r - rs
        q0 = pl.multiple_of(rl * GRID_W, GRID_W)
        k0 = pl.multiple_of(rs * GRID_W, GRID_W)
        q = q_ref[pl.ds(q0, GRID_W), :] * scale
        kw = k_ref[pl.ds(k0, win), :]
        vw = v_ref[pl.ds(k0, win), :]
        zero_b = jnp.zeros_like(q)
        q4 = jnp.concatenate([jnp.where(head == h, q, zero_b) for h in range(HEADS)], axis=0)
        s_loc = _dot_nt(q4, kw) + bias_ref[var]
        s_ctx = _dot_nt(q4, kc)
        m = jnp.maximum(jnp.max(s_loc, axis=-1, keepdims=True), jnp.max(s_ctx, axis=-1, keepdims=True))
        p_loc = jnp.exp(s_loc - m)
        p_ctx = jnp.exp(s_ctx - m)
        l = jnp.sum(p_loc, axis=-1, keepdims=True) + jnp.sum(p_ctx, axis=-1, keepdims=True)
        pv = (_dot(p_loc.astype(BF16), vw) + _dot(p_ctx.astype(BF16), vc)) / l
        acc = jnp.zeros((GRID_W, w), F32)
        for h in range(HEADS):
            acc = acc + jnp.where(head == h, pv[h * GRID_W:(h + 1) * GRID_W], 0.0)
        o_ref[pl.ds(q0, GRID_W), :] = acc.astype(o_ref.dtype)
        return carry

    lax.fori_loop(0, rows_per_step, row, 0, unroll=2)


def na_attention(u, uc, cb0, bias_tab, batch, rows_per_step):
    n = u.shape[0]
    t = n // batch
    tc = uc.shape[0] // batch
    n_rows = t // GRID_W
    steps = n_rows // rows_per_step
    tq = rows_per_step * GRID_W
    return pl.pallas_call(
        functools.partial(_na_kernel, rows_per_step=rows_per_step, n_rows=n_rows),
        grid=(batch, steps),
        in_specs=[
            pl.BlockSpec((tq, BW), lambda b, i: (b * steps + i, cb0 + CB_NQ)),
            pl.BlockSpec((t, BW), lambda b, i: (b, cb0 + CB_NK)),
            pl.BlockSpec((t, BW), lambda b, i: (b, cb0 + CB_NV)),
            pl.BlockSpec((tc, BW), lambda b, i: (b, cb0 + CB_NK)),
            pl.BlockSpec((tc, BW), lambda b, i: (b, cb0 + CB_NV)),
            pl.BlockSpec(bias_tab.shape, lambda b, i: (0, 0, 0)),
        ],
        out_specs=pl.BlockSpec((tq, BW), lambda b, i: (b * steps + i, 0)),
        out_shape=jax.ShapeDtypeStruct((n, BW), BF16),
        compiler_params=_cp(("arbitrary", "arbitrary")),
        name="na_attention",
    )(u, u, u, uc, uc, bias_tab)


def _na_ctx_kernel(q_ref, k_ref, v_ref, o_ref):
    w = q_ref.shape[1]
    lane = lax.broadcasted_iota(jnp.int32, (1, w), 1)
    head = lane // (w // HEADS)
    scale = NA_HEAD_DIM ** -0.5
    q = q_ref[...]
    k = k_ref[...]
    v = v_ref[...]
    zero_b = jnp.zeros_like(q)
    acc = jnp.zeros(q.shape, F32)
    for h in range(HEADS):
        s = _dot_nt(jnp.where(head == h, q, zero_b), k) * scale
        p = jnp.exp(s - jnp.max(s, axis=-1, keepdims=True))
        l = jnp.sum(p, axis=-1, keepdims=True)
        acc = acc + jnp.where(head == h, _dot(p.astype(BF16), v) / l, 0.0)
    o_ref[...] = acc.astype(o_ref.dtype)


def na_ctx_attention(uc, cb0, batch):
    tc = uc.shape[0] // batch
    spec = lambda cb: pl.BlockSpec((tc, BW), lambda b: (b, cb0 + cb))
    return pl.pallas_call(
        _na_ctx_kernel,
        grid=(batch,),
        in_specs=[spec(CB_NQ), spec(CB_NK), spec(CB_NV)],
        out_specs=pl.BlockSpec((tc, BW), lambda b: (b, 0)),
        out_shape=jax.ShapeDtypeStruct((uc.shape[0], BW), BF16),
        compiler_params=_cp(("arbitrary",)),
        name="na_ctx_attention",
    )(uc, uc, uc)


def _mla_prep_kernel(cq_ref, ckv_ref, kr_ref, cos_ref, sin_ref, qn_ref, kvn_ref, wq_ref, wqr_ref,
                     wk_ref, wv_ref, p1_ref, p2_ref, one_ref, q_ref, k_ref, v_ref):
    cos = cos_ref[...]
    sin = sin_ref[...]
    cos4 = jnp.concatenate([cos] * HEADS, axis=-1)
    sin4 = jnp.concatenate([sin] * HEADS, axis=-1)

    cq = cq_ref[...].astype(F32)
    ms = jnp.sum(cq * cq, axis=-1, keepdims=True) * (1.0 / MLA_Q_RANK)
    cqn = ((cq * lax.rsqrt(ms + NORM_EPS)) * qn_ref[...]).astype(BF16)
    q = _dot(cqn, wq_ref[...]) * cos4 + _dot(cqn, wqr_ref[...]) * sin4
    q_ref[...] = (q * float((MLA_NOPE + MLA_ROPE) ** -0.5 * np.log2(np.e))).astype(q_ref.dtype)

    ckv = ckv_ref[...].astype(F32)
    ms = jnp.mean(ckv * ckv, axis=-1, keepdims=True)
    ckvn = ((ckv * lax.rsqrt(ms + NORM_EPS)) * kvn_ref[...]).astype(BF16)
    kr = kr_ref[...]
    k_rot = _dot(kr, p1_ref[...]) * cos + _dot(kr, p2_ref[...]) * sin
    k = _dot(ckvn, wk_ref[...]) + jnp.concatenate([k_rot] * HEADS, axis=-1)
    k_ref[...] = k.astype(k_ref.dtype)
    v_ref[...] = (_dot(ckvn, wv_ref[...]) + one_ref[...]).astype(v_ref.dtype)


def _mla_weights(w_uq, w_ukv):
    qr = w_uq.shape[0]
    dq = MLA_NOPE + MLA_ROPE
    hr = MLA_ROPE // 2
    wq3 = w_uq.reshape(qr, HEADS, dq)
    zq = jnp.zeros((qr, HEADS, LANES - dq), F32)
    wq = jnp.concatenate([wq3, zq], axis=-1)
    x1 = wq3[..., MLA_NOPE:MLA_NOPE + hr]
    x2 = wq3[..., MLA_NOPE + hr:]
    wqr = jnp.concatenate([jnp.zeros((qr, HEADS, MLA_NOPE), F32), -x2, x1, zq], axis=-1)
    pad_rows = lambda m: jnp.pad(m.reshape(qr, HEADS * LANES), ((0, BW - qr), (0, 0)))
    kvr = w_ukv.shape[0]
    wkv3 = w_ukv.reshape(kvr, HEADS, MLA_NOPE + MLA_V)
    zk = jnp.zeros((kvr, HEADS, LANES - MLA_NOPE), F32)
    wk = jnp.concatenate([wkv3[..., :MLA_NOPE], zk], axis=-1).reshape(kvr, HEADS * LANES)
    wv = jnp.concatenate([wkv3[..., MLA_NOPE:], zk], axis=-1).reshape(kvr, HEADS * LANES)
    j = np.arange(hr)
    p1 = np.zeros((LANES, LANES), np.float32)
    p1[np.arange(MLA_ROPE), MLA_NOPE + np.arange(MLA_ROPE)] = 1.0
    p2 = np.zeros((LANES, LANES), np.float32)
    p2[hr + j, MLA_NOPE + j] = -1.0
    p2[j, MLA_NOPE + hr + j] = 1.0
    one = np.zeros((1, HEADS * LANES), np.float32)
    one[0, MLA_V + LANES * np.arange(HEADS)] = 1.0
    return (pad_rows(wq).astype(BF16), pad_rows(wqr).astype(BF16), wk.astype(BF16), wv.astype(BF16),
            jnp.asarray(p1, BF16), jnp.asarray(p2, BF16), jnp.asarray(one))


def mla_prep(u, cb0, cos, sin, q_norm, kv_norm, weights, tm, rope_blocks):
    n = u.shape[0]
    wq, wqr, wk, wv, p1, p2, one = weights
    qn = jnp.pad(q_norm, (0, BW - q_norm.shape[0])).reshape(1, BW)
    full = lambda a: pl.BlockSpec(a.shape, lambda i: (0,) * a.ndim)
    tab = pl.BlockSpec((tm, LANES), lambda i: (i % rope_blocks, 0))
    kv_cb = (cb0 + CB_MKV) * (BW // LANES)
    out = jax.ShapeDtypeStruct((n, HEADS * LANES), BF16)
    ospec = pl.BlockSpec((tm, HEADS * LANES), lambda i: (i, 0))
    return pl.pallas_call(
        _mla_prep_kernel,
        grid=(n // tm,),
        in_specs=[
            pl.BlockSpec((tm, BW), lambda i: (i, cb0 + CB_MCQ)),
            pl.BlockSpec((tm, LANES), lambda i: (i, kv_cb)),
            pl.BlockSpec((tm, LANES), lambda i: (i, kv_cb + 1)),
            tab, tab, full(qn), pl.BlockSpec((1, LANES), lambda i: (0, 0)),
            full(wq), full(wqr), full(wk), full(wv), full(p1), full(p2), full(one),
        ],
        out_specs=[ospec, ospec, ospec],
        out_shape=[out, out, out],
        compiler_params=_cp(("arbitrary",)),
        name="mla_prep",
    )(u, u, u, cos, sin, qn, kv_norm.reshape(1, LANES), wq, wqr, wk, wv, p1, p2, one)


def _flash_kernel(*refs, lens, tk):
    q_ref = refs[0]
    kv_refs = refs[1:1 + 2 * len(lens)]
    o_ref = refs[1 + 2 * len(lens)]
    q = q_ref[...]
    tq = q.shape[0]
    m = jnp.full((tq, 1), NEG, F32)
    acc = jnp.zeros((tq, LANES), F32)

    def chunk(kc, vc, m, acc):
        s = _dot_nt(q, kc)
        m_new = jnp.maximum(m, jnp.max(s, axis=-1, keepdims=True))
        p = jnp.exp2((s - m_new).astype(BF16))
        acc = jnp.exp2(m - m_new) * acc + _dot(p, vc)
        return m_new, acc

    for si, length in enumerate(lens):
        k_ref, v_ref = kv_refs[2 * si], kv_refs[2 * si + 1]
        step = min(tk, length)
        if length == step:
            m, acc = chunk(k_ref[...], v_ref[...], m, acc)
        else:
            def body(j, carry, k_ref=k_ref, v_ref=v_ref, step=step):
                j0 = pl.multiple_of(j * step, step)
                return chunk(k_ref[pl.ds(j0, step), :], v_ref[pl.ds(j0, step), :], *carry)

            m, acc = lax.fori_loop(0, length // step, body, (m, acc), unroll=4)

    lane = lax.broadcasted_iota(jnp.int32, (1, LANES), 1)
    l = jnp.sum(jnp.where(lane == MLA_V, acc, 0.0), axis=-1, keepdims=True)
    o_ref[...] = jnp.where(lane < MLA_V, acc / l, 0.0).astype(o_ref.dtype)


def flash_attention(q, kvs, batch, tq, tk):
    n = q.shape[0]
    nq = n // batch // tq
    lens = tuple(k.shape[0] // batch for k, _ in kvs)
    in_specs = [pl.BlockSpec((tq, LANES), lambda b, h, i: (b * nq + i, h))]
    args = [q]
    for (k, v), length in zip(kvs, lens):
        in_specs += [pl.BlockSpec((length, LANES), lambda b, h, i: (b, h))] * 2
        args += [k, v]
    return pl.pallas_call(
        functools.partial(_flash_kernel, lens=lens, tk=tk),
        grid=(batch, HEADS, nq),
        in_specs=in_specs,
        out_specs=pl.BlockSpec((tq, LANES), lambda b, h, i: (b * nq + i, h)),
        out_shape=jax.ShapeDtypeStruct((n, HEADS * LANES), BF16),
        compiler_params=_cp(("arbitrary", "arbitrary", "arbitrary")),
        name="mla_attention",
    )(*args)


def _merge_kernel(*refs, with_router):
    (gate_ref, yr_f_ref, yr_b_ref, yf_ref, yn_ref, ym_ref, x_ref, g1_ref, wb_ret_ref, wb_f_ref,
     wb_na_ref, wb_mla_ref, wo_ref, gain_ref, sc_ref, sh_ref) = refs[:16]
    rest = refs[16:]
    if with_router:
        rh_ref, rl_ref, x_out_ref, lg_out_ref, hp_out_ref = rest
    else:
        x_out_ref, h_out_ref = rest
    d = x_ref.shape[1]

    def gated(k, y):
        g = jax.nn.sigmoid(gate_ref[:, k * d:(k + 1) * d].astype(F32))
        return g * y

    m = gated(0, _dot(yr_f_ref[...] + yr_b_ref[...], wb_ret_ref[...]))
    m = m + gated(1, _dot(yf_ref[...], wb_f_ref[...]))
    m = m + gated(2, _dot(yn_ref[...], wb_na_ref[...]))
    m = m + gated(3, _dot(ym_ref[...], wb_mla_ref[...]))
    y = _dot(m.astype(BF16), wo_ref[...])
    x = x_ref[...] + g1_ref[0] * y
    x_out_ref[...] = x
    hn = x * lax.rsqrt(jnp.mean(x * x, axis=-1, keepdims=True) + NORM_EPS)
    h = (hn * gain_ref[...]) * (1.0 + sc_ref[0]) + sh_ref[0]
    if with_router:
        hp_out_ref[...] = _pack_bf16_pairs(h)
        h_hi = h.astype(BF16)
        h_lo = (h - h_hi.astype(F32)).astype(BF16)
        lg_out_ref[...] = (_dot(h_hi, rh_ref[...]) + _dot(h_lo, rh_ref[...])) + _dot(h_hi, rl_ref[...])
    else:
        h_out_ref[...] = h.astype(h_out_ref.dtype)


def merge(u, y_ret, y_four, y_na, y_mla, x, mods, mod_row, gain, wb, w_out, router, tm):
    n, d = x.shape
    wb_ret, wb_f, wb_na, wb_mla = wb
    full = lambda a: pl.BlockSpec(a.shape, lambda i: (0,) * a.ndim)
    br = lambda: pl.BlockSpec((tm, BW), lambda i: (i, 0))
    tok = lambda: pl.BlockSpec((tm, d), lambda i: (i, 0))
    in_specs = [
        pl.BlockSpec((tm, 4 * d), lambda i: (i, 0)),
        br(), br(), br(), br(),
        pl.BlockSpec((tm, HEADS * LANES), lambda i: (i, 0)),
        tok(), _mod_spec(d, mod_row, tm, 2),
        full(wb_ret), full(wb_f), full(wb_na), full(wb_mla), full(w_out),
        pl.BlockSpec((1, d), lambda i: (0, 0)), _mod_spec(d, mod_row, tm, 4), _mod_spec(d, mod_row, tm, 3),
    ]
    args = [u, y_ret[0], y_ret[1], y_four, y_na, y_mla, x, mods, wb_ret, wb_f, wb_na, wb_mla, w_out,
            gain.reshape(1, d), mods, mods]
    out_specs = [tok()]
    out_shape = [jax.ShapeDtypeStruct((n, d), F32)]
    if router is None:
        out_specs.append(tok())
        out_shape.append(jax.ShapeDtypeStruct((n, d), BF16))
    else:
        in_specs += [full(router[0]), full(router[1])]
        args += list(router)
        out_specs += [pl.BlockSpec((tm, LANES), lambda i: (i, 0)), pl.BlockSpec((tm, d // 2), lambda i: (i, 0))]
        out_shape += [jax.ShapeDtypeStruct((n, LANES), F32), jax.ShapeDtypeStruct((n, d // 2), jnp.int32)]
    return pl.pallas_call(
        functools.partial(_merge_kernel, with_router=router is not None),
        grid=(n // tm,),
        in_specs=in_specs,
        out_specs=out_specs,
        out_shape=out_shape,
        compiler_params=_cp(("arbitrary",)),
        name="merge",
    )(*args)


def _ffn_kernel(h_ref, wg_ref, wu_ref, wd_ref, x_ref, g2_ref, o_ref, acc_ref):
    f = pl.program_id(1)

    @pl.when(f == 0)
    def _():
        acc_ref[...] = jnp.zeros_like(acc_ref)

    h = h_ref[...]
    a = _silu(_dot(h, wg_ref[...])) * _dot(h, wu_ref[...])
    acc_ref[...] += _dot(a.astype(BF16), wd_ref[...])

    @pl.when(f == pl.num_programs(1) - 1)
    def _():
        o_ref[...] = x_ref[...] + g2_ref[0] * acc_ref[...]


def ffn(h, x, mods, mod_row, wg, wu, wd, tm, tf):
    n, d = x.shape
    nf = wg.shape[1] // tf
    return pl.pallas_call(
        _ffn_kernel,
        grid=(n // tm, nf),
        in_specs=[
            pl.BlockSpec((tm, d), lambda i, f: (i, 0)),
            pl.BlockSpec((d, tf), lambda i, f: (0, f)),
            pl.BlockSpec((d, tf), lambda i, f: (0, f)),
            pl.BlockSpec((tf, d), lambda i, f: (f, 0)),
            pl.BlockSpec((tm, d), lambda i, f: (i, 0)),
            _mod_spec(d, mod_row, tm, 5),
        ],
        out_specs=pl.BlockSpec((tm, d), lambda i, f: (i, 0)),
        out_shape=jax.ShapeDtypeStruct((n, d), F32),
        scratch_shapes=[pltpu.VMEM((tm, d), F32)],
        compiler_params=_cp(("arbitrary", "arbitrary"), VMEM_LIMIT_LARGE),
        name="ffn",
    )(h, wg, wu, wd, x, mods)


def _moe_kernel(be_ref, nu_ref, xp_ref, wg_ref, wu_ref, wd_ref, o_ref, acc_ref, x_ref):
    i = pl.program_id(0)
    f = pl.program_id(1)
    used = i < nu_ref[0]

    @pl.when(f == 0)
    def _():
        acc_ref[...] = jnp.zeros_like(acc_ref)
        x_ref[...] = _unpack_bf16_pairs(xp_ref[...]).astype(BF16)

    @pl.when(used)
    def _():
        x = x_ref[...]
        a = _silu(_dot(x, wg_ref[0])) * _dot(x, wu_ref[0])
        acc_ref[...] += _dot(a.astype(BF16), wd_ref[0])

    @pl.when(f == pl.num_programs(1) - 1)
    def _():
        o_ref[...] = _pack_bf16_pairs(acc_ref[...])


SC_CAST_BLOCK = (16, 512)


def sc_cast_bf16(w):
    e, a, b = w.shape
    br, bc = SC_CAST_BLOCK
    assert (e * a) % br == 0 and b % bc == 0
    mesh = plsc.VectorSubcoreMesh(core_axis_name="c", subcore_axis_name="s")

    @functools.partial(pl.kernel, mesh=mesh, out_type=jax.ShapeDtypeStruct((e * a, b), BF16), scratch_types=[])
    def cast(x_hbm, o_hbm):
        def body(in_v, out_v):
            @pl.loop(0, br, step=2)
            def _(r):
                @pl.loop(0, bc, step=16)
                def _(c):
                    top = in_v.at[pl.ds(r, 1), pl.ds(c, 16)][...]
                    bot = in_v.at[pl.ds(r + 1, 1), pl.ds(c, 16)][...]
                    out_v.at[pl.ds(r, 2), pl.ds(c, 16)][...] = jnp.concatenate([top, bot], axis=0).astype(BF16)

        pltpu.emit_pipeline(
            body,
            grid=(e * a // br, b // bc),
            in_specs=[pl.BlockSpec((br, bc), lambda i, j: (i, j))],
            out_specs=[pl.BlockSpec((br, bc), lambda i, j: (i, j))],
            core_axis_name=("c", "s"),
            dimension_semantics=(pltpu.PARALLEL, pltpu.PARALLEL),
        )(x_hbm, o_hbm)

    return cast(w.reshape(e * a, b)).reshape(e, a, b)


def moe_ffn(blk_exp, n_used, xb, wg, wu, wd, tm, tf):
    n, dp = xb.shape
    d = 2 * dp
    nf = wg.shape[2] // tf
    grid_spec = pltpu.PrefetchScalarGridSpec(
        num_scalar_prefetch=2,
        grid=(n // tm, nf),
        in_specs=[
            pl.BlockSpec((tm, dp), lambda i, f, be, nu: (i, 0)),
            pl.BlockSpec((1, d, tf), lambda i, f, be, nu: (be[i], 0, f)),
            pl.BlockSpec((1, d, tf), lambda i, f, be, nu: (be[i], 0, f)),
            pl.BlockSpec((1, tf, d), lambda i, f, be, nu: (be[i], f, 0)),
        ],
        out_specs=pl.BlockSpec((tm, dp), lambda i, f, be, nu: (i, 0)),
        scratch_shapes=[pltpu.VMEM((tm, d), F32), pltpu.VMEM((tm, d), BF16)],
    )
    return pl.pallas_call(
        _moe_kernel,
        grid_spec=grid_spec,
        out_shape=jax.ShapeDtypeStruct((n, dp), jnp.int32),
        compiler_params=_cp(("arbitrary", "arbitrary")),
        name="moe_ffn",
    )(blk_exp, n_used, xb, wg, wu, wd)


def _combine_kernel(x_ref, ya_ref, yb_ref, w_ref, g2_ref, gain_ref, o_ref, *, final):
    w = w_ref[...]
    y = w[:, 0:1] * _unpack_bf16_pairs(ya_ref[...]) + w[:, 1:2] * _unpack_bf16_pairs(yb_ref[...])
    x = x_ref[...] + g2_ref[0] * y
    if final:
        x = (x * lax.rsqrt(jnp.mean(x * x, axis=-1, keepdims=True) + NORM_EPS)) * gain_ref[...]
    o_ref[...] = x


def moe_combine(x, y2, w, mods, mod_row, gain, final, tm):
    n, d = x.shape
    tok = lambda: pl.BlockSpec((tm, d), lambda i: (i, 0))
    half = lambda off: pl.BlockSpec((tm, d // 2), lambda i: (i + off, 0))
    return pl.pallas_call(
        functools.partial(_combine_kernel, final=final),
        grid=(n // tm,),
        in_specs=[tok(), half(0), half(n // tm),
                  pl.BlockSpec((tm, MOE_TOP_K), lambda i: (i, 0)),
                  _mod_spec(d, mod_row, tm, 5), pl.BlockSpec((1, d), lambda i: (0, 0))],
        out_specs=tok(),
        out_shape=jax.ShapeDtypeStruct((n, d), F32),
        compiler_params=_cp(("arbitrary",)),
        name="moe_combine",
    )(x, y2, y2, w, mods, gain.reshape(1, d))


def _rmsnorm_kernel(x_ref, gain_ref, o_ref):
    x = x_ref[...]
    o_ref[...] = (x * lax.rsqrt(jnp.mean(x * x, axis=-1, keepdims=True) + NORM_EPS)) * gain_ref[...]


def rmsnorm_rows(x, gain, tm):
    n, d = x.shape
    return pl.pallas_call(
        _rmsnorm_kernel,
        grid=(n // tm,),
        in_specs=[pl.BlockSpec((tm, d), lambda i: (i, 0)), pl.BlockSpec((1, d), lambda i: (0, 0))],
        out_specs=pl.BlockSpec((tm, d), lambda i: (i, 0)),
        out_shape=jax.ShapeDtypeStruct((n, d), F32),
        compiler_params=_cp(("arbitrary",)),
        name="final_norm",
    )(x, gain.reshape(1, d))


SC_CORES = 2
SC_SUBCORES = 16
SC_CHUNK = 64


def sc_gather_rows(table, idx):
    n_out = idx.shape[0]
    width = table.shape[1]
    workers = SC_CORES * SC_SUBCORES
    per_worker = n_out // workers
    assert n_out == per_worker * workers and per_worker % SC_CHUNK == 0 and table.dtype == jnp.int32
    mesh = plsc.VectorSubcoreMesh(core_axis_name="c", subcore_axis_name="s")

    @functools.partial(
        pl.kernel, mesh=mesh,
        out_type=jax.ShapeDtypeStruct((n_out, width), table.dtype),
        scratch_types=[pltpu.VMEM((SC_CHUNK,), jnp.int32), pltpu.VMEM((SC_CHUNK, width), table.dtype),
                       pltpu.SemaphoreType.DMA],
    )
    def gather(table_hbm, idx_hbm, out_hbm, idx_v, rows_v, sem):
        base = (lax.axis_index("s") * SC_CORES + lax.axis_index("c")) * per_worker

        @pl.loop(0, per_worker // SC_CHUNK)
        def _(j):
            off = pl.multiple_of(base + j * SC_CHUNK, SC_CHUNK)
            pltpu.sync_copy(idx_hbm.at[pl.ds(off, SC_CHUNK)], idx_v)
            pltpu.async_copy(table_hbm.at[idx_v], rows_v, sem).wait()
            pltpu.sync_copy(rows_v, out_hbm.at[pl.ds(off, SC_CHUNK)])

    return gather(table, idx)


def moe_route(logits, n_experts, tm):
    n_tok = logits.shape[0]
    top_logit, top_idx = lax.top_k(logits[:, :n_experts], MOE_TOP_K)
    top_w = jax.nn.softmax(top_logit, axis=-1)
    e_flat = top_idx.reshape(-1).astype(jnp.int32)
    n_assign = e_flat.shape[0]
    onehot = (e_flat[:, None] == jnp.arange(n_experts, dtype=jnp.int32)[None, :]).astype(jnp.int32)
    rank = jnp.sum((jnp.cumsum(onehot, axis=0) - onehot) * onehot, axis=1)
    counts = jnp.sum(onehot, axis=0)
    padded = (counts + tm - 1) // tm * tm
    pad_end = jnp.cumsum(padded)
    pad_start = pad_end - padded
    dest = pad_start[e_flat] + rank
    n_rows = n_assign + n_experts * tm
    tok = jnp.arange(n_assign, dtype=jnp.int32) // MOE_TOP_K
    blk_start = jnp.arange(n_rows // tm, dtype=jnp.int32) * tm
    blk_exp = jnp.minimum(jnp.sum(pad_end[None, :] <= blk_start[:, None], axis=1), n_experts - 1)
    n_used = (pad_end[-1] // tm).reshape(1)
    _, tok_by_row = lax.sort_key_val(dest, tok)
    max_shift = n_experts * tm
    filler = jnp.arange(max_shift, dtype=jnp.int32)
    compact = jnp.concatenate([filler, tok_by_row, filler])
    shift = pad_start - (jnp.cumsum(counts) - counts)
    row_exp = jnp.repeat(blk_exp, tm)
    row_tok = jnp.zeros((n_rows,), jnp.int32)
    for e in range(n_experts):
        shifted = lax.dynamic_slice(compact, (max_shift - shift[e],), (n_rows,))
        row_tok = jnp.where(row_exp == e, shifted, row_tok)
    return row_tok, dest.reshape(n_tok, MOE_TOP_K), top_w, blk_exp.astype(jnp.int32), n_used.astype(jnp.int32)


def _rope_split(wcols):
    d, w = wcols.shape
    half = w // HEADS // 2
    return wcols.reshape(d, HEADS, 2, half).transpose(0, 2, 1, 3).reshape(d, w)


def _inproj_weights(w_in):
    d = w_in.shape[0]
    kv = (BW, BW, BW, BW, MLA_KV_RANK, MLA_ROPE)
    qs = (BW, BW, BW, BW, BW, MLA_Q_RANK, 4 * d)
    offs = np.concatenate([[0], np.cumsum(kv + qs)])
    seg = lambda i: w_in[:, offs[i]:offs[i + 1]]
    r_k, r_v, n_k, n_v, m_ckv, m_kr = (seg(i) for i in range(6))
    r_q, r_gf, r_gb, f_in, n_q, m_cq, gate = (seg(6 + i) for i in range(7))
    z = lambda n: jnp.zeros((d, n), w_in.dtype)
    cols = [gate, _rope_split(r_q), _rope_split(r_k), r_v, r_gf, r_gb, f_in, n_q, n_k, n_v,
            m_cq, z(BW - MLA_Q_RANK), m_ckv, m_kr, z(LANES - MLA_ROPE)]
    return jnp.concatenate(cols, axis=1).astype(BF16)


def _ret_rope_tables(n):
    t = jnp.arange(n)
    row = (t // GRID_W).astype(F32)
    col = (t % GRID_W).astype(F32)
    nf = RET_DK // 4
    inv = ROPE_BASE ** (-jnp.arange(nf, dtype=F32) / nf)
    ang = jnp.concatenate([row[:, None] * inv, col[:, None] * inv], axis=-1)
    return jnp.tile(jnp.cos(ang), (1, HEADS)), jnp.tile(jnp.sin(ang), (1, HEADS))


def _mla_rope_tables(n):
    t = jnp.arange(n)
    row = (t // GRID_W).astype(F32)
    col = (t % GRID_W).astype(F32)
    nf = MLA_ROPE // 4
    inv = ROPE_BASE ** (-jnp.arange(nf, dtype=F32) / nf)
    ang = jnp.concatenate([row[:, None] * inv, col[:, None] * inv], axis=-1)
    pad = jnp.zeros((n, LANES - MLA_NOPE - MLA_ROPE), F32)
    cos = jnp.concatenate([jnp.ones((n, MLA_NOPE), F32), jnp.cos(ang), jnp.cos(ang), pad], axis=-1)
    sin = jnp.concatenate([jnp.zeros((n, MLA_NOPE), F32), jnp.sin(ang), jnp.sin(ang), pad], axis=-1)
    return cos, sin


def _tile_rows(*sizes):
    for tm in (1024, 512, 256, 128):
        if all(s % tm == 0 for s in sizes):
            return tm
    raise ValueError(f"token counts {sizes} need a common multiple-of-128 row tile")


def kernel(x, c, ctx, c_ctx, ada_w, ada_b, norm_mix, norm_ffn, w_in, ret_decay_fwd, ret_decay_bwd,
           mla_q_norm, mla_kv_norm, mla_w_uq, mla_w_ukv, na_rpb, w_branch, w_out,
           ffn_w_gate, ffn_w_up, ffn_w_down, moe_router, moe_w_gate, moe_w_up, moe_w_down, norm_final):
    batch, t, d = x.shape
    tc = ctx.shape[1]
    depth = ada_w.shape[0]
    nl, ncx = batch * t, batch * tc
    assert batch < 8 and t % (16 * GRID_W) == 0 and tc % LANES == 0 and d == 4 * BW
    tm = _tile_rows(t, ncx)
    cb0 = 4 * d // BW

    xl = x.reshape(nl, d)
    xc = ctx.reshape(ncx, d)
    cc = jnp.zeros((8, d), F32).at[:batch].set(c).at[batch].set(c_ctx)
    mods = adaln(cc, ada_w, ada_b).reshape(depth, 8 * 6, 1, d)
    lat_row = lambda r0: r0 // t
    ctx_row = lambda r0: batch

    ret_cos, ret_sin = _ret_rope_tables(t)
    ret_cos_c, ret_sin_c = jnp.ones((tc, LANES), F32), jnp.zeros((tc, LANES), F32)
    mla_cos, mla_sin = _mla_rope_tables(t)
    mla_cos_c = jnp.concatenate([jnp.ones((tm, MLA_NOPE + MLA_ROPE), F32),
                                 jnp.zeros((tm, LANES - MLA_NOPE - MLA_ROPE), F32)], axis=-1)
    mla_sin_c = jnp.zeros((tm, LANES), F32)

    for i in range(depth):
        ctx_out = i < depth - 1
        md = mods[i]
        w_p = _inproj_weights(w_in[i])
        u = norm_inproj(xl, norm_mix[i], md, lat_row, w_p, tm, INPROJ_COLS)
        uc = norm_inproj(xc, norm_mix[i], md, ctx_row, w_p, tm, INPROJ_COLS)

        lg = jnp.stack([jax.nn.log_sigmoid(ret_decay_fwd[i].astype(F32)),
                        jax.nn.log_sigmoid(ret_decay_bwd[i].astype(F32))])
        zero_state = jnp.zeros((batch, 2, BW, BW), F32)
        yc_ret, s_ctx = retention(uc, cb0, lg, ret_cos_c, ret_sin_c, zero_state, batch, min(RET_CHUNK, tc))
        y_ret, _ = retention(u, cb0, lg, ret_cos, ret_sin, s_ctx, batch, RET_CHUNK)

        y_four = fourier_long(u, cb0 + CB_F, batch)

        bias_tab = _na_bias_table(na_rpb[i])
        y_na = na_attention(u, uc, cb0, bias_tab, batch, NA_ROWS_PER_STEP)

        mw = _mla_weights(mla_w_uq[i], mla_w_ukv[i])
        q_l, k_l, v_l = mla_prep(u, cb0, mla_cos, mla_sin, mla_q_norm[i], mla_kv_norm[i], mw, tm, t // tm)
        q_c, k_c, v_c = mla_prep(uc, cb0, mla_cos_c, mla_sin_c, mla_q_norm[i], mla_kv_norm[i], mw, tm, 1)
        y_mla = flash_attention(q_l, [(k_l, v_l), (k_c, v_c)], batch, min(FLASH_Q, t), FLASH_K)

        wb = w_branch[i].astype(BF16)
        wb_mla = jnp.concatenate(
            [wb[3].reshape(HEADS, MLA_V, d), jnp.zeros((HEADS, LANES - MLA_V, d), BF16)], axis=1
        ).reshape(HEADS * LANES, d)
        wbs = (wb[0], wb[1], wb[2], wb_mla)
        wo = w_out[i].astype(BF16)
        j = i // 2
        if i % 2 == 0:
            xl, h2 = merge(u, y_ret, y_four, y_na, y_mla, xl, md, lat_row, norm_ffn[i], wbs, wo, None, MERGE_ROWS)
            wg, wu, wd = ffn_w_gate[j].astype(BF16), ffn_w_up[j].astype(BF16), ffn_w_down[j].astype(BF16)
            tf = wg.shape[1] // 2
            xl = ffn(h2, xl, md, lat_row, wg, wu, wd, tm, tf)
        else:
            n_exp = moe_router.shape[2]
            r = jnp.pad(moe_router[j], ((0, 0), (0, LANES - n_exp)))
            r_hi = r.astype(BF16)
            r_lo = (r - r_hi.astype(F32)).astype(BF16)
            xl, logits, h2p = merge(u, y_ret, y_four, y_na, y_mla, xl, md, lat_row, norm_ffn[i], wbs, wo,
                                   (r_hi, r_lo), MERGE_ROWS)
            row_tok, dest, top_w, blk_exp, n_used = moe_route(logits, n_exp, MOE_ROWS)
            xb = sc_gather_rows(h2p, row_tok)
            ewg, ewu, ewd = (sc_cast_bf16(w) for w in (moe_w_gate[j], moe_w_up[j], moe_w_down[j]))
            yb = moe_ffn(blk_exp, n_used, xb, ewg, ewu, ewd, MOE_ROWS, MOE_FFN_COLS)
            y2 = sc_gather_rows(yb, dest.T.reshape(-1))
            xl = moe_combine(xl, y2, top_w, md, lat_row, norm_final, i == depth - 1, MERGE_ROWS)

        if ctx_out:
            yc_four = fourier_short(uc, cb0 + CB_F, batch)
            yc_na = na_ctx_attention(uc, cb0, batch)
            yc_mla = flash_attention(q_c, [(k_c, v_c)], batch, tc, FLASH_K)
            if i % 2 == 0:
                xc, hc2 = merge(uc, yc_ret, yc_four, yc_na, yc_mla, xc, md, ctx_row, norm_ffn[i], wbs, wo,
                                None, MERGE_ROWS)
                xc = ffn(hc2, xc, md, ctx_row, wg, wu, wd, tm, tf)
            else:
                raise NotImplementedError("context tokens through the expert mixer")

    if depth % 2 == 1:
        xl = rmsnorm_rows(xl, norm_final, tm)
    return xl.reshape(batch, t, d)
```

```python
import functools

import numpy as np
import jax
import jax.numpy as jnp
from jax import lax
from jax.experimental import pallas as pl
from jax.experimental.pallas import tpu as pltpu
from jax.experimental.pallas import tpu_sc as plsc

F32 = jnp.float32
BF16 = jnp.bfloat16

GRID_W = 64
ROPE_BASE = 10000.0
NORM_EPS = 1e-6
HEADS = 4
RET_DK = 64
FOURIER_GROUP_DIM = 64
NA_HEAD_DIM = 64
NA_WIN_R = 8
NA_WIN_C = 16
MLA_NOPE = 64
MLA_ROPE = 32
MLA_V = 64
MLA_Q_RANK = 192
MLA_KV_RANK = 128
MOE_TOP_K = 2
BW = 256

COL_GATE = 0
CB_RQ, CB_RK, CB_RV, CB_RGF, CB_RGB, CB_F, CB_NQ, CB_NK, CB_NV, CB_MCQ, CB_MKV = range(11)
LANES = 128
NEG = -1e30

VMEM_LIMIT = 48 * 1024 * 1024
VMEM_LIMIT_LARGE = 58 * 1024 * 1024

INPROJ_COLS = 2304
RET_CHUNK = 256
NA_ROWS_PER_STEP = 16
FLASH_Q, FLASH_K = 4096, 512
MERGE_ROWS = 512
MOE_ROWS, MOE_FFN_COLS = 512, 1792


def _cp(sem, vmem=VMEM_LIMIT):
    return pltpu.CompilerParams(dimension_semantics=sem, vmem_limit_bytes=vmem)


def _dot(a, b):
    return jnp.dot(a, b, preferred_element_type=F32)


def _dot_nt(a, b):
    return lax.dot_general(a, b, (((1,), (1,)), ((), ())), preferred_element_type=F32)


def _dot_tn(a, b):
    return lax.dot_general(a, b, (((0,), (0,)), ((), ())), preferred_element_type=F32)


def _silu(x):
    return x * jax.nn.sigmoid(x)


def _pack_bf16_pairs(x):
    k = x.shape[1] // 2
    lo = lax.bitcast_convert_type(x[:, :k].astype(BF16).astype(F32), jnp.uint32) >> 16
    hi = lax.bitcast_convert_type(x[:, k:].astype(BF16).astype(F32), jnp.uint32) & jnp.uint32(0xFFFF0000)
    return lax.bitcast_convert_type(lo | hi, jnp.int32)


def _unpack_bf16_pairs(p):
    u = lax.bitcast_convert_type(p, jnp.uint32)
    lo = lax.bitcast_convert_type(u << 16, F32)
    hi = lax.bitcast_convert_type(u & jnp.uint32(0xFFFF0000), F32)
    return jnp.concatenate([lo, hi], axis=-1)


def _adaln_kernel(c_ref, w_ref, b_ref, o_ref):
    s = _silu(c_ref[...])
    o_ref[0] = _dot(s.astype(BF16), w_ref[0].astype(BF16)) + b_ref[0]


def adaln(cc, ada_w, ada_b):
    depth, d, n6 = ada_w.shape
    tn = n6 // 4
    return pl.pallas_call(
        _adaln_kernel,
        grid=(depth, n6 // tn),
        in_specs=[
            pl.BlockSpec((8, d), lambda l, j: (0, 0)),
            pl.BlockSpec((1, d, tn), lambda l, j: (l, 0, j)),
            pl.BlockSpec((1, 1, tn), lambda l, j: (l, 0, j)),
        ],
        out_specs=pl.BlockSpec((1, 8, tn), lambda l, j: (l, 0, j)),
        out_shape=jax.ShapeDtypeStruct((depth, 8, n6), F32),
        compiler_params=_cp(("arbitrary", "arbitrary")),
        name="adaln",
    )(cc, ada_w, ada_b.reshape(depth, 1, n6))


def _inproj_kernel(x_ref, g_ref, sc_ref, sh_ref, w_ref, o_ref, h_ref):
    @pl.when(pl.program_id(1) == 0)
    def _():
        x = x_ref[...]
        y = x * lax.rsqrt(jnp.mean(x * x, axis=-1, keepdims=True) + NORM_EPS)
        h = (y * g_ref[...]) * (1.0 + sc_ref[0]) + sh_ref[0]
        h_ref[...] = h.astype(BF16)

    o_ref[...] = _dot(h_ref[...], w_ref[...]).astype(o_ref.dtype)


def _mod_spec(d, mod_row, tm, k):
    return pl.BlockSpec((1, 1, d), lambda i, *_: (mod_row(i * tm) * 6 + k, 0, 0))


def norm_inproj(x, gain, mods, mod_row, w, tm, tn):
    n, d = x.shape
    nc = w.shape[1]
    return pl.pallas_call(
        _inproj_kernel,
        grid=(n // tm, nc // tn),
        in_specs=[
            pl.BlockSpec((tm, d), lambda i, j: (i, 0)),
            pl.BlockSpec((1, d), lambda i, j: (0, 0)),
            _mod_spec(d, mod_row, tm, 1),
            _mod_spec(d, mod_row, tm, 0),
            pl.BlockSpec((d, tn), lambda i, j: (0, j)),
        ],
        out_specs=pl.BlockSpec((tm, tn), lambda i, j: (i, j)),
        out_shape=jax.ShapeDtypeStruct((n, nc), BF16),
        scratch_shapes=[pltpu.VMEM((tm, d), BF16)],
        compiler_params=_cp(("arbitrary", "arbitrary")),
        name="norm_inproj",
    )(x, gain.reshape(1, d), mods, mods, w)


def _ret_init(d, cs, w, lg_ref, s0_ref, lgq_ref, s_ref, decay_ref, qw_ref, kw_ref, avg_ref):
    rev = d == 1
    s_ref[d] = s0_ref[0, d]
    pos_i = lax.broadcasted_iota(jnp.int32, (cs, 1), 0).astype(F32)
    pos_j = lax.broadcasted_iota(jnp.int32, (1, cs), 1).astype(F32)
    p_i = jnp.where(rev, cs - 1.0 - pos_i, pos_i)
    p_j = jnp.where(rev, cs - 1.0 - pos_j, pos_j)
    diff = p_i - p_j
    for h in range(HEADS):
        decay_ref[d, h] = jnp.where(diff >= 0, jnp.exp(lg_ref[d, h] * jnp.maximum(diff, 0.0)), 0.0)
    lgq = lgq_ref[d]
    qw_ref[d] = jnp.exp(lgq * (p_i + 1.0))
    kw_ref[d] = jnp.exp(lgq * (cs - 1.0 - p_i))
    hd = w // HEADS
    gi = lax.broadcasted_iota(jnp.int32, (w, w), 0) // hd
    gj = lax.broadcasted_iota(jnp.int32, (w, w), 1) // hd
    avg_ref[...] = jnp.where(gi == gj, 1.0 / hd, 0.0).astype(BF16)


def _ret_chunk(d, q_ref, k_ref, v_ref, g_ref, cos_ref, sin_ref, lgv_ref, y_ref, s_ref, decay_ref, qw_ref, kw_ref,
               avg_ref):
    cs = q_ref.shape[0]
    w = q_ref.shape[1]
    half = w // 2
    lgv = lgv_ref[d]
    s_ref, decay_ref, qw_ref, kw_ref = s_ref.at[d], decay_ref.at[d], qw_ref.at[d], kw_ref.at[d]

    cos = cos_ref[...]
    sin = sin_ref[...]

    def rope(t):
        t1, t2 = t[:, :half], t[:, half:]
        return jnp.concatenate([t1 * cos - t2 * sin, t2 * cos + t1 * sin], axis=-1)

    q = rope(q_ref[...].astype(F32))
    k = rope(k_ref[...].astype(F32)) * (RET_DK ** -0.5)
    vb = v_ref[...]

    lane = lax.broadcasted_iota(jnp.int32, (1, w), 1)
    head_q = (lane % half) // (half // HEADS)
    head_v = lane // (w // HEADS)

    s_prev = s_ref[...]
    o = _dot((q * qw_ref[...]).astype(BF16), s_prev.astype(BF16))
    qb = q.astype(BF16)
    kb = k.astype(BF16)
    zero_b = jnp.zeros_like(qb)
    for h in range(HEADS):
        a = _dot_nt(jnp.where(head_q == h, qb, zero_b), kb)
        oh = _dot((a * decay_ref[h]).astype(BF16), vb)
        o = o + jnp.where(head_v == h, oh, 0.0)

    ds = _dot_tn((k * kw_ref[...]).astype(BF16), vb)
    row_head = (lax.broadcasted_iota(jnp.int32, (w, 1), 0) % half) // (half // HEADS)
    s_new = s_prev * jnp.exp(lgv * float(cs)) + jnp.where(row_head == head_v, ds, 0.0)
    s_ref[...] = s_new

    ms = _dot((o * o).astype(BF16), avg_ref[...])
    on = o * lax.rsqrt(ms + NORM_EPS)
    y_ref[...] = (_silu(g_ref[...].astype(F32)) * on).astype(y_ref.dtype)
    return s_new


def _ret_kernel(lg_ref, qf, kf, vf, gf, cosf, sinf, qb, kb, vb, gb, cosb, sinb, s0_ref, lgq_ref, lgv_ref,
                yf_ref, yb_ref, sout_ref, s_ref, decay_ref, qw_ref, kw_ref, avg_ref, *, n_chunks):
    c = pl.program_id(1)
    scratch = (s_ref, decay_ref, qw_ref, kw_ref, avg_ref)

    @pl.when(c == 0)
    def _():
        for d in range(2):
            _ret_init(d, qf.shape[0], qf.shape[1], lg_ref, s0_ref, lgq_ref, *scratch)

    s_f = _ret_chunk(0, qf, kf, vf, gf, cosf, sinf, lgv_ref, yf_ref, *scratch)
    s_b = _ret_chunk(1, qb, kb, vb, gb, cosb, sinb, lgv_ref, yb_ref, *scratch)

    @pl.when(c == n_chunks - 1)
    def _():
        sout_ref[0, 0] = s_f
        sout_ref[0, 1] = s_b


def retention(u, cb0, lg, cos, sin, s0, batch, cs):
    n = u.shape[0]
    t = n // batch
    nch = t // cs
    w = BW
    half = w // 2
    lgq = jnp.tile(jnp.repeat(lg, half // HEADS, axis=1), (1, 2)).reshape(2, 1, w)
    lgv = jnp.repeat(lg, w // HEADS, axis=1).reshape(2, 1, w)

    def chunk(d, c):
        return nch - 1 - c if d else c

    def direction(d):
        col = lambda cb: pl.BlockSpec((cs, w), lambda b, c: (b * nch + chunk(d, c), cb0 + cb))
        tab = lambda: pl.BlockSpec((cs, half), lambda b, c: (chunk(d, c), 0))
        return [col(CB_RQ), col(CB_RK), col(CB_RV), col(CB_RGF + d), tab(), tab()]

    y_spec = lambda d: pl.BlockSpec((cs, w), lambda b, c: (b * nch + chunk(d, c), 0))
    state_spec = pl.BlockSpec((1, 2, w, w), lambda b, c: (b, 0, 0, 0))
    lane_spec = pl.BlockSpec((2, 1, w), lambda b, c: (0, 0, 0))
    y_f, y_b, s_out = pl.pallas_call(
        functools.partial(_ret_kernel, n_chunks=nch),
        grid=(batch, nch),
        in_specs=[pl.BlockSpec(memory_space=pltpu.SMEM)] + direction(0) + direction(1)
        + [state_spec, lane_spec, lane_spec],
        out_specs=[y_spec(0), y_spec(1), state_spec],
        out_shape=[
            jax.ShapeDtypeStruct((n, w), BF16),
            jax.ShapeDtypeStruct((n, w), BF16),
            jax.ShapeDtypeStruct((batch, 2, w, w), F32),
        ],
        scratch_shapes=[pltpu.VMEM((2, w, w), F32), pltpu.VMEM((2, HEADS, cs, cs), F32),
                        pltpu.VMEM((2, cs, w), F32), pltpu.VMEM((2, cs, w), F32), pltpu.VMEM((w, w), BF16)],
        compiler_params=_cp(("arbitrary", "arbitrary")),
        name="retention",
    )(lg, u, u, u, u, cos, sin, u, u, u, u, cos, sin, s0, lgq, lgv)
    return (y_f, y_b), s_out


def _dft_tables(t, t1, t2):
    k1 = jnp.arange(t1, dtype=jnp.int32)
    a = jnp.arange(t1, dtype=jnp.int32)
    m = jnp.arange(t2, dtype=jnp.int32)
    ph1 = (k1[None, :, None] * (a[None, None, :] * t2 + m[:, None, None])) % t
    ang1 = ph1.astype(F32) * (2.0 * np.pi / t)
    ph2 = (m[:, None] * m[None, :]) % t2
    ang2 = ph2.astype(F32) * (2.0 * np.pi / t2)
    return (jnp.cos(ang1).astype(BF16), jnp.sin(ang1).astype(BF16),
            jnp.cos(ang2).astype(BF16), jnp.sin(ang2).astype(BF16))


def _channel_tables(width):
    ch = jnp.arange(width, dtype=jnp.int32)
    same = (ch[:, None] // FOURIER_GROUP_DIM) == (ch[None, :] // FOURIER_GROUP_DIM)
    ph = ((ch[:, None] % FOURIER_GROUP_DIM) * (ch[None, :] % FOURIER_GROUP_DIM)) % FOURIER_GROUP_DIM
    ang = ph.astype(F32) * (2.0 * np.pi / FOURIER_GROUP_DIM)
    return (jnp.where(same, jnp.cos(ang), 0.0).astype(BF16),
            jnp.where(same, jnp.sin(ang), 0.0).astype(BF16))


def _fourier_kernel(x_ref, c1_ref, s1_ref, c2_ref, s2_ref, cc_ref, sc_ref, o_ref, xa, yre, yim,
                    *, t1, t2, norm):
    xa[...] = x_ref[...].astype(F32)

    def stage1(m, carry):
        xs = xa[pl.ds(m, t1, stride=t2), :].astype(BF16)
        r0 = pl.multiple_of(m * t1, t1)
        yre[pl.ds(r0, t1), :] = _dot(c1_ref[m], xs)
        yim[pl.ds(r0, t1), :] = -_dot(s1_ref[m], xs)
        return carry

    lax.fori_loop(0, t2, stage1, 0, unroll=8)

    c2 = c2_ref[...]
    s2 = s2_ref[...]
    w2 = jnp.concatenate([jnp.concatenate([c2, s2], axis=1), jnp.concatenate([-s2, c2], axis=1)], axis=0)
    cs = jnp.concatenate([cc_ref[...], sc_ref[...]], axis=0)
    group = 8

    def stage2(j, carry):
        k1 = j * group
        rows = [pl.ds(k1 + g, t2, stride=t1) for g in range(group)]
        y = jnp.concatenate([jnp.concatenate([yre[r, :] for r in rows], axis=1),
                             jnp.concatenate([yim[r, :] for r in rows], axis=1)], axis=0).astype(BF16)
        z = _dot(w2, y).astype(BF16)
        w = z.shape[1] // group
        zz = jnp.concatenate([jnp.concatenate([z[:t2, g * w:(g + 1) * w], z[t2:, g * w:(g + 1) * w]], axis=1)
                              for g in range(group)], axis=0)
        out = _dot(zz, cs) * norm
        for g in range(group):
            xa[rows[g], :] = out[g * t2:(g + 1) * t2]
        return carry

    lax.fori_loop(0, t1 // group, stage2, 0)
    o_ref[...] = xa[...].astype(o_ref.dtype)


def fourier_long(u, cb, batch, t2=LANES):
    n = u.shape[0]
    t = n // batch
    t1 = t // t2
    c1, s1, c2, s2 = _dft_tables(t, t1, t2)
    cc, sc = _channel_tables(LANES)
    norm = float(1.0 / np.sqrt(t * FOURIER_GROUP_DIM))
    full = lambda shape: pl.BlockSpec(shape, lambda b, hh: (0,) * len(shape))
    return pl.pallas_call(
        functools.partial(_fourier_kernel, t1=t1, t2=t2, norm=norm),
        grid=(batch, BW // LANES),
        in_specs=[
            pl.BlockSpec((t, LANES), lambda b, hh: (b, cb * (BW // LANES) + hh)),
            full((t2, t1, t1)), full((t2, t1, t1)), full((t2, t2)), full((t2, t2)),
            full((LANES, LANES)), full((LANES, LANES)),
        ],
        out_specs=pl.BlockSpec((t, LANES), lambda b, hh: (b, hh)),
        out_shape=jax.ShapeDtypeStruct((n, BW), BF16),
        scratch_shapes=[pltpu.VMEM((t, LANES), F32)] * 3,
        compiler_params=_cp(("arbitrary", "arbitrary")),
        name="fourier",
    )(u, c1, s1, c2, s2, cc, sc)


def _fourier_small_kernel(x_ref, ct_ref, st_ref, cc_ref, sc_ref, o_ref, *, norm):
    x = x_ref[...]
    zr = _dot(ct_ref[...], x)
    zi = -_dot(st_ref[...], x)
    out = (_dot(zr.astype(BF16), cc_ref[...]) + _dot(zi.astype(BF16), sc_ref[...])) * norm
    o_ref[...] = out.astype(o_ref.dtype)


def fourier_short(u, cb, batch):
    n = u.shape[0]
    t = n // batch
    pos = jnp.arange(t, dtype=jnp.int32)
    ang = ((pos[:, None] * pos[None, :]) % t).astype(F32) * (2.0 * np.pi / t)
    ct, st = jnp.cos(ang).astype(BF16), jnp.sin(ang).astype(BF16)
    cc, sc = _channel_tables(BW)
    norm = float(1.0 / np.sqrt(t * FOURIER_GROUP_DIM))
    full = lambda shape: pl.BlockSpec(shape, lambda b: (0,) * len(shape))
    return pl.pallas_call(
        functools.partial(_fourier_small_kernel, norm=norm),
        grid=(batch,),
        in_specs=[pl.BlockSpec((t, BW), lambda b: (b, cb)), full((t, t)), full((t, t)),
                  full((BW, BW)), full((BW, BW))],
        out_specs=pl.BlockSpec((t, BW), lambda b: (b, 0)),
        out_shape=jax.ShapeDtypeStruct((n, BW), BF16),
        compiler_params=_cp(("arbitrary",)),
        name="fourier_ctx",
    )(u, ct, st, cc, sc)


def _na_bias_table(rpb):
    n_r, n_c = rpb.shape[1], rpb.shape[2]
    span = 2 * GRID_W
    left = GRID_W - NA_WIN_C
    vp = jnp.pad(rpb.astype(F32), ((0, 0), (0, 0), (left, span - n_c - left)))
    rep = jnp.broadcast_to(vp[:, :, None, :], (HEADS, n_r, GRID_W, span)).reshape(HEADS, n_r, GRID_W * span)
    skew = rep[..., :GRID_W * (span - 1)].reshape(HEADS, n_r, GRID_W, span - 1)
    toep = skew[..., GRID_W - 1:]
    qc = np.arange(GRID_W)[:, None]
    kc = np.arange(GRID_W)[None, :]
    start = np.clip(qc - NA_WIN_C // 2, 0, GRID_W - NA_WIN_C)
    valid = (kc >= start) & (kc < start + NA_WIN_C)
    toep = jnp.where(valid, toep, NEG)
    tab = jnp.stack([toep[:, NA_WIN_R - 1 - v:2 * NA_WIN_R - 1 - v] for v in range(NA_WIN_R)])
    tab = tab.transpose(0, 1, 3, 2, 4)
    return tab.reshape(NA_WIN_R, HEADS * GRID_W, NA_WIN_R * GRID_W)


def _na_kernel(q_ref, k_ref, v_ref, kc_ref, vc_ref, bias_ref, o_ref, *, rows_per_step, n_rows):
    i = pl.program_id(1)
    w = q_ref.shape[1]
    lane = lax.broadcasted_iota(jnp.int32, (1, w), 1)
    head = lane // (w // HEADS)
    scale = jnp.asarray(NA_HEAD_DIM ** -0.5, q_ref.dtype)
    kc = kc_ref[...]
    vc = vc_ref[...]
    win = NA_WIN_R * GRID_W

    def row(rl, carry):
        r = i * rows_per_step + rl
        rs = jnp.clip(r - NA_WIN_R // 2, 0, n_rows - NA_WIN_R)
        var = r - rs
        q0 = pl.multiple_of(rl * GRID_W, GRID_W)
        k0 = pl.multiple_of(rs * GRID_W, GRID_W)
        q = q_ref[pl.ds(q0, GRID_W), :] * scale
        kw = k_ref[pl.ds(k0, win), :]
        vw = v_ref[pl.ds(k0, win), :]
        zero_b = jnp.zeros_like(q)
        q4 = jnp.concatenate([jnp.where(head == h, q, zero_b) for h in range(HEADS)], axis=0)
        s_loc = _dot_nt(q4, kw) + bias_ref[var]
        s_ctx = _dot_nt(q4, kc)
        m = jnp.maximum(jnp.max(s_loc, axis=-1, keepdims=True), jnp.max(s_ctx, axis=-1, keepdims=True))
        p_loc = jnp.exp(s_loc - m)
        p_ctx = jnp.exp(s_ctx - m)
        l = jnp.sum(p_loc, axis=-1, keepdims=True) + jnp.sum(p_ctx, axis=-1, keepdims=True)
        pv = (_dot(p_loc.astype(BF16), vw) + _dot(p_ctx.astype(BF16), vc)) / l
        acc = jnp.zeros((GRID_W, w), F32)
        for h in range(HEADS):
            acc = acc + jnp.where(head == h, pv[h * GRID_W:(h + 1) * GRID_W], 0.0)
        o_ref[pl.ds(q0, GRID_W), :] = acc.astype(o_ref.dtype)
        return carry

    lax.fori_loop(0, rows_per_step, row, 0, unroll=2)


def na_attention(u, uc, cb0, bias_tab, batch, rows_per_step):
    n = u.shape[0]
    t = n // batch
    tc = uc.shape[0] // batch
    n_rows = t // GRID_W
    steps = n_rows // rows_per_step
    tq = rows_per_step * GRID_W
    return pl.pallas_call(
        functools.partial(_na_kernel, rows_per_step=rows_per_step, n_rows=n_rows),
        grid=(batch, steps),
        in_specs=[
            pl.BlockSpec((tq, BW), lambda b, i: (b * steps + i, cb0 + CB_NQ)),
            pl.BlockSpec((t, BW), lambda b, i: (b, cb0 + CB_NK)),
            pl.BlockSpec((t, BW), lambda b, i: (b, cb0 + CB_NV)),
            pl.BlockSpec((tc, BW), lambda b, i: (b, cb0 + CB_NK)),
            pl.BlockSpec((tc, BW), lambda b, i: (b, cb0 + CB_NV)),
            pl.BlockSpec(bias_tab.shape, lambda b, i: (0, 0, 0)),
        ],
        out_specs=pl.BlockSpec((tq, BW), lambda b, i: (b * steps + i, 0)),
        out_shape=jax.ShapeDtypeStruct((n, BW), BF16),
        compiler_params=_cp(("arbitrary", "arbitrary")),
        name="na_attention",
    )(u, u, u, uc, uc, bias_tab)


def _na_ctx_kernel(q_ref, k_ref, v_ref, o_ref):
    w = q_ref.shape[1]
    lane = lax.broadcasted_iota(jnp.int32, (1, w), 1)
    head = lane // (w // HEADS)
    scale = NA_HEAD_DIM ** -0.5
    q = q_ref[...]
    k = k_ref[...]
    v = v_ref[...]
    zero_b = jnp.zeros_like(q)
    acc = jnp.zeros(q.shape, F32)
    for h in range(HEADS):
        s = _dot_nt(jnp.where(head == h, q, zero_b), k) * scale
        p = jnp.exp(s - jnp.max(s, axis=-1, keepdims=True))
        l = jnp.sum(p, axis=-1, keepdims=True)
        acc = acc + jnp.where(head == h, _dot(p.astype(BF16), v) / l, 0.0)
    o_ref[...] = acc.astype(o_ref.dtype)


def na_ctx_attention(uc, cb0, batch):
    tc = uc.shape[0] // batch
    spec = lambda cb: pl.BlockSpec((tc, BW), lambda b: (b, cb0 + cb))
    return pl.pallas_call(
        _na_ctx_kernel,
        grid=(batch,),
        in_specs=[spec(CB_NQ), spec(CB_NK), spec(CB_NV)],
        out_specs=pl.BlockSpec((tc, BW), lambda b: (b, 0)),
        out_shape=jax.ShapeDtypeStruct((uc.shape[0], BW), BF16),
        compiler_params=_cp(("arbitrary",)),
        name="na_ctx_attention",
    )(uc, uc, uc)


def _mla_prep_kernel(cq_ref, ckv_ref, kr_ref, cos_ref, sin_ref, qn_ref, kvn_ref, wq_ref, wqr_ref,
                     wk_ref, wv_ref, p1_ref, p2_ref, one_ref, q_ref, k_ref, v_ref):
    cos = cos_ref[...]
    sin = sin_ref[...]
    cos4 = jnp.concatenate([cos] * HEADS, axis=-1)
    sin4 = jnp.concatenate([sin] * HEADS, axis=-1)

    cq = cq_ref[...].astype(F32)
    ms = jnp.sum(cq * cq, axis=-1, keepdims=True) * (1.0 / MLA_Q_RANK)
    cqn = ((cq * lax.rsqrt(ms + NORM_EPS)) * qn_ref[...]).astype(BF16)
    q = _dot(cqn, wq_ref[...]) * cos4 + _dot(cqn, wqr_ref[...]) * sin4
    q_ref[...] = (q * float((MLA_NOPE + MLA_ROPE) ** -0.5 * np.log2(np.e))).astype(q_ref.dtype)

    ckv = ckv_ref[...].astype(F32)
    ms = jnp.mean(ckv * ckv, axis=-1, keepdims=True)
    ckvn = ((ckv * lax.rsqrt(ms + NORM_EPS)) * kvn_ref[...]).astype(BF16)
    kr = kr_ref[...]
    k_rot = _dot(kr, p1_ref[...]) * cos + _dot(kr, p2_ref[...]) * sin
    k = _dot(ckvn, wk_ref[...]) + jnp.concatenate([k_rot] * HEADS, axis=-1)
    k_ref[...] = k.astype(k_ref.dtype)
    v_ref[...] = (_dot(ckvn, wv_ref[...]) + one_ref[...]).astype(v_ref.dtype)


def _mla_weights(w_uq, w_ukv):
    qr = w_uq.shape[0]
    dq = MLA_NOPE + MLA_ROPE
    hr = MLA_ROPE // 2
    wq3 = w_uq.reshape(qr, HEADS, dq)
    zq = jnp.zeros((qr, HEADS, LANES - dq), F32)
    wq = jnp.concatenate([wq3, zq], axis=-1)
    x1 = wq3[..., MLA_NOPE:MLA_NOPE + hr]
    x2 = wq3[..., MLA_NOPE + hr:]
    wqr = jnp.concatenate([jnp.zeros((qr, HEADS, MLA_NOPE), F32), -x2, x1, zq], axis=-1)
    pad_rows = lambda m: jnp.pad(m.reshape(qr, HEADS * LANES), ((0, BW - qr), (0, 0)))
    kvr = w_ukv.shape[0]
    wkv3 = w_ukv.reshape(kvr, HEADS, MLA_NOPE + MLA_V)
    zk = jnp.zeros((kvr, HEADS, LANES - MLA_NOPE), F32)
    wk = jnp.concatenate([wkv3[..., :MLA_NOPE], zk], axis=-1).reshape(kvr, HEADS * LANES)
    wv = jnp.concatenate([wkv3[..., MLA_NOPE:], zk], axis=-1).reshape(kvr, HEADS * LANES)
    j = np.arange(hr)
    p1 = np.zeros((LANES, LANES), np.float32)
    p1[np.arange(MLA_ROPE), MLA_NOPE + np.arange(MLA_ROPE)] = 1.0
    p2 = np.zeros((LANES, LANES), np.float32)
    p2[hr + j, MLA_NOPE + j] = -1.0
    p2[j, MLA_NOPE + hr + j] = 1.0
    one = np.zeros((1, HEADS * LANES), np.float32)
    one[0, MLA_V + LANES * np.arange(HEADS)] = 1.0
    return (pad_rows(wq).astype(BF16), pad_rows(wqr).astype(BF16), wk.astype(BF16), wv.astype(BF16),
            jnp.asarray(p1, BF16), jnp.asarray(p2, BF16), jnp.asarray(one))


def mla_prep(u, cb0, cos, sin, q_norm, kv_norm, weights, tm, rope_blocks):
    n = u.shape[0]
    wq, wqr, wk, wv, p1, p2, one = weights
    qn = jnp.pad(q_norm, (0, BW - q_norm.shape[0])).reshape(1, BW)
    full = lambda a: pl.BlockSpec(a.shape, lambda i: (0,) * a.ndim)
    tab = pl.BlockSpec((tm, LANES), lambda i: (i % rope_blocks, 0))
    kv_cb = (cb0 + CB_MKV) * (BW // LANES)
    out = jax.ShapeDtypeStruct((n, HEADS * LANES), BF16)
    ospec = pl.BlockSpec((tm, HEADS * LANES), lambda i: (i, 0))
    return pl.pallas_call(
        _mla_prep_kernel,
        grid=(n // tm,),
        in_specs=[
            pl.BlockSpec((tm, BW), lambda i: (i, cb0 + CB_MCQ)),
            pl.BlockSpec((tm, LANES), lambda i: (i, kv_cb)),
            pl.BlockSpec((tm, LANES), lambda i: (i, kv_cb + 1)),
            tab, tab, full(qn), pl.BlockSpec((1, LANES), lambda i: (0, 0)),
            full(wq), full(wqr), full(wk), full(wv), full(p1), full(p2), full(one),
        ],
        out_specs=[ospec, ospec, ospec],
        out_shape=[out, out, out],
        compiler_params=_cp(("arbitrary",)),
        name="mla_prep",
    )(u, u, u, cos, sin, qn, kv_norm.reshape(1, LANES), wq, wqr, wk, wv, p1, p2, one)


def _flash_kernel(*refs, lens, tk):
    q_ref = refs[0]
    kv_refs = refs[1:1 + 2 * len(lens)]
    o_ref = refs[1 + 2 * len(lens)]
    q = q_ref[...]
    tq = q.shape[0]
    m = jnp.full((tq, 1), NEG, F32)
    acc = jnp.zeros((tq, LANES), F32)

    def chunk(kc, vc, m, acc):
        s = _dot_nt(q, kc)
        m_new = jnp.maximum(m, jnp.max(s, axis=-1, keepdims=True))
        p = jnp.exp2((s - m_new).astype(BF16))
        acc = jnp.exp2(m - m_new) * acc + _dot(p, vc)
        return m_new, acc

    for si, length in enumerate(lens):
        k_ref, v_ref = kv_refs[2 * si], kv_refs[2 * si + 1]
        step = min(tk, length)
        if length == step:
            m, acc = chunk(k_ref[...], v_ref[...], m, acc)
        else:
            def body(j, carry, k_ref=k_ref, v_ref=v_ref, step=step):
                j0 = pl.multiple_of(j * step, step)
                return chunk(k_ref[pl.ds(j0, step), :], v_ref[pl.ds(j0, step), :], *carry)

            m, acc = lax.fori_loop(0, length // step, body, (m, acc), unroll=4)

    lane = lax.broadcasted_iota(jnp.int32, (1, LANES), 1)
    l = jnp.sum(jnp.where(lane == MLA_V, acc, 0.0), axis=-1, keepdims=True)
    o_ref[...] = jnp.where(lane < MLA_V, acc / l, 0.0).astype(o_ref.dtype)


def flash_attention(q, kvs, batch, tq, tk):
    n = q.shape[0]
    nq = n // batch // tq
    lens = tuple(k.shape[0] // batch for k, _ in kvs)
    in_specs = [pl.BlockSpec((tq, LANES), lambda b, h, i: (b * nq + i, h))]
    args = [q]
    for (k, v), length in zip(kvs, lens):
        in_specs += [pl.BlockSpec((length, LANES), lambda b, h, i: (b, h))] * 2
        args += [k, v]
    return pl.pallas_call(
        functools.partial(_flash_kernel, lens=lens, tk=tk),
        grid=(batch, HEADS, nq),
        in_specs=in_specs,
        out_specs=pl.BlockSpec((tq, LANES), lambda b, h, i: (b * nq + i, h)),
        out_shape=jax.ShapeDtypeStruct((n, HEADS * LANES), BF16),
        compiler_params=_cp(("arbitrary", "arbitrary", "arbitrary")),
        name="mla_attention",
    )(*args)


def _merge_kernel(*refs, with_router):
    (gate_ref, yr_f_ref, yr_b_ref, yf_ref, yn_ref, ym_ref, x_ref, g1_ref, wb_ret_ref, wb_f_ref,
     wb_na_ref, wb_mla_ref, wo_ref, gain_ref, sc_ref, sh_ref) = refs[:16]
    rest = refs[16:]
    if with_router:
        rh_ref, rl_ref, x_out_ref, lg_out_ref, hp_out_ref = rest
    else:
        x_out_ref, h_out_ref = rest
    d = x_ref.shape[1]

    def gated(k, y):
        g = jax.nn.sigmoid(gate_ref[:, k * d:(k + 1) * d].astype(F32))
        return g * y

    m = gated(0, _dot(yr_f_ref[...] + yr_b_ref[...], wb_ret_ref[...]))
    m = m + gated(1, _dot(yf_ref[...], wb_f_ref[...]))
    m = m + gated(2, _dot(yn_ref[...], wb_na_ref[...]))
    m = m + gated(3, _dot(ym_ref[...], wb_mla_ref[...]))
    y = _dot(m.astype(BF16), wo_ref[...])
    x = x_ref[...] + g1_ref[0] * y
    x_out_ref[...] = x
    hn = x * lax.rsqrt(jnp.mean(x * x, axis=-1, keepdims=True) + NORM_EPS)
    h = (hn * gain_ref[...]) * (1.0 + sc_ref[0]) + sh_ref[0]
    if with_router:
        hp_out_ref[...] = _pack_bf16_pairs(h)
        h_hi = h.astype(BF16)
        h_lo = (h - h_hi.astype(F32)).astype(BF16)
        lg_out_ref[...] = (_dot(h_hi, rh_ref[...]) + _dot(h_lo, rh_ref[...])) + _dot(h_hi, rl_ref[...])
    else:
        h_out_ref[...] = h.astype(h_out_ref.dtype)


def merge(u, y_ret, y_four, y_na, y_mla, x, mods, mod_row, gain, wb, w_out, router, tm):
    n, d = x.shape
    wb_ret, wb_f, wb_na, wb_mla = wb
    full = lambda a: pl.BlockSpec(a.shape, lambda i: (0,) * a.ndim)
    br = lambda: pl.BlockSpec((tm, BW), lambda i: (i, 0))
    tok = lambda: pl.BlockSpec((tm, d), lambda i: (i, 0))
    in_specs = [
        pl.BlockSpec((tm, 4 * d), lambda i: (i, 0)),
        br(), br(), br(), br(),
        pl.BlockSpec((tm, HEADS * LANES), lambda i: (i, 0)),
        tok(), _mod_spec(d, mod_row, tm, 2),
        full(wb_ret), full(wb_f), full(wb_na), full(wb_mla), full(w_out),
        pl.BlockSpec((1, d), lambda i: (0, 0)), _mod_spec(d, mod_row, tm, 4), _mod_spec(d, mod_row, tm, 3),
    ]
    args = [u, y_ret[0], y_ret[1], y_four, y_na, y_mla, x, mods, wb_ret, wb_f, wb_na, wb_mla, w_out,
            gain.reshape(1, d), mods, mods]
    out_specs = [tok()]
    out_shape = [jax.ShapeDtypeStruct((n, d), F32)]
    if router is None:
        out_specs.append(tok())
        out_shape.append(jax.ShapeDtypeStruct((n, d), BF16))
    else:
        in_specs += [full(router[0]), full(router[1])]
        args += list(router)
        out_specs += [pl.BlockSpec((tm, LANES), lambda i: (i, 0)), pl.BlockSpec((tm, d // 2), lambda i: (i, 0))]
        out_shape += [jax.ShapeDtypeStruct((n, LANES), F32), jax.ShapeDtypeStruct((n, d // 2), jnp.int32)]
    return pl.pallas_call(
        functools.partial(_merge_kernel, with_router=router is not None),
        grid=(n // tm,),
        in_specs=in_specs,
        out_specs=out_specs,
        out_shape=out_shape,
        compiler_params=_cp(("arbitrary",)),
        name="merge",
    )(*args)


def _ffn_kernel(h_ref, wg_ref, wu_ref, wd_ref, x_ref, g2_ref, o_ref, acc_ref):
    f = pl.program_id(1)

    @pl.when(f == 0)
    def _():
        acc_ref[...] = jnp.zeros_like(acc_ref)

    h = h_ref[...]
    a = _silu(_dot(h, wg_ref[...])) * _dot(h, wu_ref[...])
    acc_ref[...] += _dot(a.astype(BF16), wd_ref[...])

    @pl.when(f == pl.num_programs(1) - 1)
    def _():
        o_ref[...] = x_ref[...] + g2_ref[0] * acc_ref[...]


def ffn(h, x, mods, mod_row, wg, wu, wd, tm, tf):
    n, d = x.shape
    nf = wg.shape[1] // tf
    return pl.pallas_call(
        _ffn_kernel,
        grid=(n // tm, nf),
        in_specs=[
            pl.BlockSpec((tm, d), lambda i, f: (i, 0)),
            pl.BlockSpec((d, tf), lambda i, f: (0, f)),
            pl.BlockSpec((d, tf), lambda i, f: (0, f)),
            pl.BlockSpec((tf, d), lambda i, f: (f, 0)),
            pl.BlockSpec((tm, d), lambda i, f: (i, 0)),
            _mod_spec(d, mod_row, tm, 5),
        ],
        out_specs=pl.BlockSpec((tm, d), lambda i, f: (i, 0)),
        out_shape=jax.ShapeDtypeStruct((n, d), F32),
        scratch_shapes=[pltpu.VMEM((tm, d), F32)],
        compiler_params=_cp(("arbitrary", "arbitrary"), VMEM_LIMIT_LARGE),
        name="ffn",
    )(h, wg, wu, wd, x, mods)


def _moe_kernel(be_ref, nu_ref, xp_ref, wg_ref, wu_ref, wd_ref, o_ref, acc_ref, x_ref):
    i = pl.program_id(0)
    f = pl.program_id(1)
    used = i < nu_ref[0]

    @pl.when(f == 0)
    def _():
        acc_ref[...] = jnp.zeros_like(acc_ref)
        x_ref[...] = _unpack_bf16_pairs(xp_ref[...]).astype(BF16)

    @pl.when(used)
    def _():
        x = x_ref[...]
        a = _silu(_dot(x, wg_ref[0])) * _dot(x, wu_ref[0])
        acc_ref[...] += _dot(a.astype(BF16), wd_ref[0])

    @pl.when(f == pl.num_programs(1) - 1)
    def _():
        o_ref[...] = _pack_bf16_pairs(acc_ref[...])


SC_CAST_BLOCK = (16, 512)


def sc_cast_bf16(w):
    e, a, b = w.shape
    br, bc = SC_CAST_BLOCK
    assert (e * a) % br == 0 and b % bc == 0
    mesh = plsc.VectorSubcoreMesh(core_axis_name="c", subcore_axis_name="s")

    @functools.partial(pl.kernel, mesh=mesh, out_type=jax.ShapeDtypeStruct((e * a, b), BF16), scratch_types=[])
    def cast(x_hbm, o_hbm):
        def body(in_v, out_v):
            @pl.loop(0, br, step=2)
            def _(r):
                @pl.loop(0, bc, step=16)
                def _(c):
                    top = in_v.at[pl.ds(r, 1), pl.ds(c, 16)][...]
                    bot = in_v.at[pl.ds(r + 1, 1), pl.ds(c, 16)][...]
                    out_v.at[pl.ds(r, 2), pl.ds(c, 16)][...] = jnp.concatenate([top, bot], axis=0).astype(BF16)

        pltpu.emit_pipeline(
            body,
            grid=(e * a // br, b // bc),
            in_specs=[pl.BlockSpec((br, bc), lambda i, j: (i, j))],
            out_specs=[pl.BlockSpec((br, bc), lambda i, j: (i, j))],
            core_axis_name=("c", "s"),
            dimension_semantics=(pltpu.PARALLEL, pltpu.PARALLEL),
        )(x_hbm, o_hbm)

    return cast(w.reshape(e * a, b)).reshape(e, a, b)


def moe_ffn(blk_exp, n_used, xb, wg, wu, wd, tm, tf):
    n, dp = xb.shape
    d = 2 * dp
    nf = wg.shape[2] // tf
    grid_spec = pltpu.PrefetchScalarGridSpec(
        num_scalar_prefetch=2,
        grid=(n // tm, nf),
        in_specs=[
            pl.BlockSpec((tm, dp), lambda i, f, be, nu: (i, 0)),
            pl.BlockSpec((1, d, tf), lambda i, f, be, nu: (be[i], 0, f)),
            pl.BlockSpec((1, d, tf), lambda i, f, be, nu: (be[i], 0, f)),
            pl.BlockSpec((1, tf, d), lambda i, f, be, nu: (be[i], f, 0)),
        ],
        out_specs=pl.BlockSpec((tm, dp), lambda i, f, be, nu: (i, 0)),
        scratch_shapes=[pltpu.VMEM((tm, d), F32), pltpu.VMEM((tm, d), BF16)],
    )
    return pl.pallas_call(
        _moe_kernel,
        grid_spec=grid_spec,
        out_shape=jax.ShapeDtypeStruct((n, dp), jnp.int32),
        compiler_params=_cp(("arbitrary", "arbitrary")),
        name="moe_ffn",
    )(blk_exp, n_used, xb, wg, wu, wd)


def _combine_kernel(x_ref, ya_ref, yb_ref, w_ref, g2_ref, gain_ref, o_ref, *, final):
    w = w_ref[...]
    y = w[:, 0:1] * _unpack_bf16_pairs(ya_ref[...]) + w[:, 1:2] * _unpack_bf16_pairs(yb_ref[...])
    x = x_ref[...] + g2_ref[0] * y
    if final:
        x = (x * lax.rsqrt(jnp.mean(x * x, axis=-1, keepdims=True) + NORM_EPS)) * gain_ref[...]
    o_ref[...] = x


def moe_combine(x, y2, w, mods, mod_row, gain, final, tm):
    n, d = x.shape
    tok = lambda: pl.BlockSpec((tm, d), lambda i: (i, 0))
    half = lambda off: pl.BlockSpec((tm, d // 2), lambda i: (i + off, 0))
    return pl.pallas_call(
        functools.partial(_combine_kernel, final=final),
        grid=(n // tm,),
        in_specs=[tok(), half(0), half(n // tm),
                  pl.BlockSpec((tm, MOE_TOP_K), lambda i: (i, 0)),
                  _mod_spec(d, mod_row, tm, 5), pl.BlockSpec((1, d), lambda i: (0, 0))],
        out_specs=tok(),
        out_shape=jax.ShapeDtypeStruct((n, d), F32),
        compiler_params=_cp(("arbitrary",)),
        name="moe_combine",
    )(x, y2, y2, w, mods, gain.reshape(1, d))


def _rmsnorm_kernel(x_ref, gain_ref, o_ref):
    x = x_ref[...]
    o_ref[...] = (x * lax.rsqrt(jnp.mean(x * x, axis=-1, keepdims=True) + NORM_EPS)) * gain_ref[...]


def rmsnorm_rows(x, gain, tm):
    n, d = x.shape
    return pl.pallas_call(
        _rmsnorm_kernel,
        grid=(n // tm,),
        in_specs=[pl.BlockSpec((tm, d), lambda i: (i, 0)), pl.BlockSpec((1, d), lambda i: (0, 0))],
        out_specs=pl.BlockSpec((tm, d), lambda i: (i, 0)),
        out_shape=jax.ShapeDtypeStruct((n, d), F32),
        compiler_params=_cp(("arbitrary",)),
        name="final_norm",
    )(x, gain.reshape(1, d))


SC_CORES = 2
SC_SUBCORES = 16
SC_CHUNK = 64


def sc_gather_rows(table, idx):
    n_out = idx.shape[0]
    width = table.shape[1]
    workers = SC_CORES * SC_SUBCORES
    per_worker = n_out // workers
    assert n_out == per_worker * workers and per_worker % (2 * SC_CHUNK) == 0 and table.dtype == jnp.int32
    mesh = plsc.VectorSubcoreMesh(core_axis_name="c", subcore_axis_name="s")
    idx_buf = pltpu.VMEM((SC_CHUNK,), jnp.int32)
    row_buf = pltpu.VMEM((SC_CHUNK, width), table.dtype)

    @functools.partial(
        pl.kernel, mesh=mesh,
        out_type=jax.ShapeDtypeStruct((n_out, width), table.dtype),
        scratch_types=[idx_buf, idx_buf, row_buf, row_buf, pltpu.SemaphoreType.DMA, pltpu.SemaphoreType.DMA],
    )
    def gather(table_hbm, idx_hbm, out_hbm, idx_a, idx_b, rows_a, rows_b, sem_a, sem_b):
        base = (lax.axis_index("s") * SC_CORES + lax.axis_index("c")) * per_worker

        @pl.loop(0, per_worker // (2 * SC_CHUNK))
        def _(j):
            off_a = pl.multiple_of(base + j * (2 * SC_CHUNK), SC_CHUNK)
            off_b = pl.multiple_of(off_a + SC_CHUNK, SC_CHUNK)
            pltpu.sync_copy(idx_hbm.at[pl.ds(off_a, SC_CHUNK)], idx_a)
            gather_a = pltpu.async_copy(table_hbm.at[idx_a], rows_a, sem_a)
            pltpu.sync_copy(idx_hbm.at[pl.ds(off_b, SC_CHUNK)], idx_b)
            gather_b = pltpu.async_copy(table_hbm.at[idx_b], rows_b, sem_b)
            gather_a.wait()
            write_a = pltpu.async_copy(rows_a, out_hbm.at[pl.ds(off_a, SC_CHUNK)], sem_a)
            gather_b.wait()
            write_b = pltpu.async_copy(rows_b, out_hbm.at[pl.ds(off_b, SC_CHUNK)], sem_b)
            write_a.wait()
            write_b.wait()

    return gather(table, idx)


def moe_route(logits, n_experts, tm):
    n_tok = logits.shape[0]
    top_logit, top_idx = lax.top_k(logits[:, :n_experts], MOE_TOP_K)
    top_w = jax.nn.softmax(top_logit, axis=-1)
    e_flat = top_idx.reshape(-1).astype(jnp.int32)
    n_assign = e_flat.shape[0]
    onehot = (e_flat[:, None] == jnp.arange(n_experts, dtype=jnp.int32)[None, :]).astype(jnp.int32)
    rank = jnp.sum((jnp.cumsum(onehot, axis=0) - onehot) * onehot, axis=1)
    counts = jnp.sum(onehot, axis=0)
    padded = (counts + tm - 1) // tm * tm
    pad_end = jnp.cumsum(padded)
    pad_start = pad_end - padded
    dest = pad_start[e_flat] + rank
    n_rows = n_assign + n_experts * tm
    tok = jnp.arange(n_assign, dtype=jnp.int32) // MOE_TOP_K
    blk_start = jnp.arange(n_rows // tm, dtype=jnp.int32) * tm
    blk_exp = jnp.minimum(jnp.sum(pad_end[None, :] <= blk_start[:, None], axis=1), n_experts - 1)
    n_used = (pad_end[-1] // tm).reshape(1)
    _, tok_by_row = lax.sort_key_val(dest, tok)
    max_shift = n_experts * tm
    filler = jnp.arange(max_shift, dtype=jnp.int32)
    compact = jnp.concatenate([filler, tok_by_row, filler])
    shift = pad_start - (jnp.cumsum(counts) - counts)
    row_exp = jnp.repeat(blk_exp, tm)
    row_tok = jnp.zeros((n_rows,), jnp.int32)
    for e in range(n_experts):
        shifted = lax.dynamic_slice(compact, (max_shift - shift[e],), (n_rows,))
        row_tok = jnp.where(row_exp == e, shifted, row_tok)
    return row_tok, dest.reshape(n_tok, MOE_TOP_K), top_w, blk_exp.astype(jnp.int32), n_used.astype(jnp.int32)


def _rope_split(wcols):
    d, w = wcols.shape
    half = w // HEADS // 2
    return wcols.reshape(d, HEADS, 2, half).transpose(0, 2, 1, 3).reshape(d, w)


def _inproj_weights(w_in):
    d = w_in.shape[0]
    kv = (BW, BW, BW, BW, MLA_KV_RANK, MLA_ROPE)
    qs = (BW, BW, BW, BW, BW, MLA_Q_RANK, 4 * d)
    offs = np.concatenate([[0], np.cumsum(kv + qs)])
    seg = lambda i: w_in[:, offs[i]:offs[i + 1]]
    r_k, r_v, n_k, n_v, m_ckv, m_kr = (seg(i) for i in range(6))
    r_q, r_gf, r_gb, f_in, n_q, m_cq, gate = (seg(6 + i) for i in range(7))
    z = lambda n: jnp.zeros((d, n), w_in.dtype)
    cols = [gate, _rope_split(r_q), _rope_split(r_k), r_v, r_gf, r_gb, f_in, n_q, n_k, n_v,
            m_cq, z(BW - MLA_Q_RANK), m_ckv, m_kr, z(LANES - MLA_ROPE)]
    return jnp.concatenate(cols, axis=1).astype(BF16)


def _ret_rope_tables(n):
    t = jnp.arange(n)
    row = (t // GRID_W).astype(F32)
    col = (t % GRID_W).astype(F32)
    nf = RET_DK // 4
    inv = ROPE_BASE ** (-jnp.arange(nf, dtype=F32) / nf)
    ang = jnp.concatenate([row[:, None] * inv, col[:, None] * inv], axis=-1)
    return jnp.tile(jnp.cos(ang), (1, HEADS)), jnp.tile(jnp.sin(ang), (1, HEADS))


def _mla_rope_tables(n):
    t = jnp.arange(n)
    row = (t // GRID_W).astype(F32)
    col = (t % GRID_W).astype(F32)
    nf = MLA_ROPE // 4
    inv = ROPE_BASE ** (-jnp.arange(nf, dtype=F32) / nf)
    ang = jnp.concatenate([row[:, None] * inv, col[:, None] * inv], axis=-1)
    pad = jnp.zeros((n, LANES - MLA_NOPE - MLA_ROPE), F32)
    cos = jnp.concatenate([jnp.ones((n, MLA_NOPE), F32), jnp.cos(ang), jnp.cos(ang), pad], axis=-1)
    sin = jnp.concatenate([jnp.zeros((n, MLA_NOPE), F32), jnp.sin(ang), jnp.sin(ang), pad], axis=-1)
    return cos, sin


def _tile_rows(*sizes):
    for tm in (1024, 512, 256, 128):
        if all(s % tm == 0 for s in sizes):
            return tm
    raise ValueError(f"token counts {sizes} need a common multiple-of-128 row tile")


def kernel(x, c, ctx, c_ctx, ada_w, ada_b, norm_mix, norm_ffn, w_in, ret_decay_fwd, ret_decay_bwd,
           mla_q_norm, mla_kv_norm, mla_w_uq, mla_w_ukv, na_rpb, w_branch, w_out,
           ffn_w_gate, ffn_w_up, ffn_w_down, moe_router, moe_w_gate, moe_w_up, moe_w_down, norm_final):
    batch, t, d = x.shape
    tc = ctx.shape[1]
    depth = ada_w.shape[0]
    nl, ncx = batch * t, batch * tc
    assert batch < 8 and t % (16 * GRID_W) == 0 and tc % LANES == 0 and d == 4 * BW
    tm = _tile_rows(t, ncx)
    cb0 = 4 * d // BW

    xl = x.reshape(nl, d)
    xc = ctx.reshape(ncx, d)
    cc = jnp.zeros((8, d), F32).at[:batch].set(c).at[batch].set(c_ctx)
    mods = adaln(cc, ada_w, ada_b).reshape(depth, 8 * 6, 1, d)
    lat_row = lambda r0: r0 // t
    ctx_row = lambda r0: batch

    ret_cos, ret_sin = _ret_rope_tables(t)
    ret_cos_c, ret_sin_c = jnp.ones((tc, LANES), F32), jnp.zeros((tc, LANES), F32)
    mla_cos, mla_sin = _mla_rope_tables(t)
    mla_cos_c = jnp.concatenate([jnp.ones((tm, MLA_NOPE + MLA_ROPE), F32),
                                 jnp.zeros((tm, LANES - MLA_NOPE - MLA_ROPE), F32)], axis=-1)
    mla_sin_c = jnp.zeros((tm, LANES), F32)

    for i in range(depth):
        ctx_out = i < depth - 1
        md = mods[i]
        w_p = _inproj_weights(w_in[i])
        u = norm_inproj(xl, norm_mix[i], md, lat_row, w_p, tm, INPROJ_COLS)
        uc = norm_inproj(xc, norm_mix[i], md, ctx_row, w_p, tm, INPROJ_COLS)

        lg = jnp.stack([jax.nn.log_sigmoid(ret_decay_fwd[i].astype(F32)),
                        jax.nn.log_sigmoid(ret_decay_bwd[i].astype(F32))])
        zero_state = jnp.zeros((batch, 2, BW, BW), F32)
        yc_ret, s_ctx = retention(uc, cb0, lg, ret_cos_c, ret_sin_c, zero_state, batch, min(RET_CHUNK, tc))
        y_ret, _ = retention(u, cb0, lg, ret_cos, ret_sin, s_ctx, batch, RET_CHUNK)

        y_four = fourier_long(u, cb0 + CB_F, batch)

        bias_tab = _na_bias_table(na_rpb[i])
        y_na = na_attention(u, uc, cb0, bias_tab, batch, NA_ROWS_PER_STEP)

        mw = _mla_weights(mla_w_uq[i], mla_w_ukv[i])
        q_l, k_l, v_l = mla_prep(u, cb0, mla_cos, mla_sin, mla_q_norm[i], mla_kv_norm[i], mw, tm, t // tm)
        q_c, k_c, v_c = mla_prep(uc, cb0, mla_cos_c, mla_sin_c, mla_q_norm[i], mla_kv_norm[i], mw, tm, 1)
        y_mla = flash_attention(q_l, [(k_l, v_l), (k_c, v_c)], batch, min(FLASH_Q, t), FLASH_K)

        wb = w_branch[i].astype(BF16)
        wb_mla = jnp.concatenate(
            [wb[3].reshape(HEADS, MLA_V, d), jnp.zeros((HEADS, LANES - MLA_V, d), BF16)], axis=1
        ).reshape(HEADS * LANES, d)
        wbs = (wb[0], wb[1], wb[2], wb_mla)
        wo = w_out[i].astype(BF16)
        j = i // 2
        if i % 2 == 0:
            xl, h2 = merge(u, y_ret, y_four, y_na, y_mla, xl, md, lat_row, norm_ffn[i], wbs, wo, None, MERGE_ROWS)
            wg, wu, wd = ffn_w_gate[j].astype(BF16), ffn_w_up[j].astype(BF16), ffn_w_down[j].astype(BF16)
            tf = wg.shape[1] // 2
            xl = ffn(h2, xl, md, lat_row, wg, wu, wd, tm, tf)
        else:
            n_exp = moe_router.shape[2]
            r = jnp.pad(moe_router[j], ((0, 0), (0, LANES - n_exp)))
            r_hi = r.astype(BF16)
            r_lo = (r - r_hi.astype(F32)).astype(BF16)
            xl, logits, h2p = merge(u, y_ret, y_four, y_na, y_mla, xl, md, lat_row, norm_ffn[i], wbs, wo,
                                   (r_hi, r_lo), MERGE_ROWS)
            row_tok, dest, top_w, blk_exp, n_used = moe_route(logits, n_exp, MOE_ROWS)
            xb = sc_gather_rows(h2p, row_tok)
            ewg, ewu, ewd = (sc_cast_bf16(w) for w in (moe_w_gate[j], moe_w_up[j], moe_w_down[j]))
            yb = moe_ffn(blk_exp, n_used, xb, ewg, ewu, ewd, MOE_ROWS, MOE_FFN_COLS)
            y2 = sc_gather_rows(yb, dest.T.reshape(-1))
            xl = moe_combine(xl, y2, top_w, md, lat_row, norm_final, i == depth - 1, MERGE_ROWS)

        if ctx_out:
            yc_four = fourier_short(uc, cb0 + CB_F, batch)
            yc_na = na_ctx_attention(uc, cb0, batch)
            yc_mla = flash_attention(q_c, [(k_c, v_c)], batch, tc, FLASH_K)
            if i % 2 == 0:
                xc, hc2 = merge(uc, yc_ret, yc_four, yc_na, yc_mla, xc, md, ctx_row, norm_ffn[i], wbs, wo,
                                None, MERGE_ROWS)
                xc = ffn(hc2, xc, md, ctx_row, wg, wu, wd, tm, tf)
            else:
                raise NotImplementedError("context tokens through the expert mixer")

    if depth % 2 == 1:
        xl = rmsnorm_rows(xl, norm_final, tm)
    return xl.reshape(batch, t, d)
```

```python
import functools

import numpy as np
import jax
import jax.numpy as jnp
from jax import lax
from jax.experimental import pallas as pl
from jax.experimental.pallas import tpu as pltpu
from jax.experimental.pallas import tpu_sc as plsc

F32 = jnp.float32
BF16 = jnp.bfloat16

GRID_W = 64
ROPE_BASE = 10000.0
NORM_EPS = 1e-6
HEADS = 4
RET_DK = 64
FOURIER_GROUP_DIM = 64
NA_HEAD_DIM = 64
NA_WIN_R = 8
NA_WIN_C = 16
MLA_NOPE = 64
MLA_ROPE = 32
MLA_V = 64
MLA_Q_RANK = 192
MLA_KV_RANK = 128
MOE_TOP_K = 2
BW = 256

COL_GATE = 0
CB_RQ, CB_RK, CB_RV, CB_RGF, CB_RGB, CB_F, CB_NQ, CB_NK, CB_NV, CB_MCQ, CB_MKV = range(11)
LANES = 128
NEG = -1e30

VMEM_LIMIT = 48 * 1024 * 1024
VMEM_LIMIT_LARGE = 58 * 1024 * 1024

INPROJ_COLS = 2304
RET_CHUNK = 256
NA_ROWS_PER_STEP = 16
FLASH_Q, FLASH_K = 4096, 512
MERGE_ROWS = 512
MOE_ROWS, MOE_FFN_COLS = 512, 1792


def _cp(sem, vmem=VMEM_LIMIT):
    return pltpu.CompilerParams(dimension_semantics=sem, vmem_limit_bytes=vmem)


def _dot(a, b):
    return jnp.dot(a, b, preferred_element_type=F32)


def _dot_nt(a, b):
    return lax.dot_general(a, b, (((1,), (1,)), ((), ())), preferred_element_type=F32)


def _dot_tn(a, b):
    return lax.dot_general(a, b, (((0,), (0,)), ((), ())), preferred_element_type=F32)


def _silu(x):
    return x * jax.nn.sigmoid(x)


def _pack_bf16_pairs(x):
    k = x.shape[1] // 2
    lo = lax.bitcast_convert_type(x[:, :k].astype(BF16).astype(F32), jnp.uint32) >> 16
    hi = lax.bitcast_convert_type(x[:, k:].astype(BF16).astype(F32), jnp.uint32) & jnp.uint32(0xFFFF0000)
    return lax.bitcast_convert_type(lo | hi, jnp.int32)


def _unpack_bf16_pairs(p):
    u = lax.bitcast_convert_type(p, jnp.uint32)
    lo = lax.bitcast_convert_type(u << 16, F32)
    hi = lax.bitcast_convert_type(u & jnp.uint32(0xFFFF0000), F32)
    return jnp.concatenate([lo, hi], axis=-1)


def _adaln_kernel(c_ref, w_ref, b_ref, o_ref):
    s = _silu(c_ref[...])
    o_ref[0] = _dot(s.astype(BF16), w_ref[0].astype(BF16)) + b_ref[0]


def adaln(cc, ada_w, ada_b):
    depth, d, n6 = ada_w.shape
    tn = n6 // 4
    return pl.pallas_call(
        _adaln_kernel,
        grid=(depth, n6 // tn),
        in_specs=[
            pl.BlockSpec((8, d), lambda l, j: (0, 0)),
            pl.BlockSpec((1, d, tn), lambda l, j: (l, 0, j)),
            pl.BlockSpec((1, 1, tn), lambda l, j: (l, 0, j)),
        ],
        out_specs=pl.BlockSpec((1, 8, tn), lambda l, j: (l, 0, j)),
        out_shape=jax.ShapeDtypeStruct((depth, 8, n6), F32),
        compiler_params=_cp(("arbitrary", "arbitrary")),
        name="adaln",
    )(cc, ada_w, ada_b.reshape(depth, 1, n6))


def _inproj_kernel(x_ref, g_ref, sc_ref, sh_ref, w_ref, o_ref, h_ref):
    @pl.when(pl.program_id(1) == 0)
    def _():
        x = x_ref[...]
        y = x * lax.rsqrt(jnp.mean(x * x, axis=-1, keepdims=True) + NORM_EPS)
        h = (y * g_ref[...]) * (1.0 + sc_ref[0]) + sh_ref[0]
        h_ref[...] = h.astype(BF16)

    o_ref[...] = _dot(h_ref[...], w_ref[...]).astype(o_ref.dtype)


def _mod_spec(d, mod_row, tm, k):
    return pl.BlockSpec((1, 1, d), lambda i, *_: (mod_row(i * tm) * 6 + k, 0, 0))


def norm_inproj(x, gain, mods, mod_row, w, tm, tn):
    n, d = x.shape
    nc = w.shape[1]
    return pl.pallas_call(
        _inproj_kernel,
        grid=(n // tm, nc // tn),
        in_specs=[
            pl.BlockSpec((tm, d), lambda i, j: (i, 0)),
            pl.BlockSpec((1, d), lambda i, j: (0, 0)),
            _mod_spec(d, mod_row, tm, 1),
            _mod_spec(d, mod_row, tm, 0),
            pl.BlockSpec((d, tn), lambda i, j: (0, j)),
        ],
        out_specs=pl.BlockSpec((tm, tn), lambda i, j: (i, j)),
        out_shape=jax.ShapeDtypeStruct((n, nc), BF16),
        scratch_shapes=[pltpu.VMEM((tm, d), BF16)],
        compiler_params=_cp(("arbitrary", "arbitrary")),
        name="norm_inproj",
    )(x, gain.reshape(1, d), mods, mods, w)


def _ret_init(d, cs, w, lg_ref, s0_ref, lgq_ref, s_ref, decay_ref, qw_ref, kw_ref, avg_ref):
    rev = d == 1
    s_ref[d] = s0_ref[0, d]
    pos_i = lax.broadcasted_iota(jnp.int32, (cs, 1), 0).astype(F32)
    pos_j = lax.broadcasted_iota(jnp.int32, (1, cs), 1).astype(F32)
    p_i = jnp.where(rev, cs - 1.0 - pos_i, pos_i)
    p_j = jnp.where(rev, cs - 1.0 - pos_j, pos_j)
    diff = p_i - p_j
    for h in range(HEADS):
        decay_ref[d, h] = jnp.where(diff >= 0, jnp.exp(lg_ref[d, h] * jnp.maximum(diff, 0.0)), 0.0)
    lgq = lgq_ref[d]
    qw_ref[d] = jnp.exp(lgq * (p_i + 1.0))
    kw_ref[d] = jnp.exp(lgq * (cs - 1.0 - p_i))
    hd = w // HEADS
    gi = lax.broadcasted_iota(jnp.int32, (w, w), 0) // hd
    gj = lax.broadcasted_iota(jnp.int32, (w, w), 1) // hd
    avg_ref[...] = jnp.where(gi == gj, 1.0 / hd, 0.0).astype(BF16)


def _ret_chunk(d, q_ref, k_ref, v_ref, g_ref, cos_ref, sin_ref, lgv_ref, y_ref, s_ref, decay_ref, qw_ref, kw_ref,
               avg_ref):
    cs = q_ref.shape[0]
    w = q_ref.shape[1]
    half = w // 2
    lgv = lgv_ref[d]
    s_ref, decay_ref, qw_ref, kw_ref = s_ref.at[d], decay_ref.at[d], qw_ref.at[d], kw_ref.at[d]

    cos = cos_ref[...]
    sin = sin_ref[...]

    def rope(t):
        t1, t2 = t[:, :half], t[:, half:]
        return jnp.concatenate([t1 * cos - t2 * sin, t2 * cos + t1 * sin], axis=-1)

    q = rope(q_ref[...].astype(F32))
    k = rope(k_ref[...].astype(F32)) * (RET_DK ** -0.5)
    vb = v_ref[...]

    lane = lax.broadcasted_iota(jnp.int32, (1, w), 1)
    head_q = (lane % half) // (half // HEADS)
    head_v = lane // (w // HEADS)

    s_prev = s_ref[...]
    o = _dot((q * qw_ref[...]).astype(BF16), s_prev.astype(BF16))
    qb = q.astype(BF16)
    kb = k.astype(BF16)
    zero_b = jnp.zeros_like(qb)
    for h in range(HEADS):
        a = _dot_nt(jnp.where(head_q == h, qb, zero_b), kb)
        oh = _dot((a * decay_ref[h]).astype(BF16), vb)
        o = o + jnp.where(head_v == h, oh, 0.0)

    ds = _dot_tn((k * kw_ref[...]).astype(BF16), vb)
    row_head = (lax.broadcasted_iota(jnp.int32, (w, 1), 0) % half) // (half // HEADS)
    s_new = s_prev * jnp.exp(lgv * float(cs)) + jnp.where(row_head == head_v, ds, 0.0)
    s_ref[...] = s_new

    ms = _dot((o * o).astype(BF16), avg_ref[...])
    on = o * lax.rsqrt(ms + NORM_EPS)
    y_ref[...] = (_silu(g_ref[...].astype(F32)) * on).astype(y_ref.dtype)
    return s_new


def _ret_kernel(lg_ref, qf, kf, vf, gf, cosf, sinf, qb, kb, vb, gb, cosb, sinb, s0_ref, lgq_ref, lgv_ref,
                yf_ref, yb_ref, sout_ref, s_ref, decay_ref, qw_ref, kw_ref, avg_ref, *, n_chunks):
    c = pl.program_id(1)
    scratch = (s_ref, decay_ref, qw_ref, kw_ref, avg_ref)

    @pl.when(c == 0)
    def _():
        for d in range(2):
            _ret_init(d, qf.shape[0], qf.shape[1], lg_ref, s0_ref, lgq_ref, *scratch)

    s_f = _ret_chunk(0, qf, kf, vf, gf, cosf, sinf, lgv_ref, yf_ref, *scratch)
    s_b = _ret_chunk(1, qb, kb, vb, gb, cosb, sinb, lgv_ref, yb_ref, *scratch)

    @pl.when(c == n_chunks - 1)
    def _():
        sout_ref[0, 0] = s_f
        sout_ref[0, 1] = s_b


def retention(u, cb0, lg, cos, sin, s0, batch, cs):
    n = u.shape[0]
    t = n // batch
    nch = t // cs
    w = BW
    half = w // 2
    lgq = jnp.tile(jnp.repeat(lg, half // HEADS, axis=1), (1, 2)).reshape(2, 1, w)
    lgv = jnp.repeat(lg, w // HEADS, axis=1).reshape(2, 1, w)

    def chunk(d, c):
        return nch - 1 - c if d else c

    def direction(d):
        col = lambda cb: pl.BlockSpec((cs, w), lambda b, c: (b * nch + chunk(d, c), cb0 + cb))
        tab = lambda: pl.BlockSpec((cs, half), lambda b, c: (chunk(d, c), 0))
        return [col(CB_RQ), col(CB_RK), col(CB_RV), col(CB_RGF + d), tab(), tab()]

    y_spec = lambda d: pl.BlockSpec((cs, w), lambda b, c: (b * nch + chunk(d, c), 0))
    state_spec = pl.BlockSpec((1, 2, w, w), lambda b, c: (b, 0, 0, 0))
    lane_spec = pl.BlockSpec((2, 1, w), lambda b, c: (0, 0, 0))
    y_f, y_b, s_out = pl.pallas_call(
        functools.partial(_ret_kernel, n_chunks=nch),
        grid=(batch, nch),
        in_specs=[pl.BlockSpec(memory_space=pltpu.SMEM)] + direction(0) + direction(1)
        + [state_spec, lane_spec, lane_spec],
        out_specs=[y_spec(0), y_spec(1), state_spec],
        out_shape=[
            jax.ShapeDtypeStruct((n, w), BF16),
            jax.ShapeDtypeStruct((n, w), BF16),
            jax.ShapeDtypeStruct((batch, 2, w, w), F32),
        ],
        scratch_shapes=[pltpu.VMEM((2, w, w), F32), pltpu.VMEM((2, HEADS, cs, cs), F32),
                        pltpu.VMEM((2, cs, w), F32), pltpu.VMEM((2, cs, w), F32), pltpu.VMEM((w, w), BF16)],
        compiler_params=_cp(("arbitrary", "arbitrary")),
        name="retention",
    )(lg, u, u, u, u, cos, sin, u, u, u, u, cos, sin, s0, lgq, lgv)
    return (y_f, y_b), s_out


def _dft_tables(t, t1, t2):
    k1 = jnp.arange(t1, dtype=jnp.int32)
    a = jnp.arange(t1, dtype=jnp.int32)
    m = jnp.arange(t2, dtype=jnp.int32)
    ph1 = (k1[None, :, None] * (a[None, None, :] * t2 + m[:, None, None])) % t
    ang1 = ph1.astype(F32) * (2.0 * np.pi / t)
    ph2 = (m[:, None] * m[None, :]) % t2
    ang2 = ph2.astype(F32) * (2.0 * np.pi / t2)
    return (jnp.cos(ang1).astype(BF16), jnp.sin(ang1).astype(BF16),
            jnp.cos(ang2).astype(BF16), jnp.sin(ang2).astype(BF16))


def _channel_tables(width):
    ch = jnp.arange(width, dtype=jnp.int32)
    same = (ch[:, None] // FOURIER_GROUP_DIM) == (ch[None, :] // FOURIER_GROUP_DIM)
    ph = ((ch[:, None] % FOURIER_GROUP_DIM) * (ch[None, :] % FOURIER_GROUP_DIM)) % FOURIER_GROUP_DIM
    ang = ph.astype(F32) * (2.0 * np.pi / FOURIER_GROUP_DIM)
    return (jnp.where(same, jnp.cos(ang), 0.0).astype(BF16),
            jnp.where(same, jnp.sin(ang), 0.0).astype(BF16))


def _fourier_kernel(x_ref, c1_ref, s1_ref, c2_ref, s2_ref, cc_ref, sc_ref, o_ref, xa, yre, yim,
                    *, t1, t2, norm):
    xa[...] = x_ref[...].astype(F32)

    def stage1(m, carry):
        xs = xa[pl.ds(m, t1, stride=t2), :].astype(BF16)
        r0 = pl.multiple_of(m * t1, t1)
        yre[pl.ds(r0, t1), :] = _dot(c1_ref[m], xs)
        yim[pl.ds(r0, t1), :] = -_dot(s1_ref[m], xs)
        return carry

    lax.fori_loop(0, t2, stage1, 0, unroll=8)

    c2 = c2_ref[...]
    s2 = s2_ref[...]
    w2 = jnp.concatenate([jnp.concatenate([c2, s2], axis=1), jnp.concatenate([-s2, c2], axis=1)], axis=0)
    cs = jnp.concatenate([cc_ref[...], sc_ref[...]], axis=0)
    group = 8

    def stage2(j, carry):
        k1 = j * group
        rows = [pl.ds(k1 + g, t2, stride=t1) for g in range(group)]
        y = jnp.concatenate([jnp.concatenate([yre[r, :] for r in rows], axis=1),
                             jnp.concatenate([yim[r, :] for r in rows], axis=1)], axis=0).astype(BF16)
        z = _dot(w2, y).astype(BF16)
        w = z.shape[1] // group
        zz = jnp.concatenate([jnp.concatenate([z[:t2, g * w:(g + 1) * w], z[t2:, g * w:(g + 1) * w]], axis=1)
                              for g in range(group)], axis=0)
        out = _dot(zz, cs) * norm
        for g in range(group):
            xa[rows[g], :] = out[g * t2:(g + 1) * t2]
        return carry

    lax.fori_loop(0, t1 // group, stage2, 0)
    o_ref[...] = xa[...].astype(o_ref.dtype)


def fourier_long(u, cb, batch, t2=LANES):
    n = u.shape[0]
    t = n // batch
    t1 = t // t2
    c1, s1, c2, s2 = _dft_tables(t, t1, t2)
    cc, sc = _channel_tables(LANES)
    norm = float(1.0 / np.sqrt(t * FOURIER_GROUP_DIM))
    full = lambda shape: pl.BlockSpec(shape, lambda b, hh: (0,) * len(shape))
    return pl.pallas_call(
        functools.partial(_fourier_kernel, t1=t1, t2=t2, norm=norm),
        grid=(batch, BW // LANES),
        in_specs=[
            pl.BlockSpec((t, LANES), lambda b, hh: (b, cb * (BW // LANES) + hh)),
            full((t2, t1, t1)), full((t2, t1, t1)), full((t2, t2)), full((t2, t2)),
            full((LANES, LANES)), full((LANES, LANES)),
        ],
        out_specs=pl.BlockSpec((t, LANES), lambda b, hh: (b, hh)),
        out_shape=jax.ShapeDtypeStruct((n, BW), BF16),
        scratch_shapes=[pltpu.VMEM((t, LANES), F32)] * 3,
        compiler_params=_cp(("arbitrary", "arbitrary")),
        name="fourier",
    )(u, c1, s1, c2, s2, cc, sc)


def _fourier_small_kernel(x_ref, ct_ref, st_ref, cc_ref, sc_ref, o_ref, *, norm):
    x = x_ref[...]
    zr = _dot(ct_ref[...], x)
    zi = -_dot(st_ref[...], x)
    out = (_dot(zr.astype(BF16), cc_ref[...]) + _dot(zi.astype(BF16), sc_ref[...])) * norm
    o_ref[...] = out.astype(o_ref.dtype)


def fourier_short(u, cb, batch):
    n = u.shape[0]
    t = n // batch
    pos = jnp.arange(t, dtype=jnp.int32)
    ang = ((pos[:, None] * pos[None, :]) % t).astype(F32) * (2.0 * np.pi / t)
    ct, st = jnp.cos(ang).astype(BF16), jnp.sin(ang).astype(BF16)
    cc, sc = _channel_tables(BW)
    norm = float(1.0 / np.sqrt(t * FOURIER_GROUP_DIM))
    full = lambda shape: pl.BlockSpec(shape, lambda b: (0,) * len(shape))
    return pl.pallas_call(
        functools.partial(_fourier_small_kernel, norm=norm),
        grid=(batch,),
        in_specs=[pl.BlockSpec((t, BW), lambda b: (b, cb)), full((t, t)), full((t, t)),
                  full((BW, BW)), full((BW, BW))],
        out_specs=pl.BlockSpec((t, BW), lambda b: (b, 0)),
        out_shape=jax.ShapeDtypeStruct((n, BW), BF16),
        compiler_params=_cp(("arbitrary",)),
        name="fourier_ctx",
    )(u, ct, st, cc, sc)


def _na_bias_table(rpb):
    n_r, n_c = rpb.shape[1], rpb.shape[2]
    span = 2 * GRID_W
    left = GRID_W - NA_WIN_C
    vp = jnp.pad(rpb.astype(F32), ((0, 0), (0, 0), (left, span - n_c - left)))
    rep = jnp.broadcast_to(vp[:, :, None, :], (HEADS, n_r, GRID_W, span)).reshape(HEADS, n_r, GRID_W * span)
    skew = rep[..., :GRID_W * (span - 1)].reshape(HEADS, n_r, GRID_W, span - 1)
    toep = skew[..., GRID_W - 1:]
    qc = np.arange(GRID_W)[:, None]
    kc = np.arange(GRID_W)[None, :]
    start = np.clip(qc - NA_WIN_C // 2, 0, GRID_W - NA_WIN_C)
    valid = (kc >= start) & (kc < start + NA_WIN_C)
    toep = jnp.where(valid, toep, NEG)
    tab = jnp.stack([toep[:, NA_WIN_R - 1 - v:2 * NA_WIN_R - 1 - v] for v in range(NA_WIN_R)])
    tab = tab.transpose(0, 1, 3, 2, 4)
    return tab.reshape(NA_WIN_R, HEADS * GRID_W, NA_WIN_R * GRID_W)


def _na_kernel(q_ref, k_ref, v_ref, kc_ref, vc_ref, bias_ref, o_ref, *, rows_per_step, n_rows):
    i = pl.program_id(1)
    w = q_ref.shape[1]
    lane = lax.broadcasted_iota(jnp.int32, (1, w), 1)
    head = lane // (w // HEADS)
    scale = jnp.asarray(NA_HEAD_DIM ** -0.5, q_ref.dtype)
    kc = kc_ref[...]
    vc = vc_ref[...]
    win = NA_WIN_R * GRID_W

    def row(rl, carry):
        r = i * rows_per_step + rl
        rs = jnp.clip(r - NA_WIN_R // 2, 0, n_rows - NA_WIN_R)
        var = r - rs
        q0 = pl.multiple_of(rl * GRID_W, GRID_W)
        k0 = pl.multiple_of(rs * GRID_W, GRID_W)
        q = q_ref[pl.ds(q0, GRID_W), :] * scale
        kw = k_ref[pl.ds(k0, win), :]
        vw = v_ref[pl.ds(k0, win), :]
        zero_b = jnp.zeros_like(q)
        q4 = jnp.concatenate([jnp.where(head == h, q, zero_b) for h in range(HEADS)], axis=0)
        s_loc = _dot_nt(q4, kw) + bias_ref[var]
        s_ctx = _dot_nt(q4, kc)
        m = jnp.maximum(jnp.max(s_loc, axis=-1, keepdims=True), jnp.max(s_ctx, axis=-1, keepdims=True))
        p_loc = jnp.exp(s_loc - m)
        p_ctx = jnp.exp(s_ctx - m)
        l = jnp.sum(p_loc, axis=-1, keepdims=True) + jnp.sum(p_ctx, axis=-1, keepdims=True)
        pv = (_dot(p_loc.astype(BF16), vw) + _dot(p_ctx.astype(BF16), vc)) / l
        acc = jnp.zeros((GRID_W, w), F32)
        for h in range(HEADS):
            acc = acc + jnp.where(head == h, pv[h * GRID_W:(h + 1) * GRID_W], 0.0)
        o_ref[pl.ds(q0, GRID_W), :] = acc.astype(o_ref.dtype)
        return carry

    lax.fori_loop(0, rows_per_step, row, 0, unroll=8)


def na_attention(u, uc, cb0, bias_tab, batch, rows_per_step):
    n = u.shape[0]
    t = n // batch
    tc = uc.shape[0] // batch
    n_rows = t // GRID_W
    steps = n_rows // rows_per_step
    tq = rows_per_step * GRID_W
    return pl.pallas_call(
        functools.partial(_na_kernel, rows_per_step=rows_per_step, n_rows=n_rows),
        grid=(batch, steps),
        in_specs=[
            pl.BlockSpec((tq, BW), lambda b, i: (b * steps + i, cb0 + CB_NQ)),
            pl.BlockSpec((t, BW), lambda b, i: (b, cb0 + CB_NK)),
            pl.BlockSpec((t, BW), lambda b, i: (b, cb0 + CB_NV)),
            pl.BlockSpec((tc, BW), lambda b, i: (b, cb0 + CB_NK)),
            pl.BlockSpec((tc, BW), lambda b, i: (b, cb0 + CB_NV)),
            pl.BlockSpec(bias_tab.shape, lambda b, i: (0, 0, 0)),
        ],
        out_specs=pl.BlockSpec((tq, BW), lambda b, i: (b * steps + i, 0)),
        out_shape=jax.ShapeDtypeStruct((n, BW), BF16),
        compiler_params=_cp(("arbitrary", "arbitrary")),
        name="na_attention",
    )(u, u, u, uc, uc, bias_tab)


def _na_ctx_kernel(q_ref, k_ref, v_ref, o_ref):
    w = q_ref.shape[1]
    lane = lax.broadcasted_iota(jnp.int32, (1, w), 1)
    head = lane // (w // HEADS)
    scale = NA_HEAD_DIM ** -0.5
    q = q_ref[...]
    k = k_ref[...]
    v = v_ref[...]
    zero_b = jnp.zeros_like(q)
    acc = jnp.zeros(q.shape, F32)
    for h in range(HEADS):
        s = _dot_nt(jnp.where(head == h, q, zero_b), k) * scale
        p = jnp.exp(s - jnp.max(s, axis=-1, keepdims=True))
        l = jnp.sum(p, axis=-1, keepdims=True)
        acc = acc + jnp.where(head == h, _dot(p.astype(BF16), v) / l, 0.0)
    o_ref[...] = acc.astype(o_ref.dtype)


def na_ctx_attention(uc, cb0, batch):
    tc = uc.shape[0] // batch
    spec = lambda cb: pl.BlockSpec((tc, BW), lambda b: (b, cb0 + cb))
    return pl.pallas_call(
        _na_ctx_kernel,
        grid=(batch,),
        in_specs=[spec(CB_NQ), spec(CB_NK), spec(CB_NV)],
        out_specs=pl.BlockSpec((tc, BW), lambda b: (b, 0)),
        out_shape=jax.ShapeDtypeStruct((uc.shape[0], BW), BF16),
        compiler_params=_cp(("arbitrary",)),
        name="na_ctx_attention",
    )(uc, uc, uc)


def _mla_prep_kernel(cq_ref, ckv_ref, kr_ref, cos_ref, sin_ref, qn_ref, kvn_ref, wq_ref, wqr_ref,
                     wk_ref, wv_ref, p1_ref, p2_ref, one_ref, q_ref, k_ref, v_ref):
    cos = cos_ref[...]
    sin = sin_ref[...]
    cos4 = jnp.concatenate([cos] * HEADS, axis=-1)
    sin4 = jnp.concatenate([sin] * HEADS, axis=-1)

    cq = cq_ref[...].astype(F32)
    ms = jnp.sum(cq * cq, axis=-1, keepdims=True) * (1.0 / MLA_Q_RANK)
    cqn = ((cq * lax.rsqrt(ms + NORM_EPS)) * qn_ref[...]).astype(BF16)
    q = _dot(cqn, wq_ref[...]) * cos4 + _dot(cqn, wqr_ref[...]) * sin4
    q_ref[...] = (q * float((MLA_NOPE + MLA_ROPE) ** -0.5 * np.log2(np.e))).astype(q_ref.dtype)

    ckv = ckv_ref[...].astype(F32)
    ms = jnp.mean(ckv * ckv, axis=-1, keepdims=True)
    ckvn = ((ckv * lax.rsqrt(ms + NORM_EPS)) * kvn_ref[...]).astype(BF16)
    kr = kr_ref[...]
    k_rot = _dot(kr, p1_ref[...]) * cos + _dot(kr, p2_ref[...]) * sin
    k = _dot(ckvn, wk_ref[...]) + jnp.concatenate([k_rot] * HEADS, axis=-1)
    k_ref[...] = k.astype(k_ref.dtype)
    v_ref[...] = (_dot(ckvn, wv_ref[...]) + one_ref[...]).astype(v_ref.dtype)


def _mla_weights(w_uq, w_ukv):
    qr = w_uq.shape[0]
    dq = MLA_NOPE + MLA_ROPE
    hr = MLA_ROPE // 2
    wq3 = w_uq.reshape(qr, HEADS, dq)
    zq = jnp.zeros((qr, HEADS, LANES - dq), F32)
    wq = jnp.concatenate([wq3, zq], axis=-1)
    x1 = wq3[..., MLA_NOPE:MLA_NOPE + hr]
    x2 = wq3[..., MLA_NOPE + hr:]
    wqr = jnp.concatenate([jnp.zeros((qr, HEADS, MLA_NOPE), F32), -x2, x1, zq], axis=-1)
    pad_rows = lambda m: jnp.pad(m.reshape(qr, HEADS * LANES), ((0, BW - qr), (0, 0)))
    kvr = w_ukv.shape[0]
    wkv3 = w_ukv.reshape(kvr, HEADS, MLA_NOPE + MLA_V)
    zk = jnp.zeros((kvr, HEADS, LANES - MLA_NOPE), F32)
    wk = jnp.concatenate([wkv3[..., :MLA_NOPE], zk], axis=-1).reshape(kvr, HEADS * LANES)
    wv = jnp.concatenate([wkv3[..., MLA_NOPE:], zk], axis=-1).reshape(kvr, HEADS * LANES)
    j = np.arange(hr)
    p1 = np.zeros((LANES, LANES), np.float32)
    p1[np.arange(MLA_ROPE), MLA_NOPE + np.arange(MLA_ROPE)] = 1.0
    p2 = np.zeros((LANES, LANES), np.float32)
    p2[hr + j, MLA_NOPE + j] = -1.0
    p2[j, MLA_NOPE + hr + j] = 1.0
    one = np.zeros((1, HEADS * LANES), np.float32)
    one[0, MLA_V + LANES * np.arange(HEADS)] = 1.0
    return (pad_rows(wq).astype(BF16), pad_rows(wqr).astype(BF16), wk.astype(BF16), wv.astype(BF16),
            jnp.asarray(p1, BF16), jnp.asarray(p2, BF16), jnp.asarray(one))


def mla_prep(u, cb0, cos, sin, q_norm, kv_norm, weights, tm, rope_blocks):
    n = u.shape[0]
    wq, wqr, wk, wv, p1, p2, one = weights
    qn = jnp.pad(q_norm, (0, BW - q_norm.shape[0])).reshape(1, BW)
    full = lambda a: pl.BlockSpec(a.shape, lambda i: (0,) * a.ndim)
    tab = pl.BlockSpec((tm, LANES), lambda i: (i % rope_blocks, 0))
    kv_cb = (cb0 + CB_MKV) * (BW // LANES)
    out = jax.ShapeDtypeStruct((n, HEADS * LANES), BF16)
    ospec = pl.BlockSpec((tm, HEADS * LANES), lambda i: (i, 0))
    return pl.pallas_call(
        _mla_prep_kernel,
        grid=(n // tm,),
        in_specs=[
            pl.BlockSpec((tm, BW), lambda i: (i, cb0 + CB_MCQ)),
            pl.BlockSpec((tm, LANES), lambda i: (i, kv_cb)),
            pl.BlockSpec((tm, LANES), lambda i: (i, kv_cb + 1)),
            tab, tab, full(qn), pl.BlockSpec((1, LANES), lambda i: (0, 0)),
            full(wq), full(wqr), full(wk), full(wv), full(p1), full(p2), full(one),
        ],
        out_specs=[ospec, ospec, ospec],
        out_shape=[out, out, out],
        compiler_params=_cp(("arbitrary",)),
        name="mla_prep",
    )(u, u, u, cos, sin, qn, kv_norm.reshape(1, LANES), wq, wqr, wk, wv, p1, p2, one)


def _flash_kernel(*refs, lens, tk):
    q_ref = refs[0]
    kv_refs = refs[1:1 + 2 * len(lens)]
    o_ref = refs[1 + 2 * len(lens)]
    q = q_ref[...]
    tq = q.shape[0]
    m = jnp.full((tq, 1), NEG, F32)
    acc = jnp.zeros((tq, LANES), F32)

    def chunk(kc, vc, m, acc):
        s = _dot_nt(q, kc)
        m_new = jnp.maximum(m, jnp.max(s, axis=-1, keepdims=True))
        p = jnp.exp2((s - m_new).astype(BF16))
        acc = jnp.exp2(m - m_new) * acc + _dot(p, vc)
        return m_new, acc

    for si, length in enumerate(lens):
        k_ref, v_ref = kv_refs[2 * si], kv_refs[2 * si + 1]
        step = min(tk, length)
        if length == step:
            m, acc = chunk(k_ref[...], v_ref[...], m, acc)
        else:
            def body(j, carry, k_ref=k_ref, v_ref=v_ref, step=step):
                j0 = pl.multiple_of(j * step, step)
                return chunk(k_ref[pl.ds(j0, step), :], v_ref[pl.ds(j0, step), :], *carry)

            m, acc = lax.fori_loop(0, length // step, body, (m, acc), unroll=4)

    lane = lax.broadcasted_iota(jnp.int32, (1, LANES), 1)
    l = jnp.sum(jnp.where(lane == MLA_V, acc, 0.0), axis=-1, keepdims=True)
    o_ref[...] = jnp.where(lane < MLA_V, acc / l, 0.0).astype(o_ref.dtype)


def flash_attention(q, kvs, batch, tq, tk):
    n = q.shape[0]
    nq = n // batch // tq
    lens = tuple(k.shape[0] // batch for k, _ in kvs)
    in_specs = [pl.BlockSpec((tq, LANES), lambda b, h, i: (b * nq + i, h))]
    args = [q]
    for (k, v), length in zip(kvs, lens):
        in_specs += [pl.BlockSpec((length, LANES), lambda b, h, i: (b, h))] * 2
        args += [k, v]
    return pl.pallas_call(
        functools.partial(_flash_kernel, lens=lens, tk=tk),
        grid=(batch, HEADS, nq),
        in_specs=in_specs,
        out_specs=pl.BlockSpec((tq, LANES), lambda b, h, i: (b * nq + i, h)),
        out_shape=jax.ShapeDtypeStruct((n, HEADS * LANES), BF16),
        compiler_params=_cp(("arbitrary", "arbitrary", "arbitrary")),
        name="mla_attention",
    )(*args)


def _merge_kernel(*refs, with_router):
    (gate_ref, yr_f_ref, yr_b_ref, yf_ref, yn_ref, ym_ref, x_ref, g1_ref, wb_ret_ref, wb_f_ref,
     wb_na_ref, wb_mla_ref, wo_ref, gain_ref, sc_ref, sh_ref) = refs[:16]
    rest = refs[16:]
    if with_router:
        rh_ref, rl_ref, x_out_ref, lg_out_ref, hp_out_ref = rest
    else:
        x_out_ref, h_out_ref = rest
    d = x_ref.shape[1]

    def gated(k, y):
        g = jax.nn.sigmoid(gate_ref[:, k * d:(k + 1) * d].astype(F32))
        return g * y

    m = gated(0, _dot(yr_f_ref[...] + yr_b_ref[...], wb_ret_ref[...]))
    m = m + gated(1, _dot(yf_ref[...], wb_f_ref[...]))
    m = m + gated(2, _dot(yn_ref[...], wb_na_ref[...]))
    m = m + gated(3, _dot(ym_ref[...], wb_mla_ref[...]))
    y = _dot(m.astype(BF16), wo_ref[...])
    x = x_ref[...] + g1_ref[0] * y
    x_out_ref[...] = x
    hn = x * lax.rsqrt(jnp.mean(x * x, axis=-1, keepdims=True) + NORM_EPS)
    h = (hn * gain_ref[...]) * (1.0 + sc_ref[0]) + sh_ref[0]
    if with_router:
        hp_out_ref[...] = _pack_bf16_pairs(h)
        h_hi = h.astype(BF16)
        h_lo = (h - h_hi.astype(F32)).astype(BF16)
        lg_out_ref[...] = (_dot(h_hi, rh_ref[...]) + _dot(h_lo, rh_ref[...])) + _dot(h_hi, rl_ref[...])
    else:
        h_out_ref[...] = h.astype(h_out_ref.dtype)


def merge(u, y_ret, y_four, y_na, y_mla, x, mods, mod_row, gain, wb, w_out, router, tm):
    n, d = x.shape
    wb_ret, wb_f, wb_na, wb_mla = wb
    full = lambda a: pl.BlockSpec(a.shape, lambda i: (0,) * a.ndim)
    br = lambda: pl.BlockSpec((tm, BW), lambda i: (i, 0))
    tok = lambda: pl.BlockSpec((tm, d), lambda i: (i, 0))
    in_specs = [
        pl.BlockSpec((tm, 4 * d), lambda i: (i, 0)),
        br(), br(), br(), br(),
        pl.BlockSpec((tm, HEADS * LANES), lambda i: (i, 0)),
        tok(), _mod_spec(d, mod_row, tm, 2),
        full(wb_ret), full(wb_f), full(wb_na), full(wb_mla), full(w_out),
        pl.BlockSpec((1, d), lambda i: (0, 0)), _mod_spec(d, mod_row, tm, 4), _mod_spec(d, mod_row, tm, 3),
    ]
    args = [u, y_ret[0], y_ret[1], y_four, y_na, y_mla, x, mods, wb_ret, wb_f, wb_na, wb_mla, w_out,
            gain.reshape(1, d), mods, mods]
    out_specs = [tok()]
    out_shape = [jax.ShapeDtypeStruct((n, d), F32)]
    if router is None:
        out_specs.append(tok())
        out_shape.append(jax.ShapeDtypeStruct((n, d), BF16))
    else:
        in_specs += [full(router[0]), full(router[1])]
        args += list(router)
        out_specs += [pl.BlockSpec((tm, LANES), lambda i: (i, 0)), pl.BlockSpec((tm, d // 2), lambda i: (i, 0))]
        out_shape += [jax.ShapeDtypeStruct((n, LANES), F32), jax.ShapeDtypeStruct((n, d // 2), jnp.int32)]
    return pl.pallas_call(
        functools.partial(_merge_kernel, with_router=router is not None),
        grid=(n // tm,),
        in_specs=in_specs,
        out_specs=out_specs,
        out_shape=out_shape,
        compiler_params=_cp(("arbitrary",)),
        name="merge",
    )(*args)


def _ffn_kernel(h_ref, wg_ref, wu_ref, wd_ref, x_ref, g2_ref, o_ref, acc_ref):
    f = pl.program_id(1)

    @pl.when(f == 0)
    def _():
        acc_ref[...] = jnp.zeros_like(acc_ref)

    h = h_ref[...]
    a = _silu(_dot(h, wg_ref[...])) * _dot(h, wu_ref[...])
    acc_ref[...] += _dot(a.astype(BF16), wd_ref[...])

    @pl.when(f == pl.num_programs(1) - 1)
    def _():
        o_ref[...] = x_ref[...] + g2_ref[0] * acc_ref[...]


def ffn(h, x, mods, mod_row, wg, wu, wd, tm, tf):
    n, d = x.shape
    nf = wg.shape[1] // tf
    return pl.pallas_call(
        _ffn_kernel,
        grid=(n // tm, nf),
        in_specs=[
            pl.BlockSpec((tm, d), lambda i, f: (i, 0)),
            pl.BlockSpec((d, tf), lambda i, f: (0, f)),
            pl.BlockSpec((d, tf), lambda i, f: (0, f)),
            pl.BlockSpec((tf, d), lambda i, f: (f, 0)),
            pl.BlockSpec((tm, d), lambda i, f: (i, 0)),
            _mod_spec(d, mod_row, tm, 5),
        ],
        out_specs=pl.BlockSpec((tm, d), lambda i, f: (i, 0)),
        out_shape=jax.ShapeDtypeStruct((n, d), F32),
        scratch_shapes=[pltpu.VMEM((tm, d), F32)],
        compiler_params=_cp(("arbitrary", "arbitrary"), VMEM_LIMIT_LARGE),
        name="ffn",
    )(h, wg, wu, wd, x, mods)


def _moe_kernel(be_ref, nu_ref, xp_ref, wg_ref, wu_ref, wd_ref, o_ref, acc_ref, x_ref):
    i = pl.program_id(0)
    f = pl.program_id(1)
    used = i < nu_ref[0]

    @pl.when(f == 0)
    def _():
        acc_ref[...] = jnp.zeros_like(acc_ref)
        x_ref[...] = _unpack_bf16_pairs(xp_ref[...]).astype(BF16)

    @pl.when(used)
    def _():
        x = x_ref[...]
        a = _silu(_dot(x, wg_ref[0])) * _dot(x, wu_ref[0])
        acc_ref[...] += _dot(a.astype(BF16), wd_ref[0])

    @pl.when(f == pl.num_programs(1) - 1)
    def _():
        o_ref[...] = _pack_bf16_pairs(acc_ref[...])


SC_CAST_BLOCK = (16, 512)


def sc_cast_bf16(w):
    e, a, b = w.shape
    br, bc = SC_CAST_BLOCK
    assert (e * a) % br == 0 and b % bc == 0
    mesh = plsc.VectorSubcoreMesh(core_axis_name="c", subcore_axis_name="s")

    @functools.partial(pl.kernel, mesh=mesh, out_type=jax.ShapeDtypeStruct((e * a, b), BF16), scratch_types=[])
    def cast(x_hbm, o_hbm):
        def body(in_v, out_v):
            @pl.loop(0, br, step=2)
            def _(r):
                @pl.loop(0, bc, step=16)
                def _(c):
                    top = in_v.at[pl.ds(r, 1), pl.ds(c, 16)][...]
                    bot = in_v.at[pl.ds(r + 1, 1), pl.ds(c, 16)][...]
                    out_v.at[pl.ds(r, 2), pl.ds(c, 16)][...] = jnp.concatenate([top, bot], axis=0).astype(BF16)

        pltpu.emit_pipeline(
            body,
            grid=(e * a // br, b // bc),
            in_specs=[pl.BlockSpec((br, bc), lambda i, j: (i, j))],
            out_specs=[pl.BlockSpec((br, bc), lambda i, j: (i, j))],
            core_axis_name=("c", "s"),
            dimension_semantics=(pltpu.PARALLEL, pltpu.PARALLEL),
        )(x_hbm, o_hbm)

    return cast(w.reshape(e * a, b)).reshape(e, a, b)


def moe_ffn(blk_exp, n_used, xb, wg, wu, wd, tm, tf):
    n, dp = xb.shape
    d = 2 * dp
    nf = wg.shape[2] // tf
    grid_spec = pltpu.PrefetchScalarGridSpec(
        num_scalar_prefetch=2,
        grid=(n // tm, nf),
        in_specs=[
            pl.BlockSpec((tm, dp), lambda i, f, be, nu: (i, 0)),
            pl.BlockSpec((1, d, tf), lambda i, f, be, nu: (be[i], 0, f)),
            pl.BlockSpec((1, d, tf), lambda i, f, be, nu: (be[i], 0, f)),
            pl.BlockSpec((1, tf, d), lambda i, f, be, nu: (be[i], f, 0)),
        ],
        out_specs=pl.BlockSpec((tm, dp), lambda i, f, be, nu: (i, 0)),
        scratch_shapes=[pltpu.VMEM((tm, d), F32), pltpu.VMEM((tm, d), BF16)],
    )
    return pl.pallas_call(
        _moe_kernel,
        grid_spec=grid_spec,
        out_shape=jax.ShapeDtypeStruct((n, dp), jnp.int32),
        compiler_params=_cp(("arbitrary", "arbitrary")),
        name="moe_ffn",
    )(blk_exp, n_used, xb, wg, wu, wd)


def _combine_kernel(x_ref, ya_ref, yb_ref, w_ref, g2_ref, gain_ref, o_ref, *, final):
    w = w_ref[...]
    y = w[:, 0:1] * _unpack_bf16_pairs(ya_ref[...]) + w[:, 1:2] * _unpack_bf16_pairs(yb_ref[...])
    x = x_ref[...] + g2_ref[0] * y
    if final:
        x = (x * lax.rsqrt(jnp.mean(x * x, axis=-1, keepdims=True) + NORM_EPS)) * gain_ref[...]
    o_ref[...] = x


def moe_combine(x, y2, w, mods, mod_row, gain, final, tm):
    n, d = x.shape
    tok = lambda: pl.BlockSpec((tm, d), lambda i: (i, 0))
    half = lambda off: pl.BlockSpec((tm, d // 2), lambda i: (i + off, 0))
    return pl.pallas_call(
        functools.partial(_combine_kernel, final=final),
        grid=(n // tm,),
        in_specs=[tok(), half(0), half(n // tm),
                  pl.BlockSpec((tm, MOE_TOP_K), lambda i: (i, 0)),
                  _mod_spec(d, mod_row, tm, 5), pl.BlockSpec((1, d), lambda i: (0, 0))],
        out_specs=tok(),
        out_shape=jax.ShapeDtypeStruct((n, d), F32),
        compiler_params=_cp(("arbitrary",)),
        name="moe_combine",
    )(x, y2, y2, w, mods, gain.reshape(1, d))


def _rmsnorm_kernel(x_ref, gain_ref, o_ref):
    x = x_ref[...]
    o_ref[...] = (x * lax.rsqrt(jnp.mean(x * x, axis=-1, keepdims=True) + NORM_EPS)) * gain_ref[...]


def rmsnorm_rows(x, gain, tm):
    n, d = x.shape
    return pl.pallas_call(
        _rmsnorm_kernel,
        grid=(n // tm,),
        in_specs=[pl.BlockSpec((tm, d), lambda i: (i, 0)), pl.BlockSpec((1, d), lambda i: (0, 0))],
        out_specs=pl.BlockSpec((tm, d), lambda i: (i, 0)),
        out_shape=jax.ShapeDtypeStruct((n, d), F32),
        compiler_params=_cp(("arbitrary",)),
        name="final_norm",
    )(x, gain.reshape(1, d))


SC_CORES = 2
SC_SUBCORES = 16
SC_CHUNK = 64


def sc_gather_rows(table, idx):
    n_out = idx.shape[0]
    width = table.shape[1]
    workers = SC_CORES * SC_SUBCORES
    per_worker = n_out // workers
    assert n_out == per_worker * workers and per_worker % (2 * SC_CHUNK) == 0 and table.dtype == jnp.int32
    mesh = plsc.VectorSubcoreMesh(core_axis_name="c", subcore_axis_name="s")
    idx_buf = pltpu.VMEM((SC_CHUNK,), jnp.int32)
    row_buf = pltpu.VMEM((SC_CHUNK, width), table.dtype)

    @functools.partial(
        pl.kernel, mesh=mesh,
        out_type=jax.ShapeDtypeStruct((n_out, width), table.dtype),
        scratch_types=[idx_buf, idx_buf, row_buf, row_buf, pltpu.SemaphoreType.DMA, pltpu.SemaphoreType.DMA],
    )
    def gather(table_hbm, idx_hbm, out_hbm, idx_a, idx_b, rows_a, rows_b, sem_a, sem_b):
        base = (lax.axis_index("s") * SC_CORES + lax.axis_index("c")) * per_worker

        @pl.loop(0, per_worker // (2 * SC_CHUNK))
        def _(j):
            off_a = pl.multiple_of(base + j * (2 * SC_CHUNK), SC_CHUNK)
            off_b = pl.multiple_of(off_a + SC_CHUNK, SC_CHUNK)
            pltpu.sync_copy(idx_hbm.at[pl.ds(off_a, SC_CHUNK)], idx_a)
            gather_a = pltpu.async_copy(table_hbm.at[idx_a], rows_a, sem_a)
            pltpu.sync_copy(idx_hbm.at[pl.ds(off_b, SC_CHUNK)], idx_b)
            gather_b = pltpu.async_copy(table_hbm.at[idx_b], rows_b, sem_b)
            gather_a.wait()
            write_a = pltpu.async_copy(rows_a, out_hbm.at[pl.ds(off_a, SC_CHUNK)], sem_a)
            gather_b.wait()
            write_b = pltpu.async_copy(rows_b, out_hbm.at[pl.ds(off_b, SC_CHUNK)], sem_b)
            write_a.wait()
            write_b.wait()

    return gather(table, idx)


def moe_route(logits, n_experts, tm):
    n_tok = logits.shape[0]
    top_logit, top_idx = lax.top_k(logits[:, :n_experts], MOE_TOP_K)
    top_w = jax.nn.softmax(top_logit, axis=-1)
    e_flat = top_idx.reshape(-1).astype(jnp.int32)
    n_assign = e_flat.shape[0]
    onehot = (e_flat[:, None] == jnp.arange(n_experts, dtype=jnp.int32)[None, :]).astype(jnp.int32)
    rank = jnp.sum((jnp.cumsum(onehot, axis=0) - onehot) * onehot, axis=1)
    counts = jnp.sum(onehot, axis=0)
    padded = (counts + tm - 1) // tm * tm
    pad_end = jnp.cumsum(padded)
    pad_start = pad_end - padded
    dest = pad_start[e_flat] + rank
    n_rows = n_assign + n_experts * tm
    tok = jnp.arange(n_assign, dtype=jnp.int32) // MOE_TOP_K
    blk_start = jnp.arange(n_rows // tm, dtype=jnp.int32) * tm
    blk_exp = jnp.minimum(jnp.sum(pad_end[None, :] <= blk_start[:, None], axis=1), n_experts - 1)
    n_used = (pad_end[-1] // tm).reshape(1)
    _, tok_by_row = lax.sort_key_val(dest, tok)
    max_shift = n_experts * tm
    filler = jnp.arange(max_shift, dtype=jnp.int32)
    compact = jnp.concatenate([filler, tok_by_row, filler])
    shift = pad_start - (jnp.cumsum(counts) - counts)
    row_exp = jnp.repeat(blk_exp, tm)
    row_tok = jnp.zeros((n_rows,), jnp.int32)
    for e in range(n_experts):
        shifted = lax.dynamic_slice(compact, (max_shift - shift[e],), (n_rows,))
        row_tok = jnp.where(row_exp == e, shifted, row_tok)
    return row_tok, dest.reshape(n_tok, MOE_TOP_K), top_w, blk_exp.astype(jnp.int32), n_used.astype(jnp.int32)


def _rope_split(wcols):
    d, w = wcols.shape
    half = w // HEADS // 2
    return wcols.reshape(d, HEADS, 2, half).transpose(0, 2, 1, 3).reshape(d, w)


def _inproj_weights(w_in):
    d = w_in.shape[0]
    kv = (BW, BW, BW, BW, MLA_KV_RANK, MLA_ROPE)
    qs = (BW, BW, BW, BW, BW, MLA_Q_RANK, 4 * d)
    offs = np.concatenate([[0], np.cumsum(kv + qs)])
    seg = lambda i: w_in[:, offs[i]:offs[i + 1]]
    r_k, r_v, n_k, n_v, m_ckv, m_kr = (seg(i) for i in range(6))
    r_q, r_gf, r_gb, f_in, n_q, m_cq, gate = (seg(6 + i) for i in range(7))
    z = lambda n: jnp.zeros((d, n), w_in.dtype)
    cols = [gate, _rope_split(r_q), _rope_split(r_k), r_v, r_gf, r_gb, f_in, n_q, n_k, n_v,
            m_cq, z(BW - MLA_Q_RANK), m_ckv, m_kr, z(LANES - MLA_ROPE)]
    return jnp.concatenate(cols, axis=1).astype(BF16)


def _ret_rope_tables(n):
    t = jnp.arange(n)
    row = (t // GRID_W).astype(F32)
    col = (t % GRID_W).astype(F32)
    nf = RET_DK // 4
    inv = ROPE_BASE ** (-jnp.arange(nf, dtype=F32) / nf)
    ang = jnp.concatenate([row[:, None] * inv, col[:, None] * inv], axis=-1)
    return jnp.tile(jnp.cos(ang), (1, HEADS)), jnp.tile(jnp.sin(ang), (1, HEADS))


def _mla_rope_tables(n):
    t = jnp.arange(n)
    row = (t // GRID_W).astype(F32)
    col = (t % GRID_W).astype(F32)
    nf = MLA_ROPE // 4
    inv = ROPE_BASE ** (-jnp.arange(nf, dtype=F32) / nf)
    ang = jnp.concatenate([row[:, None] * inv, col[:, None] * inv], axis=-1)
    pad = jnp.zeros((n, LANES - MLA_NOPE - MLA_ROPE), F32)
    cos = jnp.concatenate([jnp.ones((n, MLA_NOPE), F32), jnp.cos(ang), jnp.cos(ang), pad], axis=-1)
    sin = jnp.concatenate([jnp.zeros((n, MLA_NOPE), F32), jnp.sin(ang), jnp.sin(ang), pad], axis=-1)
    return cos, sin


def _tile_rows(*sizes):
    for tm in (1024, 512, 256, 128):
        if all(s % tm == 0 for s in sizes):
            return tm
    raise ValueError(f"token counts {sizes} need a common multiple-of-128 row tile")


def kernel(x, c, ctx, c_ctx, ada_w, ada_b, norm_mix, norm_ffn, w_in, ret_decay_fwd, ret_decay_bwd,
           mla_q_norm, mla_kv_norm, mla_w_uq, mla_w_ukv, na_rpb, w_branch, w_out,
           ffn_w_gate, ffn_w_up, ffn_w_down, moe_router, moe_w_gate, moe_w_up, moe_w_down, norm_final):
    batch, t, d = x.shape
    tc = ctx.shape[1]
    depth = ada_w.shape[0]
    nl, ncx = batch * t, batch * tc
    assert batch < 8 and t % (16 * GRID_W) == 0 and tc % LANES == 0 and d == 4 * BW
    tm = _tile_rows(t, ncx)
    cb0 = 4 * d // BW

    xl = x.reshape(nl, d)
    xc = ctx.reshape(ncx, d)
    cc = jnp.zeros((8, d), F32).at[:batch].set(c).at[batch].set(c_ctx)
    mods = adaln(cc, ada_w, ada_b).reshape(depth, 8 * 6, 1, d)
    lat_row = lambda r0: r0 // t
    ctx_row = lambda r0: batch

    ret_cos, ret_sin = _ret_rope_tables(t)
    ret_cos_c, ret_sin_c = jnp.ones((tc, LANES), F32), jnp.zeros((tc, LANES), F32)
    mla_cos, mla_sin = _mla_rope_tables(t)
    mla_cos_c = jnp.concatenate([jnp.ones((tm, MLA_NOPE + MLA_ROPE), F32),
                                 jnp.zeros((tm, LANES - MLA_NOPE - MLA_ROPE), F32)], axis=-1)
    mla_sin_c = jnp.zeros((tm, LANES), F32)

    for i in range(depth):
        ctx_out = i < depth - 1
        md = mods[i]
        w_p = _inproj_weights(w_in[i])
        u = norm_inproj(xl, norm_mix[i], md, lat_row, w_p, tm, INPROJ_COLS)
        uc = norm_inproj(xc, norm_mix[i], md, ctx_row, w_p, tm, INPROJ_COLS)

        lg = jnp.stack([jax.nn.log_sigmoid(ret_decay_fwd[i].astype(F32)),
                        jax.nn.log_sigmoid(ret_decay_bwd[i].astype(F32))])
        zero_state = jnp.zeros((batch, 2, BW, BW), F32)
        yc_ret, s_ctx = retention(uc, cb0, lg, ret_cos_c, ret_sin_c, zero_state, batch, min(RET_CHUNK, tc))
        y_ret, _ = retention(u, cb0, lg, ret_cos, ret_sin, s_ctx, batch, RET_CHUNK)

        y_four = fourier_long(u, cb0 + CB_F, batch)

        bias_tab = _na_bias_table(na_rpb[i])
        y_na = na_attention(u, uc, cb0, bias_tab, batch, NA_ROWS_PER_STEP)

        mw = _mla_weights(mla_w_uq[i], mla_w_ukv[i])
        q_l, k_l, v_l = mla_prep(u, cb0, mla_cos, mla_sin, mla_q_norm[i], mla_kv_norm[i], mw, tm, t // tm)
        q_c, k_c, v_c = mla_prep(uc, cb0, mla_cos_c, mla_sin_c, mla_q_norm[i], mla_kv_norm[i], mw, tm, 1)
        y_mla = flash_attention(q_l, [(k_l, v_l), (k_c, v_c)], batch, min(FLASH_Q, t), FLASH_K)

        wb = w_branch[i].astype(BF16)
        wb_mla = jnp.concatenate(
            [wb[3].reshape(HEADS, MLA_V, d), jnp.zeros((HEADS, LANES - MLA_V, d), BF16)], axis=1
        ).reshape(HEADS * LANES, d)
        wbs = (wb[0], wb[1], wb[2], wb_mla)
        wo = w_out[i].astype(BF16)
        j = i // 2
        if i % 2 == 0:
            xl, h2 = merge(u, y_ret, y_four, y_na, y_mla, xl, md, lat_row, norm_ffn[i], wbs, wo, None, MERGE_ROWS)
            wg, wu, wd = ffn_w_gate[j].astype(BF16), ffn_w_up[j].astype(BF16), ffn_w_down[j].astype(BF16)
            tf = wg.shape[1] // 2
            xl = ffn(h2, xl, md, lat_row, wg, wu, wd, tm, tf)
        else:
            n_exp = moe_router.shape[2]
            r = jnp.pad(moe_router[j], ((0, 0), (0, LANES - n_exp)))
            r_hi = r.astype(BF16)
            r_lo = (r - r_hi.astype(F32)).astype(BF16)
            xl, logits, h2p = merge(u, y_ret, y_four, y_na, y_mla, xl, md, lat_row, norm_ffn[i], wbs, wo,
                                   (r_hi, r_lo), MERGE_ROWS)
            row_tok, dest, top_w, blk_exp, n_used = moe_route(logits, n_exp, MOE_ROWS)
            xb = sc_gather_rows(h2p, row_tok)
            ewg, ewu, ewd = (sc_cast_bf16(w) for w in (moe_w_gate[j], moe_w_up[j], moe_w_down[j]))
            yb = moe_ffn(blk_exp, n_used, xb, ewg, ewu, ewd, MOE_ROWS, MOE_FFN_COLS)
            y2 = sc_gather_rows(yb, dest.T.reshape(-1))
            xl = moe_combine(xl, y2, top_w, md, lat_row, norm_final, i == depth - 1, MERGE_ROWS)

        if ctx_out:
            yc_four = fourier_short(uc, cb0 + CB_F, batch)
            yc_na = na_ctx_attention(uc, cb0, batch)
            yc_mla = flash_attention(q_c, [(k_c, v_c)], batch, tc, FLASH_K)
            if i % 2 == 0:
                xc, hc2 = merge(uc, yc_ret, yc_four, yc_na, yc_mla, xc, md, ctx_row, norm_ffn[i], wbs, wo,
                                None, MERGE_ROWS)
                xc = ffn(hc2, xc, md, ctx_row, wg, wu, wd, tm, tf)
            else:
                raise NotImplementedError("context tokens through the expert mixer")

    if depth % 2 == 1:
        xl = rmsnorm_rows(xl, norm_final, tm)
    return xl.reshape(batch, t, d)
```

```python
import functools

import numpy as np
import jax
import jax.numpy as jnp
from jax import lax
from jax.experimental import pallas as pl
from jax.experimental.pallas import tpu as pltpu
from jax.experimental.pallas import tpu_sc as plsc

F32 = jnp.float32
BF16 = jnp.bfloat16

GRID_W = 64
ROPE_BASE = 10000.0
NORM_EPS = 1e-6
HEADS = 4
RET_DK = 64
FOURIER_GROUP_DIM = 64
NA_HEAD_DIM = 64
NA_WIN_R = 8
NA_WIN_C = 16
MLA_NOPE = 64
MLA_ROPE = 32
MLA_V = 64
MLA_Q_RANK = 192
MLA_KV_RANK = 128
MOE_TOP_K = 2
BW = 256

COL_GATE = 0
CB_RQ, CB_RK, CB_RV, CB_RGF, CB_RGB, CB_F, CB_NQ, CB_NK, CB_NV, CB_MCQ, CB_MKV = range(11)
LANES = 128
NEG = -1e30

VMEM_LIMIT = 48 * 1024 * 1024
VMEM_LIMIT_LARGE = 58 * 1024 * 1024

INPROJ_COLS = 2304
RET_CHUNK = 256
NA_ROWS_PER_STEP = 16
FLASH_Q, FLASH_K = 4096, 512
MERGE_ROWS = 512
MOE_ROWS, MOE_FFN_COLS = 512, 1792


def _cp(sem, vmem=VMEM_LIMIT):
    return pltpu.CompilerParams(dimension_semantics=sem, vmem_limit_bytes=vmem)


def _dot(a, b):
    return jnp.dot(a, b, preferred_element_type=F32)


def _dot_nt(a, b):
    return lax.dot_general(a, b, (((1,), (1,)), ((), ())), preferred_element_type=F32)


def _dot_tn(a, b):
    return lax.dot_general(a, b, (((0,), (0,)), ((), ())), preferred_element_type=F32)


def _silu(x):
    return x * jax.nn.sigmoid(x)


def _pack_bf16_pairs(x):
    k = x.shape[1] // 2
    lo = lax.bitcast_convert_type(x[:, :k].astype(BF16).astype(F32), jnp.uint32) >> 16
    hi = lax.bitcast_convert_type(x[:, k:].astype(BF16).astype(F32), jnp.uint32) & jnp.uint32(0xFFFF0000)
    return lax.bitcast_convert_type(lo | hi, jnp.int32)


def _unpack_bf16_pairs(p):
    u = lax.bitcast_convert_type(p, jnp.uint32)
    lo = lax.bitcast_convert_type(u << 16, F32)
    hi = lax.bitcast_convert_type(u & jnp.uint32(0xFFFF0000), F32)
    return jnp.concatenate([lo, hi], axis=-1)


def _adaln_kernel(c_ref, w_ref, b_ref, o_ref):
    s = _silu(c_ref[...])
    o_ref[0] = _dot(s.astype(BF16), w_ref[0].astype(BF16)) + b_ref[0]


def adaln(cc, ada_w, ada_b):
    depth, d, n6 = ada_w.shape
    tn = n6 // 4
    return pl.pallas_call(
        _adaln_kernel,
        grid=(depth, n6 // tn),
        in_specs=[
            pl.BlockSpec((8, d), lambda l, j: (0, 0)),
            pl.BlockSpec((1, d, tn), lambda l, j: (l, 0, j)),
            pl.BlockSpec((1, 1, tn), lambda l, j: (l, 0, j)),
        ],
        out_specs=pl.BlockSpec((1, 8, tn), lambda l, j: (l, 0, j)),
        out_shape=jax.ShapeDtypeStruct((depth, 8, n6), F32),
        compiler_params=_cp(("arbitrary", "arbitrary")),
        name="adaln",
    )(cc, ada_w, ada_b.reshape(depth, 1, n6))


def _inproj_kernel(x_ref, g_ref, sc_ref, sh_ref, w_ref, o_ref, h_ref):
    @pl.when(pl.program_id(1) == 0)
    def _():
        x = x_ref[...]
        y = x * lax.rsqrt(jnp.mean(x * x, axis=-1, keepdims=True) + NORM_EPS)
        h = (y * g_ref[...]) * (1.0 + sc_ref[0]) + sh_ref[0]
        h_ref[...] = h.astype(BF16)

    o_ref[...] = _dot(h_ref[...], w_ref[...]).astype(o_ref.dtype)


def _mod_spec(d, mod_row, tm, k):
    return pl.BlockSpec((1, 1, d), lambda i, *_: (mod_row(i * tm) * 6 + k, 0, 0))


def norm_inproj(x, gain, mods, mod_row, w, tm, tn):
    n, d = x.shape
    nc = w.shape[1]
    return pl.pallas_call(
        _inproj_kernel,
        grid=(n // tm, nc // tn),
        in_specs=[
            pl.BlockSpec((tm, d), lambda i, j: (i, 0)),
            pl.BlockSpec((1, d), lambda i, j: (0, 0)),
            _mod_spec(d, mod_row, tm, 1),
            _mod_spec(d, mod_row, tm, 0),
            pl.BlockSpec((d, tn), lambda i, j: (0, j)),
        ],
        out_specs=pl.BlockSpec((tm, tn), lambda i, j: (i, j)),
        out_shape=jax.ShapeDtypeStruct((n, nc), BF16),
        scratch_shapes=[pltpu.VMEM((tm, d), BF16)],
        compiler_params=_cp(("arbitrary", "arbitrary")),
        name="norm_inproj",
    )(x, gain.reshape(1, d), mods, mods, w)


def _ret_init(d, cs, w, lg_ref, s0_ref, lgq_ref, s_ref, decay_ref, qw_ref, kw_ref, avg_ref):
    rev = d == 1
    s_ref[d] = s0_ref[0, d]
    pos_i = lax.broadcasted_iota(jnp.int32, (cs, 1), 0).astype(F32)
    pos_j = lax.broadcasted_iota(jnp.int32, (1, cs), 1).astype(F32)
    p_i = jnp.where(rev, cs - 1.0 - pos_i, pos_i)
    p_j = jnp.where(rev, cs - 1.0 - pos_j, pos_j)
    diff = p_i - p_j
    for h in range(HEADS):
        decay_ref[d, h] = jnp.where(diff >= 0, jnp.exp(lg_ref[d, h] * jnp.maximum(diff, 0.0)), 0.0)
    lgq = lgq_ref[d]
    qw_ref[d] = jnp.exp(lgq * (p_i + 1.0))
    kw_ref[d] = jnp.exp(lgq * (cs - 1.0 - p_i))
    hd = w // HEADS
    gi = lax.broadcasted_iota(jnp.int32, (w, w), 0) // hd
    gj = lax.broadcasted_iota(jnp.int32, (w, w), 1) // hd
    avg_ref[...] = jnp.where(gi == gj, 1.0 / hd, 0.0).astype(BF16)


def _ret_chunk(d, q_ref, k_ref, v_ref, g_ref, cos_ref, sin_ref, lgv_ref, y_ref, s_ref, decay_ref, qw_ref, kw_ref,
               avg_ref):
    cs = q_ref.shape[0]
    w = q_ref.shape[1]
    half = w // 2
    lgv = lgv_ref[d]
    s_ref, decay_ref, qw_ref, kw_ref = s_ref.at[d], decay_ref.at[d], qw_ref.at[d], kw_ref.at[d]

    cos = cos_ref[...]
    sin = sin_ref[...]

    def rope(t):
        t1, t2 = t[:, :half], t[:, half:]
        return jnp.concatenate([t1 * cos - t2 * sin, t2 * cos + t1 * sin], axis=-1)

    q = rope(q_ref[...].astype(F32))
    k = rope(k_ref[...].astype(F32)) * (RET_DK ** -0.5)
    vb = v_ref[...]

    lane = lax.broadcasted_iota(jnp.int32, (1, w), 1)
    head_q = (lane % half) // (half // HEADS)
    head_v = lane // (w // HEADS)

    s_prev = s_ref[...]
    o = _dot((q * qw_ref[...]).astype(BF16), s_prev.astype(BF16))
    qb = q.astype(BF16)
    kb = k.astype(BF16)
    zero_b = jnp.zeros_like(qb)
    for h in range(HEADS):
        a = _dot_nt(jnp.where(head_q == h, qb, zero_b), kb)
        oh = _dot((a * decay_ref[h]).astype(BF16), vb)
        o = o + jnp.where(head_v == h, oh, 0.0)

    ds = _dot_tn((k * kw_ref[...]).astype(BF16), vb)
    row_head = (lax.broadcasted_iota(jnp.int32, (w, 1), 0) % half) // (half // HEADS)
    s_new = s_prev * jnp.exp(lgv * float(cs)) + jnp.where(row_head == head_v, ds, 0.0)
    s_ref[...] = s_new

    ms = _dot((o * o).astype(BF16), avg_ref[...])
    on = o * lax.rsqrt(ms + NORM_EPS)
    y_ref[...] = (_silu(g_ref[...].astype(F32)) * on).astype(y_ref.dtype)
    return s_new


def _ret_kernel(lg_ref, qf, kf, vf, gf, cosf, sinf, qb, kb, vb, gb, cosb, sinb, s0_ref, lgq_ref, lgv_ref,
                yf_ref, yb_ref, sout_ref, s_ref, decay_ref, qw_ref, kw_ref, avg_ref, *, n_chunks):
    c = pl.program_id(1)
    scratch = (s_ref, decay_ref, qw_ref, kw_ref, avg_ref)

    @pl.when(c == 0)
    def _():
        for d in range(2):
            _ret_init(d, qf.shape[0], qf.shape[1], lg_ref, s0_ref, lgq_ref, *scratch)

    s_f = _ret_chunk(0, qf, kf, vf, gf, cosf, sinf, lgv_ref, yf_ref, *scratch)
    s_b = _ret_chunk(1, qb, kb, vb, gb, cosb, sinb, lgv_ref, yb_ref, *scratch)

    @pl.when(c == n_chunks - 1)
    def _():
        sout_ref[0, 0] = s_f
        sout_ref[0, 1] = s_b


def retention(u, cb0, lg, cos, sin, s0, batch, cs):
    n = u.shape[0]
    t = n // batch
    nch = t // cs
    w = BW
    half = w // 2
    lgq = jnp.tile(jnp.repeat(lg, half // HEADS, axis=1), (1, 2)).reshape(2, 1, w)
    lgv = jnp.repeat(lg, w // HEADS, axis=1).reshape(2, 1, w)

    def chunk(d, c):
        return nch - 1 - c if d else c

    def direction(d):
        col = lambda cb: pl.BlockSpec((cs, w), lambda b, c: (b * nch + chunk(d, c), cb0 + cb))
        tab = lambda: pl.BlockSpec((cs, half), lambda b, c: (chunk(d, c), 0))
        return [col(CB_RQ), col(CB_RK), col(CB_RV), col(CB_RGF + d), tab(), tab()]

    y_spec = lambda d: pl.BlockSpec((cs, w), lambda b, c: (b * nch + chunk(d, c), 0))
    state_spec = pl.BlockSpec((1, 2, w, w), lambda b, c: (b, 0, 0, 0))
    lane_spec = pl.BlockSpec((2, 1, w), lambda b, c: (0, 0, 0))
    y_f, y_b, s_out = pl.pallas_call(
        functools.partial(_ret_kernel, n_chunks=nch),
        grid=(batch, nch),
        in_specs=[pl.BlockSpec(memory_space=pltpu.SMEM)] + direction(0) + direction(1)
        + [state_spec, lane_spec, lane_spec],
        out_specs=[y_spec(0), y_spec(1), state_spec],
        out_shape=[
            jax.ShapeDtypeStruct((n, w), BF16),
            jax.ShapeDtypeStruct((n, w), BF16),
            jax.ShapeDtypeStruct((batch, 2, w, w), F32),
        ],
        scratch_shapes=[pltpu.VMEM((2, w, w), F32), pltpu.VMEM((2, HEADS, cs, cs), F32),
                        pltpu.VMEM((2, cs, w), F32), pltpu.VMEM((2, cs, w), F32), pltpu.VMEM((w, w), BF16)],
        compiler_params=_cp(("arbitrary", "arbitrary")),
        name="retention",
    )(lg, u, u, u, u, cos, sin, u, u, u, u, cos, sin, s0, lgq, lgv)
    return (y_f, y_b), s_out


def _dft_tables(t, t1, t2):
    k1 = jnp.arange(t1, dtype=jnp.int32)
    a = jnp.arange(t1, dtype=jnp.int32)
    m = jnp.arange(t2, dtype=jnp.int32)
    ph1 = (k1[None, :, None] * (a[None, None, :] * t2 + m[:, None, None])) % t
    ang1 = ph1.astype(F32) * (2.0 * np.pi / t)
    ph2 = (m[:, None] * m[None, :]) % t2
    ang2 = ph2.astype(F32) * (2.0 * np.pi / t2)
    return (jnp.cos(ang1).astype(BF16), jnp.sin(ang1).astype(BF16),
            jnp.cos(ang2).astype(BF16), jnp.sin(ang2).astype(BF16))


def _channel_tables(width):
    ch = jnp.arange(width, dtype=jnp.int32)
    same = (ch[:, None] // FOURIER_GROUP_DIM) == (ch[None, :] // FOURIER_GROUP_DIM)
    ph = ((ch[:, None] % FOURIER_GROUP_DIM) * (ch[None, :] % FOURIER_GROUP_DIM)) % FOURIER_GROUP_DIM
    ang = ph.astype(F32) * (2.0 * np.pi / FOURIER_GROUP_DIM)
    return (jnp.where(same, jnp.cos(ang), 0.0).astype(BF16),
            jnp.where(same, jnp.sin(ang), 0.0).astype(BF16))


def _fourier_kernel(x_ref, c1_ref, s1_ref, c2_ref, s2_ref, cc_ref, sc_ref, o_ref, xa, yre, yim,
                    *, t1, t2, norm):
    xa[...] = x_ref[...].astype(F32)

    def stage1(m, carry):
        xs = xa[pl.ds(m, t1, stride=t2), :].astype(BF16)
        r0 = pl.multiple_of(m * t1, t1)
        yre[pl.ds(r0, t1), :] = _dot(c1_ref[m], xs)
        yim[pl.ds(r0, t1), :] = -_dot(s1_ref[m], xs)
        return carry

    lax.fori_loop(0, t2, stage1, 0, unroll=16)

    c2 = c2_ref[...]
    s2 = s2_ref[...]
    w2 = jnp.concatenate([jnp.concatenate([c2, s2], axis=1), jnp.concatenate([-s2, c2], axis=1)], axis=0)
    cs = jnp.concatenate([cc_ref[...], sc_ref[...]], axis=0)
    group = 8

    def stage2(j, carry):
        k1 = j * group
        rows = [pl.ds(k1 + g, t2, stride=t1) for g in range(group)]
        y = jnp.concatenate([jnp.concatenate([yre[r, :] for r in rows], axis=1),
                             jnp.concatenate([yim[r, :] for r in rows], axis=1)], axis=0).astype(BF16)
        z = _dot(w2, y).astype(BF16)
        w = z.shape[1] // group
        zz = jnp.concatenate([jnp.concatenate([z[:t2, g * w:(g + 1) * w], z[t2:, g * w:(g + 1) * w]], axis=1)
                              for g in range(group)], axis=0)
        out = _dot(zz, cs) * norm
        for g in range(group):
            xa[rows[g], :] = out[g * t2:(g + 1) * t2]
        return carry

    lax.fori_loop(0, t1 // group, stage2, 0)
    o_ref[...] = xa[...].astype(o_ref.dtype)


def fourier_long(u, cb, batch, t2=LANES):
    n = u.shape[0]
    t = n // batch
    t1 = t // t2
    c1, s1, c2, s2 = _dft_tables(t, t1, t2)
    cc, sc = _channel_tables(LANES)
    norm = float(1.0 / np.sqrt(t * FOURIER_GROUP_DIM))
    full = lambda shape: pl.BlockSpec(shape, lambda b, hh: (0,) * len(shape))
    return pl.pallas_call(
        functools.partial(_fourier_kernel, t1=t1, t2=t2, norm=norm),
        grid=(batch, BW // LANES),
        in_specs=[
            pl.BlockSpec((t, LANES), lambda b, hh: (b, cb * (BW // LANES) + hh)),
            full((t2, t1, t1)), full((t2, t1, t1)), full((t2, t2)), full((t2, t2)),
            full((LANES, LANES)), full((LANES, LANES)),
        ],
        out_specs=pl.BlockSpec((t, LANES), lambda b, hh: (b, hh)),
        out_shape=jax.ShapeDtypeStruct((n, BW), BF16),
        scratch_shapes=[pltpu.VMEM((t, LANES), F32)] * 3,
        compiler_params=_cp(("arbitrary", "arbitrary")),
        name="fourier",
    )(u, c1, s1, c2, s2, cc, sc)


def _fourier_small_kernel(x_ref, ct_ref, st_ref, cc_ref, sc_ref, o_ref, *, norm):
    x = x_ref[...]
    zr = _dot(ct_ref[...], x)
    zi = -_dot(st_ref[...], x)
    out = (_dot(zr.astype(BF16), cc_ref[...]) + _dot(zi.astype(BF16), sc_ref[...])) * norm
    o_ref[...] = out.astype(o_ref.dtype)


def fourier_short(u, cb, batch):
    n = u.shape[0]
    t = n // batch
    pos = jnp.arange(t, dtype=jnp.int32)
    ang = ((pos[:, None] * pos[None, :]) % t).astype(F32) * (2.0 * np.pi / t)
    ct, st = jnp.cos(ang).astype(BF16), jnp.sin(ang).astype(BF16)
    cc, sc = _channel_tables(BW)
    norm = float(1.0 / np.sqrt(t * FOURIER_GROUP_DIM))
    full = lambda shape: pl.BlockSpec(shape, lambda b: (0,) * len(shape))
    return pl.pallas_call(
        functools.partial(_fourier_small_kernel, norm=norm),
        grid=(batch,),
        in_specs=[pl.BlockSpec((t, BW), lambda b: (b, cb)), full((t, t)), full((t, t)),
                  full((BW, BW)), full((BW, BW))],
        out_specs=pl.BlockSpec((t, BW), lambda b: (b, 0)),
        out_shape=jax.ShapeDtypeStruct((n, BW), BF16),
        compiler_params=_cp(("arbitrary",)),
        name="fourier_ctx",
    )(u, ct, st, cc, sc)


def _na_bias_table(rpb):
    n_r, n_c = rpb.shape[1], rpb.shape[2]
    span = 2 * GRID_W
    left = GRID_W - NA_WIN_C
    vp = jnp.pad(rpb.astype(F32), ((0, 0), (0, 0), (left, span - n_c - left)))
    rep = jnp.broadcast_to(vp[:, :, None, :], (HEADS, n_r, GRID_W, span)).reshape(HEADS, n_r, GRID_W * span)
    skew = rep[..., :GRID_W * (span - 1)].reshape(HEADS, n_r, GRID_W, span - 1)
    toep = skew[..., GRID_W - 1:]
    qc = np.arange(GRID_W)[:, None]
    kc = np.arange(GRID_W)[None, :]
    start = np.clip(qc - NA_WIN_C // 2, 0, GRID_W - NA_WIN_C)
    valid = (kc >= start) & (kc < start + NA_WIN_C)
    toep = jnp.where(valid, toep, NEG)
    tab = jnp.stack([toep[:, NA_WIN_R - 1 - v:2 * NA_WIN_R - 1 - v] for v in range(NA_WIN_R)])
    tab = tab.transpose(0, 1, 3, 2, 4)
    return tab.reshape(NA_WIN_R, HEADS * GRID_W, NA_WIN_R * GRID_W)


def _na_kernel(q_ref, k_ref, v_ref, kc_ref, vc_ref, bias_ref, o_ref, *, rows_per_step, n_rows):
    i = pl.program_id(1)
    w = q_ref.shape[1]
    lane = lax.broadcasted_iota(jnp.int32, (1, w), 1)
    head = lane // (w // HEADS)
    scale = jnp.asarray(NA_HEAD_DIM ** -0.5, q_ref.dtype)
    kc = kc_ref[...]
    vc = vc_ref[...]
    win = NA_WIN_R * GRID_W

    def row(rl, carry):
        r = i * rows_per_step + rl
        rs = jnp.clip(r - NA_WIN_R // 2, 0, n_rows - NA_WIN_R)
        var = r - rs
        q0 = pl.multiple_of(rl * GRID_W, GRID_W)
        k0 = pl.multiple_of(rs * GRID_W, GRID_W)
        q = q_ref[pl.ds(q0, GRID_W), :] * scale
        kw = k_ref[pl.ds(k0, win), :]
        vw = v_ref[pl.ds(k0, win), :]
        zero_b = jnp.zeros_like(q)
        q4 = jnp.concatenate([jnp.where(head == h, q, zero_b) for h in range(HEADS)], axis=0)
        s_loc = _dot_nt(q4, kw) + bias_ref[var]
        s_ctx = _dot_nt(q4, kc)
        m = jnp.maximum(jnp.max(s_loc, axis=-1, keepdims=True), jnp.max(s_ctx, axis=-1, keepdims=True))
        p_loc = jnp.exp(s_loc - m)
        p_ctx = jnp.exp(s_ctx - m)
        l = jnp.sum(p_loc, axis=-1, keepdims=True) + jnp.sum(p_ctx, axis=-1, keepdims=True)
        pv = (_dot(p_loc.astype(BF16), vw) + _dot(p_ctx.astype(BF16), vc)) / l
        acc = jnp.zeros((GRID_W, w), F32)
        for h in range(HEADS):
            acc = acc + jnp.where(head == h, pv[h * GRID_W:(h + 1) * GRID_W], 0.0)
        o_ref[pl.ds(q0, GRID_W), :] = acc.astype(o_ref.dtype)
        return carry

    lax.fori_loop(0, rows_per_step, row, 0, unroll=8)


def na_attention(u, uc, cb0, bias_tab, batch, rows_per_step):
    n = u.shape[0]
    t = n // batch
    tc = uc.shape[0] // batch
    n_rows = t // GRID_W
    steps = n_rows // rows_per_step
    tq = rows_per_step * GRID_W
    return pl.pallas_call(
        functools.partial(_na_kernel, rows_per_step=rows_per_step, n_rows=n_rows),
        grid=(batch, steps),
        in_specs=[
            pl.BlockSpec((tq, BW), lambda b, i: (b * steps + i, cb0 + CB_NQ)),
            pl.BlockSpec((t, BW), lambda b, i: (b, cb0 + CB_NK)),
            pl.BlockSpec((t, BW), lambda b, i: (b, cb0 + CB_NV)),
            pl.BlockSpec((tc, BW), lambda b, i: (b, cb0 + CB_NK)),
            pl.BlockSpec((tc, BW), lambda b, i: (b, cb0 + CB_NV)),
            pl.BlockSpec(bias_tab.shape, lambda b, i: (0, 0, 0)),
        ],
        out_specs=pl.BlockSpec((tq, BW), lambda b, i: (b * steps + i, 0)),
        out_shape=jax.ShapeDtypeStruct((n, BW), BF16),
        compiler_params=_cp(("arbitrary", "arbitrary")),
        name="na_attention",
    )(u, u, u, uc, uc, bias_tab)


def _na_ctx_kernel(q_ref, k_ref, v_ref, o_ref):
    w = q_ref.shape[1]
    lane = lax.broadcasted_iota(jnp.int32, (1, w), 1)
    head = lane // (w // HEADS)
    scale = NA_HEAD_DIM ** -0.5
    q = q_ref[...]
    k = k_ref[...]
    v = v_ref[...]
    zero_b = jnp.zeros_like(q)
    acc = jnp.zeros(q.shape, F32)
    for h in range(HEADS):
        s = _dot_nt(jnp.where(head == h, q, zero_b), k) * scale
        p = jnp.exp(s - jnp.max(s, axis=-1, keepdims=True))
        l = jnp.sum(p, axis=-1, keepdims=True)
        acc = acc + jnp.where(head == h, _dot(p.astype(BF16), v) / l, 0.0)
    o_ref[...] = acc.astype(o_ref.dtype)


def na_ctx_attention(uc, cb0, batch):
    tc = uc.shape[0] // batch
    spec = lambda cb: pl.BlockSpec((tc, BW), lambda b: (b, cb0 + cb))
    return pl.pallas_call(
        _na_ctx_kernel,
        grid=(batch,),
        in_specs=[spec(CB_NQ), spec(CB_NK), spec(CB_NV)],
        out_specs=pl.BlockSpec((tc, BW), lambda b: (b, 0)),
        out_shape=jax.ShapeDtypeStruct((uc.shape[0], BW), BF16),
        compiler_params=_cp(("arbitrary",)),
        name="na_ctx_attention",
    )(uc, uc, uc)


def _mla_prep_kernel(cq_ref, ckv_ref, kr_ref, cos_ref, sin_ref, qn_ref, kvn_ref, wq_ref, wqr_ref,
                     wk_ref, wv_ref, p1_ref, p2_ref, one_ref, q_ref, k_ref, v_ref):
    cos = cos_ref[...]
    sin = sin_ref[...]
    cos4 = jnp.concatenate([cos] * HEADS, axis=-1)
    sin4 = jnp.concatenate([sin] * HEADS, axis=-1)

    cq = cq_ref[...].astype(F32)
    ms = jnp.sum(cq * cq, axis=-1, keepdims=True) * (1.0 / MLA_Q_RANK)
    cqn = ((cq * lax.rsqrt(ms + NORM_EPS)) * qn_ref[...]).astype(BF16)
    q = _dot(cqn, wq_ref[...]) * cos4 + _dot(cqn, wqr_ref[...]) * sin4
    q_ref[...] = (q * float((MLA_NOPE + MLA_ROPE) ** -0.5 * np.log2(np.e))).astype(q_ref.dtype)

    ckv = ckv_ref[...].astype(F32)
    ms = jnp.mean(ckv * ckv, axis=-1, keepdims=True)
    ckvn = ((ckv * lax.rsqrt(ms + NORM_EPS)) * kvn_ref[...]).astype(BF16)
    kr = kr_ref[...]
    k_rot = _dot(kr, p1_ref[...]) * cos + _dot(kr, p2_ref[...]) * sin
    k = _dot(ckvn, wk_ref[...]) + jnp.concatenate([k_rot] * HEADS, axis=-1)
    k_ref[...] = k.astype(k_ref.dtype)
    v_ref[...] = (_dot(ckvn, wv_ref[...]) + one_ref[...]).astype(v_ref.dtype)


def _mla_weights(w_uq, w_ukv):
    qr = w_uq.shape[0]
    dq = MLA_NOPE + MLA_ROPE
    hr = MLA_ROPE // 2
    wq3 = w_uq.reshape(qr, HEADS, dq)
    zq = jnp.zeros((qr, HEADS, LANES - dq), F32)
    wq = jnp.concatenate([wq3, zq], axis=-1)
    x1 = wq3[..., MLA_NOPE:MLA_NOPE + hr]
    x2 = wq3[..., MLA_NOPE + hr:]
    wqr = jnp.concatenate([jnp.zeros((qr, HEADS, MLA_NOPE), F32), -x2, x1, zq], axis=-1)
    pad_rows = lambda m: jnp.pad(m.reshape(qr, HEADS * LANES), ((0, BW - qr), (0, 0)))
    kvr = w_ukv.shape[0]
    wkv3 = w_ukv.reshape(kvr, HEADS, MLA_NOPE + MLA_V)
    zk = jnp.zeros((kvr, HEADS, LANES - MLA_NOPE), F32)
    wk = jnp.concatenate([wkv3[..., :MLA_NOPE], zk], axis=-1).reshape(kvr, HEADS * LANES)
    wv = jnp.concatenate([wkv3[..., MLA_NOPE:], zk], axis=-1).reshape(kvr, HEADS * LANES)
    j = np.arange(hr)
    p1 = np.zeros((LANES, LANES), np.float32)
    p1[np.arange(MLA_ROPE), MLA_NOPE + np.arange(MLA_ROPE)] = 1.0
    p2 = np.zeros((LANES, LANES), np.float32)
    p2[hr + j, MLA_NOPE + j] = -1.0
    p2[j, MLA_NOPE + hr + j] = 1.0
    one = np.zeros((1, HEADS * LANES), np.float32)
    one[0, MLA_V + LANES * np.arange(HEADS)] = 1.0
    return (pad_rows(wq).astype(BF16), pad_rows(wqr).astype(BF16), wk.astype(BF16), wv.astype(BF16),
            jnp.asarray(p1, BF16), jnp.asarray(p2, BF16), jnp.asarray(one))


def mla_prep(u, cb0, cos, sin, q_norm, kv_norm, weights, tm, rope_blocks):
    n = u.shape[0]
    wq, wqr, wk, wv, p1, p2, one = weights
    qn = jnp.pad(q_norm, (0, BW - q_norm.shape[0])).reshape(1, BW)
    full = lambda a: pl.BlockSpec(a.shape, lambda i: (0,) * a.ndim)
    tab = pl.BlockSpec((tm, LANES), lambda i: (i % rope_blocks, 0))
    kv_cb = (cb0 + CB_MKV) * (BW // LANES)
    out = jax.ShapeDtypeStruct((n, HEADS * LANES), BF16)
    ospec = pl.BlockSpec((tm, HEADS * LANES), lambda i: (i, 0))
    return pl.pallas_call(
        _mla_prep_kernel,
        grid=(n // tm,),
        in_specs=[
            pl.BlockSpec((tm, BW), lambda i: (i, cb0 + CB_MCQ)),
            pl.BlockSpec((tm, LANES), lambda i: (i, kv_cb)),
            pl.BlockSpec((tm, LANES), lambda i: (i, kv_cb + 1)),
            tab, tab, full(qn), pl.BlockSpec((1, LANES), lambda i: (0, 0)),
            full(wq), full(wqr), full(wk), full(wv), full(p1), full(p2), full(one),
        ],
        out_specs=[ospec, ospec, ospec],
        out_shape=[out, out, out],
        compiler_params=_cp(("arbitrary",)),
        name="mla_prep",
    )(u, u, u, cos, sin, qn, kv_norm.reshape(1, LANES), wq, wqr, wk, wv, p1, p2, one)


def _flash_kernel(*refs, lens, tk):
    q_ref = refs[0]
    kv_refs = refs[1:1 + 2 * len(lens)]
    o_ref = refs[1 + 2 * len(lens)]
    q = q_ref[...]
    tq = q.shape[0]
    m = jnp.full((tq, 1), NEG, F32)
    acc = jnp.zeros((tq, LANES), F32)

    def chunk(kc, vc, m, acc):
        s = _dot_nt(q, kc)
        m_new = jnp.maximum(m, jnp.max(s, axis=-1, keepdims=True))
        p = jnp.exp2((s - m_new).astype(BF16))
        acc = jnp.exp2(m - m_new) * acc + _dot(p, vc)
        return m_new, acc

    for si, length in enumerate(lens):
        k_ref, v_ref = kv_refs[2 * si], kv_refs[2 * si + 1]
        step = min(tk, length)
        if length == step:
            m, acc = chunk(k_ref[...], v_ref[...], m, acc)
        else:
            def body(j, carry, k_ref=k_ref, v_ref=v_ref, step=step):
                j0 = pl.multiple_of(j * step, step)
                return chunk(k_ref[pl.ds(j0, step), :], v_ref[pl.ds(j0, step), :], *carry)

            m, acc = lax.fori_loop(0, length // step, body, (m, acc), unroll=8)

    lane = lax.broadcasted_iota(jnp.int32, (1, LANES), 1)
    l = jnp.sum(jnp.where(lane == MLA_V, acc, 0.0), axis=-1, keepdims=True)
    o_ref[...] = jnp.where(lane < MLA_V, acc / l, 0.0).astype(o_ref.dtype)


def flash_attention(q, kvs, batch, tq, tk):
    n = q.shape[0]
    nq = n // batch // tq
    lens = tuple(k.shape[0] // batch for k, _ in kvs)
    in_specs = [pl.BlockSpec((tq, LANES), lambda b, h, i: (b * nq + i, h))]
    args = [q]
    for (k, v), length in zip(kvs, lens):
        in_specs += [pl.BlockSpec((length, LANES), lambda b, h, i: (b, h))] * 2
        args += [k, v]
    return pl.pallas_call(
        functools.partial(_flash_kernel, lens=lens, tk=tk),
        grid=(batch, HEADS, nq),
        in_specs=in_specs,
        out_specs=pl.BlockSpec((tq, LANES), lambda b, h, i: (b * nq + i, h)),
        out_shape=jax.ShapeDtypeStruct((n, HEADS * LANES), BF16),
        compiler_params=_cp(("arbitrary", "arbitrary", "arbitrary")),
        name="mla_attention",
    )(*args)


def _merge_kernel(*refs, with_router):
    (gate_ref, yr_f_ref, yr_b_ref, yf_ref, yn_ref, ym_ref, x_ref, g1_ref, wb_ret_ref, wb_f_ref,
     wb_na_ref, wb_mla_ref, wo_ref, gain_ref, sc_ref, sh_ref) = refs[:16]
    rest = refs[16:]
    if with_router:
        rh_ref, rl_ref, x_out_ref, lg_out_ref, hp_out_ref = rest
    else:
        x_out_ref, h_out_ref = rest
    d = x_ref.shape[1]

    def gated(k, y):
        g = jax.nn.sigmoid(gate_ref[:, k * d:(k + 1) * d].astype(F32))
        return g * y

    m = gated(0, _dot(yr_f_ref[...] + yr_b_ref[...], wb_ret_ref[...]))
    m = m + gated(1, _dot(yf_ref[...], wb_f_ref[...]))
    m = m + gated(2, _dot(yn_ref[...], wb_na_ref[...]))
    m = m + gated(3, _dot(ym_ref[...], wb_mla_ref[...]))
    y = _dot(m.astype(BF16), wo_ref[...])
    x = x_ref[...] + g1_ref[0] * y
    x_out_ref[...] = x
    hn = x * lax.rsqrt(jnp.mean(x * x, axis=-1, keepdims=True) + NORM_EPS)
    h = (hn * gain_ref[...]) * (1.0 + sc_ref[0]) + sh_ref[0]
    if with_router:
        hp_out_ref[...] = _pack_bf16_pairs(h)
        h_hi = h.astype(BF16)
        h_lo = (h - h_hi.astype(F32)).astype(BF16)
        lg_out_ref[...] = (_dot(h_hi, rh_ref[...]) + _dot(h_lo, rh_ref[...])) + _dot(h_hi, rl_ref[...])
    else:
        h_out_ref[...] = h.astype(h_out_ref.dtype)


def merge(u, y_ret, y_four, y_na, y_mla, x, mods, mod_row, gain, wb, w_out, router, tm):
    n, d = x.shape
    wb_ret, wb_f, wb_na, wb_mla = wb
    full = lambda a: pl.BlockSpec(a.shape, lambda i: (0,) * a.ndim)
    br = lambda: pl.BlockSpec((tm, BW), lambda i: (i, 0))
    tok = lambda: pl.BlockSpec((tm, d), lambda i: (i, 0))
    in_specs = [
        pl.BlockSpec((tm, 4 * d), lambda i: (i, 0)),
        br(), br(), br(), br(),
        pl.BlockSpec((tm, HEADS * LANES), lambda i: (i, 0)),
        tok(), _mod_spec(d, mod_row, tm, 2),
        full(wb_ret), full(wb_f), full(wb_na), full(wb_mla), full(w_out),
        pl.BlockSpec((1, d), lambda i: (0, 0)), _mod_spec(d, mod_row, tm, 4), _mod_spec(d, mod_row, tm, 3),
    ]
    args = [u, y_ret[0], y_ret[1], y_four, y_na, y_mla, x, mods, wb_ret, wb_f, wb_na, wb_mla, w_out,
            gain.reshape(1, d), mods, mods]
    out_specs = [tok()]
    out_shape = [jax.ShapeDtypeStruct((n, d), F32)]
    if router is None:
        out_specs.append(tok())
        out_shape.append(jax.ShapeDtypeStruct((n, d), BF16))
    else:
        in_specs += [full(router[0]), full(router[1])]
        args += list(router)
        out_specs += [pl.BlockSpec((tm, LANES), lambda i: (i, 0)), pl.BlockSpec((tm, d // 2), lambda i: (i, 0))]
        out_shape += [jax.ShapeDtypeStruct((n, LANES), F32), jax.ShapeDtypeStruct((n, d // 2), jnp.int32)]
    return pl.pallas_call(
        functools.partial(_merge_kernel, with_router=router is not None),
        grid=(n // tm,),
        in_specs=in_specs,
        out_specs=out_specs,
        out_shape=out_shape,
        compiler_params=_cp(("arbitrary",)),
        name="merge",
    )(*args)


def _ffn_kernel(h_ref, wg_ref, wu_ref, wd_ref, x_ref, g2_ref, o_ref, acc_ref):
    f = pl.program_id(1)

    @pl.when(f == 0)
    def _():
        acc_ref[...] = jnp.zeros_like(acc_ref)

    h = h_ref[...]
    a = _silu(_dot(h, wg_ref[...])) * _dot(h, wu_ref[...])
    acc_ref[...] += _dot(a.astype(BF16), wd_ref[...])

    @pl.when(f == pl.num_programs(1) - 1)
    def _():
        o_ref[...] = x_ref[...] + g2_ref[0] * acc_ref[...]


def ffn(h, x, mods, mod_row, wg, wu, wd, tm, tf):
    n, d = x.shape
    nf = wg.shape[1] // tf
    return pl.pallas_call(
        _ffn_kernel,
        grid=(n // tm, nf),
        in_specs=[
            pl.BlockSpec((tm, d), lambda i, f: (i, 0)),
            pl.BlockSpec((d, tf), lambda i, f: (0, f)),
            pl.BlockSpec((d, tf), lambda i, f: (0, f)),
            pl.BlockSpec((tf, d), lambda i, f: (f, 0)),
            pl.BlockSpec((tm, d), lambda i, f: (i, 0)),
            _mod_spec(d, mod_row, tm, 5),
        ],
        out_specs=pl.BlockSpec((tm, d), lambda i, f: (i, 0)),
        out_shape=jax.ShapeDtypeStruct((n, d), F32),
        scratch_shapes=[pltpu.VMEM((tm, d), F32)],
        compiler_params=_cp(("arbitrary", "arbitrary"), VMEM_LIMIT_LARGE),
        name="ffn",
    )(h, wg, wu, wd, x, mods)


def _moe_kernel(be_ref, nu_ref, xp_ref, wg_ref, wu_ref, wd_ref, o_ref, acc_ref, x_ref):
    i = pl.program_id(0)
    f = pl.program_id(1)
    used = i < nu_ref[0]

    @pl.when(f == 0)
    def _():
        acc_ref[...] = jnp.zeros_like(acc_ref)
        x_ref[...] = _unpack_bf16_pairs(xp_ref[...]).astype(BF16)

    @pl.when(used)
    def _():
        x = x_ref[...]
        a = _silu(_dot(x, wg_ref[0])) * _dot(x, wu_ref[0])
        acc_ref[...] += _dot(a.astype(BF16), wd_ref[0])

    @pl.when(f == pl.num_programs(1) - 1)
    def _():
        o_ref[...] = _pack_bf16_pairs(acc_ref[...])


SC_CAST_BLOCK = (16, 512)


def sc_cast_bf16(w):
    e, a, b = w.shape
    br, bc = SC_CAST_BLOCK
    assert (e * a) % br == 0 and b % bc == 0
    mesh = plsc.VectorSubcoreMesh(core_axis_name="c", subcore_axis_name="s")

    @functools.partial(pl.kernel, mesh=mesh, out_type=jax.ShapeDtypeStruct((e * a, b), BF16), scratch_types=[])
    def cast(x_hbm, o_hbm):
        def body(in_v, out_v):
            @pl.loop(0, br, step=2)
            def _(r):
                @pl.loop(0, bc, step=16)
                def _(c):
                    top = in_v.at[pl.ds(r, 1), pl.ds(c, 16)][...]
                    bot = in_v.at[pl.ds(r + 1, 1), pl.ds(c, 16)][...]
                    out_v.at[pl.ds(r, 2), pl.ds(c, 16)][...] = jnp.concatenate([top, bot], axis=0).astype(BF16)

        pltpu.emit_pipeline(
            body,
            grid=(e * a // br, b // bc),
            in_specs=[pl.BlockSpec((br, bc), lambda i, j: (i, j))],
            out_specs=[pl.BlockSpec((br, bc), lambda i, j: (i, j))],
            core_axis_name=("c", "s"),
            dimension_semantics=(pltpu.PARALLEL, pltpu.PARALLEL),
        )(x_hbm, o_hbm)

    return cast(w.reshape(e * a, b)).reshape(e, a, b)


def moe_ffn(blk_exp, n_used, xb, wg, wu, wd, tm, tf):
    n, dp = xb.shape
    d = 2 * dp
    nf = wg.shape[2] // tf
    grid_spec = pltpu.PrefetchScalarGridSpec(
        num_scalar_prefetch=2,
        grid=(n // tm, nf),
        in_specs=[
            pl.BlockSpec((tm, dp), lambda i, f, be, nu: (i, 0)),
            pl.BlockSpec((1, d, tf), lambda i, f, be, nu: (be[i], 0, f)),
            pl.BlockSpec((1, d, tf), lambda i, f, be, nu: (be[i], 0, f)),
            pl.BlockSpec((1, tf, d), lambda i, f, be, nu: (be[i], f, 0)),
        ],
        out_specs=pl.BlockSpec((tm, dp), lambda i, f, be, nu: (i, 0)),
        scratch_shapes=[pltpu.VMEM((tm, d), F32), pltpu.VMEM((tm, d), BF16)],
    )
    return pl.pallas_call(
        _moe_kernel,
        grid_spec=grid_spec,
        out_shape=jax.ShapeDtypeStruct((n, dp), jnp.int32),
        compiler_params=_cp(("arbitrary", "arbitrary")),
        name="moe_ffn",
    )(blk_exp, n_used, xb, wg, wu, wd)


def _combine_kernel(x_ref, ya_ref, yb_ref, w_ref, g2_ref, gain_ref, o_ref, *, final):
    w = w_ref[...]
    y = w[:, 0:1] * _unpack_bf16_pairs(ya_ref[...]) + w[:, 1:2] * _unpack_bf16_pairs(yb_ref[...])
    x = x_ref[...] + g2_ref[0] * y
    if final:
        x = (x * lax.rsqrt(jnp.mean(x * x, axis=-1, keepdims=True) + NORM_EPS)) * gain_ref[...]
    o_ref[...] = x


def moe_combine(x, y2, w, mods, mod_row, gain, final, tm):
    n, d = x.shape
    tok = lambda: pl.BlockSpec((tm, d), lambda i: (i, 0))
    half = lambda off: pl.BlockSpec((tm, d // 2), lambda i: (i + off, 0))
    return pl.pallas_call(
        functools.partial(_combine_kernel, final=final),
        grid=(n // tm,),
        in_specs=[tok(), half(0), half(n // tm),
                  pl.BlockSpec((tm, MOE_TOP_K), lambda i: (i, 0)),
                  _mod_spec(d, mod_row, tm, 5), pl.BlockSpec((1, d), lambda i: (0, 0))],
        out_specs=tok(),
        out_shape=jax.ShapeDtypeStruct((n, d), F32),
        compiler_params=_cp(("arbitrary",)),
        name="moe_combine",
    )(x, y2, y2, w, mods, gain.reshape(1, d))


def _rmsnorm_kernel(x_ref, gain_ref, o_ref):
    x = x_ref[...]
    o_ref[...] = (x * lax.rsqrt(jnp.mean(x * x, axis=-1, keepdims=True) + NORM_EPS)) * gain_ref[...]


def rmsnorm_rows(x, gain, tm):
    n, d = x.shape
    return pl.pallas_call(
        _rmsnorm_kernel,
        grid=(n // tm,),
        in_specs=[pl.BlockSpec((tm, d), lambda i: (i, 0)), pl.BlockSpec((1, d), lambda i: (0, 0))],
        out_specs=pl.BlockSpec((tm, d), lambda i: (i, 0)),
        out_shape=jax.ShapeDtypeStruct((n, d), F32),
        compiler_params=_cp(("arbitrary",)),
        name="final_norm",
    )(x, gain.reshape(1, d))


SC_CORES = 2
SC_SUBCORES = 16
SC_CHUNK = 64


def sc_gather_rows(table, idx):
    n_out = idx.shape[0]
    width = table.shape[1]
    workers = SC_CORES * SC_SUBCORES
    per_worker = n_out // workers
    assert n_out == per_worker * workers and per_worker % (2 * SC_CHUNK) == 0 and table.dtype == jnp.int32
    mesh = plsc.VectorSubcoreMesh(core_axis_name="c", subcore_axis_name="s")
    idx_buf = pltpu.VMEM((SC_CHUNK,), jnp.int32)
    row_buf = pltpu.VMEM((SC_CHUNK, width), table.dtype)

    @functools.partial(
        pl.kernel, mesh=mesh,
        out_type=jax.ShapeDtypeStruct((n_out, width), table.dtype),
        scratch_types=[idx_buf, idx_buf, row_buf, row_buf, pltpu.SemaphoreType.DMA, pltpu.SemaphoreType.DMA],
    )
    def gather(table_hbm, idx_hbm, out_hbm, idx_a, idx_b, rows_a, rows_b, sem_a, sem_b):
        base = (lax.axis_index("s") * SC_CORES + lax.axis_index("c")) * per_worker

        @pl.loop(0, per_worker // (2 * SC_CHUNK))
        def _(j):
            off_a = pl.multiple_of(base + j * (2 * SC_CHUNK), SC_CHUNK)
            off_b = pl.multiple_of(off_a + SC_CHUNK, SC_CHUNK)
            pltpu.sync_copy(idx_hbm.at[pl.ds(off_a, SC_CHUNK)], idx_a)
            gather_a = pltpu.async_copy(table_hbm.at[idx_a], rows_a, sem_a)
            pltpu.sync_copy(idx_hbm.at[pl.ds(off_b, SC_CHUNK)], idx_b)
            gather_b = pltpu.async_copy(table_hbm.at[idx_b], rows_b, sem_b)
            gather_a.wait()
            write_a = pltpu.async_copy(rows_a, out_hbm.at[pl.ds(off_a, SC_CHUNK)], sem_a)
            gather_b.wait()
            write_b = pltpu.async_copy(rows_b, out_hbm.at[pl.ds(off_b, SC_CHUNK)], sem_b)
            write_a.wait()
            write_b.wait()

    return gather(table, idx)


def moe_route(logits, n_experts, tm):
    n_tok = logits.shape[0]
    top_logit, top_idx = lax.top_k(logits[:, :n_experts], MOE_TOP_K)
    top_w = jax.nn.softmax(top_logit, axis=-1)
    e_flat = top_idx.reshape(-1).astype(jnp.int32)
    n_assign = e_flat.shape[0]
    onehot = (e_flat[:, None] == jnp.arange(n_experts, dtype=jnp.int32)[None, :]).astype(jnp.int32)
    rank = jnp.sum((jnp.cumsum(onehot, axis=0) - onehot) * onehot, axis=1)
    counts = jnp.sum(onehot, axis=0)
    padded = (counts + tm - 1) // tm * tm
    pad_end = jnp.cumsum(padded)
    pad_start = pad_end - padded
    dest = pad_start[e_flat] + rank
    n_rows = n_assign + n_experts * tm
    tok = jnp.arange(n_assign, dtype=jnp.int32) // MOE_TOP_K
    blk_start = jnp.arange(n_rows // tm, dtype=jnp.int32) * tm
    blk_exp = jnp.minimum(jnp.sum(pad_end[None, :] <= blk_start[:, None], axis=1), n_experts - 1)
    n_used = (pad_end[-1] // tm).reshape(1)
    _, tok_by_row = lax.sort_key_val(dest, tok)
    max_shift = n_experts * tm
    filler = jnp.arange(max_shift, dtype=jnp.int32)
    compact = jnp.concatenate([filler, tok_by_row, filler])
    shift = pad_start - (jnp.cumsum(counts) - counts)
    row_exp = jnp.repeat(blk_exp, tm)
    row_tok = jnp.zeros((n_rows,), jnp.int32)
    for e in range(n_experts):
        shifted = lax.dynamic_slice(compact, (max_shift - shift[e],), (n_rows,))
        row_tok = jnp.where(row_exp == e, shifted, row_tok)
    return row_tok, dest.reshape(n_tok, MOE_TOP_K), top_w, blk_exp.astype(jnp.int32), n_used.astype(jnp.int32)


def _rope_split(wcols):
    d, w = wcols.shape
    half = w // HEADS // 2
    return wcols.reshape(d, HEADS, 2, half).transpose(0, 2, 1, 3).reshape(d, w)


def _inproj_weights(w_in):
    d = w_in.shape[0]
    kv = (BW, BW, BW, BW, MLA_KV_RANK, MLA_ROPE)
    qs = (BW, BW, BW, BW, BW, MLA_Q_RANK, 4 * d)
    offs = np.concatenate([[0], np.cumsum(kv + qs)])
    seg = lambda i: w_in[:, offs[i]:offs[i + 1]]
    r_k, r_v, n_k, n_v, m_ckv, m_kr = (seg(i) for i in range(6))
    r_q, r_gf, r_gb, f_in, n_q, m_cq, gate = (seg(6 + i) for i in range(7))
    z = lambda n: jnp.zeros((d, n), w_in.dtype)
    cols = [gate, _rope_split(r_q), _rope_split(r_k), r_v, r_gf, r_gb, f_in, n_q, n_k, n_v,
            m_cq, z(BW - MLA_Q_RANK), m_ckv, m_kr, z(LANES - MLA_ROPE)]
    return jnp.concatenate(cols, axis=1).astype(BF16)


def _ret_rope_tables(n):
    t = jnp.arange(n)
    row = (t // GRID_W).astype(F32)
    col = (t % GRID_W).astype(F32)
    nf = RET_DK // 4
    inv = ROPE_BASE ** (-jnp.arange(nf, dtype=F32) / nf)
    ang = jnp.concatenate([row[:, None] * inv, col[:, None] * inv], axis=-1)
    return jnp.tile(jnp.cos(ang), (1, HEADS)), jnp.tile(jnp.sin(ang), (1, HEADS))


def _mla_rope_tables(n):
    t = jnp.arange(n)
    row = (t // GRID_W).astype(F32)
    col = (t % GRID_W).astype(F32)
    nf = MLA_ROPE // 4
    inv = ROPE_BASE ** (-jnp.arange(nf, dtype=F32) / nf)
    ang = jnp.concatenate([row[:, None] * inv, col[:, None] * inv], axis=-1)
    pad = jnp.zeros((n, LANES - MLA_NOPE - MLA_ROPE), F32)
    cos = jnp.concatenate([jnp.ones((n, MLA_NOPE), F32), jnp.cos(ang), jnp.cos(ang), pad], axis=-1)
    sin = jnp.concatenate([jnp.zeros((n, MLA_NOPE), F32), jnp.sin(ang), jnp.sin(ang), pad], axis=-1)
    return cos, sin


def _tile_rows(*sizes):
    for tm in (1024, 512, 256, 128):
        if all(s % tm == 0 for s in sizes):
            return tm
    raise ValueError(f"token counts {sizes} need a common multiple-of-128 row tile")


def kernel(x, c, ctx, c_ctx, ada_w, ada_b, norm_mix, norm_ffn, w_in, ret_decay_fwd, ret_decay_bwd,
           mla_q_norm, mla_kv_norm, mla_w_uq, mla_w_ukv, na_rpb, w_branch, w_out,
           ffn_w_gate, ffn_w_up, ffn_w_down, moe_router, moe_w_gate, moe_w_up, moe_w_down, norm_final):
    batch, t, d = x.shape
    tc = ctx.shape[1]
    depth = ada_w.shape[0]
    nl, ncx = batch * t, batch * tc
    assert batch < 8 and t % (16 * GRID_W) == 0 and tc % LANES == 0 and d == 4 * BW
    tm = _tile_rows(t, ncx)
    cb0 = 4 * d // BW

    xl = x.reshape(nl, d)
    xc = ctx.reshape(ncx, d)
    cc = jnp.zeros((8, d), F32).at[:batch].set(c).at[batch].set(c_ctx)
    mods = adaln(cc, ada_w, ada_b).reshape(depth, 8 * 6, 1, d)
    lat_row = lambda r0: r0 // t
    ctx_row = lambda r0: batch

    ret_cos, ret_sin = _ret_rope_tables(t)
    ret_cos_c, ret_sin_c = jnp.ones((tc, LANES), F32), jnp.zeros((tc, LANES), F32)
    mla_cos, mla_sin = _mla_rope_tables(t)
    mla_cos_c = jnp.concatenate([jnp.ones((tm, MLA_NOPE + MLA_ROPE), F32),
                                 jnp.zeros((tm, LANES - MLA_NOPE - MLA_ROPE), F32)], axis=-1)
    mla_sin_c = jnp.zeros((tm, LANES), F32)

    for i in range(depth):
        ctx_out = i < depth - 1
        md = mods[i]
        w_p = _inproj_weights(w_in[i])
        u = norm_inproj(xl, norm_mix[i], md, lat_row, w_p, tm, INPROJ_COLS)
        uc = norm_inproj(xc, norm_mix[i], md, ctx_row, w_p, tm, INPROJ_COLS)

        lg = jnp.stack([jax.nn.log_sigmoid(ret_decay_fwd[i].astype(F32)),
                        jax.nn.log_sigmoid(ret_decay_bwd[i].astype(F32))])
        zero_state = jnp.zeros((batch, 2, BW, BW), F32)
        yc_ret, s_ctx = retention(uc, cb0, lg, ret_cos_c, ret_sin_c, zero_state, batch, min(RET_CHUNK, tc))
        y_ret, _ = retention(u, cb0, lg, ret_cos, ret_sin, s_ctx, batch, RET_CHUNK)

        y_four = fourier_long(u, cb0 + CB_F, batch)

        bias_tab = _na_bias_table(na_rpb[i])
        y_na = na_attention(u, uc, cb0, bias_tab, batch, NA_ROWS_PER_STEP)

        mw = _mla_weights(mla_w_uq[i], mla_w_ukv[i])
        q_l, k_l, v_l = mla_prep(u, cb0, mla_cos, mla_sin, mla_q_norm[i], mla_kv_norm[i], mw, tm, t // tm)
        q_c, k_c, v_c = mla_prep(uc, cb0, mla_cos_c, mla_sin_c, mla_q_norm[i], mla_kv_norm[i], mw, tm, 1)
        y_mla = flash_attention(q_l, [(k_l, v_l), (k_c, v_c)], batch, min(FLASH_Q, t), FLASH_K)

        wb = w_branch[i].astype(BF16)
        wb_mla = jnp.concatenate(
            [wb[3].reshape(HEADS, MLA_V, d), jnp.zeros((HEADS, LANES - MLA_V, d), BF16)], axis=1
        ).reshape(HEADS * LANES, d)
        wbs = (wb[0], wb[1], wb[2], wb_mla)
        wo = w_out[i].astype(BF16)
        j = i // 2
        if i % 2 == 0:
            xl, h2 = merge(u, y_ret, y_four, y_na, y_mla, xl, md, lat_row, norm_ffn[i], wbs, wo, None, MERGE_ROWS)
            wg, wu, wd = ffn_w_gate[j].astype(BF16), ffn_w_up[j].astype(BF16), ffn_w_down[j].astype(BF16)
            tf = wg.shape[1] // 2
            xl = ffn(h2, xl, md, lat_row, wg, wu, wd, tm, tf)
        else:
            n_exp = moe_router.shape[2]
            r = jnp.pad(moe_router[j], ((0, 0), (0, LANES - n_exp)))
            r_hi = r.astype(BF16)
            r_lo = (r - r_hi.astype(F32)).astype(BF16)
            xl, logits, h2p = merge(u, y_ret, y_four, y_na, y_mla, xl, md, lat_row, norm_ffn[i], wbs, wo,
                                   (r_hi, r_lo), MERGE_ROWS)
            row_tok, dest, top_w, blk_exp, n_used = moe_route(logits, n_exp, MOE_ROWS)
            xb = sc_gather_rows(h2p, row_tok)
            ewg, ewu, ewd = (sc_cast_bf16(w) for w in (moe_w_gate[j], moe_w_up[j], moe_w_down[j]))
            yb = moe_ffn(blk_exp, n_used, xb, ewg, ewu, ewd, MOE_ROWS, MOE_FFN_COLS)
            y2 = sc_gather_rows(yb, dest.T.reshape(-1))
            xl = moe_combine(xl, y2, top_w, md, lat_row, norm_final, i == depth - 1, MERGE_ROWS)

        if ctx_out:
            yc_four = fourier_short(uc, cb0 + CB_F, batch)
            yc_na = na_ctx_attention(uc, cb0, batch)
            yc_mla = flash_attention(q_c, [(k_c, v_c)], batch, tc, FLASH_K)
            if i % 2 == 0:
                xc, hc2 = merge(uc, yc_ret, yc_four, yc_na, yc_mla, xc, md, ctx_row, norm_ffn[i], wbs, wo,
                                None, MERGE_ROWS)
                xc = ffn(hc2, xc, md, ctx_row, wg, wu, wd, tm, tf)
            else:
                raise NotImplementedError("context tokens through the expert mixer")

    if depth % 2 == 1:
        xl = rmsnorm_rows(xl, norm_final, tm)
    return xl.reshape(batch, t, d)
```

```python
import functools

import numpy as np
import jax
import jax.numpy as jnp
from jax import lax
from jax.experimental import pallas as pl
from jax.experimental.pallas import tpu as pltpu
from jax.experimental.pallas import tpu_sc as plsc

F32 = jnp.float32
BF16 = jnp.bfloat16

GRID_W = 64
ROPE_BASE = 10000.0
NORM_EPS = 1e-6
HEADS = 4
RET_DK = 64
FOURIER_GROUP_DIM = 64
NA_HEAD_DIM = 64
NA_WIN_R = 8
NA_WIN_C = 16
MLA_NOPE = 64
MLA_ROPE = 32
MLA_V = 64
MLA_Q_RANK = 192
MLA_KV_RANK = 128
MOE_TOP_K = 2
BW = 256

COL_GATE = 0
CB_RQ, CB_RK, CB_RV, CB_RGF, CB_RGB, CB_F, CB_NQ, CB_NK, CB_NV, CB_MCQ, CB_MKV = range(11)
LANES = 128
NEG = -1e30

VMEM_LIMIT = 48 * 1024 * 1024
VMEM_LIMIT_LARGE = 58 * 1024 * 1024

INPROJ_COLS = 2304
RET_CHUNK = 256
NA_ROWS_PER_STEP = 16
FLASH_Q, FLASH_K = 4096, 512
MERGE_ROWS = 512
MOE_ROWS, MOE_FFN_COLS = 512, 1792


def _cp(sem, vmem=VMEM_LIMIT):
    return pltpu.CompilerParams(dimension_semantics=sem, vmem_limit_bytes=vmem)


def _dot(a, b):
    return jnp.dot(a, b, preferred_element_type=F32)


def _dot_nt(a, b):
    return lax.dot_general(a, b, (((1,), (1,)), ((), ())), preferred_element_type=F32)


def _dot_tn(a, b):
    return lax.dot_general(a, b, (((0,), (0,)), ((), ())), preferred_element_type=F32)


def _silu(x):
    return x * jax.nn.sigmoid(x)


def _pack_bf16_pairs(x):
    k = x.shape[1] // 2
    lo = lax.bitcast_convert_type(x[:, :k].astype(BF16).astype(F32), jnp.uint32) >> 16
    hi = lax.bitcast_convert_type(x[:, k:].astype(BF16).astype(F32), jnp.uint32) & jnp.uint32(0xFFFF0000)
    return lax.bitcast_convert_type(lo | hi, jnp.int32)


def _unpack_bf16_pairs(p):
    u = lax.bitcast_convert_type(p, jnp.uint32)
    lo = lax.bitcast_convert_type(u << 16, F32)
    hi = lax.bitcast_convert_type(u & jnp.uint32(0xFFFF0000), F32)
    return jnp.concatenate([lo, hi], axis=-1)


def _adaln_kernel(c_ref, w_ref, b_ref, o_ref):
    s = _silu(c_ref[...])
    o_ref[0] = _dot(s.astype(BF16), w_ref[0].astype(BF16)) + b_ref[0]


def adaln(cc, ada_w, ada_b):
    depth, d, n6 = ada_w.shape
    tn = n6 // 4
    return pl.pallas_call(
        _adaln_kernel,
        grid=(depth, n6 // tn),
        in_specs=[
            pl.BlockSpec((8, d), lambda l, j: (0, 0)),
            pl.BlockSpec((1, d, tn), lambda l, j: (l, 0, j)),
            pl.BlockSpec((1, 1, tn), lambda l, j: (l, 0, j)),
        ],
        out_specs=pl.BlockSpec((1, 8, tn), lambda l, j: (l, 0, j)),
        out_shape=jax.ShapeDtypeStruct((depth, 8, n6), F32),
        compiler_params=_cp(("arbitrary", "arbitrary")),
        name="adaln",
    )(cc, ada_w, ada_b.reshape(depth, 1, n6))


def _inproj_kernel(x_ref, g_ref, sc_ref, sh_ref, w_ref, o_ref, h_ref):
    @pl.when(pl.program_id(1) == 0)
    def _():
        x = x_ref[...]
        y = x * lax.rsqrt(jnp.mean(x * x, axis=-1, keepdims=True) + NORM_EPS)
        h = (y * g_ref[...]) * (1.0 + sc_ref[0]) + sh_ref[0]
        h_ref[...] = h.astype(BF16)

    o_ref[...] = _dot(h_ref[...], w_ref[...]).astype(o_ref.dtype)


def _mod_spec(d, mod_row, tm, k):
    return pl.BlockSpec((1, 1, d), lambda i, *_: (mod_row(i * tm) * 6 + k, 0, 0))


def norm_inproj(x, gain, mods, mod_row, w, tm, tn):
    n, d = x.shape
    nc = w.shape[1]
    return pl.pallas_call(
        _inproj_kernel,
        grid=(n // tm, nc // tn),
        in_specs=[
            pl.BlockSpec((tm, d), lambda i, j: (i, 0)),
            pl.BlockSpec((1, d), lambda i, j: (0, 0)),
            _mod_spec(d, mod_row, tm, 1),
            _mod_spec(d, mod_row, tm, 0),
            pl.BlockSpec((d, tn), lambda i, j: (0, j)),
        ],
        out_specs=pl.BlockSpec((tm, tn), lambda i, j: (i, j)),
        out_shape=jax.ShapeDtypeStruct((n, nc), BF16),
        scratch_shapes=[pltpu.VMEM((tm, d), BF16)],
        compiler_params=_cp(("arbitrary", "arbitrary")),
        name="norm_inproj",
    )(x, gain.reshape(1, d), mods, mods, w)


def _ret_init(d, cs, w, lg_ref, s0_ref, lgq_ref, s_ref, decay_ref, qw_ref, kw_ref, avg_ref):
    rev = d == 1
    s_ref[d] = s0_ref[0, d]
    pos_i = lax.broadcasted_iota(jnp.int32, (cs, 1), 0).astype(F32)
    pos_j = lax.broadcasted_iota(jnp.int32, (1, cs), 1).astype(F32)
    p_i = jnp.where(rev, cs - 1.0 - pos_i, pos_i)
    p_j = jnp.where(rev, cs - 1.0 - pos_j, pos_j)
    diff = p_i - p_j
    for h in range(HEADS):
        decay_ref[d, h] = jnp.where(diff >= 0, jnp.exp(lg_ref[d, h] * jnp.maximum(diff, 0.0)), 0.0)
    lgq = lgq_ref[d]
    qw_ref[d] = jnp.exp(lgq * (p_i + 1.0))
    kw_ref[d] = jnp.exp(lgq * (cs - 1.0 - p_i))
    hd = w // HEADS
    gi = lax.broadcasted_iota(jnp.int32, (w, w), 0) // hd
    gj = lax.broadcasted_iota(jnp.int32, (w, w), 1) // hd
    avg_ref[...] = jnp.where(gi == gj, 1.0 / hd, 0.0).astype(BF16)


def _ret_chunk(d, q_ref, k_ref, v_ref, g_ref, cos_ref, sin_ref, lgv_ref, y_ref, s_ref, decay_ref, qw_ref, kw_ref,
               avg_ref):
    cs = q_ref.shape[0]
    w = q_ref.shape[1]
    half = w // 2
    lgv = lgv_ref[d]
    s_ref, decay_ref, qw_ref, kw_ref = s_ref.at[d], decay_ref.at[d], qw_ref.at[d], kw_ref.at[d]

    cos = cos_ref[...]
    sin = sin_ref[...]

    def rope(t):
        t1, t2 = t[:, :half], t[:, half:]
        return jnp.concatenate([t1 * cos - t2 * sin, t2 * cos + t1 * sin], axis=-1)

    q = rope(q_ref[...].astype(F32))
    k = rope(k_ref[...].astype(F32)) * (RET_DK ** -0.5)
    vb = v_ref[...]

    lane = lax.broadcasted_iota(jnp.int32, (1, w), 1)
    head_q = (lane % half) // (half // HEADS)
    head_v = lane // (w // HEADS)

    s_prev = s_ref[...]
    o = _dot((q * qw_ref[...]).astype(BF16), s_prev.astype(BF16))
    qb = q.astype(BF16)
    kb = k.astype(BF16)
    zero_b = jnp.zeros_like(qb)
    for h in range(HEADS):
        a = _dot_nt(jnp.where(head_q == h, qb, zero_b), kb)
        oh = _dot((a * decay_ref[h]).astype(BF16), vb)
        o = o + jnp.where(head_v == h, oh, 0.0)

    ds = _dot_tn((k * kw_ref[...]).astype(BF16), vb)
    row_head = (lax.broadcasted_iota(jnp.int32, (w, 1), 0) % half) // (half // HEADS)
    s_new = s_prev * jnp.exp(lgv * float(cs)) + jnp.where(row_head == head_v, ds, 0.0)
    s_ref[...] = s_new

    ms = _dot((o * o).astype(BF16), avg_ref[...])
    on = o * lax.rsqrt(ms + NORM_EPS)
    y_ref[...] = (_silu(g_ref[...].astype(F32)) * on).astype(y_ref.dtype)
    return s_new


def _ret_kernel(lg_ref, qf, kf, vf, gf, cosf, sinf, qb, kb, vb, gb, cosb, sinb, s0_ref, lgq_ref, lgv_ref,
                yf_ref, yb_ref, sout_ref, s_ref, decay_ref, qw_ref, kw_ref, avg_ref, *, n_chunks):
    c = pl.program_id(1)
    scratch = (s_ref, decay_ref, qw_ref, kw_ref, avg_ref)

    @pl.when(c == 0)
    def _():
        for d in range(2):
            _ret_init(d, qf.shape[0], qf.shape[1], lg_ref, s0_ref, lgq_ref, *scratch)

    s_f = _ret_chunk(0, qf, kf, vf, gf, cosf, sinf, lgv_ref, yf_ref, *scratch)
    s_b = _ret_chunk(1, qb, kb, vb, gb, cosb, sinb, lgv_ref, yb_ref, *scratch)

    @pl.when(c == n_chunks - 1)
    def _():
        sout_ref[0, 0] = s_f
        sout_ref[0, 1] = s_b


def retention(u, cb0, lg, cos, sin, s0, batch, cs):
    n = u.shape[0]
    t = n // batch
    nch = t // cs
    w = BW
    half = w // 2
    lgq = jnp.tile(jnp.repeat(lg, half // HEADS, axis=1), (1, 2)).reshape(2, 1, w)
    lgv = jnp.repeat(lg, w // HEADS, axis=1).reshape(2, 1, w)

    def chunk(d, c):
        return nch - 1 - c if d else c

    def direction(d):
        col = lambda cb: pl.BlockSpec((cs, w), lambda b, c: (b * nch + chunk(d, c), cb0 + cb))
        tab = lambda: pl.BlockSpec((cs, half), lambda b, c: (chunk(d, c), 0))
        return [col(CB_RQ), col(CB_RK), col(CB_RV), col(CB_RGF + d), tab(), tab()]

    y_spec = lambda d: pl.BlockSpec((cs, w), lambda b, c: (b * nch + chunk(d, c), 0))
    state_spec = pl.BlockSpec((1, 2, w, w), lambda b, c: (b, 0, 0, 0))
    lane_spec = pl.BlockSpec((2, 1, w), lambda b, c: (0, 0, 0))
    y_f, y_b, s_out = pl.pallas_call(
        functools.partial(_ret_kernel, n_chunks=nch),
        grid=(batch, nch),
        in_specs=[pl.BlockSpec(memory_space=pltpu.SMEM)] + direction(0) + direction(1)
        + [state_spec, lane_spec, lane_spec],
        out_specs=[y_spec(0), y_spec(1), state_spec],
        out_shape=[
            jax.ShapeDtypeStruct((n, w), BF16),
            jax.ShapeDtypeStruct((n, w), BF16),
            jax.ShapeDtypeStruct((batch, 2, w, w), F32),
        ],
        scratch_shapes=[pltpu.VMEM((2, w, w), F32), pltpu.VMEM((2, HEADS, cs, cs), F32),
                        pltpu.VMEM((2, cs, w), F32), pltpu.VMEM((2, cs, w), F32), pltpu.VMEM((w, w), BF16)],
        compiler_params=_cp(("arbitrary", "arbitrary")),
        name="retention",
    )(lg, u, u, u, u, cos, sin, u, u, u, u, cos, sin, s0, lgq, lgv)
    return (y_f, y_b), s_out


def _dft_tables(t, t1, t2):
    k1 = jnp.arange(t1, dtype=jnp.int32)
    a = jnp.arange(t1, dtype=jnp.int32)
    m = jnp.arange(t2, dtype=jnp.int32)
    ph1 = (k1[None, :, None] * (a[None, None, :] * t2 + m[:, None, None])) % t
    ang1 = ph1.astype(F32) * (2.0 * np.pi / t)
    ph2 = (m[:, None] * m[None, :]) % t2
    ang2 = ph2.astype(F32) * (2.0 * np.pi / t2)
    return (jnp.cos(ang1).astype(BF16), jnp.sin(ang1).astype(BF16),
            jnp.cos(ang2).astype(BF16), jnp.sin(ang2).astype(BF16))


def _channel_tables(width):
    ch = jnp.arange(width, dtype=jnp.int32)
    same = (ch[:, None] // FOURIER_GROUP_DIM) == (ch[None, :] // FOURIER_GROUP_DIM)
    ph = ((ch[:, None] % FOURIER_GROUP_DIM) * (ch[None, :] % FOURIER_GROUP_DIM)) % FOURIER_GROUP_DIM
    ang = ph.astype(F32) * (2.0 * np.pi / FOURIER_GROUP_DIM)
    return (jnp.where(same, jnp.cos(ang), 0.0).astype(BF16),
            jnp.where(same, jnp.sin(ang), 0.0).astype(BF16))


def _fourier_kernel(x_ref, c1_ref, s1_ref, c2_ref, s2_ref, cc_ref, sc_ref, o_ref, xa, yre, yim,
                    *, t1, t2, norm):
    xa[...] = x_ref[...].astype(F32)

    def stage1(m, carry):
        xs = xa[pl.ds(m, t1, stride=t2), :].astype(BF16)
        r0 = pl.multiple_of(m * t1, t1)
        yre[pl.ds(r0, t1), :] = _dot(c1_ref[m], xs)
        yim[pl.ds(r0, t1), :] = -_dot(s1_ref[m], xs)
        return carry

    lax.fori_loop(0, t2, stage1, 0, unroll=16)

    c2 = c2_ref[...]
    s2 = s2_ref[...]
    w2 = jnp.concatenate([jnp.concatenate([c2, s2], axis=1), jnp.concatenate([-s2, c2], axis=1)], axis=0)
    cs = jnp.concatenate([cc_ref[...], sc_ref[...]], axis=0)
    group = 8

    def stage2(j, carry):
        k1 = j * group
        rows = [pl.ds(k1 + g, t2, stride=t1) for g in range(group)]
        y = jnp.concatenate([jnp.concatenate([yre[r, :] for r in rows], axis=1),
                             jnp.concatenate([yim[r, :] for r in rows], axis=1)], axis=0).astype(BF16)
        z = _dot(w2, y).astype(BF16)
        w = z.shape[1] // group
        zz = jnp.concatenate([jnp.concatenate([z[:t2, g * w:(g + 1) * w], z[t2:, g * w:(g + 1) * w]], axis=1)
                              for g in range(group)], axis=0)
        out = _dot(zz, cs) * norm
        for g in range(group):
            xa[rows[g], :] = out[g * t2:(g + 1) * t2]
        return carry

    lax.fori_loop(0, t1 // group, stage2, 0)
    o_ref[...] = xa[...].astype(o_ref.dtype)


def fourier_long(u, cb, batch, t2=LANES):
    n = u.shape[0]
    t = n // batch
    t1 = t // t2
    c1, s1, c2, s2 = _dft_tables(t, t1, t2)
    cc, sc = _channel_tables(LANES)
    norm = float(1.0 / np.sqrt(t * FOURIER_GROUP_DIM))
    full = lambda shape: pl.BlockSpec(shape, lambda b, hh: (0,) * len(shape))
    return pl.pallas_call(
        functools.partial(_fourier_kernel, t1=t1, t2=t2, norm=norm),
        grid=(batch, BW // LANES),
        in_specs=[
            pl.BlockSpec((t, LANES), lambda b, hh: (b, cb * (BW // LANES) + hh)),
            full((t2, t1, t1)), full((t2, t1, t1)), full((t2, t2)), full((t2, t2)),
            full((LANES, LANES)), full((LANES, LANES)),
        ],
        out_specs=pl.BlockSpec((t, LANES), lambda b, hh: (b, hh)),
        out_shape=jax.ShapeDtypeStruct((n, BW), BF16),
        scratch_shapes=[pltpu.VMEM((t, LANES), F32)] * 3,
        compiler_params=_cp(("arbitrary", "arbitrary")),
        name="fourier",
    )(u, c1, s1, c2, s2, cc, sc)


def _fourier_small_kernel(x_ref, ct_ref, st_ref, cc_ref, sc_ref, o_ref, *, norm):
    x = x_ref[...]
    zr = _dot(ct_ref[...], x)
    zi = -_dot(st_ref[...], x)
    out = (_dot(zr.astype(BF16), cc_ref[...]) + _dot(zi.astype(BF16), sc_ref[...])) * norm
    o_ref[...] = out.astype(o_ref.dtype)


def fourier_short(u, cb, batch):
    n = u.shape[0]
    t = n // batch
    pos = jnp.arange(t, dtype=jnp.int32)
    ang = ((pos[:, None] * pos[None, :]) % t).astype(F32) * (2.0 * np.pi / t)
    ct, st = jnp.cos(ang).astype(BF16), jnp.sin(ang).astype(BF16)
    cc, sc = _channel_tables(BW)
    norm = float(1.0 / np.sqrt(t * FOURIER_GROUP_DIM))
    full = lambda shape: pl.BlockSpec(shape, lambda b: (0,) * len(shape))
    return pl.pallas_call(
        functools.partial(_fourier_small_kernel, norm=norm),
        grid=(batch,),
        in_specs=[pl.BlockSpec((t, BW), lambda b: (b, cb)), full((t, t)), full((t, t)),
                  full((BW, BW)), full((BW, BW))],
        out_specs=pl.BlockSpec((t, BW), lambda b: (b, 0)),
        out_shape=jax.ShapeDtypeStruct((n, BW), BF16),
        compiler_params=_cp(("arbitrary",)),
        name="fourier_ctx",
    )(u, ct, st, cc, sc)


def _na_bias_table(rpb):
    n_r, n_c = rpb.shape[1], rpb.shape[2]
    span = 2 * GRID_W
    left = GRID_W - NA_WIN_C
    vp = jnp.pad(rpb.astype(F32), ((0, 0), (0, 0), (left, span - n_c - left)))
    rep = jnp.broadcast_to(vp[:, :, None, :], (HEADS, n_r, GRID_W, span)).reshape(HEADS, n_r, GRID_W * span)
    skew = rep[..., :GRID_W * (span - 1)].reshape(HEADS, n_r, GRID_W, span - 1)
    toep = skew[..., GRID_W - 1:]
    qc = np.arange(GRID_W)[:, None]
    kc = np.arange(GRID_W)[None, :]
    start = np.clip(qc - NA_WIN_C // 2, 0, GRID_W - NA_WIN_C)
    valid = (kc >= start) & (kc < start + NA_WIN_C)
    toep = jnp.where(valid, toep, NEG)
    tab = jnp.stack([toep[:, NA_WIN_R - 1 - v:2 * NA_WIN_R - 1 - v] for v in range(NA_WIN_R)])
    tab = tab.transpose(0, 1, 3, 2, 4)
    return tab.reshape(NA_WIN_R, HEADS * GRID_W, NA_WIN_R * GRID_W)


def _na_kernel(q_ref, k_ref, v_ref, kc_ref, vc_ref, bias_ref, o_ref, *, rows_per_step, n_rows):
    i = pl.program_id(1)
    w = q_ref.shape[1]
    lane = lax.broadcasted_iota(jnp.int32, (1, w), 1)
    head = lane // (w // HEADS)
    scale = jnp.asarray(NA_HEAD_DIM ** -0.5, q_ref.dtype)
    kc = kc_ref[...]
    vc = vc_ref[...]
    win = NA_WIN_R * GRID_W

    def row(rl, carry):
        r = i * rows_per_step + rl
        rs = jnp.clip(r - NA_WIN_R // 2, 0, n_rows - NA_WIN_R)
        var = r - rs
        q0 = pl.multiple_of(rl * GRID_W, GRID_W)
        k0 = pl.multiple_of(rs * GRID_W, GRID_W)
        q = q_ref[pl.ds(q0, GRID_W), :] * scale
        kw = k_ref[pl.ds(k0, win), :]
        vw = v_ref[pl.ds(k0, win), :]
        zero_b = jnp.zeros_like(q)
        q4 = jnp.concatenate([jnp.where(head == h, q, zero_b) for h in range(HEADS)], axis=0)
        s_loc = _dot_nt(q4, kw) + bias_ref[var]
        s_ctx = _dot_nt(q4, kc)
        m = jnp.maximum(jnp.max(s_loc, axis=-1, keepdims=True), jnp.max(s_ctx, axis=-1, keepdims=True))
        p_loc = jnp.exp(s_loc - m)
        p_ctx = jnp.exp(s_ctx - m)
        l = jnp.sum(p_loc, axis=-1, keepdims=True) + jnp.sum(p_ctx, axis=-1, keepdims=True)
        pv = (_dot(p_loc.astype(BF16), vw) + _dot(p_ctx.astype(BF16), vc)) / l
        acc = jnp.zeros((GRID_W, w), F32)
        for h in range(HEADS):
            acc = acc + jnp.where(head == h, pv[h * GRID_W:(h + 1) * GRID_W], 0.0)
        o_ref[pl.ds(q0, GRID_W), :] = acc.astype(o_ref.dtype)
        return carry

    lax.fori_loop(0, rows_per_step, row, 0, unroll=8)


def na_attention(u, uc, cb0, bias_tab, batch, rows_per_step):
    n = u.shape[0]
    t = n // batch
    tc = uc.shape[0] // batch
    n_rows = t // GRID_W
    steps = n_rows // rows_per_step
    tq = rows_per_step * GRID_W
    return pl.pallas_call(
        functools.partial(_na_kernel, rows_per_step=rows_per_step, n_rows=n_rows),
        grid=(batch, steps),
        in_specs=[
            pl.BlockSpec((tq, BW), lambda b, i: (b * steps + i, cb0 + CB_NQ)),
            pl.BlockSpec((t, BW), lambda b, i: (b, cb0 + CB_NK)),
            pl.BlockSpec((t, BW), lambda b, i: (b, cb0 + CB_NV)),
            pl.BlockSpec((tc, BW), lambda b, i: (b, cb0 + CB_NK)),
            pl.BlockSpec((tc, BW), lambda b, i: (b, cb0 + CB_NV)),
            pl.BlockSpec(bias_tab.shape, lambda b, i: (0, 0, 0)),
        ],
        out_specs=pl.BlockSpec((tq, BW), lambda b, i: (b * steps + i, 0)),
        out_shape=jax.ShapeDtypeStruct((n, BW), BF16),
        compiler_params=_cp(("arbitrary", "arbitrary")),
        name="na_attention",
    )(u, u, u, uc, uc, bias_tab)


def _na_ctx_kernel(q_ref, k_ref, v_ref, o_ref):
    w = q_ref.shape[1]
    lane = lax.broadcasted_iota(jnp.int32, (1, w), 1)
    head = lane // (w // HEADS)
    scale = NA_HEAD_DIM ** -0.5
    q = q_ref[...]
    k = k_ref[...]
    v = v_ref[...]
    zero_b = jnp.zeros_like(q)
    acc = jnp.zeros(q.shape, F32)
    for h in range(HEADS):
        s = _dot_nt(jnp.where(head == h, q, zero_b), k) * scale
        p = jnp.exp(s - jnp.max(s, axis=-1, keepdims=True))
        l = jnp.sum(p, axis=-1, keepdims=True)
        acc = acc + jnp.where(head == h, _dot(p.astype(BF16), v) / l, 0.0)
    o_ref[...] = acc.astype(o_ref.dtype)


def na_ctx_attention(uc, cb0, batch):
    tc = uc.shape[0] // batch
    spec = lambda cb: pl.BlockSpec((tc, BW), lambda b: (b, cb0 + cb))
    return pl.pallas_call(
        _na_ctx_kernel,
        grid=(batch,),
        in_specs=[spec(CB_NQ), spec(CB_NK), spec(CB_NV)],
        out_specs=pl.BlockSpec((tc, BW), lambda b: (b, 0)),
        out_shape=jax.ShapeDtypeStruct((uc.shape[0], BW), BF16),
        compiler_params=_cp(("arbitrary",)),
        name="na_ctx_attention",
    )(uc, uc, uc)


def _mla_prep_kernel(cq_ref, ckv_ref, kr_ref, cos_ref, sin_ref, qn_ref, kvn_ref, wq_ref, wqr_ref,
                     wk_ref, wv_ref, p1_ref, p2_ref, one_ref, q_ref, k_ref, v_ref):
    cos = cos_ref[...]
    sin = sin_ref[...]
    cos4 = jnp.concatenate([cos] * HEADS, axis=-1)
    sin4 = jnp.concatenate([sin] * HEADS, axis=-1)

    cq = cq_ref[...].astype(F32)
    ms = jnp.sum(cq * cq, axis=-1, keepdims=True) * (1.0 / MLA_Q_RANK)
    cqn = ((cq * lax.rsqrt(ms + NORM_EPS)) * qn_ref[...]).astype(BF16)
    q = _dot(cqn, wq_ref[...]) * cos4 + _dot(cqn, wqr_ref[...]) * sin4
    q_ref[...] = (q * float((MLA_NOPE + MLA_ROPE) ** -0.5 * np.log2(np.e))).astype(q_ref.dtype)

    ckv = ckv_ref[...].astype(F32)
    ms = jnp.mean(ckv * ckv, axis=-1, keepdims=True)
    ckvn = ((ckv * lax.rsqrt(ms + NORM_EPS)) * kvn_ref[...]).astype(BF16)
    kr = kr_ref[...]
    k_rot = _dot(kr, p1_ref[...]) * cos + _dot(kr, p2_ref[...]) * sin
    k = _dot(ckvn, wk_ref[...]) + jnp.concatenate([k_rot] * HEADS, axis=-1)
    k_ref[...] = k.astype(k_ref.dtype)
    v_ref[...] = (_dot(ckvn, wv_ref[...]) + one_ref[...]).astype(v_ref.dtype)


def _mla_weights(w_uq, w_ukv):
    qr = w_uq.shape[0]
    dq = MLA_NOPE + MLA_ROPE
    hr = MLA_ROPE // 2
    wq3 = w_uq.reshape(qr, HEADS, dq)
    zq = jnp.zeros((qr, HEADS, LANES - dq), F32)
    wq = jnp.concatenate([wq3, zq], axis=-1)
    x1 = wq3[..., MLA_NOPE:MLA_NOPE + hr]
    x2 = wq3[..., MLA_NOPE + hr:]
    wqr = jnp.concatenate([jnp.zeros((qr, HEADS, MLA_NOPE), F32), -x2, x1, zq], axis=-1)
    pad_rows = lambda m: jnp.pad(m.reshape(qr, HEADS * LANES), ((0, BW - qr), (0, 0)))
    kvr = w_ukv.shape[0]
    wkv3 = w_ukv.reshape(kvr, HEADS, MLA_NOPE + MLA_V)
    zk = jnp.zeros((kvr, HEADS, LANES - MLA_NOPE), F32)
    wk = jnp.concatenate([wkv3[..., :MLA_NOPE], zk], axis=-1).reshape(kvr, HEADS * LANES)
    wv = jnp.concatenate([wkv3[..., MLA_NOPE:], zk], axis=-1).reshape(kvr, HEADS * LANES)
    j = np.arange(hr)
    p1 = np.zeros((LANES, LANES), np.float32)
    p1[np.arange(MLA_ROPE), MLA_NOPE + np.arange(MLA_ROPE)] = 1.0
    p2 = np.zeros((LANES, LANES), np.float32)
    p2[hr + j, MLA_NOPE + j] = -1.0
    p2[j, MLA_NOPE + hr + j] = 1.0
    one = np.zeros((1, HEADS * LANES), np.float32)
    one[0, MLA_V + LANES * np.arange(HEADS)] = 1.0
    return (pad_rows(wq).astype(BF16), pad_rows(wqr).astype(BF16), wk.astype(BF16), wv.astype(BF16),
            jnp.asarray(p1, BF16), jnp.asarray(p2, BF16), jnp.asarray(one))


def mla_prep(u, cb0, cos, sin, q_norm, kv_norm, weights, tm, rope_blocks):
    n = u.shape[0]
    wq, wqr, wk, wv, p1, p2, one = weights
    qn = jnp.pad(q_norm, (0, BW - q_norm.shape[0])).reshape(1, BW)
    full = lambda a: pl.BlockSpec(a.shape, lambda i: (0,) * a.ndim)
    tab = pl.BlockSpec((tm, LANES), lambda i: (i % rope_blocks, 0))
    kv_cb = (cb0 + CB_MKV) * (BW // LANES)
    out = jax.ShapeDtypeStruct((n, HEADS * LANES), BF16)
    ospec = pl.BlockSpec((tm, HEADS * LANES), lambda i: (i, 0))
    return pl.pallas_call(
        _mla_prep_kernel,
        grid=(n // tm,),
        in_specs=[
            pl.BlockSpec((tm, BW), lambda i: (i, cb0 + CB_MCQ)),
            pl.BlockSpec((tm, LANES), lambda i: (i, kv_cb)),
            pl.BlockSpec((tm, LANES), lambda i: (i, kv_cb + 1)),
            tab, tab, full(qn), pl.BlockSpec((1, LANES), lambda i: (0, 0)),
            full(wq), full(wqr), full(wk), full(wv), full(p1), full(p2), full(one),
        ],
        out_specs=[ospec, ospec, ospec],
        out_shape=[out, out, out],
        compiler_params=_cp(("arbitrary",)),
        name="mla_prep",
    )(u, u, u, cos, sin, qn, kv_norm.reshape(1, LANES), wq, wqr, wk, wv, p1, p2, one)


def _flash_kernel(*refs, lens, tk):
    q_ref = refs[0]
    kv_refs = refs[1:1 + 2 * len(lens)]
    o_ref = refs[1 + 2 * len(lens)]
    q = q_ref[...]
    tq = q.shape[0]
    m = jnp.full((tq, 1), NEG, F32)
    acc = jnp.zeros((tq, LANES), F32)

    def chunk(kc, vc, m, acc):
        s = _dot_nt(q, kc)
        m_new = jnp.maximum(m, jnp.max(s, axis=-1, keepdims=True))
        p = jnp.exp2((s - m_new).astype(BF16))
        acc = jnp.exp2(m - m_new) * acc + _dot(p, vc)
        return m_new, acc

    for si, length in enumerate(lens):
        k_ref, v_ref = kv_refs[2 * si], kv_refs[2 * si + 1]
        step = min(tk, length)
        if length == step:
            m, acc = chunk(k_ref[...], v_ref[...], m, acc)
        else:
            def body(j, carry, k_ref=k_ref, v_ref=v_ref, step=step):
                j0 = pl.multiple_of(j * step, step)
                return chunk(k_ref[pl.ds(j0, step), :], v_ref[pl.ds(j0, step), :], *carry)

            m, acc = lax.fori_loop(0, length // step, body, (m, acc), unroll=16)

    lane = lax.broadcasted_iota(jnp.int32, (1, LANES), 1)
    l = jnp.sum(jnp.where(lane == MLA_V, acc, 0.0), axis=-1, keepdims=True)
    o_ref[...] = jnp.where(lane < MLA_V, acc / l, 0.0).astype(o_ref.dtype)


def flash_attention(q, kvs, batch, tq, tk):
    n = q.shape[0]
    nq = n // batch // tq
    lens = tuple(k.shape[0] // batch for k, _ in kvs)
    in_specs = [pl.BlockSpec((tq, LANES), lambda b, h, i: (b * nq + i, h))]
    args = [q]
    for (k, v), length in zip(kvs, lens):
        in_specs += [pl.BlockSpec((length, LANES), lambda b, h, i: (b, h))] * 2
        args += [k, v]
    return pl.pallas_call(
        functools.partial(_flash_kernel, lens=lens, tk=tk),
        grid=(batch, HEADS, nq),
        in_specs=in_specs,
        out_specs=pl.BlockSpec((tq, LANES), lambda b, h, i: (b * nq + i, h)),
        out_shape=jax.ShapeDtypeStruct((n, HEADS * LANES), BF16),
        compiler_params=_cp(("arbitrary", "arbitrary", "arbitrary")),
        name="mla_attention",
    )(*args)


def _merge_kernel(*refs, with_router):
    (gate_ref, yr_f_ref, yr_b_ref, yf_ref, yn_ref, ym_ref, x_ref, g1_ref, wb_ret_ref, wb_f_ref,
     wb_na_ref, wb_mla_ref, wo_ref, gain_ref, sc_ref, sh_ref) = refs[:16]
    rest = refs[16:]
    if with_router:
        rh_ref, rl_ref, x_out_ref, lg_out_ref, hp_out_ref = rest
    else:
        x_out_ref, h_out_ref = rest
    d = x_ref.shape[1]

    def gated(k, y):
        g = jax.nn.sigmoid(gate_ref[:, k * d:(k + 1) * d].astype(F32))
        return g * y

    m = gated(0, _dot(yr_f_ref[...] + yr_b_ref[...], wb_ret_ref[...]))
    m = m + gated(1, _dot(yf_ref[...], wb_f_ref[...]))
    m = m + gated(2, _dot(yn_ref[...], wb_na_ref[...]))
    m = m + gated(3, _dot(ym_ref[...], wb_mla_ref[...]))
    y = _dot(m.astype(BF16), wo_ref[...])
    x = x_ref[...] + g1_ref[0] * y
    x_out_ref[...] = x
    hn = x * lax.rsqrt(jnp.mean(x * x, axis=-1, keepdims=True) + NORM_EPS)
    h = (hn * gain_ref[...]) * (1.0 + sc_ref[0]) + sh_ref[0]
    if with_router:
        hp_out_ref[...] = _pack_bf16_pairs(h)
        h_hi = h.astype(BF16)
        h_lo = (h - h_hi.astype(F32)).astype(BF16)
        lg_out_ref[...] = (_dot(h_hi, rh_ref[...]) + _dot(h_lo, rh_ref[...])) + _dot(h_hi, rl_ref[...])
    else:
        h_out_ref[...] = h.astype(h_out_ref.dtype)


def merge(u, y_ret, y_four, y_na, y_mla, x, mods, mod_row, gain, wb, w_out, router, tm):
    n, d = x.shape
    wb_ret, wb_f, wb_na, wb_mla = wb
    full = lambda a: pl.BlockSpec(a.shape, lambda i: (0,) * a.ndim)
    br = lambda: pl.BlockSpec((tm, BW), lambda i: (i, 0))
    tok = lambda: pl.BlockSpec((tm, d), lambda i: (i, 0))
    in_specs = [
        pl.BlockSpec((tm, 4 * d), lambda i: (i, 0)),
        br(), br(), br(), br(),
        pl.BlockSpec((tm, HEADS * LANES), lambda i: (i, 0)),
        tok(), _mod_spec(d, mod_row, tm, 2),
        full(wb_ret), full(wb_f), full(wb_na), full(wb_mla), full(w_out),
        pl.BlockSpec((1, d), lambda i: (0, 0)), _mod_spec(d, mod_row, tm, 4), _mod_spec(d, mod_row, tm, 3),
    ]
    args = [u, y_ret[0], y_ret[1], y_four, y_na, y_mla, x, mods, wb_ret, wb_f, wb_na, wb_mla, w_out,
            gain.reshape(1, d), mods, mods]
    out_specs = [tok()]
    out_shape = [jax.ShapeDtypeStruct((n, d), F32)]
    if router is None:
        out_specs.append(tok())
        out_shape.append(jax.ShapeDtypeStruct((n, d), BF16))
    else:
        in_specs += [full(router[0]), full(router[1])]
        args += list(router)
        out_specs += [pl.BlockSpec((tm, LANES), lambda i: (i, 0)), pl.BlockSpec((tm, d // 2), lambda i: (i, 0))]
        out_shape += [jax.ShapeDtypeStruct((n, LANES), F32), jax.ShapeDtypeStruct((n, d // 2), jnp.int32)]
    return pl.pallas_call(
        functools.partial(_merge_kernel, with_router=router is not None),
        grid=(n // tm,),
        in_specs=in_specs,
        out_specs=out_specs,
        out_shape=out_shape,
        compiler_params=_cp(("arbitrary",)),
        name="merge",
    )(*args)


def _ffn_kernel(h_ref, wg_ref, wu_ref, wd_ref, x_ref, g2_ref, o_ref, acc_ref):
    f = pl.program_id(1)

    @pl.when(f == 0)
    def _():
        acc_ref[...] = jnp.zeros_like(acc_ref)

    h = h_ref[...]
    a = _silu(_dot(h, wg_ref[...])) * _dot(h, wu_ref[...])
    acc_ref[...] += _dot(a.astype(BF16), wd_ref[...])

    @pl.when(f == pl.num_programs(1) - 1)
    def _():
        o_ref[...] = x_ref[...] + g2_ref[0] * acc_ref[...]


def ffn(h, x, mods, mod_row, wg, wu, wd, tm, tf):
    n, d = x.shape
    nf = wg.shape[1] // tf
    return pl.pallas_call(
        _ffn_kernel,
        grid=(n // tm, nf),
        in_specs=[
            pl.BlockSpec((tm, d), lambda i, f: (i, 0)),
            pl.BlockSpec((d, tf), lambda i, f: (0, f)),
            pl.BlockSpec((d, tf), lambda i, f: (0, f)),
            pl.BlockSpec((tf, d), lambda i, f: (f, 0)),
            pl.BlockSpec((tm, d), lambda i, f: (i, 0)),
            _mod_spec(d, mod_row, tm, 5),
        ],
        out_specs=pl.BlockSpec((tm, d), lambda i, f: (i, 0)),
        out_shape=jax.ShapeDtypeStruct((n, d), F32),
        scratch_shapes=[pltpu.VMEM((tm, d), F32)],
        compiler_params=_cp(("arbitrary", "arbitrary"), VMEM_LIMIT_LARGE),
        name="ffn",
    )(h, wg, wu, wd, x, mods)


def _moe_kernel(be_ref, nu_ref, xp_ref, wg_ref, wu_ref, wd_ref, o_ref, acc_ref, x_ref):
    i = pl.program_id(0)
    f = pl.program_id(1)
    used = i < nu_ref[0]

    @pl.when(f == 0)
    def _():
        acc_ref[...] = jnp.zeros_like(acc_ref)
        x_ref[...] = _unpack_bf16_pairs(xp_ref[...]).astype(BF16)

    @pl.when(used)
    def _():
        x = x_ref[...]
        a = _silu(_dot(x, wg_ref[0])) * _dot(x, wu_ref[0])
        acc_ref[...] += _dot(a.astype(BF16), wd_ref[0])

    @pl.when(f == pl.num_programs(1) - 1)
    def _():
        o_ref[...] = _pack_bf16_pairs(acc_ref[...])


SC_CAST_BLOCK = (16, 512)


def sc_cast_bf16(w):
    e, a, b = w.shape
    br, bc = SC_CAST_BLOCK
    assert (e * a) % br == 0 and b % bc == 0
    mesh = plsc.VectorSubcoreMesh(core_axis_name="c", subcore_axis_name="s")

    @functools.partial(pl.kernel, mesh=mesh, out_type=jax.ShapeDtypeStruct((e * a, b), BF16), scratch_types=[])
    def cast(x_hbm, o_hbm):
        def body(in_v, out_v):
            @pl.loop(0, br, step=2)
            def _(r):
                @pl.loop(0, bc, step=16)
                def _(c):
                    top = in_v.at[pl.ds(r, 1), pl.ds(c, 16)][...]
                    bot = in_v.at[pl.ds(r + 1, 1), pl.ds(c, 16)][...]
                    out_v.at[pl.ds(r, 2), pl.ds(c, 16)][...] = jnp.concatenate([top, bot], axis=0).astype(BF16)

        pltpu.emit_pipeline(
            body,
            grid=(e * a // br, b // bc),
            in_specs=[pl.BlockSpec((br, bc), lambda i, j: (i, j))],
            out_specs=[pl.BlockSpec((br, bc), lambda i, j: (i, j))],
            core_axis_name=("c", "s"),
            dimension_semantics=(pltpu.PARALLEL, pltpu.PARALLEL),
        )(x_hbm, o_hbm)

    return cast(w.reshape(e * a, b)).reshape(e, a, b)


def moe_ffn(blk_exp, n_used, xb, wg, wu, wd, tm, tf):
    n, dp = xb.shape
    d = 2 * dp
    nf = wg.shape[2] // tf
    grid_spec = pltpu.PrefetchScalarGridSpec(
        num_scalar_prefetch=2,
        grid=(n // tm, nf),
        in_specs=[
            pl.BlockSpec((tm, dp), lambda i, f, be, nu: (i, 0)),
            pl.BlockSpec((1, d, tf), lambda i, f, be, nu: (be[i], 0, f)),
            pl.BlockSpec((1, d, tf), lambda i, f, be, nu: (be[i], 0, f)),
            pl.BlockSpec((1, tf, d), lambda i, f, be, nu: (be[i], f, 0)),
        ],
        out_specs=pl.BlockSpec((tm, dp), lambda i, f, be, nu: (i, 0)),
        scratch_shapes=[pltpu.VMEM((tm, d), F32), pltpu.VMEM((tm, d), BF16)],
    )
    return pl.pallas_call(
        _moe_kernel,
        grid_spec=grid_spec,
        out_shape=jax.ShapeDtypeStruct((n, dp), jnp.int32),
        compiler_params=_cp(("arbitrary", "arbitrary")),
        name="moe_ffn",
    )(blk_exp, n_used, xb, wg, wu, wd)


def _combine_kernel(x_ref, ya_ref, yb_ref, w_ref, g2_ref, gain_ref, o_ref, *, final):
    w = w_ref[...]
    y = w[:, 0:1] * _unpack_bf16_pairs(ya_ref[...]) + w[:, 1:2] * _unpack_bf16_pairs(yb_ref[...])
    x = x_ref[...] + g2_ref[0] * y
    if final:
        x = (x * lax.rsqrt(jnp.mean(x * x, axis=-1, keepdims=True) + NORM_EPS)) * gain_ref[...]
    o_ref[...] = x


def moe_combine(x, y2, w, mods, mod_row, gain, final, tm):
    n, d = x.shape
    tok = lambda: pl.BlockSpec((tm, d), lambda i: (i, 0))
    half = lambda off: pl.BlockSpec((tm, d // 2), lambda i: (i + off, 0))
    return pl.pallas_call(
        functools.partial(_combine_kernel, final=final),
        grid=(n // tm,),
        in_specs=[tok(), half(0), half(n // tm),
                  pl.BlockSpec((tm, MOE_TOP_K), lambda i: (i, 0)),
                  _mod_spec(d, mod_row, tm, 5), pl.BlockSpec((1, d), lambda i: (0, 0))],
        out_specs=tok(),
        out_shape=jax.ShapeDtypeStruct((n, d), F32),
        compiler_params=_cp(("arbitrary",)),
        name="moe_combine",
    )(x, y2, y2, w, mods, gain.reshape(1, d))


def _rmsnorm_kernel(x_ref, gain_ref, o_ref):
    x = x_ref[...]
    o_ref[...] = (x * lax.rsqrt(jnp.mean(x * x, axis=-1, keepdims=True) + NORM_EPS)) * gain_ref[...]


def rmsnorm_rows(x, gain, tm):
    n, d = x.shape
    return pl.pallas_call(
        _rmsnorm_kernel,
        grid=(n // tm,),
        in_specs=[pl.BlockSpec((tm, d), lambda i: (i, 0)), pl.BlockSpec((1, d), lambda i: (0, 0))],
        out_specs=pl.BlockSpec((tm, d), lambda i: (i, 0)),
        out_shape=jax.ShapeDtypeStruct((n, d), F32),
        compiler_params=_cp(("arbitrary",)),
        name="final_norm",
    )(x, gain.reshape(1, d))


SC_CORES = 2
SC_SUBCORES = 16
SC_CHUNK = 64


def sc_gather_rows(table, idx):
    n_out = idx.shape[0]
    width = table.shape[1]
    workers = SC_CORES * SC_SUBCORES
    per_worker = n_out // workers
    assert n_out == per_worker * workers and per_worker % (2 * SC_CHUNK) == 0 and table.dtype == jnp.int32
    mesh = plsc.VectorSubcoreMesh(core_axis_name="c", subcore_axis_name="s")
    idx_buf = pltpu.VMEM((SC_CHUNK,), jnp.int32)
    row_buf = pltpu.VMEM((SC_CHUNK, width), table.dtype)

    @functools.partial(
        pl.kernel, mesh=mesh,
        out_type=jax.ShapeDtypeStruct((n_out, width), table.dtype),
        scratch_types=[idx_buf, idx_buf, row_buf, row_buf, pltpu.SemaphoreType.DMA, pltpu.SemaphoreType.DMA],
    )
    def gather(table_hbm, idx_hbm, out_hbm, idx_a, idx_b, rows_a, rows_b, sem_a, sem_b):
        base = (lax.axis_index("s") * SC_CORES + lax.axis_index("c")) * per_worker

        @pl.loop(0, per_worker // (2 * SC_CHUNK))
        def _(j):
            off_a = pl.multiple_of(base + j * (2 * SC_CHUNK), SC_CHUNK)
            off_b = pl.multiple_of(off_a + SC_CHUNK, SC_CHUNK)
            pltpu.sync_copy(idx_hbm.at[pl.ds(off_a, SC_CHUNK)], idx_a)
            gather_a = pltpu.async_copy(table_hbm.at[idx_a], rows_a, sem_a)
            pltpu.sync_copy(idx_hbm.at[pl.ds(off_b, SC_CHUNK)], idx_b)
            gather_b = pltpu.async_copy(table_hbm.at[idx_b], rows_b, sem_b)
            gather_a.wait()
            write_a = pltpu.async_copy(rows_a, out_hbm.at[pl.ds(off_a, SC_CHUNK)], sem_a)
            gather_b.wait()
            write_b = pltpu.async_copy(rows_b, out_hbm.at[pl.ds(off_b, SC_CHUNK)], sem_b)
            write_a.wait()
            write_b.wait()

    return gather(table, idx)


def moe_route(logits, n_experts, tm):
    n_tok = logits.shape[0]
    top_logit, top_idx = lax.top_k(logits[:, :n_experts], MOE_TOP_K)
    top_w = jax.nn.softmax(top_logit, axis=-1)
    e_flat = top_idx.reshape(-1).astype(jnp.int32)
    n_assign = e_flat.shape[0]
    onehot = (e_flat[:, None] == jnp.arange(n_experts, dtype=jnp.int32)[None, :]).astype(jnp.int32)
    rank = jnp.sum((jnp.cumsum(onehot, axis=0) - onehot) * onehot, axis=1)
    counts = jnp.sum(onehot, axis=0)
    padded = (counts + tm - 1) // tm * tm
    pad_end = jnp.cumsum(padded)
    pad_start = pad_end - padded
    dest = pad_start[e_flat] + rank
    n_rows = n_assign + n_experts * tm
    tok = jnp.arange(n_assign, dtype=jnp.int32) // MOE_TOP_K
    blk_start = jnp.arange(n_rows // tm, dtype=jnp.int32) * tm
    blk_exp = jnp.minimum(jnp.sum(pad_end[None, :] <= blk_start[:, None], axis=1), n_experts - 1)
    n_used = (pad_end[-1] // tm).reshape(1)
    _, tok_by_row = lax.sort_key_val(dest, tok)
    max_shift = n_experts * tm
    filler = jnp.arange(max_shift, dtype=jnp.int32)
    compact = jnp.concatenate([filler, tok_by_row, filler])
    shift = pad_start - (jnp.cumsum(counts) - counts)
    row_exp = jnp.repeat(blk_exp, tm)
    row_tok = jnp.zeros((n_rows,), jnp.int32)
    for e in range(n_experts):
        shifted = lax.dynamic_slice(compact, (max_shift - shift[e],), (n_rows,))
        row_tok = jnp.where(row_exp == e, shifted, row_tok)
    return row_tok, dest.reshape(n_tok, MOE_TOP_K), top_w, blk_exp.astype(jnp.int32), n_used.astype(jnp.int32)


def _rope_split(wcols):
    d, w = wcols.shape
    half = w // HEADS // 2
    return wcols.reshape(d, HEADS, 2, half).transpose(0, 2, 1, 3).reshape(d, w)


def _inproj_weights(w_in):
    d = w_in.shape[0]
    kv = (BW, BW, BW, BW, MLA_KV_RANK, MLA_ROPE)
    qs = (BW, BW, BW, BW, BW, MLA_Q_RANK, 4 * d)
    offs = np.concatenate([[0], np.cumsum(kv + qs)])
    seg = lambda i: w_in[:, offs[i]:offs[i + 1]]
    r_k, r_v, n_k, n_v, m_ckv, m_kr = (seg(i) for i in range(6))
    r_q, r_gf, r_gb, f_in, n_q, m_cq, gate = (seg(6 + i) for i in range(7))
    z = lambda n: jnp.zeros((d, n), w_in.dtype)
    cols = [gate, _rope_split(r_q), _rope_split(r_k), r_v, r_gf, r_gb, f_in, n_q, n_k, n_v,
            m_cq, z(BW - MLA_Q_RANK), m_ckv, m_kr, z(LANES - MLA_ROPE)]
    return jnp.concatenate(cols, axis=1).astype(BF16)


def _ret_rope_tables(n):
    t = jnp.arange(n)
    row = (t // GRID_W).astype(F32)
    col = (t % GRID_W).astype(F32)
    nf = RET_DK // 4
    inv = ROPE_BASE ** (-jnp.arange(nf, dtype=F32) / nf)
    ang = jnp.concatenate([row[:, None] * inv, col[:, None] * inv], axis=-1)
    return jnp.tile(jnp.cos(ang), (1, HEADS)), jnp.tile(jnp.sin(ang), (1, HEADS))


def _mla_rope_tables(n):
    t = jnp.arange(n)
    row = (t // GRID_W).astype(F32)
    col = (t % GRID_W).astype(F32)
    nf = MLA_ROPE // 4
    inv = ROPE_BASE ** (-jnp.arange(nf, dtype=F32) / nf)
    ang = jnp.concatenate([row[:, None] * inv, col[:, None] * inv], axis=-1)
    pad = jnp.zeros((n, LANES - MLA_NOPE - MLA_ROPE), F32)
    cos = jnp.concatenate([jnp.ones((n, MLA_NOPE), F32), jnp.cos(ang), jnp.cos(ang), pad], axis=-1)
    sin = jnp.concatenate([jnp.zeros((n, MLA_NOPE), F32), jnp.sin(ang), jnp.sin(ang), pad], axis=-1)
    return cos, sin


def _tile_rows(*sizes):
    for tm in (1024, 512, 256, 128):
        if all(s % tm == 0 for s in sizes):
            return tm
    raise ValueError(f"token counts {sizes} need a common multiple-of-128 row tile")


def kernel(x, c, ctx, c_ctx, ada_w, ada_b, norm_mix, norm_ffn, w_in, ret_decay_fwd, ret_decay_bwd,
           mla_q_norm, mla_kv_norm, mla_w_uq, mla_w_ukv, na_rpb, w_branch, w_out,
           ffn_w_gate, ffn_w_up, ffn_w_down, moe_router, moe_w_gate, moe_w_up, moe_w_down, norm_final):
    batch, t, d = x.shape
    tc = ctx.shape[1]
    depth = ada_w.shape[0]
    nl, ncx = batch * t, batch * tc
    assert batch < 8 and t % (16 * GRID_W) == 0 and tc % LANES == 0 and d == 4 * BW
    tm = _tile_rows(t, ncx)
    cb0 = 4 * d // BW

    xl = x.reshape(nl, d)
    xc = ctx.reshape(ncx, d)
    cc = jnp.zeros((8, d), F32).at[:batch].set(c).at[batch].set(c_ctx)
    mods = adaln(cc, ada_w, ada_b).reshape(depth, 8 * 6, 1, d)
    lat_row = lambda r0: r0 // t
    ctx_row = lambda r0: batch

    ret_cos, ret_sin = _ret_rope_tables(t)
    ret_cos_c, ret_sin_c = jnp.ones((tc, LANES), F32), jnp.zeros((tc, LANES), F32)
    mla_cos, mla_sin = _mla_rope_tables(t)
    mla_cos_c = jnp.concatenate([jnp.ones((tm, MLA_NOPE + MLA_ROPE), F32),
                                 jnp.zeros((tm, LANES - MLA_NOPE - MLA_ROPE), F32)], axis=-1)
    mla_sin_c = jnp.zeros((tm, LANES), F32)

    for i in range(depth):
        ctx_out = i < depth - 1
        md = mods[i]
        w_p = _inproj_weights(w_in[i])
        u = norm_inproj(xl, norm_mix[i], md, lat_row, w_p, tm, INPROJ_COLS)
        uc = norm_inproj(xc, norm_mix[i], md, ctx_row, w_p, tm, INPROJ_COLS)

        lg = jnp.stack([jax.nn.log_sigmoid(ret_decay_fwd[i].astype(F32)),
                        jax.nn.log_sigmoid(ret_decay_bwd[i].astype(F32))])
        zero_state = jnp.zeros((batch, 2, BW, BW), F32)
        yc_ret, s_ctx = retention(uc, cb0, lg, ret_cos_c, ret_sin_c, zero_state, batch, min(RET_CHUNK, tc))
        y_ret, _ = retention(u, cb0, lg, ret_cos, ret_sin, s_ctx, batch, RET_CHUNK)

        y_four = fourier_long(u, cb0 + CB_F, batch)

        bias_tab = _na_bias_table(na_rpb[i])
        y_na = na_attention(u, uc, cb0, bias_tab, batch, NA_ROWS_PER_STEP)

        mw = _mla_weights(mla_w_uq[i], mla_w_ukv[i])
        q_l, k_l, v_l = mla_prep(u, cb0, mla_cos, mla_sin, mla_q_norm[i], mla_kv_norm[i], mw, tm, t // tm)
        q_c, k_c, v_c = mla_prep(uc, cb0, mla_cos_c, mla_sin_c, mla_q_norm[i], mla_kv_norm[i], mw, tm, 1)
        y_mla = flash_attention(q_l, [(k_l, v_l), (k_c, v_c)], batch, min(FLASH_Q, t), FLASH_K)

        wb = w_branch[i].astype(BF16)
        wb_mla = jnp.concatenate(
            [wb[3].reshape(HEADS, MLA_V, d), jnp.zeros((HEADS, LANES - MLA_V, d), BF16)], axis=1
        ).reshape(HEADS * LANES, d)
        wbs = (wb[0], wb[1], wb[2], wb_mla)
        wo = w_out[i].astype(BF16)
        j = i // 2
        if i % 2 == 0:
            xl, h2 = merge(u, y_ret, y_four, y_na, y_mla, xl, md, lat_row, norm_ffn[i], wbs, wo, None, MERGE_ROWS)
            wg, wu, wd = ffn_w_gate[j].astype(BF16), ffn_w_up[j].astype(BF16), ffn_w_down[j].astype(BF16)
            tf = wg.shape[1] // 2
            xl = ffn(h2, xl, md, lat_row, wg, wu, wd, tm, tf)
        else:
            n_exp = moe_router.shape[2]
            r = jnp.pad(moe_router[j], ((0, 0), (0, LANES - n_exp)))
            r_hi = r.astype(BF16)
            r_lo = (r - r_hi.astype(F32)).astype(BF16)
            xl, logits, h2p = merge(u, y_ret, y_four, y_na, y_mla, xl, md, lat_row, norm_ffn[i], wbs, wo,
                                   (r_hi, r_lo), MERGE_ROWS)
            row_tok, dest, top_w, blk_exp, n_used = moe_route(logits, n_exp, MOE_ROWS)
            xb = sc_gather_rows(h2p, row_tok)
            ewg, ewu, ewd = (sc_cast_bf16(w) for w in (moe_w_gate[j], moe_w_up[j], moe_w_down[j]))
            yb = moe_ffn(blk_exp, n_used, xb, ewg, ewu, ewd, MOE_ROWS, MOE_FFN_COLS)
            y2 = sc_gather_rows(yb, dest.T.reshape(-1))
            xl = moe_combine(xl, y2, top_w, md, lat_row, norm_final, i == depth - 1, MERGE_ROWS)

        if ctx_out:
            yc_four = fourier_short(uc, cb0 + CB_F, batch)
            yc_na = na_ctx_attention(uc, cb0, batch)
            yc_mla = flash_attention(q_c, [(k_c, v_c)], batch, tc, FLASH_K)
            if i % 2 == 0:
                xc, hc2 = merge(uc, yc_ret, yc_four, yc_na, yc_mla, xc, md, ctx_row, norm_ffn[i], wbs, wo,
                                None, MERGE_ROWS)
                xc = ffn(hc2, xc, md, ctx_row, wg, wu, wd, tm, tf)
            else:
                raise NotImplementedError("context tokens through the expert mixer")

    if depth % 2 == 1:
        xl = rmsnorm_rows(xl, norm_final, tm)
    return xl.reshape(batch, t, d)
```
